```python
import math
import jax, jax.numpy as jnp
from jax import lax
import numpy as np

D_MODEL = 2048
BATCH = 2
SEQ = 8192
DEPTH = 2

N_MIXERS = 2
N_S5 = (DEPTH + 1) // 2
N_DSA = DEPTH // 2
S5_GROUP = 16
S5_GROUPS = D_MODEL // S5_GROUP
S5_STATE = 64
DT_MIN = 0.001
DT_MAX = 0.1
N_HEADS = 16
HEAD_DIM = D_MODEL // N_HEADS
IDX_HEADS = 16
IDX_DIM = 64
TOPK_MAX = 256
Q_BLOCK = 128
ROPE_THETA = 10000.0
DSA_IN = 3 * D_MODEL + IDX_HEADS * IDX_DIM + IDX_HEADS + IDX_DIM
D_FF = ((8 * D_MODEL // 3 + 255) // 256) * 256
ALPHA = (2.0 * DEPTH) ** 0.25
BETA = (8.0 * DEPTH) ** -0.25
LN_EPS = 1e-5

kernel_name = "hybrid_s5_dsa_deepnorm_adaln"


def layer_norm(x, g, b):
    xf = x.astype(jnp.float32)
    mu = jnp.mean(xf, axis=-1, keepdims=True)
    var = jnp.mean(jnp.square(xf - mu), axis=-1, keepdims=True)
    y = (xf - mu) * lax.rsqrt(var + LN_EPS)
    return (y * g.astype(jnp.float32) + b.astype(jnp.float32)).astype(x.dtype)


def rope_tables(positions, dim):
    inv = 1.0 / (ROPE_THETA ** (jnp.arange(0, dim, 2, dtype=jnp.float32) / dim))
    ang = positions.astype(jnp.float32)[..., None] * inv
    return jnp.cos(ang), jnp.sin(ang)


def apply_rope(x, cos, sin):
    half = x.shape[-1] // 2
    xf = x.astype(jnp.float32)
    x1, x2 = xf[..., :half], xf[..., half:]
    cs, sn = cos[:, :, None, :], sin[:, :, None, :]
    return jnp.concatenate([x1 * cs - x2 * sn, x2 * cs + x1 * sn], axis=-1).astype(x.dtype)


def s5_mixer(h, in_w, a_re, a_im, log_dt, b_re, b_im, c_re, c_im, d_skip, glu_w, glu_b):
    Bsz, L, D = h.shape
    f32 = jnp.float32
    u = h @ in_w
    ug = u.reshape(Bsz, L, S5_GROUPS, S5_GROUP).astype(f32).astype(jnp.complex64)
    lam = lax.complex(a_re.astype(f32), a_im.astype(f32))
    dt = jnp.exp(log_dt.astype(f32))[:, None]
    lam_bar = jnp.exp(lam * dt)
    bmat = lax.complex(b_re.astype(f32), b_im.astype(f32))
    b_bar = ((lam_bar - 1.0) / lam)[..., None] * bmat
    bu = jnp.einsum('gnp,blgp->lbgn', b_bar, ug)
    a_seq = jnp.broadcast_to(lam_bar, (L,) + lam_bar.shape)

    def combine(left, right):
        a_l, b_l = left
        a_r, b_r = right
        return a_r * a_l, a_r[:, None] * b_l + b_r

    _, states = lax.associative_scan(combine, (a_seq, bu), axis=0)
    cmat = lax.complex(c_re.astype(f32), c_im.astype(f32))
    y = jnp.real(jnp.einsum('gpn,lbgn->blgp', cmat, states)).reshape(Bsz, L, D)
    y = y + d_skip.astype(f32) * u.astype(f32)
    g = jax.nn.gelu(y).astype(h.dtype)
    z = g @ glu_w + glu_b
    val, gate = jnp.split(z, 2, axis=-1)
    return val * jax.nn.sigmoid(gate)


def dsa_mixer(h, cos_h, sin_h, cos_i, sin_i, in_w, out_w):
    Bsz, L, D = h.shape
    f32 = jnp.float32
    k_top = min(TOPK_MAX, L // 4)
    proj = h @ in_w
    q, k, v, qi, wi, ki = jnp.split(
        proj, [D, 2 * D, 3 * D, 3 * D + IDX_HEADS * IDX_DIM, 3 * D + IDX_HEADS * IDX_DIM + IDX_HEADS], axis=-1)
    q = apply_rope(q.reshape(Bsz, L, N_HEADS, HEAD_DIM), cos_h, sin_h)
    k = apply_rope(k.reshape(Bsz, L, N_HEADS, HEAD_DIM), cos_h, sin_h)
    v = v.reshape(Bsz, L, N_HEADS, HEAD_DIM)
    qi = apply_rope(qi.reshape(Bsz, L, IDX_HEADS, IDX_DIM), cos_i, sin_i)
    ki = apply_rope(ki[:, :, None, :], cos_i, sin_i)[:, :, 0, :]
    wi = wi * (IDX_HEADS ** -0.5)
    key_pos = jnp.arange(L)
    gather = jax.vmap(lambda arr, idx: arr[idx])

    def block(i):
        start = i * Q_BLOCK
        qb = lax.dynamic_slice_in_dim(q, start, Q_BLOCK, axis=1)
        qib = lax.dynamic_slice_in_dim(qi, start, Q_BLOCK, axis=1)
        wib = lax.dynamic_slice_in_dim(wi, start, Q_BLOCK, axis=1)
        qpos = start + jnp.arange(Q_BLOCK)
        causal = key_pos[None, :] <= qpos[:, None]
        s_idx = jnp.einsum('bqhd,bsd->bqhs', qib, ki, preferred_element_type=f32) * (IDX_DIM ** -0.5)
        score = jnp.einsum('bqh,bqhs->bqs', wib.astype(f32), jax.nn.relu(s_idx))
        score = jnp.where(causal[None], score, -jnp.inf)
        _, sel = lax.top_k(score, k_top)
        valid = sel <= qpos[None, :, None]
        kg = gather(k, sel)
        vg = gather(v, sel)
        logits = jnp.einsum('bqhd,bqkhd->bhqk', qb, kg, preferred_element_type=f32) * (HEAD_DIM ** -0.5)
        logits = jnp.where(valid[:, None], logits, -jnp.inf)
        p = jax.nn.softmax(logits, axis=-1).astype(v.dtype)
        ob = jnp.einsum('bhqk,bqkhd->bqhd', p, vg)
        return ob.reshape(Bsz, Q_BLOCK, D)

    out = lax.map(block, jnp.arange(L // Q_BLOCK))
    out = jnp.transpose(out, (1, 0, 2, 3)).reshape(Bsz, L, D)
    return out @ out_w


def swiglu(h, w_in, w_out):
    g, u = jnp.split(h @ w_in, 2, axis=-1)
    return (jax.nn.silu(g) * u) @ w_out


def setup_inputs(seed: int = 0) -> dict:
    key = jax.random.key(seed)
    ks = jax.random.split(key, 24)
    f32 = jnp.float32
    D = D_MODEL

    def nrm(k, shape, scale):
        return jax.random.normal(k, shape, f32) * scale

    x = nrm(ks[0], (BATCH, SEQ, D), 1.0)
    c = nrm(ks[1], (BATCH, D), 1.0)
    positions = (jax.random.randint(ks[2], (BATCH, 1), 0, 4096) + jnp.arange(SEQ)[None, :]).astype(jnp.int32)
    ada_w = nrm(ks[3], (DEPTH, D, 6 * D), 0.1 * D ** -0.5)
    ada_b = nrm(ks[4], (DEPTH, 6 * D), 0.01)
    ln_g = 1.0 + nrm(ks[5], (DEPTH, 2, D), 0.02)
    ln_b = nrm(ks[6], (DEPTH, 2, D), 0.02)
    s5_in_w = nrm(ks[7], (N_S5, D, D), D ** -0.5)
    n_idx = jnp.arange(S5_STATE, dtype=f32)
    s5_a_re = -0.5 * jnp.exp(nrm(ks[8], (N_S5, S5_GROUPS, S5_STATE), 0.05))
    s5_a_im = math.pi * n_idx + nrm(ks[9], (N_S5, S5_GROUPS, S5_STATE), 0.05)
    s5_log_dt = jax.random.uniform(ks[10], (N_S5, S5_GROUPS), f32, math.log(DT_MIN), math.log(DT_MAX))
    s5_b_re = nrm(ks[11], (N_S5, S5_GROUPS, S5_STATE, S5_GROUP), (2 * S5_GROUP) ** -0.5)
    s5_b_im = nrm(ks[12], (N_S5, S5_GROUPS, S5_STATE, S5_GROUP), (2 * S5_GROUP) ** -0.5)
    s5_c_re = nrm(ks[13], (N_S5, S5_GROUPS, S5_GROUP, S5_STATE), S5_STATE ** -0.5)
    s5_c_im = nrm(ks[14], (N_S5, S5_GROUPS, S5_GROUP, S5_STATE), S5_STATE ** -0.5)
    s5_d = nrm(ks[15], (N_S5, D), 1.0)
    s5_glu_w = nrm(ks[16], (N_S5, D, 2 * D), D ** -0.5)
    s5_glu_w = s5_glu_w.at[:, :, :D].multiply(BETA)
    s5_glu_b = nrm(ks[17], (N_S5, 2 * D), 0.01)
    dsa_in_w = nrm(ks[18], (N_DSA, D, DSA_IN), D ** -0.5)
    dsa_out_w = nrm(ks[19], (N_DSA, D, D), BETA * D ** -0.5)
    ffn_w_in = nrm(ks[20], (DEPTH, D, 2 * D_FF), D ** -0.5)
    ffn_w_out = nrm(ks[21], (DEPTH, D_FF, D), BETA * D_FF ** -0.5)
    return {"x": x, "c": c, "positions": positions, "ada_w": ada_w, "ada_b": ada_b,
            "ln_g": ln_g, "ln_b": ln_b, "s5_in_w": s5_in_w, "s5_a_re": s5_a_re, "s5_a_im": s5_a_im,
            "s5_log_dt": s5_log_dt, "s5_b_re": s5_b_re, "s5_b_im": s5_b_im, "s5_c_re": s5_c_re,
            "s5_c_im": s5_c_im, "s5_d": s5_d, "s5_glu_w": s5_glu_w, "s5_glu_b": s5_glu_b,
            "dsa_in_w": dsa_in_w, "dsa_out_w": dsa_out_w, "ffn_w_in": ffn_w_in, "ffn_w_out": ffn_w_out}


def reference(x, c, positions, ada_w, ada_b, ln_g, ln_b, s5_in_w, s5_a_re, s5_a_im, s5_log_dt,
              s5_b_re, s5_b_im, s5_c_re, s5_c_im, s5_d, s5_glu_w, s5_glu_b, dsa_in_w, dsa_out_w,
              ffn_w_in, ffn_w_out):
    cos_h, sin_h = rope_tables(positions, HEAD_DIM)
    cos_i, sin_i = rope_tables(positions, IDX_DIM)
    c_act = jax.nn.silu(c)
    for i in range(DEPTH):
        mod = c_act @ ada_w[i] + ada_b[i]
        sh1, sc1, g1, sh2, sc2, g2 = [m[:, None, :] for m in jnp.split(mod, 6, axis=-1)]
        h = x * (1.0 + sc1) + sh1
        j = i // N_MIXERS
        if i % N_MIXERS == 0:
            y = s5_mixer(h, s5_in_w[j], s5_a_re[j], s5_a_im[j], s5_log_dt[j], s5_b_re[j], s5_b_im[j],
                         s5_c_re[j], s5_c_im[j], s5_d[j], s5_glu_w[j], s5_glu_b[j])
        else:
            y = dsa_mixer(h, cos_h, sin_h, cos_i, sin_i, dsa_in_w[j], dsa_out_w[j])
        x = layer_norm(ALPHA * x + (1.0 + g1) * y, ln_g[i, 0], ln_b[i, 0])
        h = x * (1.0 + sc2) + sh2
        x = layer_norm(ALPHA * x + (1.0 + g2) * swiglu(h, ffn_w_in[i], ffn_w_out[i]), ln_g[i, 1], ln_b[i, 1])
    return x
```

```python
import functools
import math

import jax
import jax.numpy as jnp
from jax import lax
from jax.experimental import pallas as pl
from jax.experimental.pallas import tpu as pltpu

F32 = jnp.float32
BF16 = jnp.bfloat16
I32 = jnp.int32

S5_GROUP = 16
S5_STATE = 64
HEAD_DIM = 128
IDX_DIM = 64
TOPK_MAX = 256
ROPE_THETA = 10000.0
LN_EPS = 1e-5

LANES = 128
SUBLANES = 8
VMEM_LIMIT_BYTES = 56 * 1024 * 1024

S5_PACK_GROUPS = 16
S5_SEG = 64

INT_MIN = -(2 ** 31)
INT_MAX = 2 ** 31 - 1
MASK_VALUE = -2e30
M_INIT = -1e30


def _pick(n, cands):
    for c in cands:
        if n % c == 0:
            return c
    return n


def _params(sem):
    return pltpu.CompilerParams(dimension_semantics=sem, vmem_limit_bytes=VMEM_LIMIT_BYTES)


def _ada_kernel(c_ref, w_ref, b_ref, o_ref):
    ca = jax.nn.silu(c_ref[...]).astype(BF16)
    o_ref[0] = jnp.dot(ca, w_ref[0].astype(BF16), preferred_element_type=F32) + b_ref[0]


def _ada_mod(c, ada_w, ada_b):
    bsz, d = c.shape
    depth, _, n = ada_w.shape
    rows = SUBLANES * ((bsz + SUBLANES - 1) // SUBLANES)
    cp = jnp.zeros((rows, d), F32).at[:bsz].set(c)
    tn = _pick(n, (1024, 512, 256, 128))
    out = pl.pallas_call(
        _ada_kernel,
        out_shape=jax.ShapeDtypeStruct((depth, rows, n), F32),
        grid=(depth, n // tn),
        in_specs=[
            pl.BlockSpec((rows, d), lambda l, j: (0, 0)),
            pl.BlockSpec((1, d, tn), lambda l, j: (l, 0, j)),
            pl.BlockSpec((1, 1, tn), lambda l, j: (l, 0, j)),
        ],
        out_specs=pl.BlockSpec((1, rows, tn), lambda l, j: (l, 0, j)),
        compiler_params=_params(("arbitrary", "arbitrary")),
        name="ada_mod",
    )(cp, ada_w, ada_b.reshape(depth, 1, n))
    return out[:, :bsz]


def _modulate(x_ref, sc_ref, sh_ref):
    return (x_ref[0] * (1.0 + sc_ref[0]) + sh_ref[0]).astype(BF16)


def _modmm_kernel(x_ref, sc_ref, sh_ref, w_ref, o_ref, h_scr):
    @pl.when(pl.program_id(2) == 0)
    def _():
        h_scr[...] = _modulate(x_ref, sc_ref, sh_ref)

    o_ref[0] = jnp.dot(h_scr[...], w_ref[...], preferred_element_type=F32).astype(o_ref.dtype)


def _mod_matmul(x, sc, sh, w, out_dtype):
    bsz, seq, d = x.shape
    n = w.shape[1]
    tm = _pick(seq, (512, 256, 128))
    tn = _pick(n, (1024, 512, 256, 128))
    return pl.pallas_call(
        _modmm_kernel,
        out_shape=jax.ShapeDtypeStruct((bsz, seq, n), out_dtype),
        grid=(bsz, seq // tm, n // tn),
        in_specs=[
            pl.BlockSpec((1, tm, d), lambda b, i, j: (b, i, 0)),
            pl.BlockSpec((1, 1, d), lambda b, i, j: (b, 0, 0)),
            pl.BlockSpec((1, 1, d), lambda b, i, j: (b, 0, 0)),
            pl.BlockSpec((d, tn), lambda b, i, j: (0, j)),
        ],
        out_specs=pl.BlockSpec((1, tm, tn), lambda b, i, j: (b, i, j)),
        scratch_shapes=[pltpu.VMEM((tm, d), BF16)],
        compiler_params=_params(("parallel", "parallel", "arbitrary")),
        name="mod_matmul",
    )(x, sc, sh, w)


def _s5_kernel(u_ref, bb_ref, cb_ref, lre_ref, lim_ref, ltre_ref, ltim_ref, d_ref, o_ref,
               xs_scr, carry_scr, cin_scr, *, seg, width):
    ns = xs_scr.shape[1] // 2

    @pl.when(pl.program_id(2) == 0)
    def _():
        carry_scr[...] = jnp.zeros_like(carry_scr)

    u = u_ref[0]
    xs_scr[...] = jnp.dot(u.astype(BF16), bb_ref[0], preferred_element_type=F32)

    for part in range(ns // width):
        cr = slice(part * width, (part + 1) * width)
        ci = slice(ns + part * width, ns + (part + 1) * width)
        lr = jnp.broadcast_to(lre_ref[0, :, cr], (SUBLANES, width))
        li = jnp.broadcast_to(lim_ref[0, :, cr], (SUBLANES, width))

        def local_step(i, st, cr=cr, ci=ci, lr=lr, li=li):
            sr, si = st
            r0 = pl.multiple_of(i * SUBLANES, SUBLANES)
            nr = lr * sr - li * si + xs_scr[pl.ds(r0, SUBLANES), cr]
            ni = lr * si + li * sr + xs_scr[pl.ds(r0, SUBLANES), ci]
            xs_scr[pl.ds(r0, SUBLANES), cr] = nr
            xs_scr[pl.ds(r0, SUBLANES), ci] = ni
            return nr, ni

        zero = jnp.zeros((SUBLANES, width), F32)
        er, ei = lax.fori_loop(0, seg, local_step, (zero, zero), unroll=2)

        ltr = ltre_ref[0, :, cr]
        lti = ltim_ref[0, :, cr]
        c_r = carry_scr[:, cr]
        c_i = carry_scr[:, ci]
        for s in range(SUBLANES):
            cin_scr[s:s + 1, cr] = c_r
            cin_scr[s:s + 1, ci] = c_i
            e_r = er[s:s + 1, :]
            e_i = ei[s:s + 1, :]
            c_r, c_i = ltr * c_r - lti * c_i + e_r, ltr * c_i + lti * c_r + e_i
        carry_scr[:, cr] = c_r
        carry_scr[:, ci] = c_i

        def carry_step(i, st, cr=cr, ci=ci, lr=lr, li=li):
            pr, pi_ = st
            r0 = pl.multiple_of(i * SUBLANES, SUBLANES)
            nr = lr * pr - li * pi_
            ni = lr * pi_ + li * pr
            xs_scr[pl.ds(r0, SUBLANES), cr] += nr
            xs_scr[pl.ds(r0, SUBLANES), ci] += ni
            return nr, ni

        lax.fori_loop(0, seg, carry_step, (cin_scr[:, cr], cin_scr[:, ci]), unroll=2)

    y = jnp.dot(xs_scr[...].astype(BF16), cb_ref[0], preferred_element_type=F32)
    y = y + d_ref[0] * u
    o_ref[0] = jax.nn.gelu(y).astype(o_ref.dtype)


def _cmul(ar, ai, br, bi):
    return ar * br - ai * bi, ar * bi + ai * br


def _s5_discretize(a_re, a_im, log_dt, b_re, b_im, c_re, c_im, seg):
    g, n = a_re.shape
    p = b_re.shape[-1]
    pg = S5_PACK_GROUPS
    packs = g // pg
    a_re, a_im = a_re.astype(F32), a_im.astype(F32)
    dt = jnp.exp(log_dt.astype(F32))[:, None]
    mag = jnp.exp(a_re * dt)
    lb_re, lb_im = mag * jnp.cos(a_im * dt), mag * jnp.sin(a_im * dt)
    den = a_re * a_re + a_im * a_im
    nr, ni = lb_re - 1.0, lb_im
    f_re = (nr * a_re + ni * a_im) / den
    f_im = (ni * a_re - nr * a_im) / den
    bb_re, bb_im = _cmul(f_re[..., None], f_im[..., None], b_re.astype(F32), b_im.astype(F32))
    lt_re, lt_im = lb_re, lb_im
    for _ in range(int(math.log2(seg))):
        lt_re, lt_im = _cmul(lt_re, lt_im, lt_re, lt_im)
    eye = jnp.eye(pg, dtype=F32)

    def in_blk(m):
        return jnp.einsum("kgnp,gh->kgphn", m.reshape(packs, pg, n, p), eye).reshape(packs, pg * p, pg * n)

    def out_blk(m):
        return jnp.einsum("kgpn,gh->kgnhp", m.reshape(packs, pg, p, n), eye).reshape(packs, pg * n, pg * p)

    b_blk = jnp.concatenate([in_blk(bb_re), in_blk(bb_im)], axis=-1).astype(BF16)
    c_blk = jnp.concatenate([out_blk(c_re.astype(F32)), out_blk(-c_im.astype(F32))], axis=1).astype(BF16)

    def vec(m):
        return m.reshape(packs, 1, pg * n)

    return b_blk, c_blk, vec(lb_re), vec(lb_im), vec(lt_re), vec(lt_im)


def _s5_scan(u, prep, d_skip, seg):
    bsz, seq, d = u.shape
    b_blk, c_blk, lre, lim, ltre, ltim = prep
    packs, pw, ns2 = b_blk.shape
    ns = ns2 // 2
    rows = SUBLANES * seg
    width = _pick(ns, (512, 256, 128))
    kern = functools.partial(_s5_kernel, seg=seg, width=width)
    vspec = pl.BlockSpec((1, 1, ns), lambda b, k, m: (k, 0, 0))
    return pl.pallas_call(
        kern,
        out_shape=jax.ShapeDtypeStruct((bsz, seq, d), BF16),
        grid=(bsz, packs, seq // rows),
        in_specs=[
            pl.BlockSpec((1, rows, pw), lambda b, k, m: (b, m, k)),
            pl.BlockSpec((1, pw, ns2), lambda b, k, m: (k, 0, 0)),
            pl.BlockSpec((1, ns2, pw), lambda b, k, m: (k, 0, 0)),
            vspec, vspec, vspec, vspec,
            pl.BlockSpec((1, 1, pw), lambda b, k, m: (k, 0, 0)),
        ],
        out_specs=pl.BlockSpec((1, rows, pw), lambda b, k, m: (b, m, k)),
        scratch_shapes=[
            pltpu.VMEM((rows, ns2), F32),
            pltpu.VMEM((1, ns2), F32),
            pltpu.VMEM((SUBLANES, ns2), F32),
        ],
        compiler_params=_params(("parallel", "parallel", "arbitrary")),
        name="s5_scan",
    )(u, b_blk, c_blk, lre, lim, ltre, ltim, d_skip.reshape(packs, 1, pw).astype(F32))


def _layer_norm_rows(tiles, lng_ref, lnb_ref, o_ref, d):
    tn = tiles[0].shape[1]
    tot = tiles[0].sum(axis=1, keepdims=True)
    for t in tiles[1:]:
        tot = tot + t.sum(axis=1, keepdims=True)
    mu = tot * (1.0 / d)
    sq = jnp.square(tiles[0] - mu).sum(axis=1, keepdims=True)
    for t in tiles[1:]:
        sq = sq + jnp.square(t - mu).sum(axis=1, keepdims=True)
    inv = lax.rsqrt(sq * (1.0 / d) + LN_EPS)
    for k, t in enumerate(tiles):
        cs = slice(k * tn, (k + 1) * tn)
        o_ref[0, :, cs] = ((t - mu) * inv * lng_ref[:, cs] + lnb_ref[:, cs]).astype(o_ref.dtype)


def _mmln_kernel(*refs, glu, nj, alpha, d):
    if glu:
        a_ref, w1_ref, w2_ref, b1_ref, b2_ref, x_ref, g_ref, lng_ref, lnb_ref, o_ref, r_scr = refs
    else:
        a_ref, w1_ref, x_ref, g_ref, lng_ref, lnb_ref, o_ref, r_scr = refs
    j = pl.program_id(2)
    a = a_ref[0]
    y = jnp.dot(a, w1_ref[...], preferred_element_type=F32)
    if glu:
        y = y + b1_ref[...]
        gate = jnp.dot(a, w2_ref[...], preferred_element_type=F32) + b2_ref[...]
        y = y * jax.nn.sigmoid(gate)
    r_scr[j] = alpha * x_ref[0] + (1.0 + g_ref[0]) * y

    @pl.when(j == nj - 1)
    def _():
        _layer_norm_rows([r_scr[t] for t in range(nj)], lng_ref, lnb_ref, o_ref, d)


def _matmul_res_ln(a, w, bias, xres, gate, ln_g, ln_b, alpha, glu):
    bsz, seq, k = a.shape
    d = xres.shape[-1]
    tm = _pick(seq, (512, 256, 128))
    tn = _pick(d, (512, 256, 128))
    nj = d // tn
    a_spec = pl.BlockSpec((1, tm, k), lambda b, i, j: (b, i, 0))
    w1_spec = pl.BlockSpec((k, tn), lambda b, i, j: (0, j))
    tail_specs = [
        pl.BlockSpec((1, tm, tn), lambda b, i, j: (b, i, j)),
        pl.BlockSpec((1, 1, tn), lambda b, i, j: (b, 0, j)),
        pl.BlockSpec((1, d), lambda b, i, j: (0, 0)),
        pl.BlockSpec((1, d), lambda b, i, j: (0, 0)),
    ]
    tail = (xres, gate, ln_g.reshape(1, d).astype(F32), ln_b.reshape(1, d).astype(F32))
    if glu:
        in_specs = [a_spec, w1_spec,
                    pl.BlockSpec((k, tn), lambda b, i, j: (0, j + nj)),
                    pl.BlockSpec((1, tn), lambda b, i, j: (0, j)),
                    pl.BlockSpec((1, tn), lambda b, i, j: (0, j + nj))] + tail_specs
        b2d = bias.reshape(1, 2 * d).astype(F32)
        args = (a, w, w, b2d, b2d) + tail
    else:
        in_specs = [a_spec, w1_spec] + tail_specs
        args = (a, w) + tail
    kern = functools.partial(_mmln_kernel, glu=glu, nj=nj, alpha=alpha, d=d)
    return pl.pallas_call(
        kern,
        out_shape=jax.ShapeDtypeStruct((bsz, seq, d), F32),
        grid=(bsz, seq // tm, nj),
        in_specs=in_specs,
        out_specs=pl.BlockSpec((1, tm, d), lambda b, i, j: (b, i, 0)),
        scratch_shapes=[pltpu.VMEM((nj, tm, tn), F32)],
        compiler_params=_params(("parallel", "parallel", "arbitrary")),
        name="matmul_res_ln",
    )(*args)


def _ffn_kernel(x_ref, sc_ref, sh_ref, g_ref, wg_ref, wu_ref, wo_ref, lng_ref, lnb_ref, o_ref,
                h_scr, acc_scr, *, nf, alpha, d, tn):
    f = pl.program_id(2)

    @pl.when(f == 0)
    def _():
        h_scr[...] = _modulate(x_ref, sc_ref, sh_ref)
        acc_scr[...] = jnp.zeros_like(acc_scr)

    h = h_scr[...]
    a_g = jnp.dot(h, wg_ref[...], preferred_element_type=F32)
    a_u = jnp.dot(h, wu_ref[...], preferred_element_type=F32)
    act = (jax.nn.silu(a_g) * a_u).astype(BF16)
    acc_scr[...] += jnp.dot(act, wo_ref[...], preferred_element_type=F32)

    @pl.when(f == nf - 1)
    def _():
        tiles = []
        for k in range(d // tn):
            cs = slice(k * tn, (k + 1) * tn)
            tiles.append(alpha * x_ref[0, :, cs] + (1.0 + g_ref[0, :, cs]) * acc_scr[:, cs])
        _layer_norm_rows(tiles, lng_ref, lnb_ref, o_ref, d)


def _ffn_res_ln(x, sc, sh, gate, w_in, w_out, ln_g, ln_b, alpha):
    bsz, seq, d = x.shape
    dff = w_out.shape[0]
    tm = _pick(seq, (512, 256, 128))
    tf = _pick(dff, (512, 256, 128))
    nf = dff // tf
    tn = _pick(d, (512, 256, 128))
    vec = pl.BlockSpec((1, 1, d), lambda b, i, f: (b, 0, 0))
    kern = functools.partial(_ffn_kernel, nf=nf, alpha=alpha, d=d, tn=tn)
    return pl.pallas_call(
        kern,
        out_shape=jax.ShapeDtypeStruct((bsz, seq, d), F32),
        grid=(bsz, seq // tm, nf),
        in_specs=[
            pl.BlockSpec((1, tm, d), lambda b, i, f: (b, i, 0)),
            vec, vec, vec,
            pl.BlockSpec((d, tf), lambda b, i, f: (0, f)),
            pl.BlockSpec((d, tf), lambda b, i, f: (0, f + nf)),
            pl.BlockSpec((tf, d), lambda b, i, f: (f, 0)),
            pl.BlockSpec((1, d), lambda b, i, f: (0, 0)),
            pl.BlockSpec((1, d), lambda b, i, f: (0, 0)),
        ],
        out_specs=pl.BlockSpec((1, tm, d), lambda b, i, f: (b, i, 0)),
        scratch_shapes=[pltpu.VMEM((tm, d), BF16), pltpu.VMEM((tm, d), F32)],
        compiler_params=_params(("parallel", "parallel", "arbitrary")),
        name="ffn_res_ln",
    )(x, sc, sh, gate, w_in, w_in, w_out, ln_g.reshape(1, d).astype(F32), ln_b.reshape(1, d).astype(F32))


def _qkv_kernel(x_ref, sc_ref, sh_ref, w_ref, cos_ref, sin_ref, o_ref, h_scr, *, tiles_per_tensor, q_scale):
    j = pl.program_id(2)

    @pl.when(j == 0)
    def _():
        h_scr[...] = _modulate(x_ref, sc_ref, sh_ref)

    acc = jnp.dot(h_scr[...], w_ref[...], preferred_element_type=F32)
    heads = acc.shape[1] // HEAD_DIM
    tensor = j // tiles_per_tensor

    def write(rope, scale):
        for hh in range(heads):
            xc = acc[:, hh * HEAD_DIM:(hh + 1) * HEAD_DIM]
            if rope:
                xc = xc * cos_ref[0] + pltpu.roll(xc, HEAD_DIM // 2, 1) * sin_ref[0]
            if scale != 1.0:
                xc = xc * scale
            o_ref[0, hh] = xc.astype(o_ref.dtype)

    pl.when(tensor == 0)(lambda: write(True, q_scale))
    pl.when(tensor == 1)(lambda: write(True, 1.0))
    pl.when(tensor == 2)(lambda: write(False, 1.0))


def _qkv_proj(x, sc, sh, w_qkv, cos_t, sin_t):
    bsz, seq, d = x.shape
    tm = _pick(seq, (512, 256, 128))
    tn = _pick(d, (1024, 512, 256, 128))
    hpt = tn // HEAD_DIM
    kern = functools.partial(_qkv_kernel, tiles_per_tensor=d // tn, q_scale=HEAD_DIM ** -0.5)
    return pl.pallas_call(
        kern,
        out_shape=jax.ShapeDtypeStruct((bsz, 3 * d // HEAD_DIM, seq, HEAD_DIM), BF16),
        grid=(bsz, seq // tm, 3 * d // tn),
        in_specs=[
            pl.BlockSpec((1, tm, d), lambda b, i, j: (b, i, 0)),
            pl.BlockSpec((1, 1, d), lambda b, i, j: (b, 0, 0)),
            pl.BlockSpec((1, 1, d), lambda b, i, j: (b, 0, 0)),
            pl.BlockSpec((d, tn), lambda b, i, j: (0, j)),
            pl.BlockSpec((1, tm, HEAD_DIM), lambda b, i, j: (b, i, 0)),
            pl.BlockSpec((1, tm, HEAD_DIM), lambda b, i, j: (b, i, 0)),
        ],
        out_specs=pl.BlockSpec((1, hpt, tm, HEAD_DIM), lambda b, i, j: (b, j, i, 0)),
        scratch_shapes=[pltpu.VMEM((tm, d), BF16)],
        compiler_params=_params(("parallel", "parallel", "arbitrary")),
        name="dsa_qkv_proj",
    )(x, sc, sh, w_qkv, cos_t, sin_t)


def _idx_kernel(x_ref, sc_ref, sh_ref, w_ref, cos_ref, sin_ref, qi_ref, kia_ref, kib_ref, wi_ref,
                *, n_pair, idx_heads, w_scale):
    h = _modulate(x_ref, sc_ref, sh_ref)
    acc = jnp.dot(h, w_ref[...], preferred_element_type=F32)
    tm = acc.shape[0]
    cos = cos_ref[0]
    sin = sin_ref[0]
    lane = lax.broadcasted_iota(I32, (tm, LANES), 1)
    first_half = (lane % IDX_DIM) < IDX_DIM // 2

    def rope(xc):
        partner = jnp.where(first_half, pltpu.roll(xc, LANES - IDX_DIM // 2, 1), pltpu.roll(xc, IDX_DIM // 2, 1))
        return xc * cos + partner * sin

    for p in range(n_pair):
        cs = slice(p * LANES, (p + 1) * LANES)
        qi_ref[0, :, cs] = rope(acc[:, cs]).astype(qi_ref.dtype)
    last = acc[:, n_pair * LANES:]
    ka = jnp.where(lane < IDX_DIM, rope(last), 0.0)
    kia_ref[0] = ka.astype(kia_ref.dtype)
    kib_ref[0] = pltpu.roll(ka, IDX_DIM, 1).astype(kib_ref.dtype)
    wi_ref[0] = jnp.where(lane < idx_heads, pltpu.roll(last, LANES - IDX_DIM, 1), 0.0) * w_scale


def _idx_proj(x, sc, sh, w_idx, cos_t, sin_t, idx_heads):
    bsz, seq, d = x.shape
    n = w_idx.shape[1]
    n_pair = idx_heads // 2
    tm = _pick(seq, (512, 256, 128))
    kern = functools.partial(_idx_kernel, n_pair=n_pair, idx_heads=idx_heads,
                             w_scale=(idx_heads ** -0.5) * (IDX_DIM ** -0.5))
    row = lambda b, i: (b, i, 0)
    return pl.pallas_call(
        kern,
        out_shape=(
            jax.ShapeDtypeStruct((bsz, seq, n_pair * LANES), BF16),
            jax.ShapeDtypeStruct((bsz, seq, LANES), BF16),
            jax.ShapeDtypeStruct((bsz, seq, LANES), BF16),
            jax.ShapeDtypeStruct((bsz, seq, LANES), F32),
        ),
        grid=(bsz, seq // tm),
        in_specs=[
            pl.BlockSpec((1, tm, d), row),
            pl.BlockSpec((1, 1, d), lambda b, i: (b, 0, 0)),
            pl.BlockSpec((1, 1, d), lambda b, i: (b, 0, 0)),
            pl.BlockSpec((d, n), lambda b, i: (0, 0)),
            pl.BlockSpec((1, tm, LANES), row),
            pl.BlockSpec((1, tm, LANES), row),
        ],
        out_specs=(
            pl.BlockSpec((1, tm, n_pair * LANES), row),
            pl.BlockSpec((1, tm, LANES), row),
            pl.BlockSpec((1, tm, LANES), row),
            pl.BlockSpec((1, tm, LANES), row),
        ),
        compiler_params=_params(("parallel", "parallel")),
        name="dsa_idx_proj",
    )(x, sc, sh, w_idx, cos_t, sin_t)


def _dsa_kernel(q_ref, k_ref, v_ref, qi_ref, kia_ref, kib_ref, wi_ref, o_ref,
                keys_scr, thr_scr, tie_scr, m_scr, l_scr, acc_scr,
                *, n_pair, k_top, idx_bits):
    heads, qb_rows, _ = acc_scr.shape
    kb_rows = keys_scr.shape[2]
    qb = pl.program_id(1)
    kb = pl.program_id(2)
    q0 = qb * qb_rows
    last_kb = (q0 + qb_rows - 1) // kb_rows
    n_chunk = last_kb + 1
    row = lax.broadcasted_iota(I32, (qb_rows, kb_rows), 0) + q0
    col = lax.broadcasted_iota(I32, (qb_rows, kb_rows), 1)
    nt = (((1,), (1,)), ((), ()))

    @pl.when(kb == 0)
    def _select():
        wi = wi_ref[0]

        def score_chunk(c, carry):
            k0 = pl.multiple_of(c * kb_rows, kb_rows)
            ka = kia_ref[0, pl.ds(k0, kb_rows), :]
            kbm = kib_ref[0, pl.ds(k0, kb_rows), :]
            sc = jnp.zeros((qb_rows, kb_rows), F32)
            for p in range(n_pair):
                qp = qi_ref[0, :, p * LANES:(p + 1) * LANES]
                sa = lax.dot_general(qp, ka, nt, preferred_element_type=F32)
                sb = lax.dot_general(qp, kbm, nt, preferred_element_type=F32)
                sc = sc + wi[:, 2 * p:2 * p + 1] * jnp.maximum(sa, 0.0)
                sc = sc + wi[:, 2 * p + 1:2 * p + 2] * jnp.maximum(sb, 0.0)
            bits = lax.bitcast_convert_type(sc, I32)
            key = bits ^ ((bits >> 31) & INT_MAX)
            keys_scr[c] = jnp.where(col + k0 <= row, key, INT_MIN)
            return carry

        lax.fori_loop(0, n_chunk, score_chunk, 0)

        def count(pred):
            def body(c, acc):
                hit = pred(keys_scr[c], col + c * kb_rows)
                return acc + jnp.sum(jnp.where(hit, 1.0, 0.0), axis=1, keepdims=True)
            return lax.fori_loop(0, n_chunk, body, jnp.zeros((qb_rows, 1), F32))

        def thr_bit(i, t):
            cand = t + (jnp.int32(1) << (31 - i))
            cnt = count(lambda kk, idx: kk >= cand)
            return jnp.where(cnt >= k_top, cand, t)

        thr = lax.fori_loop(0, 32, thr_bit, jnp.full((qb_rows, 1), INT_MIN, I32))
        thr_scr[...] = thr
        tie_scr[...] = jnp.full((qb_rows, 1), INT_MAX, I32)
        n_gt = count(lambda kk, idx: kk > thr)
        n_ge = count(lambda kk, idx: kk >= thr)

        @pl.when(jnp.max(n_ge) > k_top)
        def _ties():
            need = k_top - n_gt

            def tie_bit(i, jt):
                cand = jt + (jnp.int32(1) << (idx_bits - 1 - i))
                cnt = count(lambda kk, idx: (kk == thr) & (idx < cand))
                return jnp.where(cnt < need, cand, jt)

            tie_scr[...] = lax.fori_loop(0, idx_bits, tie_bit, jnp.zeros((qb_rows, 1), I32))

        tie = tie_scr[...]

        def bias_chunk(c, carry):
            kk = keys_scr[c]
            idx = col + c * kb_rows
            sel = (kk > thr) | ((kk == thr) & (idx <= tie))
            bias = jnp.where(sel & (idx <= row), 0.0, MASK_VALUE)
            keys_scr[c] = lax.bitcast_convert_type(bias, I32)
            return carry

        lax.fori_loop(0, n_chunk, bias_chunk, 0)
        m_scr[...] = jnp.full(m_scr.shape, M_INIT, F32)
        l_scr[...] = jnp.zeros_like(l_scr)
        acc_scr[...] = jnp.zeros_like(acc_scr)

    @pl.when(kb <= last_kb)
    def _attend():
        bias = lax.bitcast_convert_type(keys_scr[kb], F32)

        def head(h, carry):
            s = lax.dot_general(q_ref[0, h], k_ref[0, h], nt, preferred_element_type=F32) + bias
            m_old = m_scr[h]
            m_new = jnp.maximum(m_old, jnp.max(s, axis=1, keepdims=True))
            alpha = jnp.exp(m_old - m_new)
            p = jnp.exp(s - m_new)
            l_scr[h] = alpha * l_scr[h] + jnp.sum(p, axis=1, keepdims=True)
            acc_scr[h] = alpha * acc_scr[h] + jnp.dot(p.astype(BF16), v_ref[0, h], preferred_element_type=F32)
            m_scr[h] = m_new
            return carry

        lax.fori_loop(0, heads, head, 0)

    @pl.when(kb == pl.num_programs(2) - 1)
    def _finish():
        for h in range(heads):
            o_ref[0, :, h * HEAD_DIM:(h + 1) * HEAD_DIM] = (acc_scr[h] / l_scr[h]).astype(o_ref.dtype)


def _dsa_attention(qkv, qi, kia, kib, wi, k_top):
    bsz, heads3, seq, _ = qkv.shape
    heads = heads3 // 3
    d = heads * HEAD_DIM
    n_pair = qi.shape[-1] // LANES
    qb_rows = _pick(seq, (256, 128))
    kb_rows = _pick(seq, (512, 256, 128))
    n_kb = seq // kb_rows

    def kv_block(qb, kb):
        return jnp.minimum(kb, (qb * qb_rows + qb_rows - 1) // kb_rows)

    kern = functools.partial(_dsa_kernel, n_pair=n_pair, k_top=k_top, idx_bits=seq.bit_length())
    return pl.pallas_call(
        kern,
        out_shape=jax.ShapeDtypeStruct((bsz, seq, d), BF16),
        grid=(bsz, seq // qb_rows, n_kb),
        in_specs=[
            pl.BlockSpec((1, heads, qb_rows, HEAD_DIM), lambda b, i, j: (b, 0, i, 0)),
            pl.BlockSpec((1, heads, kb_rows, HEAD_DIM), lambda b, i, j: (b, 1, kv_block(i, j), 0)),
            pl.BlockSpec((1, heads, kb_rows, HEAD_DIM), lambda b, i, j: (b, 2, kv_block(i, j), 0)),
            pl.BlockSpec((1, qb_rows, n_pair * LANES), lambda b, i, j: (b, i, 0)),
            pl.BlockSpec((1, seq, LANES), lambda b, i, j: (b, 0, 0)),
            pl.BlockSpec((1, seq, LANES), lambda b, i, j: (b, 0, 0)),
            pl.BlockSpec((1, qb_rows, LANES), lambda b, i, j: (b, i, 0)),
        ],
        out_specs=pl.BlockSpec((1, qb_rows, d), lambda b, i, j: (b, i, 0)),
        scratch_shapes=[
            pltpu.VMEM((n_kb, qb_rows, kb_rows), I32),
            pltpu.VMEM((qb_rows, 1), I32),
            pltpu.VMEM((qb_rows, 1), I32),
            pltpu.VMEM((heads, qb_rows, 1), F32),
            pltpu.VMEM((heads, qb_rows, 1), F32),
            pltpu.VMEM((heads, qb_rows, HEAD_DIM), F32),
        ],
        compiler_params=_params(("parallel", "parallel", "arbitrary")),
        name="dsa_select_attend",
    )(qkv, qkv, qkv, qi, kia, kib, wi)


def _rope_tables(positions, dim, repeats):
    inv = 1.0 / (ROPE_THETA ** (jnp.arange(0, dim, 2, dtype=F32) / dim))
    ang = positions.astype(F32)[..., None] * inv
    cos, sin = jnp.cos(ang), jnp.sin(ang)
    cos_t = jnp.concatenate([cos, cos] * repeats, axis=-1)
    sin_t = jnp.concatenate([-sin, sin] * repeats, axis=-1)
    return cos_t, sin_t


def _segment_major(x, seg):
    bsz, seq, d = x.shape
    y = x.reshape(bsz, seq // (SUBLANES * seg), SUBLANES, seg, d)
    return jnp.swapaxes(y, 2, 3).reshape(bsz, seq, d)


def _token_major(x, seg):
    bsz, seq, d = x.shape
    y = x.reshape(bsz, seq // (SUBLANES * seg), seg, SUBLANES, d)
    return jnp.swapaxes(y, 2, 3).reshape(bsz, seq, d)


def kernel(x, c, positions, ada_w, ada_b, ln_g, ln_b, s5_in_w, s5_a_re, s5_a_im, s5_log_dt, s5_b_re, s5_b_im, s5_c_re, s5_c_im, s5_d, s5_glu_w, s5_glu_b, dsa_in_w, dsa_out_w, ffn_w_in, ffn_w_out):
    bsz, seq, d = x.shape
    depth = ada_w.shape[0]
    alpha = (2.0 * depth) ** 0.25
    idx_heads = (dsa_in_w.shape[-1] - 3 * d - IDX_DIM) // (IDX_DIM + 1)
    k_top = min(TOPK_MAX, seq // 4)
    seg = min(S5_SEG, seq // SUBLANES)

    cos_h, sin_h = _rope_tables(positions, HEAD_DIM, 1)
    cos_i, sin_i = _rope_tables(positions, IDX_DIM, 2)
    mod = _ada_mod(c, ada_w, ada_b)

    for i in range(depth):
        sh1, sc1, g1, sh2, sc2, g2 = [m[:, None, :] for m in jnp.split(mod[i], 6, axis=-1)]
        j = i // 2
        if i % 2 == 0:
            xp = _segment_major(x, seg)
            prep = _s5_discretize(s5_a_re[j], s5_a_im[j], s5_log_dt[j], s5_b_re[j], s5_b_im[j],
                                  s5_c_re[j], s5_c_im[j], seg)
            u = _mod_matmul(xp, sc1, sh1, s5_in_w[j].astype(BF16), F32)
            gl = _s5_scan(u, prep, s5_d[j], seg)
            x1 = _matmul_res_ln(gl, s5_glu_w[j].astype(BF16), s5_glu_b[j], xp, g1,
                                ln_g[i, 0], ln_b[i, 0], alpha, glu=True)
        else:
            xp = x
            w = dsa_in_w[j]
            n_qi = idx_heads * IDX_DIM
            w_qkv = w[:, :3 * d].astype(BF16)
            pad = jnp.zeros((d, LANES - IDX_DIM - idx_heads), w.dtype)
            w_idx = jnp.concatenate(
                [w[:, 3 * d:3 * d + n_qi], w[:, 3 * d + n_qi + idx_heads:], w[:, 3 * d + n_qi:3 * d + n_qi + idx_heads], pad],
                axis=1).astype(BF16)
            qkv = _qkv_proj(xp, sc1, sh1, w_qkv, cos_h, sin_h)
            qi, kia, kib, wi = _idx_proj(xp, sc1, sh1, w_idx, cos_i, sin_i, idx_heads)
            att = _dsa_attention(qkv, qi, kia, kib, wi, k_top)
            x1 = _matmul_res_ln(att, dsa_out_w[j].astype(BF16), None, xp, g1,
                                ln_g[i, 0], ln_b[i, 0], alpha, glu=False)
        x2 = _ffn_res_ln(x1, sc2, sh2, g2, ffn_w_in[i].astype(BF16), ffn_w_out[i].astype(BF16),
                         ln_g[i, 1], ln_b[i, 1], alpha)
        x = _token_major(x2, seg) if i % 2 == 0 else x2
    return x
```

```python
import functools
import math

import jax
import jax.numpy as jnp
from jax import lax
from jax.experimental import pallas as pl
from jax.experimental.pallas import tpu as pltpu

F32 = jnp.float32
BF16 = jnp.bfloat16
I32 = jnp.int32

S5_GROUP = 16
S5_STATE = 64
HEAD_DIM = 128
IDX_DIM = 64
TOPK_MAX = 256
ROPE_THETA = 10000.0
LN_EPS = 1e-5

LANES = 128
SUBLANES = 8
VMEM_LIMIT_BYTES = 56 * 1024 * 1024

S5_PACK_GROUPS = 16
S5_SEG = 64

INT_MIN = -(2 ** 31)
INT_MAX = 2 ** 31 - 1
SEARCH_ROWS = 128

MASK_VALUE = -2e30
M_INIT = -1e30


def _pick(n, cands):
    for c in cands:
        if n % c == 0:
            return c
    return n


def _params(sem):
    return pltpu.CompilerParams(dimension_semantics=sem, vmem_limit_bytes=VMEM_LIMIT_BYTES)


def _ada_kernel(c_ref, w_ref, b_ref, o_ref):
    ca = jax.nn.silu(c_ref[...]).astype(BF16)
    o_ref[0] = jnp.dot(ca, w_ref[0].astype(BF16), preferred_element_type=F32) + b_ref[0]


def _ada_mod(c, ada_w, ada_b):
    bsz, d = c.shape
    depth, _, n = ada_w.shape
    rows = SUBLANES * ((bsz + SUBLANES - 1) // SUBLANES)
    cp = jnp.zeros((rows, d), F32).at[:bsz].set(c)
    tn = _pick(n, (1024, 512, 256, 128))
    out = pl.pallas_call(
        _ada_kernel,
        out_shape=jax.ShapeDtypeStruct((depth, rows, n), F32),
        grid=(depth, n // tn),
        in_specs=[
            pl.BlockSpec((rows, d), lambda l, j: (0, 0)),
            pl.BlockSpec((1, d, tn), lambda l, j: (l, 0, j)),
            pl.BlockSpec((1, 1, tn), lambda l, j: (l, 0, j)),
        ],
        out_specs=pl.BlockSpec((1, rows, tn), lambda l, j: (l, 0, j)),
        compiler_params=_params(("arbitrary", "arbitrary")),
        name="ada_mod",
    )(cp, ada_w, ada_b.reshape(depth, 1, n))
    return out[:, :bsz]


def _modulate(x_ref, sc_ref, sh_ref):
    return (x_ref[0] * (1.0 + sc_ref[0]) + sh_ref[0]).astype(BF16)


def _modmm_kernel(x_ref, sc_ref, sh_ref, w_ref, o_ref, h_scr):
    @pl.when(pl.program_id(2) == 0)
    def _():
        h_scr[...] = _modulate(x_ref, sc_ref, sh_ref)

    o_ref[0] = jnp.dot(h_scr[...], w_ref[...], preferred_element_type=F32).astype(o_ref.dtype)


def _mod_matmul(x, sc, sh, w, out_dtype):
    bsz, seq, d = x.shape
    n = w.shape[1]
    tm = _pick(seq, (512, 256, 128))
    tn = _pick(n, (1024, 512, 256, 128))
    return pl.pallas_call(
        _modmm_kernel,
        out_shape=jax.ShapeDtypeStruct((bsz, seq, n), out_dtype),
        grid=(bsz, seq // tm, n // tn),
        in_specs=[
            pl.BlockSpec((1, tm, d), lambda b, i, j: (b, i, 0)),
            pl.BlockSpec((1, 1, d), lambda b, i, j: (b, 0, 0)),
            pl.BlockSpec((1, 1, d), lambda b, i, j: (b, 0, 0)),
            pl.BlockSpec((d, tn), lambda b, i, j: (0, j)),
        ],
        out_specs=pl.BlockSpec((1, tm, tn), lambda b, i, j: (b, i, j)),
        scratch_shapes=[pltpu.VMEM((tm, d), BF16)],
        compiler_params=_params(("parallel", "parallel", "arbitrary")),
        name="mod_matmul",
    )(x, sc, sh, w)


def _s5_kernel(u_ref, bb_ref, cb_ref, lre_ref, lim_ref, ltre_ref, ltim_ref, d_ref, o_ref,
               xs_scr, carry_scr, cin_scr, *, seg, width):
    ns = xs_scr.shape[1] // 2

    @pl.when(pl.program_id(2) == 0)
    def _():
        carry_scr[...] = jnp.zeros_like(carry_scr)

    u = u_ref[0]
    xs_scr[...] = jnp.dot(u.astype(BF16), bb_ref[0], preferred_element_type=F32)

    for part in range(ns // width):
        cr = slice(part * width, (part + 1) * width)
        ci = slice(ns + part * width, ns + (part + 1) * width)
        lr = jnp.broadcast_to(lre_ref[0, :, cr], (SUBLANES, width))
        li = jnp.broadcast_to(lim_ref[0, :, cr], (SUBLANES, width))

        def local_step(i, st, cr=cr, ci=ci, lr=lr, li=li):
            sr, si = st
            r0 = pl.multiple_of(i * SUBLANES, SUBLANES)
            nr = lr * sr - li * si + xs_scr[pl.ds(r0, SUBLANES), cr]
            ni = lr * si + li * sr + xs_scr[pl.ds(r0, SUBLANES), ci]
            xs_scr[pl.ds(r0, SUBLANES), cr] = nr
            xs_scr[pl.ds(r0, SUBLANES), ci] = ni
            return nr, ni

        zero = jnp.zeros((SUBLANES, width), F32)
        er, ei = lax.fori_loop(0, seg, local_step, (zero, zero), unroll=2)

        ltr = ltre_ref[0, :, cr]
        lti = ltim_ref[0, :, cr]
        c_r = carry_scr[:, cr]
        c_i = carry_scr[:, ci]
        for s in range(SUBLANES):
            cin_scr[s:s + 1, cr] = c_r
            cin_scr[s:s + 1, ci] = c_i
            e_r = er[s:s + 1, :]
            e_i = ei[s:s + 1, :]
            c_r, c_i = ltr * c_r - lti * c_i + e_r, ltr * c_i + lti * c_r + e_i
        carry_scr[:, cr] = c_r
        carry_scr[:, ci] = c_i

        def carry_step(i, st, cr=cr, ci=ci, lr=lr, li=li):
            pr, pi_ = st
            r0 = pl.multiple_of(i * SUBLANES, SUBLANES)
            nr = lr * pr - li * pi_
            ni = lr * pi_ + li * pr
            xs_scr[pl.ds(r0, SUBLANES), cr] += nr
            xs_scr[pl.ds(r0, SUBLANES), ci] += ni
            return nr, ni

        lax.fori_loop(0, seg, carry_step, (cin_scr[:, cr], cin_scr[:, ci]), unroll=2)

    y = jnp.dot(xs_scr[...].astype(BF16), cb_ref[0], preferred_element_type=F32)
    y = y + d_ref[0] * u
    o_ref[0] = jax.nn.gelu(y).astype(o_ref.dtype)


def _cmul(ar, ai, br, bi):
    return ar * br - ai * bi, ar * bi + ai * br


def _s5_discretize(a_re, a_im, log_dt, b_re, b_im, c_re, c_im, seg):
    g, n = a_re.shape
    p = b_re.shape[-1]
    pg = S5_PACK_GROUPS
    packs = g // pg
    a_re, a_im = a_re.astype(F32), a_im.astype(F32)
    dt = jnp.exp(log_dt.astype(F32))[:, None]
    mag = jnp.exp(a_re * dt)
    lb_re, lb_im = mag * jnp.cos(a_im * dt), mag * jnp.sin(a_im * dt)
    den = a_re * a_re + a_im * a_im
    nr, ni = lb_re - 1.0, lb_im
    f_re = (nr * a_re + ni * a_im) / den
    f_im = (ni * a_re - nr * a_im) / den
    bb_re, bb_im = _cmul(f_re[..., None], f_im[..., None], b_re.astype(F32), b_im.astype(F32))
    lt_re, lt_im = lb_re, lb_im
    for _ in range(int(math.log2(seg))):
        lt_re, lt_im = _cmul(lt_re, lt_im, lt_re, lt_im)
    eye = jnp.eye(pg, dtype=F32)

    def in_blk(m):
        return jnp.einsum("kgnp,gh->kgphn", m.reshape(packs, pg, n, p), eye).reshape(packs, pg * p, pg * n)

    def out_blk(m):
        return jnp.einsum("kgpn,gh->kgnhp", m.reshape(packs, pg, p, n), eye).reshape(packs, pg * n, pg * p)

    b_blk = jnp.concatenate([in_blk(bb_re), in_blk(bb_im)], axis=-1).astype(BF16)
    c_blk = jnp.concatenate([out_blk(c_re.astype(F32)), out_blk(-c_im.astype(F32))], axis=1).astype(BF16)

    def vec(m):
        return m.reshape(packs, 1, pg * n)

    return b_blk, c_blk, vec(lb_re), vec(lb_im), vec(lt_re), vec(lt_im)


def _s5_scan(u, prep, d_skip, seg):
    bsz, seq, d = u.shape
    b_blk, c_blk, lre, lim, ltre, ltim = prep
    packs, pw, ns2 = b_blk.shape
    ns = ns2 // 2
    rows = SUBLANES * seg
    width = _pick(ns, (512, 256, 128))
    kern = functools.partial(_s5_kernel, seg=seg, width=width)
    vspec = pl.BlockSpec((1, 1, ns), lambda b, k, m: (k, 0, 0))
    return pl.pallas_call(
        kern,
        out_shape=jax.ShapeDtypeStruct((bsz, seq, d), BF16),
        grid=(bsz, packs, seq // rows),
        in_specs=[
            pl.BlockSpec((1, rows, pw), lambda b, k, m: (b, m, k)),
            pl.BlockSpec((1, pw, ns2), lambda b, k, m: (k, 0, 0)),
            pl.BlockSpec((1, ns2, pw), lambda b, k, m: (k, 0, 0)),
            vspec, vspec, vspec, vspec,
            pl.BlockSpec((1, 1, pw), lambda b, k, m: (k, 0, 0)),
        ],
        out_specs=pl.BlockSpec((1, rows, pw), lambda b, k, m: (b, m, k)),
        scratch_shapes=[
            pltpu.VMEM((rows, ns2), F32),
            pltpu.VMEM((1, ns2), F32),
            pltpu.VMEM((SUBLANES, ns2), F32),
        ],
        compiler_params=_params(("parallel", "parallel", "arbitrary")),
        name="s5_scan",
    )(u, b_blk, c_blk, lre, lim, ltre, ltim, d_skip.reshape(packs, 1, pw).astype(F32))


def _layer_norm_rows(tiles, lng_ref, lnb_ref, o_ref, d):
    tn = tiles[0].shape[1]
    tot = tiles[0].sum(axis=1, keepdims=True)
    for t in tiles[1:]:
        tot = tot + t.sum(axis=1, keepdims=True)
    mu = tot * (1.0 / d)
    sq = jnp.square(tiles[0] - mu).sum(axis=1, keepdims=True)
    for t in tiles[1:]:
        sq = sq + jnp.square(t - mu).sum(axis=1, keepdims=True)
    inv = lax.rsqrt(sq * (1.0 / d) + LN_EPS)
    for k, t in enumerate(tiles):
        cs = slice(k * tn, (k + 1) * tn)
        o_ref[0, :, cs] = ((t - mu) * inv * lng_ref[:, cs] + lnb_ref[:, cs]).astype(o_ref.dtype)


def _mmln_kernel(*refs, glu, nj, alpha, d):
    if glu:
        a_ref, w1_ref, w2_ref, b1_ref, b2_ref, x_ref, g_ref, lng_ref, lnb_ref, o_ref, r_scr = refs
    else:
        a_ref, w1_ref, x_ref, g_ref, lng_ref, lnb_ref, o_ref, r_scr = refs
    j = pl.program_id(2)
    a = a_ref[0]
    y = jnp.dot(a, w1_ref[...], preferred_element_type=F32)
    if glu:
        y = y + b1_ref[...]
        gate = jnp.dot(a, w2_ref[...], preferred_element_type=F32) + b2_ref[...]
        y = y * jax.nn.sigmoid(gate)
    r_scr[j] = alpha * x_ref[0] + (1.0 + g_ref[0]) * y

    @pl.when(j == nj - 1)
    def _():
        _layer_norm_rows([r_scr[t] for t in range(nj)], lng_ref, lnb_ref, o_ref, d)


def _matmul_res_ln(a, w, bias, xres, gate, ln_g, ln_b, alpha, glu):
    bsz, seq, k = a.shape
    d = xres.shape[-1]
    tm = _pick(seq, (512, 256, 128))
    tn = _pick(d, (512, 256, 128))
    nj = d // tn
    a_spec = pl.BlockSpec((1, tm, k), lambda b, i, j: (b, i, 0))
    w1_spec = pl.BlockSpec((k, tn), lambda b, i, j: (0, j))
    tail_specs = [
        pl.BlockSpec((1, tm, tn), lambda b, i, j: (b, i, j)),
        pl.BlockSpec((1, 1, tn), lambda b, i, j: (b, 0, j)),
        pl.BlockSpec((1, d), lambda b, i, j: (0, 0)),
        pl.BlockSpec((1, d), lambda b, i, j: (0, 0)),
    ]
    tail = (xres, gate, ln_g.reshape(1, d).astype(F32), ln_b.reshape(1, d).astype(F32))
    if glu:
        in_specs = [a_spec, w1_spec,
                    pl.BlockSpec((k, tn), lambda b, i, j: (0, j + nj)),
                    pl.BlockSpec((1, tn), lambda b, i, j: (0, j)),
                    pl.BlockSpec((1, tn), lambda b, i, j: (0, j + nj))] + tail_specs
        b2d = bias.reshape(1, 2 * d).astype(F32)
        args = (a, w, w, b2d, b2d) + tail
    else:
        in_specs = [a_spec, w1_spec] + tail_specs
        args = (a, w) + tail
    kern = functools.partial(_mmln_kernel, glu=glu, nj=nj, alpha=alpha, d=d)
    return pl.pallas_call(
        kern,
        out_shape=jax.ShapeDtypeStruct((bsz, seq, d), F32),
        grid=(bsz, seq // tm, nj),
        in_specs=in_specs,
        out_specs=pl.BlockSpec((1, tm, d), lambda b, i, j: (b, i, 0)),
        scratch_shapes=[pltpu.VMEM((nj, tm, tn), F32)],
        compiler_params=_params(("parallel", "parallel", "arbitrary")),
        name="matmul_res_ln",
    )(*args)


def _ffn_kernel(x_ref, sc_ref, sh_ref, g_ref, wg_ref, wu_ref, wo_ref, lng_ref, lnb_ref, o_ref,
                h_scr, acc_scr, *, nf, alpha, d, tn):
    f = pl.program_id(2)

    @pl.when(f == 0)
    def _():
        h_scr[...] = _modulate(x_ref, sc_ref, sh_ref)
        acc_scr[...] = jnp.zeros_like(acc_scr)

    h = h_scr[...]
    a_g = jnp.dot(h, wg_ref[...], preferred_element_type=F32)
    a_u = jnp.dot(h, wu_ref[...], preferred_element_type=F32)
    act = (jax.nn.silu(a_g) * a_u).astype(BF16)
    acc_scr[...] += jnp.dot(act, wo_ref[...], preferred_element_type=F32)

    @pl.when(f == nf - 1)
    def _():
        tiles = []
        for k in range(d // tn):
            cs = slice(k * tn, (k + 1) * tn)
            tiles.append(alpha * x_ref[0, :, cs] + (1.0 + g_ref[0, :, cs]) * acc_scr[:, cs])
        _layer_norm_rows(tiles, lng_ref, lnb_ref, o_ref, d)


def _ffn_res_ln(x, sc, sh, gate, w_in, w_out, ln_g, ln_b, alpha):
    bsz, seq, d = x.shape
    dff = w_out.shape[0]
    tm = _pick(seq, (512, 256, 128))
    tf = _pick(dff, (512, 256, 128))
    nf = dff // tf
    tn = _pick(d, (512, 256, 128))
    vec = pl.BlockSpec((1, 1, d), lambda b, i, f: (b, 0, 0))
    kern = functools.partial(_ffn_kernel, nf=nf, alpha=alpha, d=d, tn=tn)
    return pl.pallas_call(
        kern,
        out_shape=jax.ShapeDtypeStruct((bsz, seq, d), F32),
        grid=(bsz, seq // tm, nf),
        in_specs=[
            pl.BlockSpec((1, tm, d), lambda b, i, f: (b, i, 0)),
            vec, vec, vec,
            pl.BlockSpec((d, tf), lambda b, i, f: (0, f)),
            pl.BlockSpec((d, tf), lambda b, i, f: (0, f + nf)),
            pl.BlockSpec((tf, d), lambda b, i, f: (f, 0)),
            pl.BlockSpec((1, d), lambda b, i, f: (0, 0)),
            pl.BlockSpec((1, d), lambda b, i, f: (0, 0)),
        ],
        out_specs=pl.BlockSpec((1, tm, d), lambda b, i, f: (b, i, 0)),
        scratch_shapes=[pltpu.VMEM((tm, d), BF16), pltpu.VMEM((tm, d), F32)],
        compiler_params=_params(("parallel", "parallel", "arbitrary")),
        name="ffn_res_ln",
    )(x, sc, sh, gate, w_in, w_in, w_out, ln_g.reshape(1, d).astype(F32), ln_b.reshape(1, d).astype(F32))


def _qkv_kernel(x_ref, sc_ref, sh_ref, w_ref, cos_ref, sin_ref, o_ref, h_scr, *, tiles_per_tensor, q_scale):
    j = pl.program_id(2)

    @pl.when(j == 0)
    def _():
        h_scr[...] = _modulate(x_ref, sc_ref, sh_ref)

    acc = jnp.dot(h_scr[...], w_ref[...], preferred_element_type=F32)
    heads = acc.shape[1] // HEAD_DIM
    tensor = j // tiles_per_tensor

    def write(rope, scale):
        for hh in range(heads):
            xc = acc[:, hh * HEAD_DIM:(hh + 1) * HEAD_DIM]
            if rope:
                xc = xc * cos_ref[0] + pltpu.roll(xc, HEAD_DIM // 2, 1) * sin_ref[0]
            if scale != 1.0:
                xc = xc * scale
            o_ref[0, hh] = xc.astype(o_ref.dtype)

    pl.when(tensor == 0)(lambda: write(True, q_scale))
    pl.when(tensor == 1)(lambda: write(True, 1.0))
    pl.when(tensor == 2)(lambda: write(False, 1.0))


def _qkv_proj(x, sc, sh, w_qkv, cos_t, sin_t):
    bsz, seq, d = x.shape
    tm = _pick(seq, (512, 256, 128))
    tn = _pick(d, (1024, 512, 256, 128))
    hpt = tn // HEAD_DIM
    kern = functools.partial(_qkv_kernel, tiles_per_tensor=d // tn, q_scale=HEAD_DIM ** -0.5 * math.log2(math.e))
    return pl.pallas_call(
        kern,
        out_shape=jax.ShapeDtypeStruct((bsz, 3 * d // HEAD_DIM, seq, HEAD_DIM), BF16),
        grid=(bsz, seq // tm, 3 * d // tn),
        in_specs=[
            pl.BlockSpec((1, tm, d), lambda b, i, j: (b, i, 0)),
            pl.BlockSpec((1, 1, d), lambda b, i, j: (b, 0, 0)),
            pl.BlockSpec((1, 1, d), lambda b, i, j: (b, 0, 0)),
            pl.BlockSpec((d, tn), lambda b, i, j: (0, j)),
            pl.BlockSpec((1, tm, HEAD_DIM), lambda b, i, j: (b, i, 0)),
            pl.BlockSpec((1, tm, HEAD_DIM), lambda b, i, j: (b, i, 0)),
        ],
        out_specs=pl.BlockSpec((1, hpt, tm, HEAD_DIM), lambda b, i, j: (b, j, i, 0)),
        scratch_shapes=[pltpu.VMEM((tm, d), BF16)],
        compiler_params=_params(("parallel", "parallel", "arbitrary")),
        name="dsa_qkv_proj",
    )(x, sc, sh, w_qkv, cos_t, sin_t)


def _idx_kernel(x_ref, sc_ref, sh_ref, w_ref, cos_ref, sin_ref, qi_ref, kia_ref, kib_ref, wi_ref,
                *, n_pair, idx_heads, w_scale):
    h = _modulate(x_ref, sc_ref, sh_ref)
    acc = jnp.dot(h, w_ref[...], preferred_element_type=F32)
    tm = acc.shape[0]
    cos = cos_ref[0]
    sin = sin_ref[0]
    lane = lax.broadcasted_iota(I32, (tm, LANES), 1)
    first_half = (lane % IDX_DIM) < IDX_DIM // 2

    def rope(xc):
        partner = jnp.where(first_half, pltpu.roll(xc, LANES - IDX_DIM // 2, 1), pltpu.roll(xc, IDX_DIM // 2, 1))
        return xc * cos + partner * sin

    for p in range(n_pair):
        cs = slice(p * LANES, (p + 1) * LANES)
        qi_ref[0, :, cs] = rope(acc[:, cs]).astype(qi_ref.dtype)
    last = acc[:, n_pair * LANES:]
    ka = jnp.where(lane < IDX_DIM, rope(last), 0.0)
    kia_ref[0] = ka.astype(kia_ref.dtype)
    kib_ref[0] = pltpu.roll(ka, IDX_DIM, 1).astype(kib_ref.dtype)
    wi_ref[0] = jnp.where(lane < idx_heads, pltpu.roll(last, LANES - IDX_DIM, 1), 0.0) * w_scale


def _idx_proj(x, sc, sh, w_idx, cos_t, sin_t, idx_heads):
    bsz, seq, d = x.shape
    n = w_idx.shape[1]
    n_pair = idx_heads // 2
    tm = _pick(seq, (512, 256, 128))
    kern = functools.partial(_idx_kernel, n_pair=n_pair, idx_heads=idx_heads,
                             w_scale=(idx_heads ** -0.5) * (IDX_DIM ** -0.5))
    row = lambda b, i: (b, i, 0)
    return pl.pallas_call(
        kern,
        out_shape=(
            jax.ShapeDtypeStruct((bsz, seq, n_pair * LANES), BF16),
            jax.ShapeDtypeStruct((bsz, seq, LANES), BF16),
            jax.ShapeDtypeStruct((bsz, seq, LANES), BF16),
            jax.ShapeDtypeStruct((bsz, seq, LANES), F32),
        ),
        grid=(bsz, seq // tm),
        in_specs=[
            pl.BlockSpec((1, tm, d), row),
            pl.BlockSpec((1, 1, d), lambda b, i: (b, 0, 0)),
            pl.BlockSpec((1, 1, d), lambda b, i: (b, 0, 0)),
            pl.BlockSpec((d, n), lambda b, i: (0, 0)),
            pl.BlockSpec((1, tm, LANES), row),
            pl.BlockSpec((1, tm, LANES), row),
        ],
        out_specs=(
            pl.BlockSpec((1, tm, n_pair * LANES), row),
            pl.BlockSpec((1, tm, LANES), row),
            pl.BlockSpec((1, tm, LANES), row),
            pl.BlockSpec((1, tm, LANES), row),
        ),
        compiler_params=_params(("parallel", "parallel")),
        name="dsa_idx_proj",
    )(x, sc, sh, w_idx, cos_t, sin_t)


def _dsa_kernel(q_ref, k_ref, v_ref, qi_ref, kia_ref, kib_ref, wi_ref, o_ref,
                keys_scr, thr_scr, tie_scr, m_scr, l_scr, acc_scr, s0_scr, s1_scr, r0_scr, r1_scr,
                *, n_pair, k_top, idx_bits):
    heads, qb_rows, _ = acc_scr.shape
    kb_rows = keys_scr.shape[2]
    qb = pl.program_id(1)
    kb = pl.program_id(2)
    q0 = qb * qb_rows
    last_kb = (q0 + qb_rows - 1) // kb_rows
    n_chunk = last_kb + 1
    row = lax.broadcasted_iota(I32, (qb_rows, kb_rows), 0) + q0
    col = lax.broadcasted_iota(I32, (qb_rows, kb_rows), 1)
    nt = (((1,), (1,)), ((), ()))

    @pl.when(kb == 0)
    def _select():
        wi = wi_ref[0]

        def score_chunk(c, carry):
            k0 = pl.multiple_of(c * kb_rows, kb_rows)
            ka = kia_ref[0, pl.ds(k0, kb_rows), :]
            kbm = kib_ref[0, pl.ds(k0, kb_rows), :]
            sc = jnp.zeros((qb_rows, kb_rows), F32)
            for p in range(n_pair):
                qp = qi_ref[0, :, p * LANES:(p + 1) * LANES]
                sa = lax.dot_general(qp, ka, nt, preferred_element_type=F32)
                sb = lax.dot_general(qp, kbm, nt, preferred_element_type=F32)
                sc = sc + wi[:, 2 * p:2 * p + 1] * jnp.maximum(sa, 0.0)
                sc = sc + wi[:, 2 * p + 1:2 * p + 2] * jnp.maximum(sb, 0.0)
            bits = lax.bitcast_convert_type(sc, I32)
            key = bits ^ ((bits >> 31) & INT_MAX)
            keys_scr[c] = jnp.where(col + k0 <= row, key, INT_MIN)
            return carry

        lax.fori_loop(0, n_chunk, score_chunk, 0)

        rg = min(qb_rows, SEARCH_ROWS)
        lane_col = lax.broadcasted_iota(I32, (rg, LANES), 1)

        for r in range(qb_rows // rg):
            rs = slice(r * rg, (r + 1) * rg)

            def count(pred, rs=rs):
                def body(c, acc):
                    for t in range(kb_rows // LANES):
                        kk = keys_scr[c, rs, t * LANES:(t + 1) * LANES]
                        idx = lane_col + (c * kb_rows + t * LANES)
                        acc = acc + jnp.where(pred(kk, idx), 1.0, 0.0)
                    return acc
                acc = lax.fori_loop(0, n_chunk, body, jnp.zeros((rg, LANES), F32))
                return jnp.sum(acc, axis=1, keepdims=True)

            def wide(v):
                return jnp.broadcast_to(v, (rg, LANES))

            def thr_bit(i, t, count=count):
                cand = t + (jnp.int32(1) << (31 - i))
                cand_w = wide(cand)
                cnt = count(lambda kk, idx: kk >= cand_w)
                return jnp.where(cnt >= k_top, cand, t)

            thr_r = lax.fori_loop(0, 32, thr_bit, jnp.full((rg, 1), INT_MIN, I32))
            thr_w = wide(thr_r)
            thr_scr[rs, :] = thr_r
            tie_scr[rs, :] = jnp.full((rg, 1), INT_MAX, I32)
            n_gt = count(lambda kk, idx: kk > thr_w)
            n_ge = count(lambda kk, idx: kk >= thr_w)

            @pl.when(jnp.max(n_ge) > k_top)
            def _ties(count=count, thr_w=thr_w, n_gt=n_gt, rs=rs):
                need = k_top - n_gt

                def tie_bit(i, jt):
                    cand = jt + (jnp.int32(1) << (idx_bits - 1 - i))
                    cand_w = wide(cand)
                    cnt = count(lambda kk, idx: (kk == thr_w) & (idx < cand_w))
                    return jnp.where(cnt < need, cand, jt)

                tie_scr[rs, :] = lax.fori_loop(0, idx_bits, tie_bit, jnp.zeros((rg, 1), I32))

        thr = thr_scr[...]
        tie = tie_scr[...]

        def bias_chunk(c, carry):
            kk = keys_scr[c]
            idx = col + c * kb_rows
            sel = (kk > thr) | ((kk == thr) & (idx <= tie))
            bias = jnp.where(sel & (idx <= row), 0.0, MASK_VALUE)
            keys_scr[c] = lax.bitcast_convert_type(bias, I32)
            return carry

        lax.fori_loop(0, n_chunk, bias_chunk, 0)
        m_scr[...] = jnp.full(m_scr.shape, M_INIT, F32)
        l_scr[...] = jnp.zeros_like(l_scr)
        acc_scr[...] = jnp.zeros_like(acc_scr)

    @pl.when(kb <= last_kb)
    def _attend():
        ones = jnp.ones((kb_rows, HEAD_DIM), BF16)

        def logits(h, s_ref, r_ref):
            bias = lax.bitcast_convert_type(keys_scr[kb], F32)
            s = lax.dot_general(q_ref[0, h], k_ref[0, h], nt, preferred_element_type=F32) + bias
            s_ref[...] = s
            r_ref[...] = jnp.broadcast_to(jnp.max(s, axis=1, keepdims=True), r_ref.shape)

        def accumulate(h, s_ref, r_ref):
            m_old = m_scr[h]
            m_new = jnp.maximum(m_old, r_ref[...])
            alpha = jnp.exp2(m_old - m_new)
            p = jnp.concatenate(
                [jnp.exp2(s_ref[:, t * LANES:(t + 1) * LANES] - m_new).astype(BF16)
                 for t in range(kb_rows // LANES)], axis=1)
            v_ext = jnp.concatenate([v_ref[0, h], ones], axis=1)
            pv = jnp.dot(p, v_ext, preferred_element_type=F32)
            acc_scr[h] = alpha * acc_scr[h] + pv[:, :HEAD_DIM]
            l_scr[h] = alpha * l_scr[h] + pv[:, HEAD_DIM:]
            m_scr[h] = m_new

        bufs = ((s0_scr, r0_scr), (s1_scr, r1_scr))
        logits(0, *bufs[0])
        for h in range(heads):
            if h + 1 < heads:
                logits(h + 1, *bufs[(h + 1) % 2])
            accumulate(h, *bufs[h % 2])

    @pl.when(kb == pl.num_programs(2) - 1)
    def _finish():
        for h in range(heads):
            o_ref[0, :, h * HEAD_DIM:(h + 1) * HEAD_DIM] = (acc_scr[h] / l_scr[h]).astype(o_ref.dtype)


def _dsa_attention(qkv, qi, kia, kib, wi, k_top):
    bsz, heads3, seq, _ = qkv.shape
    heads = heads3 // 3
    d = heads * HEAD_DIM
    n_pair = qi.shape[-1] // LANES
    qb_rows = _pick(seq, (256, 128))
    kb_rows = _pick(seq, (512, 256, 128))
    n_kb = seq // kb_rows

    def kv_block(qb, kb):
        return jnp.minimum(kb, (qb * qb_rows + qb_rows - 1) // kb_rows)

    kern = functools.partial(_dsa_kernel, n_pair=n_pair, k_top=k_top, idx_bits=seq.bit_length())
    return pl.pallas_call(
        kern,
        out_shape=jax.ShapeDtypeStruct((bsz, seq, d), BF16),
        grid=(bsz, seq // qb_rows, n_kb),
        in_specs=[
            pl.BlockSpec((1, heads, qb_rows, HEAD_DIM), lambda b, i, j: (b, 0, i, 0)),
            pl.BlockSpec((1, heads, kb_rows, HEAD_DIM), lambda b, i, j: (b, 1, kv_block(i, j), 0)),
            pl.BlockSpec((1, heads, kb_rows, HEAD_DIM), lambda b, i, j: (b, 2, kv_block(i, j), 0)),
            pl.BlockSpec((1, qb_rows, n_pair * LANES), lambda b, i, j: (b, i, 0)),
            pl.BlockSpec((1, seq, LANES), lambda b, i, j: (b, 0, 0)),
            pl.BlockSpec((1, seq, LANES), lambda b, i, j: (b, 0, 0)),
            pl.BlockSpec((1, qb_rows, LANES), lambda b, i, j: (b, i, 0)),
        ],
        out_specs=pl.BlockSpec((1, qb_rows, d), lambda b, i, j: (b, i, 0)),
        scratch_shapes=[
            pltpu.VMEM((n_kb, qb_rows, kb_rows), I32),
            pltpu.VMEM((qb_rows, 1), I32),
            pltpu.VMEM((qb_rows, 1), I32),
            pltpu.VMEM((heads, qb_rows, LANES), F32),
            pltpu.VMEM((heads, qb_rows, LANES), F32),
            pltpu.VMEM((heads, qb_rows, HEAD_DIM), F32),
            pltpu.VMEM((qb_rows, kb_rows), F32),
            pltpu.VMEM((qb_rows, kb_rows), F32),
            pltpu.VMEM((qb_rows, LANES), F32),
            pltpu.VMEM((qb_rows, LANES), F32),
        ],
        compiler_params=_params(("parallel", "parallel", "arbitrary")),
        name="dsa_select_attend",
    )(qkv, qkv, qkv, qi, kia, kib, wi)


def _rope_tables(positions, dim, repeats):
    inv = 1.0 / (ROPE_THETA ** (jnp.arange(0, dim, 2, dtype=F32) / dim))
    ang = positions.astype(F32)[..., None] * inv
    cos, sin = jnp.cos(ang), jnp.sin(ang)
    cos_t = jnp.concatenate([cos, cos] * repeats, axis=-1)
    sin_t = jnp.concatenate([-sin, sin] * repeats, axis=-1)
    return cos_t, sin_t


def _segment_major(x, seg):
    bsz, seq, d = x.shape
    y = x.reshape(bsz, seq // (SUBLANES * seg), SUBLANES, seg, d)
    return jnp.swapaxes(y, 2, 3).reshape(bsz, seq, d)


def _token_major(x, seg):
    bsz, seq, d = x.shape
    y = x.reshape(bsz, seq // (SUBLANES * seg), seg, SUBLANES, d)
    return jnp.swapaxes(y, 2, 3).reshape(bsz, seq, d)


def kernel(x, c, positions, ada_w, ada_b, ln_g, ln_b, s5_in_w, s5_a_re, s5_a_im, s5_log_dt, s5_b_re, s5_b_im, s5_c_re, s5_c_im, s5_d, s5_glu_w, s5_glu_b, dsa_in_w, dsa_out_w, ffn_w_in, ffn_w_out):
    bsz, seq, d = x.shape
    depth = ada_w.shape[0]
    alpha = (2.0 * depth) ** 0.25
    idx_heads = (dsa_in_w.shape[-1] - 3 * d - IDX_DIM) // (IDX_DIM + 1)
    k_top = min(TOPK_MAX, seq // 4)
    seg = min(S5_SEG, seq // SUBLANES)

    cos_h, sin_h = _rope_tables(positions, HEAD_DIM, 1)
    cos_i, sin_i = _rope_tables(positions, IDX_DIM, 2)
    mod = _ada_mod(c, ada_w, ada_b)

    for i in range(depth):
        sh1, sc1, g1, sh2, sc2, g2 = [m[:, None, :] for m in jnp.split(mod[i], 6, axis=-1)]
        j = i // 2
        if i % 2 == 0:
            xp = _segment_major(x, seg)
            prep = _s5_discretize(s5_a_re[j], s5_a_im[j], s5_log_dt[j], s5_b_re[j], s5_b_im[j],
                                  s5_c_re[j], s5_c_im[j], seg)
            u = _mod_matmul(xp, sc1, sh1, s5_in_w[j].astype(BF16), F32)
            gl = _s5_scan(u, prep, s5_d[j], seg)
            x1 = _matmul_res_ln(gl, s5_glu_w[j].astype(BF16), s5_glu_b[j], xp, g1,
                                ln_g[i, 0], ln_b[i, 0], alpha, glu=True)
        else:
            xp = x
            w = dsa_in_w[j]
            n_qi = idx_heads * IDX_DIM
            w_qkv = w[:, :3 * d].astype(BF16)
            pad = jnp.zeros((d, LANES - IDX_DIM - idx_heads), w.dtype)
            w_idx = jnp.concatenate(
                [w[:, 3 * d:3 * d + n_qi], w[:, 3 * d + n_qi + idx_heads:], w[:, 3 * d + n_qi:3 * d + n_qi + idx_heads], pad],
                axis=1).astype(BF16)
            qkv = _qkv_proj(xp, sc1, sh1, w_qkv, cos_h, sin_h)
            qi, kia, kib, wi = _idx_proj(xp, sc1, sh1, w_idx, cos_i, sin_i, idx_heads)
            att = _dsa_attention(qkv, qi, kia, kib, wi, k_top)
            x1 = _matmul_res_ln(att, dsa_out_w[j].astype(BF16), None, xp, g1,
                                ln_g[i, 0], ln_b[i, 0], alpha, glu=False)
        x2 = _ffn_res_ln(x1, sc2, sh2, g2, ffn_w_in[i].astype(BF16), ffn_w_out[i].astype(BF16),
                         ln_g[i, 1], ln_b[i, 1], alpha)
        x = _token_major(x2, seg) if i % 2 == 0 else x2
    return x
```

```python
import functools
import math

import jax
import jax.numpy as jnp
from jax import lax
from jax.experimental import pallas as pl
from jax.experimental.pallas import tpu as pltpu

F32 = jnp.float32
BF16 = jnp.bfloat16
I32 = jnp.int32

S5_GROUP = 16
S5_STATE = 64
HEAD_DIM = 128
IDX_DIM = 64
TOPK_MAX = 256
ROPE_THETA = 10000.0
LN_EPS = 1e-5

LANES = 128
SUBLANES = 8
VMEM_LIMIT_BYTES = 56 * 1024 * 1024

S5_PACK_GROUPS = 16
S5_SEG = 64

INT_MIN = -(2 ** 31)
INT_MAX = 2 ** 31 - 1
SEARCH_ROWS = 128
WORD_BITS = 32

MASK_VALUE = -2e30
M_INIT = -1e30


def _pick(n, cands):
    for c in cands:
        if n % c == 0:
            return c
    return n


def _params(sem):
    return pltpu.CompilerParams(dimension_semantics=sem, vmem_limit_bytes=VMEM_LIMIT_BYTES)


def _ada_kernel(c_ref, w_ref, b_ref, o_ref):
    ca = jax.nn.silu(c_ref[...]).astype(BF16)
    o_ref[0] = jnp.dot(ca, w_ref[0].astype(BF16), preferred_element_type=F32) + b_ref[0]


def _ada_mod(c, ada_w, ada_b):
    bsz, d = c.shape
    depth, _, n = ada_w.shape
    rows = SUBLANES * ((bsz + SUBLANES - 1) // SUBLANES)
    cp = jnp.zeros((rows, d), F32).at[:bsz].set(c)
    tn = _pick(n, (1024, 512, 256, 128))
    out = pl.pallas_call(
        _ada_kernel,
        out_shape=jax.ShapeDtypeStruct((depth, rows, n), F32),
        grid=(depth, n // tn),
        in_specs=[
            pl.BlockSpec((rows, d), lambda l, j: (0, 0)),
            pl.BlockSpec((1, d, tn), lambda l, j: (l, 0, j)),
            pl.BlockSpec((1, 1, tn), lambda l, j: (l, 0, j)),
        ],
        out_specs=pl.BlockSpec((1, rows, tn), lambda l, j: (l, 0, j)),
        compiler_params=_params(("arbitrary", "arbitrary")),
        name="ada_mod",
    )(cp, ada_w, ada_b.reshape(depth, 1, n))
    return out[:, :bsz]


def _modulate(x_ref, sc_ref, sh_ref):
    return (x_ref[0] * (1.0 + sc_ref[0]) + sh_ref[0]).astype(BF16)


def _modmm_kernel(x_ref, sc_ref, sh_ref, w_ref, o_ref, h_scr):
    @pl.when(pl.program_id(2) == 0)
    def _():
        h_scr[...] = _modulate(x_ref, sc_ref, sh_ref)

    o_ref[0] = jnp.dot(h_scr[...], w_ref[...], preferred_element_type=F32).astype(o_ref.dtype)


def _mod_matmul(x, sc, sh, w, out_dtype):
    bsz, seq, d = x.shape
    n = w.shape[1]
    tm = _pick(seq, (512, 256, 128))
    tn = _pick(n, (1024, 512, 256, 128))
    return pl.pallas_call(
        _modmm_kernel,
        out_shape=jax.ShapeDtypeStruct((bsz, seq, n), out_dtype),
        grid=(bsz, seq // tm, n // tn),
        in_specs=[
            pl.BlockSpec((1, tm, d), lambda b, i, j: (b, i, 0)),
            pl.BlockSpec((1, 1, d), lambda b, i, j: (b, 0, 0)),
            pl.BlockSpec((1, 1, d), lambda b, i, j: (b, 0, 0)),
            pl.BlockSpec((d, tn), lambda b, i, j: (0, j)),
        ],
        out_specs=pl.BlockSpec((1, tm, tn), lambda b, i, j: (b, i, j)),
        scratch_shapes=[pltpu.VMEM((tm, d), BF16)],
        compiler_params=_params(("parallel", "parallel", "arbitrary")),
        name="mod_matmul",
    )(x, sc, sh, w)


def _s5_kernel(u_ref, bb_ref, cb_ref, lre_ref, lim_ref, ltre_ref, ltim_ref, d_ref, o_ref,
               xs_scr, carry_scr, cin_scr, *, seg, width):
    ns = xs_scr.shape[1] // 2

    @pl.when(pl.program_id(2) == 0)
    def _():
        carry_scr[...] = jnp.zeros_like(carry_scr)

    u = u_ref[0]
    xs_scr[...] = jnp.dot(u.astype(BF16), bb_ref[0], preferred_element_type=F32)

    for part in range(ns // width):
        cr = slice(part * width, (part + 1) * width)
        ci = slice(ns + part * width, ns + (part + 1) * width)
        lr = jnp.broadcast_to(lre_ref[0, :, cr], (SUBLANES, width))
        li = jnp.broadcast_to(lim_ref[0, :, cr], (SUBLANES, width))

        def local_step(i, st, cr=cr, ci=ci, lr=lr, li=li):
            sr, si = st
            r0 = pl.multiple_of(i * SUBLANES, SUBLANES)
            nr = lr * sr - li * si + xs_scr[pl.ds(r0, SUBLANES), cr]
            ni = lr * si + li * sr + xs_scr[pl.ds(r0, SUBLANES), ci]
            xs_scr[pl.ds(r0, SUBLANES), cr] = nr
            xs_scr[pl.ds(r0, SUBLANES), ci] = ni
            return nr, ni

        zero = jnp.zeros((SUBLANES, width), F32)
        er, ei = lax.fori_loop(0, seg, local_step, (zero, zero), unroll=2)

        ltr = ltre_ref[0, :, cr]
        lti = ltim_ref[0, :, cr]
        c_r = carry_scr[:, cr]
        c_i = carry_scr[:, ci]
        for s in range(SUBLANES):
            cin_scr[s:s + 1, cr] = c_r
            cin_scr[s:s + 1, ci] = c_i
            e_r = er[s:s + 1, :]
            e_i = ei[s:s + 1, :]
            c_r, c_i = ltr * c_r - lti * c_i + e_r, ltr * c_i + lti * c_r + e_i
        carry_scr[:, cr] = c_r
        carry_scr[:, ci] = c_i

        def carry_step(i, st, cr=cr, ci=ci, lr=lr, li=li):
            pr, pi_ = st
            r0 = pl.multiple_of(i * SUBLANES, SUBLANES)
            nr = lr * pr - li * pi_
            ni = lr * pi_ + li * pr
            xs_scr[pl.ds(r0, SUBLANES), cr] += nr
            xs_scr[pl.ds(r0, SUBLANES), ci] += ni
            return nr, ni

        lax.fori_loop(0, seg, carry_step, (cin_scr[:, cr], cin_scr[:, ci]), unroll=2)

    y = jnp.dot(xs_scr[...].astype(BF16), cb_ref[0], preferred_element_type=F32)
    y = y + d_ref[0] * u
    o_ref[0] = jax.nn.gelu(y).astype(o_ref.dtype)


def _cmul(ar, ai, br, bi):
    return ar * br - ai * bi, ar * bi + ai * br


def _s5_discretize(a_re, a_im, log_dt, b_re, b_im, c_re, c_im, seg):
    g, n = a_re.shape
    p = b_re.shape[-1]
    pg = S5_PACK_GROUPS
    packs = g // pg
    a_re, a_im = a_re.astype(F32), a_im.astype(F32)
    dt = jnp.exp(log_dt.astype(F32))[:, None]
    mag = jnp.exp(a_re * dt)
    lb_re, lb_im = mag * jnp.cos(a_im * dt), mag * jnp.sin(a_im * dt)
    den = a_re * a_re + a_im * a_im
    nr, ni = lb_re - 1.0, lb_im
    f_re = (nr * a_re + ni * a_im) / den
    f_im = (ni * a_re - nr * a_im) / den
    bb_re, bb_im = _cmul(f_re[..., None], f_im[..., None], b_re.astype(F32), b_im.astype(F32))
    lt_re, lt_im = lb_re, lb_im
    for _ in range(int(math.log2(seg))):
        lt_re, lt_im = _cmul(lt_re, lt_im, lt_re, lt_im)
    eye = jnp.eye(pg, dtype=F32)

    def in_blk(m):
        return jnp.einsum("kgnp,gh->kgphn", m.reshape(packs, pg, n, p), eye).reshape(packs, pg * p, pg * n)

    def out_blk(m):
        return jnp.einsum("kgpn,gh->kgnhp", m.reshape(packs, pg, p, n), eye).reshape(packs, pg * n, pg * p)

    b_blk = jnp.concatenate([in_blk(bb_re), in_blk(bb_im)], axis=-1).astype(BF16)
    c_blk = jnp.concatenate([out_blk(c_re.astype(F32)), out_blk(-c_im.astype(F32))], axis=1).astype(BF16)

    def vec(m):
        return m.reshape(packs, 1, pg * n)

    return b_blk, c_blk, vec(lb_re), vec(lb_im), vec(lt_re), vec(lt_im)


def _s5_scan(u, prep, d_skip, seg):
    bsz, seq, d = u.shape
    b_blk, c_blk, lre, lim, ltre, ltim = prep
    packs, pw, ns2 = b_blk.shape
    ns = ns2 // 2
    rows = SUBLANES * seg
    width = _pick(ns, (512, 256, 128))
    kern = functools.partial(_s5_kernel, seg=seg, width=width)
    vspec = pl.BlockSpec((1, 1, ns), lambda b, k, m: (k, 0, 0))
    return pl.pallas_call(
        kern,
        out_shape=jax.ShapeDtypeStruct((bsz, seq, d), BF16),
        grid=(bsz, packs, seq // rows),
        in_specs=[
            pl.BlockSpec((1, rows, pw), lambda b, k, m: (b, m, k)),
            pl.BlockSpec((1, pw, ns2), lambda b, k, m: (k, 0, 0)),
            pl.BlockSpec((1, ns2, pw), lambda b, k, m: (k, 0, 0)),
            vspec, vspec, vspec, vspec,
            pl.BlockSpec((1, 1, pw), lambda b, k, m: (k, 0, 0)),
        ],
        out_specs=pl.BlockSpec((1, rows, pw), lambda b, k, m: (b, m, k)),
        scratch_shapes=[
            pltpu.VMEM((rows, ns2), F32),
            pltpu.VMEM((1, ns2), F32),
            pltpu.VMEM((SUBLANES, ns2), F32),
        ],
        compiler_params=_params(("parallel", "parallel", "arbitrary")),
        name="s5_scan",
    )(u, b_blk, c_blk, lre, lim, ltre, ltim, d_skip.reshape(packs, 1, pw).astype(F32))


def _layer_norm_rows(tiles, lng_ref, lnb_ref, o_ref, d):
    tn = tiles[0].shape[1]
    tot = tiles[0].sum(axis=1, keepdims=True)
    for t in tiles[1:]:
        tot = tot + t.sum(axis=1, keepdims=True)
    mu = tot * (1.0 / d)
    sq = jnp.square(tiles[0] - mu).sum(axis=1, keepdims=True)
    for t in tiles[1:]:
        sq = sq + jnp.square(t - mu).sum(axis=1, keepdims=True)
    inv = lax.rsqrt(sq * (1.0 / d) + LN_EPS)
    for k, t in enumerate(tiles):
        cs = slice(k * tn, (k + 1) * tn)
        o_ref[0, :, cs] = ((t - mu) * inv * lng_ref[:, cs] + lnb_ref[:, cs]).astype(o_ref.dtype)


def _mmln_kernel(*refs, glu, nj, alpha, d):
    if glu:
        a_ref, w1_ref, w2_ref, b1_ref, b2_ref, x_ref, g_ref, lng_ref, lnb_ref, o_ref, r_scr = refs
    else:
        a_ref, w1_ref, x_ref, g_ref, lng_ref, lnb_ref, o_ref, r_scr = refs
    j = pl.program_id(2)
    a = a_ref[0]
    y = jnp.dot(a, w1_ref[...], preferred_element_type=F32)
    if glu:
        y = y + b1_ref[...]
        gate = jnp.dot(a, w2_ref[...], preferred_element_type=F32) + b2_ref[...]
        y = y * jax.nn.sigmoid(gate)
    r_scr[j] = alpha * x_ref[0] + (1.0 + g_ref[0]) * y

    @pl.when(j == nj - 1)
    def _():
        _layer_norm_rows([r_scr[t] for t in range(nj)], lng_ref, lnb_ref, o_ref, d)


def _matmul_res_ln(a, w, bias, xres, gate, ln_g, ln_b, alpha, glu):
    bsz, seq, k = a.shape
    d = xres.shape[-1]
    tm = _pick(seq, (512, 256, 128))
    tn = _pick(d, (512, 256, 128))
    nj = d // tn
    a_spec = pl.BlockSpec((1, tm, k), lambda b, i, j: (b, i, 0))
    w1_spec = pl.BlockSpec((k, tn), lambda b, i, j: (0, j))
    tail_specs = [
        pl.BlockSpec((1, tm, tn), lambda b, i, j: (b, i, j)),
        pl.BlockSpec((1, 1, tn), lambda b, i, j: (b, 0, j)),
        pl.BlockSpec((1, d), lambda b, i, j: (0, 0)),
        pl.BlockSpec((1, d), lambda b, i, j: (0, 0)),
    ]
    tail = (xres, gate, ln_g.reshape(1, d).astype(F32), ln_b.reshape(1, d).astype(F32))
    if glu:
        in_specs = [a_spec, w1_spec,
                    pl.BlockSpec((k, tn), lambda b, i, j: (0, j + nj)),
                    pl.BlockSpec((1, tn), lambda b, i, j: (0, j)),
                    pl.BlockSpec((1, tn), lambda b, i, j: (0, j + nj))] + tail_specs
        b2d = bias.reshape(1, 2 * d).astype(F32)
        args = (a, w, w, b2d, b2d) + tail
    else:
        in_specs = [a_spec, w1_spec] + tail_specs
        args = (a, w) + tail
    kern = functools.partial(_mmln_kernel, glu=glu, nj=nj, alpha=alpha, d=d)
    return pl.pallas_call(
        kern,
        out_shape=jax.ShapeDtypeStruct((bsz, seq, d), F32),
        grid=(bsz, seq // tm, nj),
        in_specs=in_specs,
        out_specs=pl.BlockSpec((1, tm, d), lambda b, i, j: (b, i, 0)),
        scratch_shapes=[pltpu.VMEM((nj, tm, tn), F32)],
        compiler_params=_params(("parallel", "parallel", "arbitrary")),
        name="matmul_res_ln",
    )(*args)


def _ffn_kernel(x_ref, sc_ref, sh_ref, g_ref, wg_ref, wu_ref, wo_ref, lng_ref, lnb_ref, o_ref,
                h_scr, acc_scr, *, nf, alpha, d, tn):
    f = pl.program_id(2)

    @pl.when(f == 0)
    def _():
        h_scr[...] = _modulate(x_ref, sc_ref, sh_ref)
        acc_scr[...] = jnp.zeros_like(acc_scr)

    h = h_scr[...]
    a_g = jnp.dot(h, wg_ref[...], preferred_element_type=F32)
    a_u = jnp.dot(h, wu_ref[...], preferred_element_type=F32)
    act = (jax.nn.silu(a_g) * a_u).astype(BF16)
    acc_scr[...] += jnp.dot(act, wo_ref[...], preferred_element_type=F32)

    @pl.when(f == nf - 1)
    def _():
        tiles = []
        for k in range(d // tn):
            cs = slice(k * tn, (k + 1) * tn)
            tiles.append(alpha * x_ref[0, :, cs] + (1.0 + g_ref[0, :, cs]) * acc_scr[:, cs])
        _layer_norm_rows(tiles, lng_ref, lnb_ref, o_ref, d)


def _ffn_res_ln(x, sc, sh, gate, w_in, w_out, ln_g, ln_b, alpha):
    bsz, seq, d = x.shape
    dff = w_out.shape[0]
    tm = _pick(seq, (512, 256, 128))
    tf = _pick(dff, (512, 256, 128))
    nf = dff // tf
    tn = _pick(d, (512, 256, 128))
    vec = pl.BlockSpec((1, 1, d), lambda b, i, f: (b, 0, 0))
    kern = functools.partial(_ffn_kernel, nf=nf, alpha=alpha, d=d, tn=tn)
    return pl.pallas_call(
        kern,
        out_shape=jax.ShapeDtypeStruct((bsz, seq, d), F32),
        grid=(bsz, seq // tm, nf),
        in_specs=[
            pl.BlockSpec((1, tm, d), lambda b, i, f: (b, i, 0)),
            vec, vec, vec,
            pl.BlockSpec((d, tf), lambda b, i, f: (0, f)),
            pl.BlockSpec((d, tf), lambda b, i, f: (0, f + nf)),
            pl.BlockSpec((tf, d), lambda b, i, f: (f, 0)),
            pl.BlockSpec((1, d), lambda b, i, f: (0, 0)),
            pl.BlockSpec((1, d), lambda b, i, f: (0, 0)),
        ],
        out_specs=pl.BlockSpec((1, tm, d), lambda b, i, f: (b, i, 0)),
        scratch_shapes=[pltpu.VMEM((tm, d), BF16), pltpu.VMEM((tm, d), F32)],
        compiler_params=_params(("parallel", "parallel", "arbitrary")),
        name="ffn_res_ln",
    )(x, sc, sh, gate, w_in, w_in, w_out, ln_g.reshape(1, d).astype(F32), ln_b.reshape(1, d).astype(F32))


def _qkv_kernel(x_ref, sc_ref, sh_ref, w_ref, cos_ref, sin_ref, o_ref, h_scr, *, tiles_per_tensor, q_scale):
    j = pl.program_id(2)

    @pl.when(j == 0)
    def _():
        h_scr[...] = _modulate(x_ref, sc_ref, sh_ref)

    acc = jnp.dot(h_scr[...], w_ref[...], preferred_element_type=F32)
    heads = acc.shape[1] // HEAD_DIM
    tensor = j // tiles_per_tensor

    def write(rope, scale):
        for hh in range(heads):
            xc = acc[:, hh * HEAD_DIM:(hh + 1) * HEAD_DIM]
            if rope:
                xc = xc * cos_ref[0] + pltpu.roll(xc, HEAD_DIM // 2, 1) * sin_ref[0]
            if scale != 1.0:
                xc = xc * scale
            o_ref[0, hh] = xc.astype(o_ref.dtype)

    pl.when(tensor == 0)(lambda: write(True, q_scale))
    pl.when(tensor == 1)(lambda: write(True, 1.0))
    pl.when(tensor == 2)(lambda: write(False, 1.0))


def _qkv_proj(x, sc, sh, w_qkv, cos_t, sin_t):
    bsz, seq, d = x.shape
    tm = _pick(seq, (512, 256, 128))
    tn = _pick(d, (1024, 512, 256, 128))
    hpt = tn // HEAD_DIM
    kern = functools.partial(_qkv_kernel, tiles_per_tensor=d // tn, q_scale=HEAD_DIM ** -0.5 * math.log2(math.e))
    return pl.pallas_call(
        kern,
        out_shape=jax.ShapeDtypeStruct((bsz, 3 * d // HEAD_DIM, seq, HEAD_DIM), BF16),
        grid=(bsz, seq // tm, 3 * d // tn),
        in_specs=[
            pl.BlockSpec((1, tm, d), lambda b, i, j: (b, i, 0)),
            pl.BlockSpec((1, 1, d), lambda b, i, j: (b, 0, 0)),
            pl.BlockSpec((1, 1, d), lambda b, i, j: (b, 0, 0)),
            pl.BlockSpec((d, tn), lambda b, i, j: (0, j)),
            pl.BlockSpec((1, tm, HEAD_DIM), lambda b, i, j: (b, i, 0)),
            pl.BlockSpec((1, tm, HEAD_DIM), lambda b, i, j: (b, i, 0)),
        ],
        out_specs=pl.BlockSpec((1, hpt, tm, HEAD_DIM), lambda b, i, j: (b, j, i, 0)),
        scratch_shapes=[pltpu.VMEM((tm, d), BF16)],
        compiler_params=_params(("parallel", "parallel", "arbitrary")),
        name="dsa_qkv_proj",
    )(x, sc, sh, w_qkv, cos_t, sin_t)


def _idx_kernel(x_ref, sc_ref, sh_ref, w_ref, cos_ref, sin_ref, qi_ref, kia_ref, kib_ref, wi_ref,
                *, n_pair, idx_heads, w_scale):
    h = _modulate(x_ref, sc_ref, sh_ref)
    acc = jnp.dot(h, w_ref[...], preferred_element_type=F32)
    tm = acc.shape[0]
    cos = cos_ref[0]
    sin = sin_ref[0]
    lane = lax.broadcasted_iota(I32, (tm, LANES), 1)
    first_half = (lane % IDX_DIM) < IDX_DIM // 2

    def rope(xc):
        partner = jnp.where(first_half, pltpu.roll(xc, LANES - IDX_DIM // 2, 1), pltpu.roll(xc, IDX_DIM // 2, 1))
        return xc * cos + partner * sin

    for p in range(n_pair):
        cs = slice(p * LANES, (p + 1) * LANES)
        qi_ref[0, :, cs] = rope(acc[:, cs]).astype(qi_ref.dtype)
    last = acc[:, n_pair * LANES:]
    ka = jnp.where(lane < IDX_DIM, rope(last), 0.0)
    kia_ref[0] = ka.astype(kia_ref.dtype)
    kib_ref[0] = pltpu.roll(ka, IDX_DIM, 1).astype(kib_ref.dtype)
    wi_ref[0] = jnp.where(lane < idx_heads, pltpu.roll(last, LANES - IDX_DIM, 1), 0.0) * w_scale


def _idx_proj(x, sc, sh, w_idx, cos_t, sin_t, idx_heads):
    bsz, seq, d = x.shape
    n = w_idx.shape[1]
    n_pair = idx_heads // 2
    tm = _pick(seq, (512, 256, 128))
    kern = functools.partial(_idx_kernel, n_pair=n_pair, idx_heads=idx_heads,
                             w_scale=(idx_heads ** -0.5) * (IDX_DIM ** -0.5))
    row = lambda b, i: (b, i, 0)
    return pl.pallas_call(
        kern,
        out_shape=(
            jax.ShapeDtypeStruct((bsz, seq, n_pair * LANES), BF16),
            jax.ShapeDtypeStruct((bsz, seq, LANES), BF16),
            jax.ShapeDtypeStruct((bsz, seq, LANES), BF16),
            jax.ShapeDtypeStruct((bsz, seq, LANES), F32),
        ),
        grid=(bsz, seq // tm),
        in_specs=[
            pl.BlockSpec((1, tm, d), row),
            pl.BlockSpec((1, 1, d), lambda b, i: (b, 0, 0)),
            pl.BlockSpec((1, 1, d), lambda b, i: (b, 0, 0)),
            pl.BlockSpec((d, n), lambda b, i: (0, 0)),
            pl.BlockSpec((1, tm, LANES), row),
            pl.BlockSpec((1, tm, LANES), row),
        ],
        out_specs=(
            pl.BlockSpec((1, tm, n_pair * LANES), row),
            pl.BlockSpec((1, tm, LANES), row),
            pl.BlockSpec((1, tm, LANES), row),
            pl.BlockSpec((1, tm, LANES), row),
        ),
        compiler_params=_params(("parallel", "parallel")),
        name="dsa_idx_proj",
    )(x, sc, sh, w_idx, cos_t, sin_t)


def _bit_transpose32(words):
    a = list(words)
    j = WORD_BITS // 2
    mask = 0x0000FFFF
    while j:
        k = 0
        while k < WORD_BITS:
            t = (a[k] ^ lax.shift_right_logical(a[k + j], jnp.int32(j))) & jnp.int32(mask)
            a[k] = a[k] ^ t
            a[k + j] = a[k + j] ^ (t << j)
            k = (k + j + 1) & ~j
        j >>= 1
        if j:
            mask = (mask ^ (mask << j)) & 0xFFFFFFFF
    return a


def _dsa_kernel(q_ref, k_ref, v_ref, qi_ref, kia_ref, kib_ref, wi_ref, o_ref,
                keys_scr, planes_scr, thr_scr, tie_scr, ngt_scr, nge_scr,
                m_scr, l_scr, acc_scr, s0_scr, s1_scr, r0_scr, r1_scr,
                *, n_pair, k_top, idx_bits):
    heads, qb_rows, _ = acc_scr.shape
    kb_rows = keys_scr.shape[2]
    qb = pl.program_id(1)
    kb = pl.program_id(2)
    q0 = qb * qb_rows
    last_kb = (q0 + qb_rows - 1) // kb_rows
    n_chunk = last_kb + 1
    row = lax.broadcasted_iota(I32, (qb_rows, kb_rows), 0) + q0
    col = lax.broadcasted_iota(I32, (qb_rows, kb_rows), 1)
    nt = (((1,), (1,)), ((), ()))

    @pl.when(kb == 0)
    def _select():
        wi = wi_ref[0]

        def score_chunk(c, carry):
            k0 = pl.multiple_of(c * kb_rows, kb_rows)
            ka = kia_ref[0, pl.ds(k0, kb_rows), :]
            kbm = kib_ref[0, pl.ds(k0, kb_rows), :]
            sc = jnp.zeros((qb_rows, kb_rows), F32)
            for p in range(n_pair):
                qp = qi_ref[0, :, p * LANES:(p + 1) * LANES]
                sa = lax.dot_general(qp, ka, nt, preferred_element_type=F32)
                sb = lax.dot_general(qp, kbm, nt, preferred_element_type=F32)
                sc = sc + wi[:, 2 * p:2 * p + 1] * jnp.maximum(sa, 0.0)
                sc = sc + wi[:, 2 * p + 1:2 * p + 2] * jnp.maximum(sb, 0.0)
            bits = lax.bitcast_convert_type(sc, I32)
            key = bits ^ ((bits >> 31) & INT_MAX)
            keys_scr[c] = jnp.where(col + k0 <= row, key, INT_MIN)
            return carry

        lax.fori_loop(0, n_chunk, score_chunk, 0)

        def fill_chunk(c, carry):
            keys_scr[c] = jnp.full((qb_rows, kb_rows), INT_MIN, I32)
            return carry

        lax.fori_loop(n_chunk, keys_scr.shape[0], fill_chunk, 0)

        slices_per_chunk = kb_rows // LANES
        n_slices = keys_scr.shape[0] * slices_per_chunk
        n_sets = planes_scr.shape[2] // LANES

        def pack_rows(g, carry):
            r0 = pl.multiple_of(g * SUBLANES, SUBLANES)
            for st in range(n_sets):
                words = []
                for s in range(WORD_BITS):
                    sl = st * WORD_BITS + s
                    if sl < n_slices:
                        c, off = divmod(sl, slices_per_chunk)
                        words.append(keys_scr[c, pl.ds(r0, SUBLANES), off * LANES:(off + 1) * LANES])
                    else:
                        words.append(jnp.full((SUBLANES, LANES), INT_MIN, I32))
                words = _bit_transpose32(words)
                words[0] = ~words[0]
                for i in range(WORD_BITS):
                    planes_scr[i, pl.ds(r0, SUBLANES), st * LANES:(st + 1) * LANES] = words[i]
            return carry

        lax.fori_loop(0, qb_rows // SUBLANES, pack_rows, 0)

        lane_ones = jnp.ones((LANES, LANES), BF16)

        def row_total(pc):
            tot = pc[:, :LANES]
            for st in range(1, n_sets):
                tot = tot + pc[:, st * LANES:(st + 1) * LANES]
            return jnp.dot(tot.astype(F32).astype(BF16), lane_ones, preferred_element_type=F32)

        def select_bit(i, carry):
            cand, above, prefix = carry
            ones = cand & planes_scr[i]
            reach = above + row_total(lax.population_count(ones))
            take = reach >= k_top
            take_all = jnp.concatenate([take] * n_sets, axis=1)
            cand = jnp.where(take_all, ones, cand ^ ones)
            above = jnp.where(take, above, reach)
            prefix = jnp.where(take, prefix | (jnp.int32(1) << (31 - i)), prefix)
            return cand, above, prefix

        cand, above, prefix = lax.fori_loop(
            0, WORD_BITS, select_bit,
            (jnp.full((qb_rows, n_sets * LANES), -1, I32),
             jnp.zeros((qb_rows, LANES), F32), jnp.zeros((qb_rows, LANES), I32)))
        equal = row_total(lax.population_count(cand))
        thr_scr[...] = (prefix ^ INT_MIN)[:, :1]
        tie_scr[...] = jnp.full((qb_rows, 1), INT_MAX, I32)
        ngt_scr[...] = above[:, :1]
        nge_scr[...] = (above + equal)[:, :1]

        rg = min(qb_rows, SEARCH_ROWS)
        lane_col = lax.broadcasted_iota(I32, (rg, LANES), 1)

        for r in range(qb_rows // rg):
            rs = slice(r * rg, (r + 1) * rg)

            def count(pred, rs=rs):
                def body(c, acc):
                    for t in range(kb_rows // LANES):
                        kk = keys_scr[c, rs, t * LANES:(t + 1) * LANES]
                        idx = lane_col + (c * kb_rows + t * LANES)
                        acc = acc + jnp.where(pred(kk, idx), 1.0, 0.0)
                    return acc
                acc = lax.fori_loop(0, n_chunk, body, jnp.zeros((rg, LANES), F32))
                return jnp.sum(acc, axis=1, keepdims=True)

            def wide(v):
                return jnp.broadcast_to(v, (rg, LANES))

            @pl.when(jnp.max(nge_scr[rs, :]) > k_top)
            def _ties(count=count, rs=rs):
                thr_w = wide(thr_scr[rs, :])
                need = k_top - ngt_scr[rs, :]

                def tie_bit(i, jt):
                    cand = jt + (jnp.int32(1) << (idx_bits - 1 - i))
                    cand_w = wide(cand)
                    cnt = count(lambda kk, idx: (kk == thr_w) & (idx < cand_w))
                    return jnp.where(cnt < need, cand, jt)

                tie_scr[rs, :] = lax.fori_loop(0, idx_bits, tie_bit, jnp.zeros((rg, 1), I32))

        thr = thr_scr[...]
        tie = tie_scr[...]

        def bias_chunk(c, carry):
            kk = keys_scr[c]
            idx = col + c * kb_rows
            sel = (kk > thr) | ((kk == thr) & (idx <= tie))
            bias = jnp.where(sel & (idx <= row), 0.0, MASK_VALUE)
            keys_scr[c] = lax.bitcast_convert_type(bias, I32)
            return carry

        lax.fori_loop(0, n_chunk, bias_chunk, 0)
        m_scr[...] = jnp.full(m_scr.shape, M_INIT, F32)
        l_scr[...] = jnp.zeros_like(l_scr)
        acc_scr[...] = jnp.zeros_like(acc_scr)

    @pl.when(kb <= last_kb)
    def _attend():
        ones = jnp.ones((kb_rows, HEAD_DIM), BF16)

        def logits(h, s_ref, r_ref):
            bias = lax.bitcast_convert_type(keys_scr[kb], F32)
            s = lax.dot_general(q_ref[0, h], k_ref[0, h], nt, preferred_element_type=F32) + bias
            s_ref[...] = s
            r_ref[...] = jnp.broadcast_to(jnp.max(s, axis=1, keepdims=True), r_ref.shape)

        def accumulate(h, s_ref, r_ref):
            m_old = m_scr[h]
            m_new = jnp.maximum(m_old, r_ref[...])
            alpha = jnp.exp2(m_old - m_new)
            p = jnp.concatenate(
                [jnp.exp2(s_ref[:, t * LANES:(t + 1) * LANES] - m_new).astype(BF16)
                 for t in range(kb_rows // LANES)], axis=1)
            v_ext = jnp.concatenate([v_ref[0, h], ones], axis=1)
            pv = jnp.dot(p, v_ext, preferred_element_type=F32)
            acc_scr[h] = alpha * acc_scr[h] + pv[:, :HEAD_DIM]
            l_scr[h] = alpha * l_scr[h] + pv[:, HEAD_DIM:]
            m_scr[h] = m_new

        bufs = ((s0_scr, r0_scr), (s1_scr, r1_scr))
        logits(0, *bufs[0])
        for h in range(heads):
            if h + 1 < heads:
                logits(h + 1, *bufs[(h + 1) % 2])
            accumulate(h, *bufs[h % 2])

    @pl.when(kb == pl.num_programs(2) - 1)
    def _finish():
        for h in range(heads):
            o_ref[0, :, h * HEAD_DIM:(h + 1) * HEAD_DIM] = (acc_scr[h] / l_scr[h]).astype(o_ref.dtype)


def _dsa_attention(qkv, qi, kia, kib, wi, k_top):
    bsz, heads3, seq, _ = qkv.shape
    heads = heads3 // 3
    d = heads * HEAD_DIM
    n_pair = qi.shape[-1] // LANES
    qb_rows = _pick(seq, (256, 128))
    kb_rows = _pick(seq, (512, 256, 128))
    n_kb = seq // kb_rows
    n_sets = -(-seq // (WORD_BITS * LANES))

    def kv_block(qb, kb):
        return jnp.minimum(kb, (qb * qb_rows + qb_rows - 1) // kb_rows)

    kern = functools.partial(_dsa_kernel, n_pair=n_pair, k_top=k_top, idx_bits=seq.bit_length())
    return pl.pallas_call(
        kern,
        out_shape=jax.ShapeDtypeStruct((bsz, seq, d), BF16),
        grid=(bsz, seq // qb_rows, n_kb),
        in_specs=[
            pl.BlockSpec((1, heads, qb_rows, HEAD_DIM), lambda b, i, j: (b, 0, i, 0)),
            pl.BlockSpec((1, heads, kb_rows, HEAD_DIM), lambda b, i, j: (b, 1, kv_block(i, j), 0)),
            pl.BlockSpec((1, heads, kb_rows, HEAD_DIM), lambda b, i, j: (b, 2, kv_block(i, j), 0)),
            pl.BlockSpec((1, qb_rows, n_pair * LANES), lambda b, i, j: (b, i, 0)),
            pl.BlockSpec((1, seq, LANES), lambda b, i, j: (b, 0, 0)),
            pl.BlockSpec((1, seq, LANES), lambda b, i, j: (b, 0, 0)),
            pl.BlockSpec((1, qb_rows, LANES), lambda b, i, j: (b, i, 0)),
        ],
        out_specs=pl.BlockSpec((1, qb_rows, d), lambda b, i, j: (b, i, 0)),
        scratch_shapes=[
            pltpu.VMEM((n_kb, qb_rows, kb_rows), I32),
            pltpu.VMEM((WORD_BITS, qb_rows, n_sets * LANES), I32),
            pltpu.VMEM((qb_rows, 1), I32),
            pltpu.VMEM((qb_rows, 1), I32),
            pltpu.VMEM((qb_rows, 1), F32),
            pltpu.VMEM((qb_rows, 1), F32),
            pltpu.VMEM((heads, qb_rows, LANES), F32),
            pltpu.VMEM((heads, qb_rows, LANES), F32),
            pltpu.VMEM((heads, qb_rows, HEAD_DIM), F32),
            pltpu.VMEM((qb_rows, kb_rows), F32),
            pltpu.VMEM((qb_rows, kb_rows), F32),
            pltpu.VMEM((qb_rows, LANES), F32),
            pltpu.VMEM((qb_rows, LANES), F32),
        ],
        compiler_params=_params(("parallel", "parallel", "arbitrary")),
        name="dsa_select_attend",
    )(qkv, qkv, qkv, qi, kia, kib, wi)


def _rope_tables(positions, dim, repeats):
    inv = 1.0 / (ROPE_THETA ** (jnp.arange(0, dim, 2, dtype=F32) / dim))
    ang = positions.astype(F32)[..., None] * inv
    cos, sin = jnp.cos(ang), jnp.sin(ang)
    cos_t = jnp.concatenate([cos, cos] * repeats, axis=-1)
    sin_t = jnp.concatenate([-sin, sin] * repeats, axis=-1)
    return cos_t, sin_t


def _segment_major(x, seg):
    bsz, seq, d = x.shape
    y = x.reshape(bsz, seq // (SUBLANES * seg), SUBLANES, seg, d)
    return jnp.swapaxes(y, 2, 3).reshape(bsz, seq, d)


def _token_major(x, seg):
    bsz, seq, d = x.shape
    y = x.reshape(bsz, seq // (SUBLANES * seg), seg, SUBLANES, d)
    return jnp.swapaxes(y, 2, 3).reshape(bsz, seq, d)


def kernel(x, c, positions, ada_w, ada_b, ln_g, ln_b, s5_in_w, s5_a_re, s5_a_im, s5_log_dt, s5_b_re, s5_b_im, s5_c_re, s5_c_im, s5_d, s5_glu_w, s5_glu_b, dsa_in_w, dsa_out_w, ffn_w_in, ffn_w_out):
    bsz, seq, d = x.shape
    depth = ada_w.shape[0]
    alpha = (2.0 * depth) ** 0.25
    idx_heads = (dsa_in_w.shape[-1] - 3 * d - IDX_DIM) // (IDX_DIM + 1)
    k_top = min(TOPK_MAX, seq // 4)
    seg = min(S5_SEG, seq // SUBLANES)

    cos_h, sin_h = _rope_tables(positions, HEAD_DIM, 1)
    cos_i, sin_i = _rope_tables(positions, IDX_DIM, 2)
    mod = _ada_mod(c, ada_w, ada_b)

    for i in range(depth):
        sh1, sc1, g1, sh2, sc2, g2 = [m[:, None, :] for m in jnp.split(mod[i], 6, axis=-1)]
        j = i // 2
        if i % 2 == 0:
            xp = _segment_major(x, seg)
            prep = _s5_discretize(s5_a_re[j], s5_a_im[j], s5_log_dt[j], s5_b_re[j], s5_b_im[j],
                                  s5_c_re[j], s5_c_im[j], seg)
            u = _mod_matmul(xp, sc1, sh1, s5_in_w[j].astype(BF16), F32)
            gl = _s5_scan(u, prep, s5_d[j], seg)
            x1 = _matmul_res_ln(gl, s5_glu_w[j].astype(BF16), s5_glu_b[j], xp, g1,
                                ln_g[i, 0], ln_b[i, 0], alpha, glu=True)
        else:
            xp = x
            w = dsa_in_w[j]
            n_qi = idx_heads * IDX_DIM
            w_qkv = w[:, :3 * d].astype(BF16)
            pad = jnp.zeros((d, LANES - IDX_DIM - idx_heads), w.dtype)
            w_idx = jnp.concatenate(
                [w[:, 3 * d:3 * d + n_qi], w[:, 3 * d + n_qi + idx_heads:], w[:, 3 * d + n_qi:3 * d + n_qi + idx_heads], pad],
                axis=1).astype(BF16)
            qkv = _qkv_proj(xp, sc1, sh1, w_qkv, cos_h, sin_h)
            qi, kia, kib, wi = _idx_proj(xp, sc1, sh1, w_idx, cos_i, sin_i, idx_heads)
            att = _dsa_attention(qkv, qi, kia, kib, wi, k_top)
            x1 = _matmul_res_ln(att, dsa_out_w[j].astype(BF16), None, xp, g1,
                                ln_g[i, 0], ln_b[i, 0], alpha, glu=False)
        x2 = _ffn_res_ln(x1, sc2, sh2, g2, ffn_w_in[i].astype(BF16), ffn_w_out[i].astype(BF16),
                         ln_g[i, 1], ln_b[i, 1], alpha)
        x = _token_major(x2, seg) if i % 2 == 0 else x2
    return x
```

```python
import functools
import math

import jax
import jax.numpy as jnp
from jax import lax
from jax.experimental import pallas as pl
from jax.experimental.pallas import tpu as pltpu

F32 = jnp.float32
BF16 = jnp.bfloat16
I32 = jnp.int32

S5_GROUP = 16
S5_STATE = 64
HEAD_DIM = 128
IDX_DIM = 64
TOPK_MAX = 256
ROPE_THETA = 10000.0
LN_EPS = 1e-5

LANES = 128
SUBLANES = 8
VMEM_LIMIT_BYTES = 56 * 1024 * 1024

S5_PACK_GROUPS = 16
S5_SEG = 64

INT_MIN = -(2 ** 31)
INT_MAX = 2 ** 31 - 1
SEARCH_ROWS = 128
WORD_BITS = 32

MASK_VALUE = -2e30
M_INIT = -1e30


def _pick(n, cands):
    for c in cands:
        if n % c == 0:
            return c
    return n


def _params(sem):
    return pltpu.CompilerParams(dimension_semantics=sem, vmem_limit_bytes=VMEM_LIMIT_BYTES)


def _ada_kernel(c_ref, w_ref, b_ref, o_ref):
    ca = jax.nn.silu(c_ref[...]).astype(BF16)
    o_ref[0] = jnp.dot(ca, w_ref[0].astype(BF16), preferred_element_type=F32) + b_ref[0]


def _ada_mod(c, ada_w, ada_b):
    bsz, d = c.shape
    depth, _, n = ada_w.shape
    rows = SUBLANES * ((bsz + SUBLANES - 1) // SUBLANES)
    cp = jnp.zeros((rows, d), F32).at[:bsz].set(c)
    tn = _pick(n, (1024, 512, 256, 128))
    out = pl.pallas_call(
        _ada_kernel,
        out_shape=jax.ShapeDtypeStruct((depth, rows, n), F32),
        grid=(depth, n // tn),
        in_specs=[
            pl.BlockSpec((rows, d), lambda l, j: (0, 0)),
            pl.BlockSpec((1, d, tn), lambda l, j: (l, 0, j)),
            pl.BlockSpec((1, 1, tn), lambda l, j: (l, 0, j)),
        ],
        out_specs=pl.BlockSpec((1, rows, tn), lambda l, j: (l, 0, j)),
        compiler_params=_params(("arbitrary", "arbitrary")),
        name="ada_mod",
    )(cp, ada_w, ada_b.reshape(depth, 1, n))
    return out[:, :bsz]


def _modulate(x_ref, sc_ref, sh_ref):
    return (x_ref[0] * (1.0 + sc_ref[0]) + sh_ref[0]).astype(BF16)


def _modmm_kernel(x_ref, sc_ref, sh_ref, w_ref, o_ref, h_scr):
    @pl.when(pl.program_id(2) == 0)
    def _():
        h_scr[...] = _modulate(x_ref, sc_ref, sh_ref)

    o_ref[0] = jnp.dot(h_scr[...], w_ref[...], preferred_element_type=F32).astype(o_ref.dtype)


def _mod_matmul(x, sc, sh, w, out_dtype):
    bsz, seq, d = x.shape
    n = w.shape[1]
    tm = _pick(seq, (512, 256, 128))
    tn = _pick(n, (1024, 512, 256, 128))
    return pl.pallas_call(
        _modmm_kernel,
        out_shape=jax.ShapeDtypeStruct((bsz, seq, n), out_dtype),
        grid=(bsz, seq // tm, n // tn),
        in_specs=[
            pl.BlockSpec((1, tm, d), lambda b, i, j: (b, i, 0)),
            pl.BlockSpec((1, 1, d), lambda b, i, j: (b, 0, 0)),
            pl.BlockSpec((1, 1, d), lambda b, i, j: (b, 0, 0)),
            pl.BlockSpec((d, tn), lambda b, i, j: (0, j)),
        ],
        out_specs=pl.BlockSpec((1, tm, tn), lambda b, i, j: (b, i, j)),
        scratch_shapes=[pltpu.VMEM((tm, d), BF16)],
        compiler_params=_params(("parallel", "parallel", "arbitrary")),
        name="mod_matmul",
    )(x, sc, sh, w)


def _s5_kernel(u_ref, bb_ref, cb_ref, lre_ref, lim_ref, ltre_ref, ltim_ref, d_ref, o_ref,
               xs_scr, carry_scr, cin_scr, *, seg, width):
    ns = xs_scr.shape[1] // 2

    @pl.when(pl.program_id(2) == 0)
    def _():
        carry_scr[...] = jnp.zeros_like(carry_scr)

    u = u_ref[0]
    xs_scr[...] = jnp.dot(u.astype(BF16), bb_ref[0], preferred_element_type=F32)

    for part in range(ns // width):
        cr = slice(part * width, (part + 1) * width)
        ci = slice(ns + part * width, ns + (part + 1) * width)
        lr = jnp.broadcast_to(lre_ref[0, :, cr], (SUBLANES, width))
        li = jnp.broadcast_to(lim_ref[0, :, cr], (SUBLANES, width))

        def local_step(i, st, cr=cr, ci=ci, lr=lr, li=li):
            sr, si = st
            r0 = pl.multiple_of(i * SUBLANES, SUBLANES)
            nr = lr * sr - li * si + xs_scr[pl.ds(r0, SUBLANES), cr]
            ni = lr * si + li * sr + xs_scr[pl.ds(r0, SUBLANES), ci]
            xs_scr[pl.ds(r0, SUBLANES), cr] = nr
            xs_scr[pl.ds(r0, SUBLANES), ci] = ni
            return nr, ni

        zero = jnp.zeros((SUBLANES, width), F32)
        er, ei = lax.fori_loop(0, seg, local_step, (zero, zero), unroll=2)

        ltr = ltre_ref[0, :, cr]
        lti = ltim_ref[0, :, cr]
        c_r = carry_scr[:, cr]
        c_i = carry_scr[:, ci]
        for s in range(SUBLANES):
            cin_scr[s:s + 1, cr] = c_r
            cin_scr[s:s + 1, ci] = c_i
            e_r = er[s:s + 1, :]
            e_i = ei[s:s + 1, :]
            c_r, c_i = ltr * c_r - lti * c_i + e_r, ltr * c_i + lti * c_r + e_i
        carry_scr[:, cr] = c_r
        carry_scr[:, ci] = c_i

        def carry_step(i, st, cr=cr, ci=ci, lr=lr, li=li):
            pr, pi_ = st
            r0 = pl.multiple_of(i * SUBLANES, SUBLANES)
            nr = lr * pr - li * pi_
            ni = lr * pi_ + li * pr
            xs_scr[pl.ds(r0, SUBLANES), cr] += nr
            xs_scr[pl.ds(r0, SUBLANES), ci] += ni
            return nr, ni

        lax.fori_loop(0, seg, carry_step, (cin_scr[:, cr], cin_scr[:, ci]), unroll=2)

    y = jnp.dot(xs_scr[...].astype(BF16), cb_ref[0], preferred_element_type=F32)
    y = y + d_ref[0] * u
    o_ref[0] = jax.nn.gelu(y).astype(o_ref.dtype)


def _cmul(ar, ai, br, bi):
    return ar * br - ai * bi, ar * bi + ai * br


def _s5_discretize(a_re, a_im, log_dt, b_re, b_im, c_re, c_im, seg):
    g, n = a_re.shape
    p = b_re.shape[-1]
    pg = S5_PACK_GROUPS
    packs = g // pg
    a_re, a_im = a_re.astype(F32), a_im.astype(F32)
    dt = jnp.exp(log_dt.astype(F32))[:, None]
    mag = jnp.exp(a_re * dt)
    lb_re, lb_im = mag * jnp.cos(a_im * dt), mag * jnp.sin(a_im * dt)
    den = a_re * a_re + a_im * a_im
    nr, ni = lb_re - 1.0, lb_im
    f_re = (nr * a_re + ni * a_im) / den
    f_im = (ni * a_re - nr * a_im) / den
    bb_re, bb_im = _cmul(f_re[..., None], f_im[..., None], b_re.astype(F32), b_im.astype(F32))
    lt_re, lt_im = lb_re, lb_im
    for _ in range(int(math.log2(seg))):
        lt_re, lt_im = _cmul(lt_re, lt_im, lt_re, lt_im)
    eye = jnp.eye(pg, dtype=F32)

    def in_blk(m):
        return jnp.einsum("kgnp,gh->kgphn", m.reshape(packs, pg, n, p), eye).reshape(packs, pg * p, pg * n)

    def out_blk(m):
        return jnp.einsum("kgpn,gh->kgnhp", m.reshape(packs, pg, p, n), eye).reshape(packs, pg * n, pg * p)

    b_blk = jnp.concatenate([in_blk(bb_re), in_blk(bb_im)], axis=-1).astype(BF16)
    c_blk = jnp.concatenate([out_blk(c_re.astype(F32)), out_blk(-c_im.astype(F32))], axis=1).astype(BF16)

    def vec(m):
        return m.reshape(packs, 1, pg * n)

    return b_blk, c_blk, vec(lb_re), vec(lb_im), vec(lt_re), vec(lt_im)


def _s5_scan(u, prep, d_skip, seg):
    bsz, seq, d = u.shape
    b_blk, c_blk, lre, lim, ltre, ltim = prep
    packs, pw, ns2 = b_blk.shape
    ns = ns2 // 2
    rows = SUBLANES * seg
    width = _pick(ns, (512, 256, 128))
    kern = functools.partial(_s5_kernel, seg=seg, width=width)
    vspec = pl.BlockSpec((1, 1, ns), lambda b, k, m: (k, 0, 0))
    return pl.pallas_call(
        kern,
        out_shape=jax.ShapeDtypeStruct((bsz, seq, d), BF16),
        grid=(bsz, packs, seq // rows),
        in_specs=[
            pl.BlockSpec((1, rows, pw), lambda b, k, m: (b, m, k)),
            pl.BlockSpec((1, pw, ns2), lambda b, k, m: (k, 0, 0)),
            pl.BlockSpec((1, ns2, pw), lambda b, k, m: (k, 0, 0)),
            vspec, vspec, vspec, vspec,
            pl.BlockSpec((1, 1, pw), lambda b, k, m: (k, 0, 0)),
        ],
        out_specs=pl.BlockSpec((1, rows, pw), lambda b, k, m: (b, m, k)),
        scratch_shapes=[
            pltpu.VMEM((rows, ns2), F32),
            pltpu.VMEM((1, ns2), F32),
            pltpu.VMEM((SUBLANES, ns2), F32),
        ],
        compiler_params=_params(("parallel", "parallel", "arbitrary")),
        name="s5_scan",
    )(u, b_blk, c_blk, lre, lim, ltre, ltim, d_skip.reshape(packs, 1, pw).astype(F32))


def _layer_norm_rows(tiles, lng_ref, lnb_ref, o_ref, d):
    tn = tiles[0].shape[1]
    tot = tiles[0].sum(axis=1, keepdims=True)
    for t in tiles[1:]:
        tot = tot + t.sum(axis=1, keepdims=True)
    mu = tot * (1.0 / d)
    sq = jnp.square(tiles[0] - mu).sum(axis=1, keepdims=True)
    for t in tiles[1:]:
        sq = sq + jnp.square(t - mu).sum(axis=1, keepdims=True)
    inv = lax.rsqrt(sq * (1.0 / d) + LN_EPS)
    for k, t in enumerate(tiles):
        cs = slice(k * tn, (k + 1) * tn)
        o_ref[0, :, cs] = ((t - mu) * inv * lng_ref[:, cs] + lnb_ref[:, cs]).astype(o_ref.dtype)


def _mmln_kernel(*refs, glu, nj, alpha, d):
    if glu:
        a_ref, w1_ref, w2_ref, b1_ref, b2_ref, x_ref, g_ref, lng_ref, lnb_ref, o_ref, r_scr = refs
    else:
        a_ref, w1_ref, x_ref, g_ref, lng_ref, lnb_ref, o_ref, r_scr = refs
    j = pl.program_id(2)
    a = a_ref[0]
    y = jnp.dot(a, w1_ref[...], preferred_element_type=F32)
    if glu:
        y = y + b1_ref[...]
        gate = jnp.dot(a, w2_ref[...], preferred_element_type=F32) + b2_ref[...]
        y = y * jax.nn.sigmoid(gate)
    r_scr[j] = alpha * x_ref[0] + (1.0 + g_ref[0]) * y

    @pl.when(j == nj - 1)
    def _():
        _layer_norm_rows([r_scr[t] for t in range(nj)], lng_ref, lnb_ref, o_ref, d)


def _matmul_res_ln(a, w, bias, xres, gate, ln_g, ln_b, alpha, glu):
    bsz, seq, k = a.shape
    d = xres.shape[-1]
    tm = _pick(seq, (512, 256, 128))
    tn = _pick(d, (512, 256, 128))
    nj = d // tn
    a_spec = pl.BlockSpec((1, tm, k), lambda b, i, j: (b, i, 0))
    w1_spec = pl.BlockSpec((k, tn), lambda b, i, j: (0, j))
    tail_specs = [
        pl.BlockSpec((1, tm, tn), lambda b, i, j: (b, i, j)),
        pl.BlockSpec((1, 1, tn), lambda b, i, j: (b, 0, j)),
        pl.BlockSpec((1, d), lambda b, i, j: (0, 0)),
        pl.BlockSpec((1, d), lambda b, i, j: (0, 0)),
    ]
    tail = (xres, gate, ln_g.reshape(1, d).astype(F32), ln_b.reshape(1, d).astype(F32))
    if glu:
        in_specs = [a_spec, w1_spec,
                    pl.BlockSpec((k, tn), lambda b, i, j: (0, j + nj)),
                    pl.BlockSpec((1, tn), lambda b, i, j: (0, j)),
                    pl.BlockSpec((1, tn), lambda b, i, j: (0, j + nj))] + tail_specs
        b2d = bias.reshape(1, 2 * d).astype(F32)
        args = (a, w, w, b2d, b2d) + tail
    else:
        in_specs = [a_spec, w1_spec] + tail_specs
        args = (a, w) + tail
    kern = functools.partial(_mmln_kernel, glu=glu, nj=nj, alpha=alpha, d=d)
    return pl.pallas_call(
        kern,
        out_shape=jax.ShapeDtypeStruct((bsz, seq, d), F32),
        grid=(bsz, seq // tm, nj),
        in_specs=in_specs,
        out_specs=pl.BlockSpec((1, tm, d), lambda b, i, j: (b, i, 0)),
        scratch_shapes=[pltpu.VMEM((nj, tm, tn), F32)],
        compiler_params=_params(("parallel", "parallel", "arbitrary")),
        name="matmul_res_ln",
    )(*args)


def _ffn_kernel(x_ref, sc_ref, sh_ref, g_ref, wg_ref, wu_ref, wo_ref, lng_ref, lnb_ref, o_ref,
                h_scr, acc_scr, *, nf, alpha, d, tn):
    f = pl.program_id(2)

    @pl.when(f == 0)
    def _():
        h_scr[...] = _modulate(x_ref, sc_ref, sh_ref)
        acc_scr[...] = jnp.zeros_like(acc_scr)

    h = h_scr[...]
    a_g = jnp.dot(h, wg_ref[...], preferred_element_type=F32)
    a_u = jnp.dot(h, wu_ref[...], preferred_element_type=F32)
    act = (jax.nn.silu(a_g) * a_u).astype(BF16)
    acc_scr[...] += jnp.dot(act, wo_ref[...], preferred_element_type=F32)

    @pl.when(f == nf - 1)
    def _():
        tiles = []
        for k in range(d // tn):
            cs = slice(k * tn, (k + 1) * tn)
            tiles.append(alpha * x_ref[0, :, cs] + (1.0 + g_ref[0, :, cs]) * acc_scr[:, cs])
        _layer_norm_rows(tiles, lng_ref, lnb_ref, o_ref, d)


def _ffn_res_ln(x, sc, sh, gate, w_in, w_out, ln_g, ln_b, alpha):
    bsz, seq, d = x.shape
    dff = w_out.shape[0]
    tm = _pick(seq, (512, 256, 128))
    tf = _pick(dff, (512, 256, 128))
    nf = dff // tf
    tn = _pick(d, (512, 256, 128))
    vec = pl.BlockSpec((1, 1, d), lambda b, i, f: (b, 0, 0))
    kern = functools.partial(_ffn_kernel, nf=nf, alpha=alpha, d=d, tn=tn)
    return pl.pallas_call(
        kern,
        out_shape=jax.ShapeDtypeStruct((bsz, seq, d), F32),
        grid=(bsz, seq // tm, nf),
        in_specs=[
            pl.BlockSpec((1, tm, d), lambda b, i, f: (b, i, 0)),
            vec, vec, vec,
            pl.BlockSpec((d, tf), lambda b, i, f: (0, f)),
            pl.BlockSpec((d, tf), lambda b, i, f: (0, f + nf)),
            pl.BlockSpec((tf, d), lambda b, i, f: (f, 0)),
            pl.BlockSpec((1, d), lambda b, i, f: (0, 0)),
            pl.BlockSpec((1, d), lambda b, i, f: (0, 0)),
        ],
        out_specs=pl.BlockSpec((1, tm, d), lambda b, i, f: (b, i, 0)),
        scratch_shapes=[pltpu.VMEM((tm, d), BF16), pltpu.VMEM((tm, d), F32)],
        compiler_params=_params(("parallel", "parallel", "arbitrary")),
        name="ffn_res_ln",
    )(x, sc, sh, gate, w_in, w_in, w_out, ln_g.reshape(1, d).astype(F32), ln_b.reshape(1, d).astype(F32))


def _qkv_kernel(x_ref, sc_ref, sh_ref, w_ref, cos_ref, sin_ref, o_ref, h_scr, *, tiles_per_tensor, q_scale):
    j = pl.program_id(2)

    @pl.when(j == 0)
    def _():
        h_scr[...] = _modulate(x_ref, sc_ref, sh_ref)

    acc = jnp.dot(h_scr[...], w_ref[...], preferred_element_type=F32)
    heads = acc.shape[1] // HEAD_DIM
    tensor = j // tiles_per_tensor

    def write(rope, scale):
        for hh in range(heads):
            xc = acc[:, hh * HEAD_DIM:(hh + 1) * HEAD_DIM]
            if rope:
                xc = xc * cos_ref[0] + pltpu.roll(xc, HEAD_DIM // 2, 1) * sin_ref[0]
            if scale != 1.0:
                xc = xc * scale
            o_ref[0, hh] = xc.astype(o_ref.dtype)

    pl.when(tensor == 0)(lambda: write(True, q_scale))
    pl.when(tensor == 1)(lambda: write(True, 1.0))
    pl.when(tensor == 2)(lambda: write(False, 1.0))


def _qkv_proj(x, sc, sh, w_qkv, cos_t, sin_t):
    bsz, seq, d = x.shape
    tm = _pick(seq, (512, 256, 128))
    tn = _pick(d, (1024, 512, 256, 128))
    hpt = tn // HEAD_DIM
    kern = functools.partial(_qkv_kernel, tiles_per_tensor=d // tn, q_scale=HEAD_DIM ** -0.5 * math.log2(math.e))
    return pl.pallas_call(
        kern,
        out_shape=jax.ShapeDtypeStruct((bsz, 3 * d // HEAD_DIM, seq, HEAD_DIM), BF16),
        grid=(bsz, seq // tm, 3 * d // tn),
        in_specs=[
            pl.BlockSpec((1, tm, d), lambda b, i, j: (b, i, 0)),
            pl.BlockSpec((1, 1, d), lambda b, i, j: (b, 0, 0)),
            pl.BlockSpec((1, 1, d), lambda b, i, j: (b, 0, 0)),
            pl.BlockSpec((d, tn), lambda b, i, j: (0, j)),
            pl.BlockSpec((1, tm, HEAD_DIM), lambda b, i, j: (b, i, 0)),
            pl.BlockSpec((1, tm, HEAD_DIM), lambda b, i, j: (b, i, 0)),
        ],
        out_specs=pl.BlockSpec((1, hpt, tm, HEAD_DIM), lambda b, i, j: (b, j, i, 0)),
        scratch_shapes=[pltpu.VMEM((tm, d), BF16)],
        compiler_params=_params(("parallel", "parallel", "arbitrary")),
        name="dsa_qkv_proj",
    )(x, sc, sh, w_qkv, cos_t, sin_t)


def _idx_kernel(x_ref, sc_ref, sh_ref, w_ref, cos_ref, sin_ref, qi_ref, kia_ref, kib_ref, wi_ref,
                *, n_pair, idx_heads, w_scale):
    h = _modulate(x_ref, sc_ref, sh_ref)
    acc = jnp.dot(h, w_ref[...], preferred_element_type=F32)
    tm = acc.shape[0]
    cos = cos_ref[0]
    sin = sin_ref[0]
    lane = lax.broadcasted_iota(I32, (tm, LANES), 1)
    first_half = (lane % IDX_DIM) < IDX_DIM // 2

    def rope(xc):
        partner = jnp.where(first_half, pltpu.roll(xc, LANES - IDX_DIM // 2, 1), pltpu.roll(xc, IDX_DIM // 2, 1))
        return xc * cos + partner * sin

    for p in range(n_pair):
        cs = slice(p * LANES, (p + 1) * LANES)
        qi_ref[0, :, cs] = rope(acc[:, cs]).astype(qi_ref.dtype)
    last = acc[:, n_pair * LANES:]
    ka = jnp.where(lane < IDX_DIM, rope(last), 0.0)
    kia_ref[0] = ka.astype(kia_ref.dtype)
    kib_ref[0] = pltpu.roll(ka, IDX_DIM, 1).astype(kib_ref.dtype)
    wi_ref[0] = jnp.where(lane < idx_heads, pltpu.roll(last, LANES - IDX_DIM, 1), 0.0) * w_scale


def _idx_proj(x, sc, sh, w_idx, cos_t, sin_t, idx_heads):
    bsz, seq, d = x.shape
    n = w_idx.shape[1]
    n_pair = idx_heads // 2
    tm = _pick(seq, (512, 256, 128))
    kern = functools.partial(_idx_kernel, n_pair=n_pair, idx_heads=idx_heads,
                             w_scale=(idx_heads ** -0.5) * (IDX_DIM ** -0.5))
    row = lambda b, i: (b, i, 0)
    return pl.pallas_call(
        kern,
        out_shape=(
            jax.ShapeDtypeStruct((bsz, seq, n_pair * LANES), BF16),
            jax.ShapeDtypeStruct((bsz, seq, LANES), BF16),
            jax.ShapeDtypeStruct((bsz, seq, LANES), BF16),
            jax.ShapeDtypeStruct((bsz, seq, LANES), F32),
        ),
        grid=(bsz, seq // tm),
        in_specs=[
            pl.BlockSpec((1, tm, d), row),
            pl.BlockSpec((1, 1, d), lambda b, i: (b, 0, 0)),
            pl.BlockSpec((1, 1, d), lambda b, i: (b, 0, 0)),
            pl.BlockSpec((d, n), lambda b, i: (0, 0)),
            pl.BlockSpec((1, tm, LANES), row),
            pl.BlockSpec((1, tm, LANES), row),
        ],
        out_specs=(
            pl.BlockSpec((1, tm, n_pair * LANES), row),
            pl.BlockSpec((1, tm, LANES), row),
            pl.BlockSpec((1, tm, LANES), row),
            pl.BlockSpec((1, tm, LANES), row),
        ),
        compiler_params=_params(("parallel", "parallel")),
        name="dsa_idx_proj",
    )(x, sc, sh, w_idx, cos_t, sin_t)


def _bit_transpose32(words):
    a = list(words)
    j = WORD_BITS // 2
    mask = 0x0000FFFF
    while j:
        k = 0
        while k < WORD_BITS:
            t = (a[k] ^ lax.shift_right_logical(a[k + j], jnp.int32(j))) & jnp.int32(mask)
            a[k] = a[k] ^ t
            a[k + j] = a[k + j] ^ (t << j)
            k = (k + j + 1) & ~j
        j >>= 1
        if j:
            mask = (mask ^ (mask << j)) & 0xFFFFFFFF
    return a


def _dsa_kernel(qb_tab, kb_tab, q_ref, k_ref, v_ref, qi_ref, kia_ref, kib_ref, wi_ref, o_ref,
                keys_scr, planes_scr, thr_scr, tie_scr, ngt_scr, nge_scr,
                m_scr, l_scr, acc_scr, bias_scr, s0_scr, s1_scr, r0_scr, r1_scr,
                *, n_pair, k_top, idx_bits):
    heads, qb_rows, _ = acc_scr.shape
    kb_rows = keys_scr.shape[2]
    qb = qb_tab[pl.program_id(1)]
    kb = kb_tab[pl.program_id(1)]
    q0 = qb * qb_rows
    last_kb = (q0 + qb_rows - 1) // kb_rows
    n_chunk = last_kb + 1
    row = lax.broadcasted_iota(I32, (qb_rows, kb_rows), 0) + q0
    col = lax.broadcasted_iota(I32, (qb_rows, kb_rows), 1)
    nt = (((1,), (1,)), ((), ()))

    @pl.when(kb == 0)
    def _select():
        wi = wi_ref[0]

        def score_chunk(c, carry):
            k0 = pl.multiple_of(c * kb_rows, kb_rows)
            ka = kia_ref[0, pl.ds(k0, kb_rows), :]
            kbm = kib_ref[0, pl.ds(k0, kb_rows), :]
            sc = jnp.zeros((qb_rows, kb_rows), F32)
            for p in range(n_pair):
                qp = qi_ref[0, :, p * LANES:(p + 1) * LANES]
                sa = lax.dot_general(qp, ka, nt, preferred_element_type=F32)
                sb = lax.dot_general(qp, kbm, nt, preferred_element_type=F32)
                sc = sc + wi[:, 2 * p:2 * p + 1] * jnp.maximum(sa, 0.0)
                sc = sc + wi[:, 2 * p + 1:2 * p + 2] * jnp.maximum(sb, 0.0)
            bits = lax.bitcast_convert_type(sc, I32)
            key = bits ^ ((bits >> 31) & INT_MAX)
            keys_scr[c] = jnp.where(col + k0 <= row, key, INT_MIN)
            return carry

        lax.fori_loop(0, n_chunk, score_chunk, 0)

        def fill_chunk(c, carry):
            keys_scr[c] = jnp.full((qb_rows, kb_rows), INT_MIN, I32)
            return carry

        lax.fori_loop(n_chunk, keys_scr.shape[0], fill_chunk, 0)

        slices_per_chunk = kb_rows // LANES
        n_slices = keys_scr.shape[0] * slices_per_chunk
        n_sets = planes_scr.shape[2] // LANES

        def pack_rows(g, carry):
            r0 = pl.multiple_of(g * SUBLANES, SUBLANES)
            for st in range(n_sets):
                words = []
                for s in range(WORD_BITS):
                    sl = st * WORD_BITS + s
                    if sl < n_slices:
                        c, off = divmod(sl, slices_per_chunk)
                        words.append(keys_scr[c, pl.ds(r0, SUBLANES), off * LANES:(off + 1) * LANES])
                    else:
                        words.append(jnp.full((SUBLANES, LANES), INT_MIN, I32))
                words = _bit_transpose32(words)
                words[0] = ~words[0]
                for i in range(WORD_BITS):
                    planes_scr[i, pl.ds(r0, SUBLANES), st * LANES:(st + 1) * LANES] = words[i]
            return carry

        lax.fori_loop(0, qb_rows // SUBLANES, pack_rows, 0)

        lane_ones = jnp.ones((LANES, LANES), BF16)

        def row_total(pc):
            tot = pc[:, :LANES]
            for st in range(1, n_sets):
                tot = tot + pc[:, st * LANES:(st + 1) * LANES]
            return jnp.dot(tot.astype(F32).astype(BF16), lane_ones, preferred_element_type=F32)

        def all_sets(mask):
            return jnp.concatenate([mask] * n_sets, axis=1)

        def select_two_bits(i, carry):
            cand, above, prefix = carry
            hi = planes_scr[2 * i]
            lo = planes_scr[2 * i + 1]
            c1 = cand & hi
            c0 = cand ^ c1
            c11 = c1 & lo
            c10 = c1 ^ c11
            c01 = c0 & lo
            c00 = c0 ^ c01
            r11 = above + row_total(lax.population_count(c11))
            r10 = r11 + row_total(lax.population_count(c10))
            r01 = r10 + row_total(lax.population_count(c01))
            t11 = r11 >= k_top
            t10 = r10 >= k_top
            t01 = r01 >= k_top
            cand = jnp.where(all_sets(t11), c11,
                             jnp.where(all_sets(t10), c10, jnp.where(all_sets(t01), c01, c00)))
            above = jnp.where(t11, above, jnp.where(t10, r11, jnp.where(t01, r10, r01)))
            bits = jnp.where(t11, 3, jnp.where(t10, 2, jnp.where(t01, 1, 0)))
            prefix = prefix | (bits << (WORD_BITS - 2 - 2 * i))
            return cand, above, prefix

        cand, above, prefix = lax.fori_loop(
            0, WORD_BITS // 2, select_two_bits,
            (jnp.full((qb_rows, n_sets * LANES), -1, I32),
             jnp.zeros((qb_rows, LANES), F32), jnp.zeros((qb_rows, LANES), I32)))
        equal = row_total(lax.population_count(cand))
        thr_scr[...] = (prefix ^ INT_MIN)[:, :1]
        tie_scr[...] = jnp.full((qb_rows, 1), INT_MAX, I32)
        ngt_scr[...] = above[:, :1]
        nge_scr[...] = (above + equal)[:, :1]

        rg = min(qb_rows, SEARCH_ROWS)
        lane_col = lax.broadcasted_iota(I32, (rg, LANES), 1)

        for r in range(qb_rows // rg):
            rs = slice(r * rg, (r + 1) * rg)

            def count(pred, rs=rs):
                def body(c, acc):
                    for t in range(kb_rows // LANES):
                        kk = keys_scr[c, rs, t * LANES:(t + 1) * LANES]
                        idx = lane_col + (c * kb_rows + t * LANES)
                        acc = acc + jnp.where(pred(kk, idx), 1.0, 0.0)
                    return acc
                acc = lax.fori_loop(0, n_chunk, body, jnp.zeros((rg, LANES), F32))
                return jnp.sum(acc, axis=1, keepdims=True)

            def wide(v):
                return jnp.broadcast_to(v, (rg, LANES))

            @pl.when(jnp.max(nge_scr[rs, :]) > k_top)
            def _ties(count=count, rs=rs):
                thr_w = wide(thr_scr[rs, :])
                need = k_top - ngt_scr[rs, :]

                def tie_bit(i, jt):
                    cand = jt + (jnp.int32(1) << (idx_bits - 1 - i))
                    cand_w = wide(cand)
                    cnt = count(lambda kk, idx: (kk == thr_w) & (idx < cand_w))
                    return jnp.where(cnt < need, cand, jt)

                tie_scr[rs, :] = lax.fori_loop(0, idx_bits, tie_bit, jnp.zeros((rg, 1), I32))

        m_scr[...] = jnp.full(m_scr.shape, M_INIT, F32)
        l_scr[...] = jnp.zeros_like(l_scr)
        acc_scr[...] = jnp.zeros_like(acc_scr)

    def _attend():
        ones = jnp.ones((kb_rows, HEAD_DIM), BF16)

        kk = keys_scr[kb]
        thr = thr_scr[...]
        lim = jnp.minimum(tie_scr[...], row[:, :1])
        tie_bias = jnp.where(col + kb * kb_rows <= lim, 0.0, MASK_VALUE)
        bias_scr[...] = jnp.where(kk > thr, 0.0, jnp.where(kk == thr, tie_bias, MASK_VALUE))

        def logits(h, s_ref, r_ref):
            s = lax.dot_general(q_ref[0, h], k_ref[0, h], nt, preferred_element_type=F32) + bias_scr[...]
            s_ref[...] = s
            r_ref[...] = jnp.broadcast_to(jnp.max(s, axis=1, keepdims=True), r_ref.shape)

        def accumulate(h, s_ref, r_ref):
            m_old = m_scr[h]
            m_new = jnp.maximum(m_old, r_ref[...])
            alpha = jnp.exp2(m_old - m_new)
            p = jnp.concatenate(
                [jnp.exp2(s_ref[:, t * LANES:(t + 1) * LANES] - m_new).astype(BF16)
                 for t in range(kb_rows // LANES)], axis=1)
            v_ext = jnp.concatenate([v_ref[0, h], ones], axis=1)
            pv = jnp.dot(p, v_ext, preferred_element_type=F32)
            acc_scr[h] = alpha * acc_scr[h] + pv[:, :HEAD_DIM]
            l_scr[h] = alpha * l_scr[h] + pv[:, HEAD_DIM:]
            m_scr[h] = m_new

        bufs = ((s0_scr, r0_scr), (s1_scr, r1_scr))
        logits(0, *bufs[0])
        for h in range(heads):
            if h + 1 < heads:
                logits(h + 1, *bufs[(h + 1) % 2])
            accumulate(h, *bufs[h % 2])

    _attend()

    @pl.when(kb == last_kb)
    def _finish():
        for h in range(heads):
            o_ref[0, :, h * HEAD_DIM:(h + 1) * HEAD_DIM] = (acc_scr[h] / l_scr[h]).astype(o_ref.dtype)


def _dsa_attention(qkv, qi, kia, kib, wi, k_top):
    bsz, heads3, seq, _ = qkv.shape
    heads = heads3 // 3
    d = heads * HEAD_DIM
    n_pair = qi.shape[-1] // LANES
    qb_rows = _pick(seq, (256, 128))
    kb_rows = _pick(seq, (512, 256, 128))
    n_kb = seq // kb_rows
    n_sets = -(-seq // (WORD_BITS * LANES))

    pairs = [(i, j) for i in range(seq // qb_rows) for j in range((i * qb_rows + qb_rows - 1) // kb_rows + 1)]
    qb_tab = jnp.asarray([p[0] for p in pairs], I32)
    kb_tab = jnp.asarray([p[1] for p in pairs], I32)

    kern = functools.partial(_dsa_kernel, n_pair=n_pair, k_top=k_top, idx_bits=seq.bit_length())
    grid_spec = pltpu.PrefetchScalarGridSpec(
        num_scalar_prefetch=2,
        grid=(bsz, len(pairs)),
        in_specs=[
            pl.BlockSpec((1, heads, qb_rows, HEAD_DIM), lambda b, s, qt, kt: (b, 0, qt[s], 0)),
            pl.BlockSpec((1, heads, kb_rows, HEAD_DIM), lambda b, s, qt, kt: (b, 1, kt[s], 0)),
            pl.BlockSpec((1, heads, kb_rows, HEAD_DIM), lambda b, s, qt, kt: (b, 2, kt[s], 0)),
            pl.BlockSpec((1, qb_rows, n_pair * LANES), lambda b, s, qt, kt: (b, qt[s], 0)),
            pl.BlockSpec((1, seq, LANES), lambda b, s, qt, kt: (b, 0, 0)),
            pl.BlockSpec((1, seq, LANES), lambda b, s, qt, kt: (b, 0, 0)),
            pl.BlockSpec((1, qb_rows, LANES), lambda b, s, qt, kt: (b, qt[s], 0)),
        ],
        out_specs=pl.BlockSpec((1, qb_rows, d), lambda b, s, qt, kt: (b, qt[s], 0)),
        scratch_shapes=[
            pltpu.VMEM((n_kb, qb_rows, kb_rows), I32),
            pltpu.VMEM((WORD_BITS, qb_rows, n_sets * LANES), I32),
            pltpu.VMEM((qb_rows, 1), I32),
            pltpu.VMEM((qb_rows, 1), I32),
            pltpu.VMEM((qb_rows, 1), F32),
            pltpu.VMEM((qb_rows, 1), F32),
            pltpu.VMEM((heads, qb_rows, LANES), F32),
            pltpu.VMEM((heads, qb_rows, LANES), F32),
            pltpu.VMEM((heads, qb_rows, HEAD_DIM), F32),
            pltpu.VMEM((qb_rows, kb_rows), F32),
            pltpu.VMEM((qb_rows, kb_rows), F32),
            pltpu.VMEM((qb_rows, kb_rows), F32),
            pltpu.VMEM((qb_rows, LANES), F32),
            pltpu.VMEM((qb_rows, LANES), F32),
        ],
    )
    return pl.pallas_call(
        kern,
        out_shape=jax.ShapeDtypeStruct((bsz, seq, d), BF16),
        grid_spec=grid_spec,
        compiler_params=_params(("parallel", "arbitrary")),
        name="dsa_select_attend",
    )(qb_tab, kb_tab, qkv, qkv, qkv, qi, kia, kib, wi)


def _rope_tables(positions, dim, repeats):
    inv = 1.0 / (ROPE_THETA ** (jnp.arange(0, dim, 2, dtype=F32) / dim))
    ang = positions.astype(F32)[..., None] * inv
    cos, sin = jnp.cos(ang), jnp.sin(ang)
    cos_t = jnp.concatenate([cos, cos] * repeats, axis=-1)
    sin_t = jnp.concatenate([-sin, sin] * repeats, axis=-1)
    return cos_t, sin_t


def _segment_major(x, seg):
    bsz, seq, d = x.shape
    y = x.reshape(bsz, seq // (SUBLANES * seg), SUBLANES, seg, d)
    return jnp.swapaxes(y, 2, 3).reshape(bsz, seq, d)


def _token_major(x, seg):
    bsz, seq, d = x.shape
    y = x.reshape(bsz, seq // (SUBLANES * seg), seg, SUBLANES, d)
    return jnp.swapaxes(y, 2, 3).reshape(bsz, seq, d)


def kernel(x, c, positions, ada_w, ada_b, ln_g, ln_b, s5_in_w, s5_a_re, s5_a_im, s5_log_dt, s5_b_re, s5_b_im, s5_c_re, s5_c_im, s5_d, s5_glu_w, s5_glu_b, dsa_in_w, dsa_out_w, ffn_w_in, ffn_w_out):
    bsz, seq, d = x.shape
    depth = ada_w.shape[0]
    alpha = (2.0 * depth) ** 0.25
    idx_heads = (dsa_in_w.shape[-1] - 3 * d - IDX_DIM) // (IDX_DIM + 1)
    k_top = min(TOPK_MAX, seq // 4)
    seg = min(S5_SEG, seq // SUBLANES)

    cos_h, sin_h = _rope_tables(positions, HEAD_DIM, 1)
    cos_i, sin_i = _rope_tables(positions, IDX_DIM, 2)
    mod = _ada_mod(c, ada_w, ada_b)

    for i in range(depth):
        sh1, sc1, g1, sh2, sc2, g2 = [m[:, None, :] for m in jnp.split(mod[i], 6, axis=-1)]
        j = i // 2
        if i % 2 == 0:
            xp = _segment_major(x, seg)
            prep = _s5_discretize(s5_a_re[j], s5_a_im[j], s5_log_dt[j], s5_b_re[j], s5_b_im[j],
                                  s5_c_re[j], s5_c_im[j], seg)
            u = _mod_matmul(xp, sc1, sh1, s5_in_w[j].astype(BF16), F32)
            gl = _s5_scan(u, prep, s5_d[j], seg)
            x1 = _matmul_res_ln(gl, s5_glu_w[j].astype(BF16), s5_glu_b[j], xp, g1,
                                ln_g[i, 0], ln_b[i, 0], alpha, glu=True)
        else:
            xp = x
            w = dsa_in_w[j]
            n_qi = idx_heads * IDX_DIM
            w_qkv = w[:, :3 * d].astype(BF16)
            pad = jnp.zeros((d, LANES - IDX_DIM - idx_heads), w.dtype)
            w_idx = jnp.concatenate(
                [w[:, 3 * d:3 * d + n_qi], w[:, 3 * d + n_qi + idx_heads:], w[:, 3 * d + n_qi:3 * d + n_qi + idx_heads], pad],
                axis=1).astype(BF16)
            qkv = _qkv_proj(xp, sc1, sh1, w_qkv, cos_h, sin_h)
            qi, kia, kib, wi = _idx_proj(xp, sc1, sh1, w_idx, cos_i, sin_i, idx_heads)
            att = _dsa_attention(qkv, qi, kia, kib, wi, k_top)
            x1 = _matmul_res_ln(att, dsa_out_w[j].astype(BF16), None, xp, g1,
                                ln_g[i, 0], ln_b[i, 0], alpha, glu=False)
        x2 = _ffn_res_ln(x1, sc2, sh2, g2, ffn_w_in[i].astype(BF16), ffn_w_out[i].astype(BF16),
                         ln_g[i, 1], ln_b[i, 1], alpha)
        x = _token_major(x2, seg) if i % 2 == 0 else x2
    return x
```

```python
import functools
import math

import jax
import jax.numpy as jnp
from jax import lax
from jax.experimental import pallas as pl
from jax.experimental.pallas import tpu as pltpu

F32 = jnp.float32
BF16 = jnp.bfloat16
I32 = jnp.int32

S5_GROUP = 16
S5_STATE = 64
HEAD_DIM = 128
IDX_DIM = 64
TOPK_MAX = 256
ROPE_THETA = 10000.0
LN_EPS = 1e-5

LANES = 128
SUBLANES = 8
VMEM_LIMIT_BYTES = 56 * 1024 * 1024

S5_PACK_GROUPS = 16
S5_SEG = 64

INT_MIN = -(2 ** 31)
INT_MAX = 2 ** 31 - 1
SEARCH_ROWS = 128
WORD_BITS = 32

MASK_VALUE = -2e30
M_INIT = -1e30


def _pick(n, cands):
    for c in cands:
        if n % c == 0:
            return c
    return n


def _row_halves(rows):
    half = rows // 2
    return (slice(0, half), slice(half, rows))


def _params(sem):
    return pltpu.CompilerParams(dimension_semantics=sem, vmem_limit_bytes=VMEM_LIMIT_BYTES)


def _ada_kernel(c_ref, w_ref, b_ref, o_ref):
    ca = jax.nn.silu(c_ref[...]).astype(BF16)
    o_ref[0] = jnp.dot(ca, w_ref[0].astype(BF16), preferred_element_type=F32) + b_ref[0]


def _ada_mod(c, ada_w, ada_b):
    bsz, d = c.shape
    depth, _, n = ada_w.shape
    rows = SUBLANES * ((bsz + SUBLANES - 1) // SUBLANES)
    cp = jnp.zeros((rows, d), F32).at[:bsz].set(c)
    tn = _pick(n, (1024, 512, 256, 128))
    out = pl.pallas_call(
        _ada_kernel,
        out_shape=jax.ShapeDtypeStruct((depth, rows, n), F32),
        grid=(depth, n // tn),
        in_specs=[
            pl.BlockSpec((rows, d), lambda l, j: (0, 0)),
            pl.BlockSpec((1, d, tn), lambda l, j: (l, 0, j)),
            pl.BlockSpec((1, 1, tn), lambda l, j: (l, 0, j)),
        ],
        out_specs=pl.BlockSpec((1, rows, tn), lambda l, j: (l, 0, j)),
        compiler_params=_params(("arbitrary", "arbitrary")),
        name="ada_mod",
    )(cp, ada_w, ada_b.reshape(depth, 1, n))
    return out[:, :bsz]


def _modulate(x_ref, sc_ref, sh_ref):
    return (x_ref[0] * (1.0 + sc_ref[0]) + sh_ref[0]).astype(BF16)


def _modmm_kernel(x_ref, sc_ref, sh_ref, w_ref, o_ref, h_scr):
    @pl.when(pl.program_id(2) == 0)
    def _():
        h_scr[...] = _modulate(x_ref, sc_ref, sh_ref)

    for rs in _row_halves(h_scr.shape[0]):
        o_ref[0, rs, :] = jnp.dot(h_scr[rs, :], w_ref[...], preferred_element_type=F32).astype(o_ref.dtype)


def _mod_matmul(x, sc, sh, w, out_dtype):
    bsz, seq, d = x.shape
    n = w.shape[1]
    tm = _pick(seq, (512, 256, 128))
    tn = _pick(n, (1024, 512, 256, 128))
    return pl.pallas_call(
        _modmm_kernel,
        out_shape=jax.ShapeDtypeStruct((bsz, seq, n), out_dtype),
        grid=(bsz, seq // tm, n // tn),
        in_specs=[
            pl.BlockSpec((1, tm, d), lambda b, i, j: (b, i, 0)),
            pl.BlockSpec((1, 1, d), lambda b, i, j: (b, 0, 0)),
            pl.BlockSpec((1, 1, d), lambda b, i, j: (b, 0, 0)),
            pl.BlockSpec((d, tn), lambda b, i, j: (0, j)),
        ],
        out_specs=pl.BlockSpec((1, tm, tn), lambda b, i, j: (b, i, j)),
        scratch_shapes=[pltpu.VMEM((tm, d), BF16)],
        compiler_params=_params(("parallel", "parallel", "arbitrary")),
        name="mod_matmul",
    )(x, sc, sh, w)


def _s5_kernel(*refs, seg, width, lane_tiles):
    u_refs = refs[:lane_tiles]
    (bb_ref, cb_ref, lre_ref, lim_ref, ltre_ref, ltim_ref, d_ref, o_ref,
     xs_scr, carry_scr, cin_scr, up_scr) = refs[lane_tiles:]
    ns = xs_scr.shape[1] // 2

    @pl.when(pl.program_id(2) == 0)
    def _():
        carry_scr[...] = jnp.zeros_like(carry_scr)

    for t, u_ref in enumerate(u_refs):
        for i in range(seg):
            up_scr[t, i * SUBLANES:(i + 1) * SUBLANES, :] = u_ref[0, pl.ds(i, SUBLANES, stride=seg), :]
    u = jnp.concatenate([up_scr[t] for t in range(len(u_refs))], axis=1)
    halves = _row_halves(u.shape[0])
    for rs in halves:
        xs_scr[rs, :] = jnp.dot(u[rs].astype(BF16), bb_ref[0], preferred_element_type=F32)

    for part in range(ns // width):
        cr = slice(part * width, (part + 1) * width)
        ci = slice(ns + part * width, ns + (part + 1) * width)
        lr = jnp.broadcast_to(lre_ref[0, :, cr], (SUBLANES, width))
        li = jnp.broadcast_to(lim_ref[0, :, cr], (SUBLANES, width))

        def local_step(i, st, cr=cr, ci=ci, lr=lr, li=li):
            sr, si = st
            r0 = pl.multiple_of(i * SUBLANES, SUBLANES)
            nr = lr * sr - li * si + xs_scr[pl.ds(r0, SUBLANES), cr]
            ni = lr * si + li * sr + xs_scr[pl.ds(r0, SUBLANES), ci]
            xs_scr[pl.ds(r0, SUBLANES), cr] = nr
            xs_scr[pl.ds(r0, SUBLANES), ci] = ni
            return nr, ni

        zero = jnp.zeros((SUBLANES, width), F32)
        er, ei = lax.fori_loop(0, seg, local_step, (zero, zero), unroll=2)

        ltr = ltre_ref[0, :, cr]
        lti = ltim_ref[0, :, cr]
        c_r = carry_scr[:, cr]
        c_i = carry_scr[:, ci]
        for s in range(SUBLANES):
            cin_scr[s:s + 1, cr] = c_r
            cin_scr[s:s + 1, ci] = c_i
            e_r = er[s:s + 1, :]
            e_i = ei[s:s + 1, :]
            c_r, c_i = ltr * c_r - lti * c_i + e_r, ltr * c_i + lti * c_r + e_i
        carry_scr[:, cr] = c_r
        carry_scr[:, ci] = c_i

        def carry_step(i, st, cr=cr, ci=ci, lr=lr, li=li):
            pr, pi_ = st
            r0 = pl.multiple_of(i * SUBLANES, SUBLANES)
            nr = lr * pr - li * pi_
            ni = lr * pi_ + li * pr
            xs_scr[pl.ds(r0, SUBLANES), cr] += nr
            xs_scr[pl.ds(r0, SUBLANES), ci] += ni
            return nr, ni

        lax.fori_loop(0, seg, carry_step, (cin_scr[:, cr], cin_scr[:, ci]), unroll=2)

    for rs in halves:
        y = jnp.dot(xs_scr[rs, :].astype(BF16), cb_ref[0], preferred_element_type=F32)
        g = jax.nn.gelu(y + d_ref[0] * u[rs])
        for t in range(len(u_refs)):
            up_scr[t, rs, :] = g[:, t * LANES:(t + 1) * LANES]
    for t in range(len(u_refs)):
        for s in range(SUBLANES):
            o_ref[0, s * seg:(s + 1) * seg, t * LANES:(t + 1) * LANES] = (
                up_scr[t, pl.ds(s, seg, stride=SUBLANES), :].astype(o_ref.dtype))


def _cmul(ar, ai, br, bi):
    return ar * br - ai * bi, ar * bi + ai * br


def _s5_discretize(a_re, a_im, log_dt, b_re, b_im, c_re, c_im, seg):
    g, n = a_re.shape
    p = b_re.shape[-1]
    pg = S5_PACK_GROUPS
    packs = g // pg
    a_re, a_im = a_re.astype(F32), a_im.astype(F32)
    dt = jnp.exp(log_dt.astype(F32))[:, None]
    mag = jnp.exp(a_re * dt)
    lb_re, lb_im = mag * jnp.cos(a_im * dt), mag * jnp.sin(a_im * dt)
    den = a_re * a_re + a_im * a_im
    nr, ni = lb_re - 1.0, lb_im
    f_re = (nr * a_re + ni * a_im) / den
    f_im = (ni * a_re - nr * a_im) / den
    bb_re, bb_im = _cmul(f_re[..., None], f_im[..., None], b_re.astype(F32), b_im.astype(F32))
    lt_re, lt_im = lb_re, lb_im
    for _ in range(int(math.log2(seg))):
        lt_re, lt_im = _cmul(lt_re, lt_im, lt_re, lt_im)
    eye = jnp.eye(pg, dtype=F32)

    def in_blk(m):
        return jnp.einsum("kgnp,gh->kgphn", m.reshape(packs, pg, n, p), eye).reshape(packs, pg * p, pg * n)

    def out_blk(m):
        return jnp.einsum("kgpn,gh->kgnhp", m.reshape(packs, pg, p, n), eye).reshape(packs, pg * n, pg * p)

    b_blk = jnp.concatenate([in_blk(bb_re), in_blk(bb_im)], axis=-1).astype(BF16)
    c_blk = jnp.concatenate([out_blk(c_re.astype(F32)), out_blk(-c_im.astype(F32))], axis=1).astype(BF16)

    def vec(m):
        return m.reshape(packs, 1, pg * n)

    return b_blk, c_blk, vec(lb_re), vec(lb_im), vec(lt_re), vec(lt_im)


def _s5_scan(u, prep, d_skip, seg):
    bsz, seq, d = u.shape
    b_blk, c_blk, lre, lim, ltre, ltim = prep
    packs, pw, ns2 = b_blk.shape
    ns = ns2 // 2
    rows = SUBLANES * seg
    width = _pick(ns, (1024, 512, 256, 128))
    lane_tiles = pw // LANES
    kern = functools.partial(_s5_kernel, seg=seg, width=width, lane_tiles=lane_tiles)
    vspec = pl.BlockSpec((1, 1, ns), lambda b, k, m: (k, 0, 0))
    u_specs = [pl.BlockSpec((1, rows, LANES), lambda b, k, m, t=t: (b, m, k * lane_tiles + t))
               for t in range(lane_tiles)]
    return pl.pallas_call(
        kern,
        out_shape=jax.ShapeDtypeStruct((bsz, seq, d), BF16),
        grid=(bsz, packs, seq // rows),
        in_specs=u_specs + [
            pl.BlockSpec((1, pw, ns2), lambda b, k, m: (k, 0, 0)),
            pl.BlockSpec((1, ns2, pw), lambda b, k, m: (k, 0, 0)),
            vspec, vspec, vspec, vspec,
            pl.BlockSpec((1, 1, pw), lambda b, k, m: (k, 0, 0)),
        ],
        out_specs=pl.BlockSpec((1, rows, pw), lambda b, k, m: (b, m, k)),
        scratch_shapes=[
            pltpu.VMEM((rows, ns2), F32),
            pltpu.VMEM((1, ns2), F32),
            pltpu.VMEM((SUBLANES, ns2), F32),
            pltpu.VMEM((lane_tiles, rows, LANES), F32),
        ],
        compiler_params=_params(("parallel", "parallel", "arbitrary")),
        name="s5_scan",
    )(*([u] * lane_tiles), b_blk, c_blk, lre, lim, ltre, ltim, d_skip.reshape(packs, 1, pw).astype(F32))


def _layer_norm_rows(tiles, lng_ref, lnb_ref, o_ref, d):
    tn = tiles[0].shape[1]
    tot = tiles[0].sum(axis=1, keepdims=True)
    for t in tiles[1:]:
        tot = tot + t.sum(axis=1, keepdims=True)
    mu = tot * (1.0 / d)
    sq = jnp.square(tiles[0] - mu).sum(axis=1, keepdims=True)
    for t in tiles[1:]:
        sq = sq + jnp.square(t - mu).sum(axis=1, keepdims=True)
    inv = lax.rsqrt(sq * (1.0 / d) + LN_EPS)
    for k, t in enumerate(tiles):
        cs = slice(k * tn, (k + 1) * tn)
        o_ref[0, :, cs] = ((t - mu) * inv * lng_ref[:, cs] + lnb_ref[:, cs]).astype(o_ref.dtype)


def _mmln_kernel(*refs, glu, nj, alpha, d):
    if glu:
        a_ref, w1_ref, w2_ref, b1_ref, b2_ref, x_ref, g_ref, lng_ref, lnb_ref, o_ref, r_scr = refs
    else:
        a_ref, w1_ref, x_ref, g_ref, lng_ref, lnb_ref, o_ref, r_scr = refs
    j = pl.program_id(2)
    for rs in _row_halves(a_ref.shape[1]):
        a = a_ref[0, rs, :]
        y = jnp.dot(a, w1_ref[...], preferred_element_type=F32)
        if glu:
            y = y + b1_ref[...]
            gate = jnp.dot(a, w2_ref[...], preferred_element_type=F32) + b2_ref[...]
            y = y * jax.nn.sigmoid(gate)
        r_scr[j, rs, :] = alpha * x_ref[0, rs, :] + (1.0 + g_ref[0]) * y

    @pl.when(j == nj - 1)
    def _():
        _layer_norm_rows([r_scr[t] for t in range(nj)], lng_ref, lnb_ref, o_ref, d)


def _matmul_res_ln(a, w, bias, xres, gate, ln_g, ln_b, alpha, glu):
    bsz, seq, k = a.shape
    d = xres.shape[-1]
    tm = _pick(seq, (512, 256, 128))
    tn = _pick(d, (512, 256, 128))
    nj = d // tn
    a_spec = pl.BlockSpec((1, tm, k), lambda b, i, j: (b, i, 0))
    w1_spec = pl.BlockSpec((k, tn), lambda b, i, j: (0, j))
    tail_specs = [
        pl.BlockSpec((1, tm, tn), lambda b, i, j: (b, i, j)),
        pl.BlockSpec((1, 1, tn), lambda b, i, j: (b, 0, j)),
        pl.BlockSpec((1, d), lambda b, i, j: (0, 0)),
        pl.BlockSpec((1, d), lambda b, i, j: (0, 0)),
    ]
    tail = (xres, gate, ln_g.reshape(1, d).astype(F32), ln_b.reshape(1, d).astype(F32))
    if glu:
        in_specs = [a_spec, w1_spec,
                    pl.BlockSpec((k, tn), lambda b, i, j: (0, j + nj)),
                    pl.BlockSpec((1, tn), lambda b, i, j: (0, j)),
                    pl.BlockSpec((1, tn), lambda b, i, j: (0, j + nj))] + tail_specs
        b2d = bias.reshape(1, 2 * d).astype(F32)
        args = (a, w, w, b2d, b2d) + tail
    else:
        in_specs = [a_spec, w1_spec] + tail_specs
        args = (a, w) + tail
    kern = functools.partial(_mmln_kernel, glu=glu, nj=nj, alpha=alpha, d=d)
    return pl.pallas_call(
        kern,
        out_shape=jax.ShapeDtypeStruct((bsz, seq, d), F32),
        grid=(bsz, seq // tm, nj),
        in_specs=in_specs,
        out_specs=pl.BlockSpec((1, tm, d), lambda b, i, j: (b, i, 0)),
        scratch_shapes=[pltpu.VMEM((nj, tm, tn), F32)],
        compiler_params=_params(("parallel", "parallel", "arbitrary")),
        name="matmul_res_ln",
    )(*args)


def _ffn_kernel(x_ref, sc_ref, sh_ref, g_ref, wg_ref, wu_ref, wo_ref, lng_ref, lnb_ref, o_ref,
                h_scr, acc_scr, *, nf, alpha, d, tn):
    f = pl.program_id(2)

    @pl.when(f == 0)
    def _():
        h_scr[...] = _modulate(x_ref, sc_ref, sh_ref)
        acc_scr[...] = jnp.zeros_like(acc_scr)

    h = h_scr[...]
    a_g = jnp.dot(h, wg_ref[...], preferred_element_type=F32)
    a_u = jnp.dot(h, wu_ref[...], preferred_element_type=F32)
    act = (jax.nn.silu(a_g) * a_u).astype(BF16)
    acc_scr[...] += jnp.dot(act, wo_ref[...], preferred_element_type=F32)

    @pl.when(f == nf - 1)
    def _():
        tiles = []
        for k in range(d // tn):
            cs = slice(k * tn, (k + 1) * tn)
            tiles.append(alpha * x_ref[0, :, cs] + (1.0 + g_ref[0, :, cs]) * acc_scr[:, cs])
        _layer_norm_rows(tiles, lng_ref, lnb_ref, o_ref, d)


def _ffn_res_ln(x, sc, sh, gate, w_in, w_out, ln_g, ln_b, alpha):
    bsz, seq, d = x.shape
    dff = w_out.shape[0]
    tm = _pick(seq, (512, 256, 128))
    tf = _pick(dff, (512, 256, 128))
    nf = dff // tf
    tn = _pick(d, (512, 256, 128))
    vec = pl.BlockSpec((1, 1, d), lambda b, i, f: (b, 0, 0))
    kern = functools.partial(_ffn_kernel, nf=nf, alpha=alpha, d=d, tn=tn)
    return pl.pallas_call(
        kern,
        out_shape=jax.ShapeDtypeStruct((bsz, seq, d), F32),
        grid=(bsz, seq // tm, nf),
        in_specs=[
            pl.BlockSpec((1, tm, d), lambda b, i, f: (b, i, 0)),
            vec, vec, vec,
            pl.BlockSpec((d, tf), lambda b, i, f: (0, f)),
            pl.BlockSpec((d, tf), lambda b, i, f: (0, f + nf)),
            pl.BlockSpec((tf, d), lambda b, i, f: (f, 0)),
            pl.BlockSpec((1, d), lambda b, i, f: (0, 0)),
            pl.BlockSpec((1, d), lambda b, i, f: (0, 0)),
        ],
        out_specs=pl.BlockSpec((1, tm, d), lambda b, i, f: (b, i, 0)),
        scratch_shapes=[pltpu.VMEM((tm, d), BF16), pltpu.VMEM((tm, d), F32)],
        compiler_params=_params(("parallel", "parallel", "arbitrary")),
        name="ffn_res_ln",
    )(x, sc, sh, gate, w_in, w_in, w_out, ln_g.reshape(1, d).astype(F32), ln_b.reshape(1, d).astype(F32))


def _qkv_kernel(x_ref, sc_ref, sh_ref, w_ref, cos_ref, sin_ref, o_ref, h_scr, *, tiles_per_tensor, q_scale):
    j = pl.program_id(2)

    @pl.when(j == 0)
    def _():
        h_scr[...] = _modulate(x_ref, sc_ref, sh_ref)

    halves = _row_halves(h_scr.shape[0])
    accs = [jnp.dot(h_scr[rs, :], w_ref[...], preferred_element_type=F32) for rs in halves]
    heads = w_ref.shape[1] // HEAD_DIM
    tensor = j // tiles_per_tensor

    def write(rope, scale):
        for rs, acc in zip(halves, accs):
            for hh in range(heads):
                xc = acc[:, hh * HEAD_DIM:(hh + 1) * HEAD_DIM]
                if rope:
                    xc = xc * cos_ref[0, rs, :] + pltpu.roll(xc, HEAD_DIM // 2, 1) * sin_ref[0, rs, :]
                if scale != 1.0:
                    xc = xc * scale
                o_ref[0, hh, rs, :] = xc.astype(o_ref.dtype)

    pl.when(tensor == 0)(lambda: write(True, q_scale))
    pl.when(tensor == 1)(lambda: write(True, 1.0))
    pl.when(tensor == 2)(lambda: write(False, 1.0))


def _qkv_proj(x, sc, sh, w_qkv, cos_t, sin_t):
    bsz, seq, d = x.shape
    tm = _pick(seq, (512, 256, 128))
    tn = _pick(d, (1024, 512, 256, 128))
    hpt = tn // HEAD_DIM
    kern = functools.partial(_qkv_kernel, tiles_per_tensor=d // tn, q_scale=HEAD_DIM ** -0.5 * math.log2(math.e))
    return pl.pallas_call(
        kern,
        out_shape=jax.ShapeDtypeStruct((bsz, 3 * d // HEAD_DIM, seq, HEAD_DIM), BF16),
        grid=(bsz, seq // tm, 3 * d // tn),
        in_specs=[
            pl.BlockSpec((1, tm, d), lambda b, i, j: (b, i, 0)),
            pl.BlockSpec((1, 1, d), lambda b, i, j: (b, 0, 0)),
            pl.BlockSpec((1, 1, d), lambda b, i, j: (b, 0, 0)),
            pl.BlockSpec((d, tn), lambda b, i, j: (0, j)),
            pl.BlockSpec((1, tm, HEAD_DIM), lambda b, i, j: (b, i, 0)),
            pl.BlockSpec((1, tm, HEAD_DIM), lambda b, i, j: (b, i, 0)),
        ],
        out_specs=pl.BlockSpec((1, hpt, tm, HEAD_DIM), lambda b, i, j: (b, j, i, 0)),
        scratch_shapes=[pltpu.VMEM((tm, d), BF16)],
        compiler_params=_params(("parallel", "parallel", "arbitrary")),
        name="dsa_qkv_proj",
    )(x, sc, sh, w_qkv, cos_t, sin_t)


def _idx_kernel(x_ref, sc_ref, sh_ref, w_ref, cos_ref, sin_ref, qi_ref, kia_ref, kib_ref, wi_ref,
                *, n_pair, idx_heads, w_scale):
    h = _modulate(x_ref, sc_ref, sh_ref)
    halves = _row_halves(h.shape[0])
    accs = [jnp.dot(h[rs], w_ref[...], preferred_element_type=F32) for rs in halves]
    lane = lax.broadcasted_iota(I32, (halves[0].stop, LANES), 1)
    first_half = (lane % IDX_DIM) < IDX_DIM // 2

    for rs, acc in zip(halves, accs):
        cos = cos_ref[0, rs, :]
        sin = sin_ref[0, rs, :]

        def rope(xc, cos=cos, sin=sin):
            partner = jnp.where(first_half, pltpu.roll(xc, LANES - IDX_DIM // 2, 1),
                                pltpu.roll(xc, IDX_DIM // 2, 1))
            return xc * cos + partner * sin

        for p in range(n_pair):
            cs = slice(p * LANES, (p + 1) * LANES)
            qi_ref[0, rs, cs] = rope(acc[:, cs]).astype(qi_ref.dtype)
        last = acc[:, n_pair * LANES:]
        ka = jnp.where(lane < IDX_DIM, rope(last), 0.0)
        kia_ref[0, rs, :] = ka.astype(kia_ref.dtype)
        kib_ref[0, rs, :] = pltpu.roll(ka, IDX_DIM, 1).astype(kib_ref.dtype)
        wi_ref[0, rs, :] = jnp.where(lane < idx_heads, pltpu.roll(last, LANES - IDX_DIM, 1), 0.0) * w_scale


def _idx_proj(x, sc, sh, w_idx, cos_t, sin_t, idx_heads):
    bsz, seq, d = x.shape
    n = w_idx.shape[1]
    n_pair = idx_heads // 2
    tm = _pick(seq, (512, 256, 128))
    kern = functools.partial(_idx_kernel, n_pair=n_pair, idx_heads=idx_heads,
                             w_scale=(idx_heads ** -0.5) * (IDX_DIM ** -0.5))
    row = lambda b, i: (b, i, 0)
    return pl.pallas_call(
        kern,
        out_shape=(
            jax.ShapeDtypeStruct((bsz, seq, n_pair * LANES), BF16),
            jax.ShapeDtypeStruct((bsz, seq, LANES), BF16),
            jax.ShapeDtypeStruct((bsz, seq, LANES), BF16),
            jax.ShapeDtypeStruct((bsz, seq, LANES), F32),
        ),
        grid=(bsz, seq // tm),
        in_specs=[
            pl.BlockSpec((1, tm, d), row),
            pl.BlockSpec((1, 1, d), lambda b, i: (b, 0, 0)),
            pl.BlockSpec((1, 1, d), lambda b, i: (b, 0, 0)),
            pl.BlockSpec((d, n), lambda b, i: (0, 0)),
            pl.BlockSpec((1, tm, LANES), row),
            pl.BlockSpec((1, tm, LANES), row),
        ],
        out_specs=(
            pl.BlockSpec((1, tm, n_pair * LANES), row),
            pl.BlockSpec((1, tm, LANES), row),
            pl.BlockSpec((1, tm, LANES), row),
            pl.BlockSpec((1, tm, LANES), row),
        ),
        compiler_params=_params(("parallel", "parallel")),
        name="dsa_idx_proj",
    )(x, sc, sh, w_idx, cos_t, sin_t)


def _bit_transpose32(words):
    a = list(words)
    j = WORD_BITS // 2
    mask = 0x0000FFFF
    while j:
        k = 0
        while k < WORD_BITS:
            t = (a[k] ^ lax.shift_right_logical(a[k + j], jnp.int32(j))) & jnp.int32(mask)
            a[k] = a[k] ^ t
            a[k + j] = a[k + j] ^ (t << j)
            k = (k + j + 1) & ~j
        j >>= 1
        if j:
            mask = (mask ^ (mask << j)) & 0xFFFFFFFF
    return a


def _dsa_kernel(qb_tab, kb_tab, q_ref, k_ref, v_ref, qi_ref, kia_ref, kib_ref, wi_ref, o_ref,
                keys_scr, planes_scr, thr_scr, tie_scr, ngt_scr, nge_scr,
                m_scr, l_scr, acc_scr, bias_scr, s0_scr, s1_scr, r0_scr, r1_scr,
                *, n_pair, k_top, idx_bits):
    heads, qb_rows, _ = acc_scr.shape
    kb_rows = keys_scr.shape[2]
    qb = qb_tab[pl.program_id(1)]
    kb = kb_tab[pl.program_id(1)]
    q0 = qb * qb_rows
    last_kb = (q0 + qb_rows - 1) // kb_rows
    n_chunk = last_kb + 1
    row = lax.broadcasted_iota(I32, (qb_rows, kb_rows), 0) + q0
    col = lax.broadcasted_iota(I32, (qb_rows, kb_rows), 1)
    nt = (((1,), (1,)), ((), ()))

    @pl.when(kb == 0)
    def _select():
        wi = wi_ref[0]

        def score_chunk(c, carry):
            k0 = pl.multiple_of(c * kb_rows, kb_rows)
            ka = kia_ref[0, pl.ds(k0, kb_rows), :]
            kbm = kib_ref[0, pl.ds(k0, kb_rows), :]
            sc = jnp.zeros((qb_rows, kb_rows), F32)
            for p in range(n_pair):
                qp = qi_ref[0, :, p * LANES:(p + 1) * LANES]
                sa = lax.dot_general(qp, ka, nt, preferred_element_type=F32)
                sb = lax.dot_general(qp, kbm, nt, preferred_element_type=F32)
                sc = sc + wi[:, 2 * p:2 * p + 1] * jnp.maximum(sa, 0.0)
                sc = sc + wi[:, 2 * p + 1:2 * p + 2] * jnp.maximum(sb, 0.0)
            bits = lax.bitcast_convert_type(sc, I32)
            key = bits ^ ((bits >> 31) & INT_MAX)
            keys_scr[c] = jnp.where(col + k0 <= row, key, INT_MIN)
            return carry

        lax.fori_loop(0, n_chunk, score_chunk, 0)

        def fill_chunk(c, carry):
            keys_scr[c] = jnp.full((qb_rows, kb_rows), INT_MIN, I32)
            return carry

        lax.fori_loop(n_chunk, keys_scr.shape[0], fill_chunk, 0)

        slices_per_chunk = kb_rows // LANES
        n_slices = keys_scr.shape[0] * slices_per_chunk
        n_sets = planes_scr.shape[2] // LANES

        def pack_rows(g, carry):
            r0 = pl.multiple_of(g * SUBLANES, SUBLANES)
            for st in range(n_sets):
                words = []
                for s in range(WORD_BITS):
                    sl = st * WORD_BITS + s
                    if sl < n_slices:
                        c, off = divmod(sl, slices_per_chunk)
                        words.append(keys_scr[c, pl.ds(r0, SUBLANES), off * LANES:(off + 1) * LANES])
                    else:
                        words.append(jnp.full((SUBLANES, LANES), INT_MIN, I32))
                words = _bit_transpose32(words)
                words[0] = ~words[0]
                for i in range(WORD_BITS):
                    planes_scr[i, pl.ds(r0, SUBLANES), st * LANES:(st + 1) * LANES] = words[i]
            return carry

        lax.fori_loop(0, qb_rows // SUBLANES, pack_rows, 0)

        lane_ones = jnp.ones((LANES, LANES), BF16)

        def row_total(pc):
            tot = pc[:, :LANES]
            for st in range(1, n_sets):
                tot = tot + pc[:, st * LANES:(st + 1) * LANES]
            return jnp.dot(tot.astype(F32).astype(BF16), lane_ones, preferred_element_type=F32)

        def all_sets(mask):
            return jnp.concatenate([mask] * n_sets, axis=1)

        def select_two_bits(i, carry):
            cand, above, prefix = carry
            hi = planes_scr[2 * i]
            lo = planes_scr[2 * i + 1]
            c1 = cand & hi
            c0 = cand ^ c1
            c11 = c1 & lo
            c10 = c1 ^ c11
            c01 = c0 & lo
            c00 = c0 ^ c01
            r11 = above + row_total(lax.population_count(c11))
            r10 = r11 + row_total(lax.population_count(c10))
            r01 = r10 + row_total(lax.population_count(c01))
            t11 = r11 >= k_top
            t10 = r10 >= k_top
            t01 = r01 >= k_top
            cand = jnp.where(all_sets(t11), c11,
                             jnp.where(all_sets(t10), c10, jnp.where(all_sets(t01), c01, c00)))
            above = jnp.where(t11, above, jnp.where(t10, r11, jnp.where(t01, r10, r01)))
            bits = jnp.where(t11, 3, jnp.where(t10, 2, jnp.where(t01, 1, 0)))
            prefix = prefix | (bits << (WORD_BITS - 2 - 2 * i))
            return cand, above, prefix

        cand, above, prefix = lax.fori_loop(
            0, WORD_BITS // 2, select_two_bits,
            (jnp.full((qb_rows, n_sets * LANES), -1, I32),
             jnp.zeros((qb_rows, LANES), F32), jnp.zeros((qb_rows, LANES), I32)))
        equal = row_total(lax.population_count(cand))
        thr_scr[...] = (prefix ^ INT_MIN)[:, :1]
        tie_scr[...] = jnp.full((qb_rows, 1), INT_MAX, I32)
        ngt_scr[...] = above[:, :1]
        nge_scr[...] = (above + equal)[:, :1]

        rg = min(qb_rows, SEARCH_ROWS)
        lane_col = lax.broadcasted_iota(I32, (rg, LANES), 1)

        for r in range(qb_rows // rg):
            rs = slice(r * rg, (r + 1) * rg)

            def count(pred, rs=rs):
                def body(c, acc):
                    for t in range(kb_rows // LANES):
                        kk = keys_scr[c, rs, t * LANES:(t + 1) * LANES]
                        idx = lane_col + (c * kb_rows + t * LANES)
                        acc = acc + jnp.where(pred(kk, idx), 1.0, 0.0)
                    return acc
                acc = lax.fori_loop(0, n_chunk, body, jnp.zeros((rg, LANES), F32))
                return jnp.sum(acc, axis=1, keepdims=True)

            def wide(v):
                return jnp.broadcast_to(v, (rg, LANES))

            @pl.when(jnp.max(nge_scr[rs, :]) > k_top)
            def _ties(count=count, rs=rs):
                thr_w = wide(thr_scr[rs, :])
                need = k_top - ngt_scr[rs, :]

                def tie_bit(i, jt):
                    cand = jt + (jnp.int32(1) << (idx_bits - 1 - i))
                    cand_w = wide(cand)
                    cnt = count(lambda kk, idx: (kk == thr_w) & (idx < cand_w))
                    return jnp.where(cnt < need, cand, jt)

                tie_scr[rs, :] = lax.fori_loop(0, idx_bits, tie_bit, jnp.zeros((rg, 1), I32))

        m_scr[...] = jnp.full(m_scr.shape, M_INIT, F32)
        l_scr[...] = jnp.zeros_like(l_scr)
        acc_scr[...] = jnp.zeros_like(acc_scr)

    def _attend():
        ones = jnp.ones((kb_rows, HEAD_DIM), BF16)

        kk = keys_scr[kb]
        thr = thr_scr[...]
        lim = jnp.minimum(tie_scr[...], row[:, :1])
        tie_bias = jnp.where(col + kb * kb_rows <= lim, 0.0, MASK_VALUE)
        bias_scr[...] = jnp.where(kk > thr, 0.0, jnp.where(kk == thr, tie_bias, MASK_VALUE))

        def logits(h, s_ref, r_ref):
            s = lax.dot_general(q_ref[0, h], k_ref[0, h], nt, preferred_element_type=F32) + bias_scr[...]
            s_ref[...] = s
            r_ref[...] = jnp.broadcast_to(jnp.max(s, axis=1, keepdims=True), r_ref.shape)

        def accumulate(h, s_ref, r_ref):
            m_old = m_scr[h]
            m_new = jnp.maximum(m_old, r_ref[...])
            alpha = jnp.exp2(m_old - m_new)
            p = jnp.concatenate(
                [jnp.exp2(s_ref[:, t * LANES:(t + 1) * LANES] - m_new).astype(BF16)
                 for t in range(kb_rows // LANES)], axis=1)
            v_ext = jnp.concatenate([v_ref[0, h], ones], axis=1)
            pv = jnp.dot(p, v_ext, preferred_element_type=F32)
            acc_scr[h] = alpha * acc_scr[h] + pv[:, :HEAD_DIM]
            l_scr[h] = alpha * l_scr[h] + pv[:, HEAD_DIM:]
            m_scr[h] = m_new

        bufs = ((s0_scr, r0_scr), (s1_scr, r1_scr))
        logits(0, *bufs[0])
        for h in range(heads):
            if h + 1 < heads:
                logits(h + 1, *bufs[(h + 1) % 2])
            accumulate(h, *bufs[h % 2])

    _attend()

    @pl.when(kb == last_kb)
    def _finish():
        for h in range(heads):
            o_ref[0, :, h * HEAD_DIM:(h + 1) * HEAD_DIM] = (acc_scr[h] / l_scr[h]).astype(o_ref.dtype)


def _dsa_attention(qkv, qi, kia, kib, wi, k_top):
    bsz, heads3, seq, _ = qkv.shape
    heads = heads3 // 3
    d = heads * HEAD_DIM
    n_pair = qi.shape[-1] // LANES
    qb_rows = _pick(seq, (256, 128))
    kb_rows = _pick(seq, (512, 256, 128))
    n_kb = seq // kb_rows
    n_sets = -(-seq // (WORD_BITS * LANES))

    pairs = [(i, j) for i in range(seq // qb_rows) for j in range((i * qb_rows + qb_rows - 1) // kb_rows + 1)]
    qb_tab = jnp.asarray([p[0] for p in pairs], I32)
    kb_tab = jnp.asarray([p[1] for p in pairs], I32)

    kern = functools.partial(_dsa_kernel, n_pair=n_pair, k_top=k_top, idx_bits=seq.bit_length())
    grid_spec = pltpu.PrefetchScalarGridSpec(
        num_scalar_prefetch=2,
        grid=(bsz, len(pairs)),
        in_specs=[
            pl.BlockSpec((1, heads, qb_rows, HEAD_DIM), lambda b, s, qt, kt: (b, 0, qt[s], 0)),
            pl.BlockSpec((1, heads, kb_rows, HEAD_DIM), lambda b, s, qt, kt: (b, 1, kt[s], 0)),
            pl.BlockSpec((1, heads, kb_rows, HEAD_DIM), lambda b, s, qt, kt: (b, 2, kt[s], 0)),
            pl.BlockSpec((1, qb_rows, n_pair * LANES), lambda b, s, qt, kt: (b, qt[s], 0)),
            pl.BlockSpec((1, seq, LANES), lambda b, s, qt, kt: (b, 0, 0)),
            pl.BlockSpec((1, seq, LANES), lambda b, s, qt, kt: (b, 0, 0)),
            pl.BlockSpec((1, qb_rows, LANES), lambda b, s, qt, kt: (b, qt[s], 0)),
        ],
        out_specs=pl.BlockSpec((1, qb_rows, d), lambda b, s, qt, kt: (b, qt[s], 0)),
        scratch_shapes=[
            pltpu.VMEM((n_kb, qb_rows, kb_rows), I32),
            pltpu.VMEM((WORD_BITS, qb_rows, n_sets * LANES), I32),
            pltpu.VMEM((qb_rows, 1), I32),
            pltpu.VMEM((qb_rows, 1), I32),
            pltpu.VMEM((qb_rows, 1), F32),
            pltpu.VMEM((qb_rows, 1), F32),
            pltpu.VMEM((heads, qb_rows, LANES), F32),
            pltpu.VMEM((heads, qb_rows, LANES), F32),
            pltpu.VMEM((heads, qb_rows, HEAD_DIM), F32),
            pltpu.VMEM((qb_rows, kb_rows), F32),
            pltpu.VMEM((qb_rows, kb_rows), F32),
            pltpu.VMEM((qb_rows, kb_rows), F32),
            pltpu.VMEM((qb_rows, LANES), F32),
            pltpu.VMEM((qb_rows, LANES), F32),
        ],
    )
    return pl.pallas_call(
        kern,
        out_shape=jax.ShapeDtypeStruct((bsz, seq, d), BF16),
        grid_spec=grid_spec,
        compiler_params=_params(("parallel", "arbitrary")),
        name="dsa_select_attend",
    )(qb_tab, kb_tab, qkv, qkv, qkv, qi, kia, kib, wi)


def _rope_tables(positions, dim, repeats):
    inv = 1.0 / (ROPE_THETA ** (jnp.arange(0, dim, 2, dtype=F32) / dim))
    ang = positions.astype(F32)[..., None] * inv
    cos, sin = jnp.cos(ang), jnp.sin(ang)
    cos_t = jnp.concatenate([cos, cos] * repeats, axis=-1)
    sin_t = jnp.concatenate([-sin, sin] * repeats, axis=-1)
    return cos_t, sin_t


def kernel(x, c, positions, ada_w, ada_b, ln_g, ln_b, s5_in_w, s5_a_re, s5_a_im, s5_log_dt, s5_b_re, s5_b_im, s5_c_re, s5_c_im, s5_d, s5_glu_w, s5_glu_b, dsa_in_w, dsa_out_w, ffn_w_in, ffn_w_out):
    bsz, seq, d = x.shape
    depth = ada_w.shape[0]
    alpha = (2.0 * depth) ** 0.25
    idx_heads = (dsa_in_w.shape[-1] - 3 * d - IDX_DIM) // (IDX_DIM + 1)
    k_top = min(TOPK_MAX, seq // 4)
    seg = min(S5_SEG, seq // SUBLANES)

    cos_h, sin_h = _rope_tables(positions, HEAD_DIM, 1)
    cos_i, sin_i = _rope_tables(positions, IDX_DIM, 2)
    mod = _ada_mod(c, ada_w, ada_b)

    for i in range(depth):
        sh1, sc1, g1, sh2, sc2, g2 = [m[:, None, :] for m in jnp.split(mod[i], 6, axis=-1)]
        j = i // 2
        if i % 2 == 0:
            prep =_s5_discretize(s5_a_re[j], s5_a_im[j], s5_log_dt[j], s5_b_re[j], s5_b_im[j],
                                  s5_c_re[j], s5_c_im[j], seg)
            u = _mod_matmul(x, sc1, sh1, s5_in_w[j].astype(BF16), F32)
            gl = _s5_scan(u, prep, s5_d[j], seg)
            x1 = _matmul_res_ln(gl, s5_glu_w[j].astype(BF16), s5_glu_b[j], x, g1,
                                ln_g[i, 0], ln_b[i, 0], alpha, glu=True)
        else:
            w = dsa_in_w[j]
            n_qi = idx_heads * IDX_DIM
            w_qkv = w[:, :3 * d].astype(BF16)
            pad = jnp.zeros((d, LANES - IDX_DIM - idx_heads), w.dtype)
            w_idx = jnp.concatenate(
                [w[:, 3 * d:3 * d + n_qi], w[:, 3 * d + n_qi + idx_heads:], w[:, 3 * d + n_qi:3 * d + n_qi + idx_heads], pad],
                axis=1).astype(BF16)
            qkv = _qkv_proj(x, sc1, sh1, w_qkv, cos_h, sin_h)
            qi, kia, kib, wi = _idx_proj(x, sc1, sh1, w_idx, cos_i, sin_i, idx_heads)
            att = _dsa_attention(qkv, qi, kia, kib, wi, k_top)
            x1 = _matmul_res_ln(att, dsa_out_w[j].astype(BF16), None, x, g1,
                                ln_g[i, 0], ln_b[i, 0], alpha, glu=False)
        x = _ffn_res_ln(x1, sc2, sh2, g2, ffn_w_in[i].astype(BF16), ffn_w_out[i].astype(BF16),
                        ln_g[i, 1], ln_b[i, 1], alpha)
    return x
```

```python
import functools
import math

import jax
import jax.numpy as jnp
from jax import lax
from jax.experimental import pallas as pl
from jax.experimental.pallas import tpu as pltpu

F32 = jnp.float32
BF16 = jnp.bfloat16
I32 = jnp.int32

S5_GROUP = 16
S5_STATE = 64
HEAD_DIM = 128
IDX_DIM = 64
TOPK_MAX = 256
ROPE_THETA = 10000.0
LN_EPS = 1e-5

LANES = 128
SUBLANES = 8
VMEM_LIMIT_BYTES = 56 * 1024 * 1024
RESIDENT_WEIGHT_BYTES = 16 * 1024 * 1024

S5_PACK_GROUPS = 16
S5_SEG = 64

INT_MIN = -(2 ** 31)
INT_MAX = 2 ** 31 - 1
SEARCH_ROWS = 128
WORD_BITS = 32

MASK_VALUE = -2e30
M_INIT = -1e30


def _pick(n, cands):
    for c in cands:
        if n % c == 0:
            return c
    return n


def _row_halves(rows):
    half = rows // 2
    return (slice(0, half), slice(half, rows))


def _params(sem):
    return pltpu.CompilerParams(dimension_semantics=sem, vmem_limit_bytes=VMEM_LIMIT_BYTES)


def _ada_kernel(c_ref, w_ref, b_ref, o_ref):
    ca = jax.nn.silu(c_ref[...]).astype(BF16)
    o_ref[0] = jnp.dot(ca, w_ref[0].astype(BF16), preferred_element_type=F32) + b_ref[0]


def _ada_mod(c, ada_w, ada_b):
    bsz, d = c.shape
    depth, _, n = ada_w.shape
    rows = SUBLANES * ((bsz + SUBLANES - 1) // SUBLANES)
    cp = jnp.zeros((rows, d), F32).at[:bsz].set(c)
    tn = _pick(n, (1024, 512, 256, 128))
    out = pl.pallas_call(
        _ada_kernel,
        out_shape=jax.ShapeDtypeStruct((depth, rows, n), F32),
        grid=(depth, n // tn),
        in_specs=[
            pl.BlockSpec((rows, d), lambda l, j: (0, 0)),
            pl.BlockSpec((1, d, tn), lambda l, j: (l, 0, j)),
            pl.BlockSpec((1, 1, tn), lambda l, j: (l, 0, j)),
        ],
        out_specs=pl.BlockSpec((1, rows, tn), lambda l, j: (l, 0, j)),
        compiler_params=_params(("arbitrary", "arbitrary")),
        name="ada_mod",
    )(cp, ada_w, ada_b.reshape(depth, 1, n))
    return out[:, :bsz]


def _modulate(x_ref, sc_ref, sh_ref):
    return (x_ref[0] * (1.0 + sc_ref[0]) + sh_ref[0]).astype(BF16)


def _modmm_kernel(x_ref, sc_ref, sh_ref, w_ref, o_ref, h_scr):
    @pl.when(pl.program_id(2) == 0)
    def _():
        h_scr[...] = _modulate(x_ref, sc_ref, sh_ref)

    for rs in _row_halves(h_scr.shape[0]):
        o_ref[0, rs, :] = jnp.dot(h_scr[rs, :], w_ref[...], preferred_element_type=F32).astype(o_ref.dtype)


def _mod_matmul(x, sc, sh, w, out_dtype):
    bsz, seq, d = x.shape
    n = w.shape[1]
    tm = _pick(seq, (512, 256, 128))
    tn = n if 2 * d * n * w.dtype.itemsize <= RESIDENT_WEIGHT_BYTES else _pick(n, (1024, 512, 256, 128))
    return pl.pallas_call(
        _modmm_kernel,
        out_shape=jax.ShapeDtypeStruct((bsz, seq, n), out_dtype),
        grid=(bsz, seq // tm, n // tn),
        in_specs=[
            pl.BlockSpec((1, tm, d), lambda b, i, j: (b, i, 0)),
            pl.BlockSpec((1, 1, d), lambda b, i, j: (b, 0, 0)),
            pl.BlockSpec((1, 1, d), lambda b, i, j: (b, 0, 0)),
            pl.BlockSpec((d, tn), lambda b, i, j: (0, j)),
        ],
        out_specs=pl.BlockSpec((1, tm, tn), lambda b, i, j: (b, i, j)),
        scratch_shapes=[pltpu.VMEM((tm, d), BF16)],
        compiler_params=_params(("parallel", "parallel", "arbitrary")),
        name="mod_matmul",
    )(x, sc, sh, w)


def _s5_kernel(*refs, seg, width, lane_tiles):
    u_refs = refs[:lane_tiles]
    (bb_ref, cb_ref, lre_ref, lim_ref, ltre_ref, ltim_ref, d_ref, o_ref,
     xs_scr, carry_scr, cin_scr, up_scr) = refs[lane_tiles:]
    ns = xs_scr.shape[1] // 2

    @pl.when(pl.program_id(2) == 0)
    def _():
        carry_scr[...] = jnp.zeros_like(carry_scr)

    for t, u_ref in enumerate(u_refs):
        for i in range(seg):
            up_scr[t, i * SUBLANES:(i + 1) * SUBLANES, :] = u_ref[0, pl.ds(i, SUBLANES, stride=seg), :]
    u = jnp.concatenate([up_scr[t] for t in range(len(u_refs))], axis=1)
    halves = _row_halves(u.shape[0])
    for rs in halves:
        xs_scr[rs, :] = jnp.dot(u[rs].astype(BF16), bb_ref[0], preferred_element_type=F32)

    for part in range(ns // width):
        cr = slice(part * width, (part + 1) * width)
        ci = slice(ns + part * width, ns + (part + 1) * width)
        lr = jnp.broadcast_to(lre_ref[0, :, cr], (SUBLANES, width))
        li = jnp.broadcast_to(lim_ref[0, :, cr], (SUBLANES, width))

        def local_step(i, st, cr=cr, ci=ci, lr=lr, li=li):
            sr, si = st
            r0 = pl.multiple_of(i * SUBLANES, SUBLANES)
            nr = lr * sr - li * si + xs_scr[pl.ds(r0, SUBLANES), cr]
            ni = lr * si + li * sr + xs_scr[pl.ds(r0, SUBLANES), ci]
            xs_scr[pl.ds(r0, SUBLANES), cr] = nr
            xs_scr[pl.ds(r0, SUBLANES), ci] = ni
            return nr, ni

        zero = jnp.zeros((SUBLANES, width), F32)
        er, ei = lax.fori_loop(0, seg, local_step, (zero, zero), unroll=2)

        ltr = ltre_ref[0, :, cr]
        lti = ltim_ref[0, :, cr]
        c_r = carry_scr[:, cr]
        c_i = carry_scr[:, ci]
        for s in range(SUBLANES):
            cin_scr[s:s + 1, cr] = c_r
            cin_scr[s:s + 1, ci] = c_i
            e_r = er[s:s + 1, :]
            e_i = ei[s:s + 1, :]
            c_r, c_i = ltr * c_r - lti * c_i + e_r, ltr * c_i + lti * c_r + e_i
        carry_scr[:, cr] = c_r
        carry_scr[:, ci] = c_i

        def carry_step(i, st, cr=cr, ci=ci, lr=lr, li=li):
            pr, pi_ = st
            r0 = pl.multiple_of(i * SUBLANES, SUBLANES)
            nr = lr * pr - li * pi_
            ni = lr * pi_ + li * pr
            xs_scr[pl.ds(r0, SUBLANES), cr] += nr
            xs_scr[pl.ds(r0, SUBLANES), ci] += ni
            return nr, ni

        lax.fori_loop(0, seg, carry_step, (cin_scr[:, cr], cin_scr[:, ci]), unroll=2)

    for rs in halves:
        y = jnp.dot(xs_scr[rs, :].astype(BF16), cb_ref[0], preferred_element_type=F32)
        g = jax.nn.gelu(y + d_ref[0] * u[rs])
        for t in range(len(u_refs)):
            up_scr[t, rs, :] = g[:, t * LANES:(t + 1) * LANES]
    for t in range(len(u_refs)):
        for s in range(SUBLANES):
            o_ref[0, s * seg:(s + 1) * seg, t * LANES:(t + 1) * LANES] = (
                up_scr[t, pl.ds(s, seg, stride=SUBLANES), :].astype(o_ref.dtype))


def _cmul(ar, ai, br, bi):
    return ar * br - ai * bi, ar * bi + ai * br


def _s5_discretize(a_re, a_im, log_dt, b_re, b_im, c_re, c_im, seg):
    g, n = a_re.shape
    p = b_re.shape[-1]
    pg = S5_PACK_GROUPS
    packs = g // pg
    a_re, a_im = a_re.astype(F32), a_im.astype(F32)
    dt = jnp.exp(log_dt.astype(F32))[:, None]
    mag = jnp.exp(a_re * dt)
    lb_re, lb_im = mag * jnp.cos(a_im * dt), mag * jnp.sin(a_im * dt)
    den = a_re * a_re + a_im * a_im
    nr, ni = lb_re - 1.0, lb_im
    f_re = (nr * a_re + ni * a_im) / den
    f_im = (ni * a_re - nr * a_im) / den
    bb_re, bb_im = _cmul(f_re[..., None], f_im[..., None], b_re.astype(F32), b_im.astype(F32))
    lt_re, lt_im = lb_re, lb_im
    for _ in range(int(math.log2(seg))):
        lt_re, lt_im = _cmul(lt_re, lt_im, lt_re, lt_im)
    eye = jnp.eye(pg, dtype=F32)

    def in_blk(m):
        return jnp.einsum("kgnp,gh->kgphn", m.reshape(packs, pg, n, p), eye).reshape(packs, pg * p, pg * n)

    def out_blk(m):
        return jnp.einsum("kgpn,gh->kgnhp", m.reshape(packs, pg, p, n), eye).reshape(packs, pg * n, pg * p)

    b_blk = jnp.concatenate([in_blk(bb_re), in_blk(bb_im)], axis=-1).astype(BF16)
    c_blk = jnp.concatenate([out_blk(c_re.astype(F32)), out_blk(-c_im.astype(F32))], axis=1).astype(BF16)

    def vec(m):
        return m.reshape(packs, 1, pg * n)

    return b_blk, c_blk, vec(lb_re), vec(lb_im), vec(lt_re), vec(lt_im)


def _s5_scan(u, prep, d_skip, seg):
    bsz, seq, d = u.shape
    b_blk, c_blk, lre, lim, ltre, ltim = prep
    packs, pw, ns2 = b_blk.shape
    ns = ns2 // 2
    rows = SUBLANES * seg
    width = _pick(ns, (1024, 512, 256, 128))
    lane_tiles = pw // LANES
    kern = functools.partial(_s5_kernel, seg=seg, width=width, lane_tiles=lane_tiles)
    vspec = pl.BlockSpec((1, 1, ns), lambda b, k, m: (k, 0, 0))
    u_specs = [pl.BlockSpec((1, rows, LANES), lambda b, k, m, t=t: (b, m, k * lane_tiles + t))
               for t in range(lane_tiles)]
    return pl.pallas_call(
        kern,
        out_shape=jax.ShapeDtypeStruct((bsz, seq, d), BF16),
        grid=(bsz, packs, seq // rows),
        in_specs=u_specs + [
            pl.BlockSpec((1, pw, ns2), lambda b, k, m: (k, 0, 0)),
            pl.BlockSpec((1, ns2, pw), lambda b, k, m: (k, 0, 0)),
            vspec, vspec, vspec, vspec,
            pl.BlockSpec((1, 1, pw), lambda b, k, m: (k, 0, 0)),
        ],
        out_specs=pl.BlockSpec((1, rows, pw), lambda b, k, m: (b, m, k)),
        scratch_shapes=[
            pltpu.VMEM((rows, ns2), F32),
            pltpu.VMEM((1, ns2), F32),
            pltpu.VMEM((SUBLANES, ns2), F32),
            pltpu.VMEM((lane_tiles, rows, LANES), F32),
        ],
        compiler_params=_params(("parallel", "parallel", "arbitrary")),
        name="s5_scan",
    )(*([u] * lane_tiles), b_blk, c_blk, lre, lim, ltre, ltim, d_skip.reshape(packs, 1, pw).astype(F32))


def _layer_norm_rows(tiles, lng_ref, lnb_ref, o_ref, d):
    tn = tiles[0].shape[1]
    tot = tiles[0].sum(axis=1, keepdims=True)
    for t in tiles[1:]:
        tot = tot + t.sum(axis=1, keepdims=True)
    mu = tot * (1.0 / d)
    sq = jnp.square(tiles[0] - mu).sum(axis=1, keepdims=True)
    for t in tiles[1:]:
        sq = sq + jnp.square(t - mu).sum(axis=1, keepdims=True)
    inv = lax.rsqrt(sq * (1.0 / d) + LN_EPS)
    for k, t in enumerate(tiles):
        cs = slice(k * tn, (k + 1) * tn)
        o_ref[0, :, cs] = ((t - mu) * inv * lng_ref[:, cs] + lnb_ref[:, cs]).astype(o_ref.dtype)


def _mmln_kernel(*refs, glu, nj, alpha, d):
    if glu:
        a_ref, w1_ref, w2_ref, b1_ref, b2_ref, x_ref, g_ref, lng_ref, lnb_ref, o_ref, r_scr = refs
    else:
        a_ref, w1_ref, x_ref, g_ref, lng_ref, lnb_ref, o_ref, r_scr = refs
    j = pl.program_id(2)
    for rs in _row_halves(a_ref.shape[1]):
        a = a_ref[0, rs, :]
        y = jnp.dot(a, w1_ref[...], preferred_element_type=F32)
        if glu:
            y = y + b1_ref[...]
            gate = jnp.dot(a, w2_ref[...], preferred_element_type=F32) + b2_ref[...]
            y = y * jax.nn.sigmoid(gate)
        r_scr[j, rs, :] = alpha * x_ref[0, rs, :] + (1.0 + g_ref[0]) * y

    @pl.when(j == nj - 1)
    def _():
        _layer_norm_rows([r_scr[t] for t in range(nj)], lng_ref, lnb_ref, o_ref, d)


def _matmul_res_ln(a, w, bias, xres, gate, ln_g, ln_b, alpha, glu):
    bsz, seq, k = a.shape
    d = xres.shape[-1]
    if glu:
        tm = _pick(seq, (1024, 512, 256, 128))
        tn = _pick(d, (512, 256, 128))
    else:
        tm = _pick(seq, (512, 256, 128))
        tn = d if 2 * k * d * w.dtype.itemsize <= RESIDENT_WEIGHT_BYTES else _pick(d, (512, 256, 128))
    nj = d // tn
    a_spec = pl.BlockSpec((1, tm, k), lambda b, i, j: (b, i, 0))
    w1_spec = pl.BlockSpec((k, tn), lambda b, i, j: (0, j))
    tail_specs = [
        pl.BlockSpec((1, tm, tn), lambda b, i, j: (b, i, j)),
        pl.BlockSpec((1, 1, tn), lambda b, i, j: (b, 0, j)),
        pl.BlockSpec((1, d), lambda b, i, j: (0, 0)),
        pl.BlockSpec((1, d), lambda b, i, j: (0, 0)),
    ]
    tail = (xres, gate, ln_g.reshape(1, d).astype(F32), ln_b.reshape(1, d).astype(F32))
    if glu:
        in_specs = [a_spec, w1_spec,
                    pl.BlockSpec((k, tn), lambda b, i, j: (0, j + nj)),
                    pl.BlockSpec((1, tn), lambda b, i, j: (0, j)),
                    pl.BlockSpec((1, tn), lambda b, i, j: (0, j + nj))] + tail_specs
        b2d = bias.reshape(1, 2 * d).astype(F32)
        args = (a, w, w, b2d, b2d) + tail
    else:
        in_specs = [a_spec, w1_spec] + tail_specs
        args = (a, w) + tail
    kern = functools.partial(_mmln_kernel, glu=glu, nj=nj, alpha=alpha, d=d)
    return pl.pallas_call(
        kern,
        out_shape=jax.ShapeDtypeStruct((bsz, seq, d), F32),
        grid=(bsz, seq // tm, nj),
        in_specs=in_specs,
        out_specs=pl.BlockSpec((1, tm, d), lambda b, i, j: (b, i, 0)),
        scratch_shapes=[pltpu.VMEM((nj, tm, tn), F32)],
        compiler_params=_params(("parallel", "parallel", "arbitrary")),
        name="matmul_res_ln",
    )(*args)


def _ffn_kernel(x_ref, sc_ref, sh_ref, g_ref, wg_ref, wu_ref, wo_ref, lng_ref, lnb_ref, o_ref,
                h_scr, acc_scr, *, nf, alpha, d, tn):
    f = pl.program_id(2)

    @pl.when(f == 0)
    def _():
        h_scr[...] = _modulate(x_ref, sc_ref, sh_ref)
        acc_scr[...] = jnp.zeros_like(acc_scr)

    h = h_scr[...]
    a_g = jnp.dot(h, wg_ref[...], preferred_element_type=F32)
    a_u = jnp.dot(h, wu_ref[...], preferred_element_type=F32)
    act = (jax.nn.silu(a_g) * a_u).astype(BF16)
    acc_scr[...] += jnp.dot(act, wo_ref[...], preferred_element_type=F32)

    @pl.when(f == nf - 1)
    def _():
        tiles = []
        for k in range(d // tn):
            cs = slice(k * tn, (k + 1) * tn)
            tiles.append(alpha * x_ref[0, :, cs] + (1.0 + g_ref[0, :, cs]) * acc_scr[:, cs])
        _layer_norm_rows(tiles, lng_ref, lnb_ref, o_ref, d)


def _ffn_res_ln(x, sc, sh, gate, w_in, w_out, ln_g, ln_b, alpha):
    bsz, seq, d = x.shape
    dff = w_out.shape[0]
    tm = _pick(seq, (512, 256, 128))
    tf = _pick(dff, (512, 256, 128))
    nf = dff // tf
    tn = _pick(d, (512, 256, 128))
    vec = pl.BlockSpec((1, 1, d), lambda b, i, f: (b, 0, 0))
    kern = functools.partial(_ffn_kernel, nf=nf, alpha=alpha, d=d, tn=tn)
    return pl.pallas_call(
        kern,
        out_shape=jax.ShapeDtypeStruct((bsz, seq, d), F32),
        grid=(bsz, seq // tm, nf),
        in_specs=[
            pl.BlockSpec((1, tm, d), lambda b, i, f: (b, i, 0)),
            vec, vec, vec,
            pl.BlockSpec((d, tf), lambda b, i, f: (0, f)),
            pl.BlockSpec((d, tf), lambda b, i, f: (0, f + nf)),
            pl.BlockSpec((tf, d), lambda b, i, f: (f, 0)),
            pl.BlockSpec((1, d), lambda b, i, f: (0, 0)),
            pl.BlockSpec((1, d), lambda b, i, f: (0, 0)),
        ],
        out_specs=pl.BlockSpec((1, tm, d), lambda b, i, f: (b, i, 0)),
        scratch_shapes=[pltpu.VMEM((tm, d), BF16), pltpu.VMEM((tm, d), F32)],
        compiler_params=_params(("parallel", "parallel", "arbitrary")),
        name="ffn_res_ln",
    )(x, sc, sh, gate, w_in, w_in, w_out, ln_g.reshape(1, d).astype(F32), ln_b.reshape(1, d).astype(F32))


def _qkv_kernel(x_ref, sc_ref, sh_ref, w_ref, cos_ref, sin_ref, o_ref, h_scr, *, tiles_per_tensor, q_scale):
    j = pl.program_id(2)

    @pl.when(j == 0)
    def _():
        h_scr[...] = _modulate(x_ref, sc_ref, sh_ref)

    halves = _row_halves(h_scr.shape[0])
    accs = [jnp.dot(h_scr[rs, :], w_ref[...], preferred_element_type=F32) for rs in halves]
    heads = w_ref.shape[1] // HEAD_DIM
    tensor = j // tiles_per_tensor

    def write(rope, scale):
        for rs, acc in zip(halves, accs):
            for hh in range(heads):
                xc = acc[:, hh * HEAD_DIM:(hh + 1) * HEAD_DIM]
                if rope:
                    xc = xc * cos_ref[0, rs, :] + pltpu.roll(xc, HEAD_DIM // 2, 1) * sin_ref[0, rs, :]
                if scale != 1.0:
                    xc = xc * scale
                o_ref[0, hh, rs, :] = xc.astype(o_ref.dtype)

    pl.when(tensor == 0)(lambda: write(True, q_scale))
    pl.when(tensor == 1)(lambda: write(True, 1.0))
    pl.when(tensor == 2)(lambda: write(False, 1.0))


def _qkv_proj(x, sc, sh, w_qkv, cos_t, sin_t):
    bsz, seq, d = x.shape
    tm = _pick(seq, (1024, 512, 256, 128))
    tn = _pick(d, (1024, 512, 256, 128))
    hpt = tn // HEAD_DIM
    kern = functools.partial(_qkv_kernel, tiles_per_tensor=d // tn, q_scale=HEAD_DIM ** -0.5 * math.log2(math.e))
    return pl.pallas_call(
        kern,
        out_shape=jax.ShapeDtypeStruct((bsz, 3 * d // HEAD_DIM, seq, HEAD_DIM), BF16),
        grid=(bsz, seq // tm, 3 * d // tn),
        in_specs=[
            pl.BlockSpec((1, tm, d), lambda b, i, j: (b, i, 0)),
            pl.BlockSpec((1, 1, d), lambda b, i, j: (b, 0, 0)),
            pl.BlockSpec((1, 1, d), lambda b, i, j: (b, 0, 0)),
            pl.BlockSpec((d, tn), lambda b, i, j: (0, j)),
            pl.BlockSpec((1, tm, HEAD_DIM), lambda b, i, j: (b, i, 0)),
            pl.BlockSpec((1, tm, HEAD_DIM), lambda b, i, j: (b, i, 0)),
        ],
        out_specs=pl.BlockSpec((1, hpt, tm, HEAD_DIM), lambda b, i, j: (b, j, i, 0)),
        scratch_shapes=[pltpu.VMEM((tm, d), BF16)],
        compiler_params=_params(("parallel", "parallel", "arbitrary")),
        name="dsa_qkv_proj",
    )(x, sc, sh, w_qkv, cos_t, sin_t)


def _idx_kernel(x_ref, sc_ref, sh_ref, w_ref, cos_ref, sin_ref, qi_ref, kia_ref, kib_ref, wi_ref,
                *, n_pair, idx_heads, w_scale):
    h = _modulate(x_ref, sc_ref, sh_ref)
    halves = _row_halves(h.shape[0])
    accs = [jnp.dot(h[rs], w_ref[...], preferred_element_type=F32) for rs in halves]
    lane = lax.broadcasted_iota(I32, (halves[0].stop, LANES), 1)
    first_half = (lane % IDX_DIM) < IDX_DIM // 2

    for rs, acc in zip(halves, accs):
        cos = cos_ref[0, rs, :]
        sin = sin_ref[0, rs, :]

        def rope(xc, cos=cos, sin=sin):
            partner = jnp.where(first_half, pltpu.roll(xc, LANES - IDX_DIM // 2, 1),
                                pltpu.roll(xc, IDX_DIM // 2, 1))
            return xc * cos + partner * sin

        for p in range(n_pair):
            cs = slice(p * LANES, (p + 1) * LANES)
            qi_ref[0, rs, cs] = rope(acc[:, cs]).astype(qi_ref.dtype)
        last = acc[:, n_pair * LANES:]
        ka = jnp.where(lane < IDX_DIM, rope(last), 0.0)
        kia_ref[0, rs, :] = ka.astype(kia_ref.dtype)
        kib_ref[0, rs, :] = pltpu.roll(ka, IDX_DIM, 1).astype(kib_ref.dtype)
        wi_ref[0, rs, :] = jnp.where(lane < idx_heads, pltpu.roll(last, LANES - IDX_DIM, 1), 0.0) * w_scale


def _idx_proj(x, sc, sh, w_idx, cos_t, sin_t, idx_heads):
    bsz, seq, d = x.shape
    n = w_idx.shape[1]
    n_pair = idx_heads // 2
    tm = _pick(seq, (512, 256, 128))
    kern = functools.partial(_idx_kernel, n_pair=n_pair, idx_heads=idx_heads,
                             w_scale=(idx_heads ** -0.5) * (IDX_DIM ** -0.5))
    row = lambda b, i: (b, i, 0)
    return pl.pallas_call(
        kern,
        out_shape=(
            jax.ShapeDtypeStruct((bsz, seq, n_pair * LANES), BF16),
            jax.ShapeDtypeStruct((bsz, seq, LANES), BF16),
            jax.ShapeDtypeStruct((bsz, seq, LANES), BF16),
            jax.ShapeDtypeStruct((bsz, seq, LANES), F32),
        ),
        grid=(bsz, seq // tm),
        in_specs=[
            pl.BlockSpec((1, tm, d), row),
            pl.BlockSpec((1, 1, d), lambda b, i: (b, 0, 0)),
            pl.BlockSpec((1, 1, d), lambda b, i: (b, 0, 0)),
            pl.BlockSpec((d, n), lambda b, i: (0, 0)),
            pl.BlockSpec((1, tm, LANES), row),
            pl.BlockSpec((1, tm, LANES), row),
        ],
        out_specs=(
            pl.BlockSpec((1, tm, n_pair * LANES), row),
            pl.BlockSpec((1, tm, LANES), row),
            pl.BlockSpec((1, tm, LANES), row),
            pl.BlockSpec((1, tm, LANES), row),
        ),
        compiler_params=_params(("parallel", "parallel")),
        name="dsa_idx_proj",
    )(x, sc, sh, w_idx, cos_t, sin_t)


def _bit_transpose32(words):
    a = list(words)
    j = WORD_BITS // 2
    mask = 0x0000FFFF
    while j:
        k = 0
        while k < WORD_BITS:
            t = (a[k] ^ lax.shift_right_logical(a[k + j], jnp.int32(j))) & jnp.int32(mask)
            a[k] = a[k] ^ t
            a[k + j] = a[k + j] ^ (t << j)
            k = (k + j + 1) & ~j
        j >>= 1
        if j:
            mask = (mask ^ (mask << j)) & 0xFFFFFFFF
    return a


def _dsa_kernel(qb_tab, kb_tab, q_ref, k_ref, v_ref, qi_ref, kia_ref, kib_ref, wi_ref, o_ref,
                keys_scr, planes_scr, thr_scr, tie_scr, ngt_scr, nge_scr,
                m_scr, l_scr, acc_scr, bias_scr, s0_scr, s1_scr, r0_scr, r1_scr,
                *, n_pair, k_top, idx_bits):
    heads, qb_rows, _ = acc_scr.shape
    kb_rows = keys_scr.shape[2]
    qb = qb_tab[pl.program_id(1)]
    kb = kb_tab[pl.program_id(1)]
    q0 = qb * qb_rows
    last_kb = (q0 + qb_rows - 1) // kb_rows
    n_chunk = last_kb + 1
    row = lax.broadcasted_iota(I32, (qb_rows, kb_rows), 0) + q0
    col = lax.broadcasted_iota(I32, (qb_rows, kb_rows), 1)
    nt = (((1,), (1,)), ((), ()))

    @pl.when(kb == 0)
    def _select():
        wi = wi_ref[0]

        def score_chunk(c, carry):
            k0 = pl.multiple_of(c * kb_rows, kb_rows)
            ka = kia_ref[0, pl.ds(k0, kb_rows), :]
            kbm = kib_ref[0, pl.ds(k0, kb_rows), :]
            sc = jnp.zeros((qb_rows, kb_rows), F32)
            for p in range(n_pair):
                qp = qi_ref[0, :, p * LANES:(p + 1) * LANES]
                sa = lax.dot_general(qp, ka, nt, preferred_element_type=F32)
                sb = lax.dot_general(qp, kbm, nt, preferred_element_type=F32)
                sc = sc + wi[:, 2 * p:2 * p + 1] * jnp.maximum(sa, 0.0)
                sc = sc + wi[:, 2 * p + 1:2 * p + 2] * jnp.maximum(sb, 0.0)
            bits = lax.bitcast_convert_type(sc, I32)
            key = bits ^ ((bits >> 31) & INT_MAX)
            keys_scr[c] = jnp.where(col + k0 <= row, key, INT_MIN)
            return carry

        lax.fori_loop(0, n_chunk, score_chunk, 0)

        def fill_chunk(c, carry):
            keys_scr[c] = jnp.full((qb_rows, kb_rows), INT_MIN, I32)
            return carry

        lax.fori_loop(n_chunk, keys_scr.shape[0], fill_chunk, 0)

        slices_per_chunk = kb_rows // LANES
        n_slices = keys_scr.shape[0] * slices_per_chunk
        n_sets = planes_scr.shape[2] // LANES

        def pack_rows(g, carry):
            r0 = pl.multiple_of(g * SUBLANES, SUBLANES)
            for st in range(n_sets):
                words = []
                for s in range(WORD_BITS):
                    sl = st * WORD_BITS + s
                    if sl < n_slices:
                        c, off = divmod(sl, slices_per_chunk)
                        words.append(keys_scr[c, pl.ds(r0, SUBLANES), off * LANES:(off + 1) * LANES])
                    else:
                        words.append(jnp.full((SUBLANES, LANES), INT_MIN, I32))
                words = _bit_transpose32(words)
                words[0] = ~words[0]
                for i in range(WORD_BITS):
                    planes_scr[i, pl.ds(r0, SUBLANES), st * LANES:(st + 1) * LANES] = words[i]
            return carry

        lax.fori_loop(0, qb_rows // SUBLANES, pack_rows, 0)

        lane_ones = jnp.ones((LANES, LANES), BF16)

        def row_total(pc):
            tot = pc[:, :LANES]
            for st in range(1, n_sets):
                tot = tot + pc[:, st * LANES:(st + 1) * LANES]
            return jnp.dot(tot.astype(F32).astype(BF16), lane_ones, preferred_element_type=F32)

        def all_sets(mask):
            return jnp.concatenate([mask] * n_sets, axis=1)

        def select_two_bits(i, carry):
            cand, above, prefix = carry
            hi = planes_scr[2 * i]
            lo = planes_scr[2 * i + 1]
            c1 = cand & hi
            c0 = cand ^ c1
            c11 = c1 & lo
            c10 = c1 ^ c11
            c01 = c0 & lo
            c00 = c0 ^ c01
            r11 = above + row_total(lax.population_count(c11))
            r10 = r11 + row_total(lax.population_count(c10))
            r01 = r10 + row_total(lax.population_count(c01))
            t11 = r11 >= k_top
            t10 = r10 >= k_top
            t01 = r01 >= k_top
            cand = jnp.where(all_sets(t11), c11,
                             jnp.where(all_sets(t10), c10, jnp.where(all_sets(t01), c01, c00)))
            above = jnp.where(t11, above, jnp.where(t10, r11, jnp.where(t01, r10, r01)))
            bits = jnp.where(t11, 3, jnp.where(t10, 2, jnp.where(t01, 1, 0)))
            prefix = prefix | (bits << (WORD_BITS - 2 - 2 * i))
            return cand, above, prefix

        cand, above, prefix = lax.fori_loop(
            0, WORD_BITS // 2, select_two_bits,
            (jnp.full((qb_rows, n_sets * LANES), -1, I32),
             jnp.zeros((qb_rows, LANES), F32), jnp.zeros((qb_rows, LANES), I32)))
        equal = row_total(lax.population_count(cand))
        thr_scr[...] = (prefix ^ INT_MIN)[:, :1]
        tie_scr[...] = jnp.full((qb_rows, 1), INT_MAX, I32)
        ngt_scr[...] = above[:, :1]
        nge_scr[...] = (above + equal)[:, :1]

        rg = min(qb_rows, SEARCH_ROWS)
        lane_col = lax.broadcasted_iota(I32, (rg, LANES), 1)

        for r in range(qb_rows // rg):
            rs = slice(r * rg, (r + 1) * rg)

            def count(pred, rs=rs):
                def body(c, acc):
                    for t in range(kb_rows // LANES):
                        kk = keys_scr[c, rs, t * LANES:(t + 1) * LANES]
                        idx = lane_col + (c * kb_rows + t * LANES)
                        acc = acc + jnp.where(pred(kk, idx), 1.0, 0.0)
                    return acc
                acc = lax.fori_loop(0, n_chunk, body, jnp.zeros((rg, LANES), F32))
                return jnp.sum(acc, axis=1, keepdims=True)

            def wide(v):
                return jnp.broadcast_to(v, (rg, LANES))

            @pl.when(jnp.max(nge_scr[rs, :]) > k_top)
            def _ties(count=count, rs=rs):
                thr_w = wide(thr_scr[rs, :])
                need = k_top - ngt_scr[rs, :]

                def tie_bit(i, jt):
                    cand = jt + (jnp.int32(1) << (idx_bits - 1 - i))
                    cand_w = wide(cand)
                    cnt = count(lambda kk, idx: (kk == thr_w) & (idx < cand_w))
                    return jnp.where(cnt < need, cand, jt)

                tie_scr[rs, :] = lax.fori_loop(0, idx_bits, tie_bit, jnp.zeros((rg, 1), I32))

        m_scr[...] = jnp.full(m_scr.shape, M_INIT, F32)
        l_scr[...] = jnp.zeros_like(l_scr)
        acc_scr[...] = jnp.zeros_like(acc_scr)

    def _attend():
        ones = jnp.ones((kb_rows, HEAD_DIM), BF16)

        kk = keys_scr[kb]
        thr = thr_scr[...]
        lim = jnp.minimum(tie_scr[...], row[:, :1])
        tie_bias = jnp.where(col + kb * kb_rows <= lim, 0.0, MASK_VALUE)
        bias_scr[...] = jnp.where(kk > thr, 0.0, jnp.where(kk == thr, tie_bias, MASK_VALUE))

        def logits(h, s_ref, r_ref):
            s = lax.dot_general(q_ref[0, h], k_ref[0, h], nt, preferred_element_type=F32) + bias_scr[...]
            s_ref[...] = s
            r_ref[...] = jnp.broadcast_to(jnp.max(s, axis=1, keepdims=True), r_ref.shape)

        def accumulate(h, s_ref, r_ref):
            m_old = m_scr[h]
            m_new = jnp.maximum(m_old, r_ref[...])
            alpha = jnp.exp2(m_old - m_new)
            p = jnp.concatenate(
                [jnp.exp2(s_ref[:, t * LANES:(t + 1) * LANES] - m_new).astype(BF16)
                 for t in range(kb_rows // LANES)], axis=1)
            v_ext = jnp.concatenate([v_ref[0, h], ones], axis=1)
            pv = jnp.dot(p, v_ext, preferred_element_type=F32)
            acc_scr[h] = alpha * acc_scr[h] + pv[:, :HEAD_DIM]
            l_scr[h] = alpha * l_scr[h] + pv[:, HEAD_DIM:]
            m_scr[h] = m_new

        bufs = ((s0_scr, r0_scr), (s1_scr, r1_scr))
        logits(0, *bufs[0])
        for h in range(heads):
            if h + 1 < heads:
                logits(h + 1, *bufs[(h + 1) % 2])
            accumulate(h, *bufs[h % 2])

    _attend()

    @pl.when(kb == last_kb)
    def _finish():
        for h in range(heads):
            o_ref[0, :, h * HEAD_DIM:(h + 1) * HEAD_DIM] = (acc_scr[h] / l_scr[h]).astype(o_ref.dtype)


def _dsa_attention(qkv, qi, kia, kib, wi, k_top):
    bsz, heads3, seq, _ = qkv.shape
    heads = heads3 // 3
    d = heads * HEAD_DIM
    n_pair = qi.shape[-1] // LANES
    qb_rows = _pick(seq, (256, 128))
    kb_rows = _pick(seq, (512, 256, 128))
    n_kb = seq // kb_rows
    n_sets = -(-seq // (WORD_BITS * LANES))

    pairs = [(i, j) for i in range(seq // qb_rows) for j in range((i * qb_rows + qb_rows - 1) // kb_rows + 1)]
    qb_tab = jnp.asarray([p[0] for p in pairs], I32)
    kb_tab = jnp.asarray([p[1] for p in pairs], I32)

    kern = functools.partial(_dsa_kernel, n_pair=n_pair, k_top=k_top, idx_bits=seq.bit_length())
    grid_spec = pltpu.PrefetchScalarGridSpec(
        num_scalar_prefetch=2,
        grid=(bsz, len(pairs)),
        in_specs=[
            pl.BlockSpec((1, heads, qb_rows, HEAD_DIM), lambda b, s, qt, kt: (b, 0, qt[s], 0)),
            pl.BlockSpec((1, heads, kb_rows, HEAD_DIM), lambda b, s, qt, kt: (b, 1, kt[s], 0)),
            pl.BlockSpec((1, heads, kb_rows, HEAD_DIM), lambda b, s, qt, kt: (b, 2, kt[s], 0)),
            pl.BlockSpec((1, qb_rows, n_pair * LANES), lambda b, s, qt, kt: (b, qt[s], 0)),
            pl.BlockSpec((1, seq, LANES), lambda b, s, qt, kt: (b, 0, 0)),
            pl.BlockSpec((1, seq, LANES), lambda b, s, qt, kt: (b, 0, 0)),
            pl.BlockSpec((1, qb_rows, LANES), lambda b, s, qt, kt: (b, qt[s], 0)),
        ],
        out_specs=pl.BlockSpec((1, qb_rows, d), lambda b, s, qt, kt: (b, qt[s], 0)),
        scratch_shapes=[
            pltpu.VMEM((n_kb, qb_rows, kb_rows), I32),
            pltpu.VMEM((WORD_BITS, qb_rows, n_sets * LANES), I32),
            pltpu.VMEM((qb_rows, 1), I32),
            pltpu.VMEM((qb_rows, 1), I32),
            pltpu.VMEM((qb_rows, 1), F32),
            pltpu.VMEM((qb_rows, 1), F32),
            pltpu.VMEM((heads, qb_rows, LANES), F32),
            pltpu.VMEM((heads, qb_rows, LANES), F32),
            pltpu.VMEM((heads, qb_rows, HEAD_DIM), F32),
            pltpu.VMEM((qb_rows, kb_rows), F32),
            pltpu.VMEM((qb_rows, kb_rows), F32),
            pltpu.VMEM((qb_rows, kb_rows), F32),
            pltpu.VMEM((qb_rows, LANES), F32),
            pltpu.VMEM((qb_rows, LANES), F32),
        ],
    )
    return pl.pallas_call(
        kern,
        out_shape=jax.ShapeDtypeStruct((bsz, seq, d), BF16),
        grid_spec=grid_spec,
        compiler_params=_params(("parallel", "arbitrary")),
        name="dsa_select_attend",
    )(qb_tab, kb_tab, qkv, qkv, qkv, qi, kia, kib, wi)


def _rope_tables(positions, dim, repeats):
    inv = 1.0 / (ROPE_THETA ** (jnp.arange(0, dim, 2, dtype=F32) / dim))
    ang = positions.astype(F32)[..., None] * inv
    cos, sin = jnp.cos(ang), jnp.sin(ang)
    cos_t = jnp.concatenate([cos, cos] * repeats, axis=-1)
    sin_t = jnp.concatenate([-sin, sin] * repeats, axis=-1)
    return cos_t, sin_t


def kernel(x, c, positions, ada_w, ada_b, ln_g, ln_b, s5_in_w, s5_a_re, s5_a_im, s5_log_dt, s5_b_re, s5_b_im, s5_c_re, s5_c_im, s5_d, s5_glu_w, s5_glu_b, dsa_in_w, dsa_out_w, ffn_w_in, ffn_w_out):
    bsz, seq, d = x.shape
    depth = ada_w.shape[0]
    alpha = (2.0 * depth) ** 0.25
    idx_heads = (dsa_in_w.shape[-1] - 3 * d - IDX_DIM) // (IDX_DIM + 1)
    k_top = min(TOPK_MAX, seq // 4)
    seg = min(S5_SEG, seq // SUBLANES)

    cos_h, sin_h = _rope_tables(positions, HEAD_DIM, 1)
    cos_i, sin_i = _rope_tables(positions, IDX_DIM, 2)
    mod = _ada_mod(c, ada_w, ada_b)

    for i in range(depth):
        sh1, sc1, g1, sh2, sc2, g2 = [m[:, None, :] for m in jnp.split(mod[i], 6, axis=-1)]
        j = i // 2
        if i % 2 == 0:
            prep =_s5_discretize(s5_a_re[j], s5_a_im[j], s5_log_dt[j], s5_b_re[j], s5_b_im[j],
                                  s5_c_re[j], s5_c_im[j], seg)
            u = _mod_matmul(x, sc1, sh1, s5_in_w[j].astype(BF16), F32)
            gl = _s5_scan(u, prep, s5_d[j], seg)
            x1 = _matmul_res_ln(gl, s5_glu_w[j].astype(BF16), s5_glu_b[j], x, g1,
                                ln_g[i, 0], ln_b[i, 0], alpha, glu=True)
        else:
            w = dsa_in_w[j]
            n_qi = idx_heads * IDX_DIM
            w_qkv = w[:, :3 * d].astype(BF16)
            pad = jnp.zeros((d, LANES - IDX_DIM - idx_heads), w.dtype)
            w_idx = jnp.concatenate(
                [w[:, 3 * d:3 * d + n_qi], w[:, 3 * d + n_qi + idx_heads:], w[:, 3 * d + n_qi:3 * d + n_qi + idx_heads], pad],
                axis=1).astype(BF16)
            qkv = _qkv_proj(x, sc1, sh1, w_qkv, cos_h, sin_h)
            qi, kia, kib, wi = _idx_proj(x, sc1, sh1, w_idx, cos_i, sin_i, idx_heads)
            att = _dsa_attention(qkv, qi, kia, kib, wi, k_top)
            x1 = _matmul_res_ln(att, dsa_out_w[j].astype(BF16), None, x, g1,
                                ln_g[i, 0], ln_b[i, 0], alpha, glu=False)
        x = _ffn_res_ln(x1, sc2, sh2, g2, ffn_w_in[i].astype(BF16), ffn_w_out[i].astype(BF16),
                        ln_g[i, 1], ln_b[i, 1], alpha)
    return x
```

```python
import functools
import math

import jax
import jax.numpy as jnp
from jax import lax
from jax.experimental import pallas as pl
from jax.experimental.pallas import tpu as pltpu

F32 = jnp.float32
BF16 = jnp.bfloat16
I32 = jnp.int32

S5_GROUP = 16
S5_STATE = 64
HEAD_DIM = 128
IDX_DIM = 64
TOPK_MAX = 256
ROPE_THETA = 10000.0
LN_EPS = 1e-5

LANES = 128
SUBLANES = 8
MXU_WIDTH = 256
VMEM_LIMIT_BYTES = 56 * 1024 * 1024
RESIDENT_WEIGHT_BYTES = 16 * 1024 * 1024

S5_PACK_GROUPS = 16
S5_SEG = 64

INT_MIN = -(2 ** 31)
INT_MAX = 2 ** 31 - 1
SEARCH_ROWS = 128
WORD_BITS = 32

MASK_VALUE = -2e30
M_INIT = -1e30


def _pick(n, cands):
    for c in cands:
        if n % c == 0:
            return c
    return n


def _row_halves(rows):
    half = rows // 2
    return (slice(0, half), slice(half, rows))


def _params(sem):
    return pltpu.CompilerParams(dimension_semantics=sem, vmem_limit_bytes=VMEM_LIMIT_BYTES)


def _ada_kernel(c_ref, w_ref, b_ref, o_ref):
    ca = jax.nn.silu(c_ref[...]).astype(BF16)
    o_ref[0] = jnp.dot(ca, w_ref[0].astype(BF16), preferred_element_type=F32) + b_ref[0]


def _ada_mod(c, ada_w, ada_b):
    bsz, d = c.shape
    depth, _, n = ada_w.shape
    rows = SUBLANES * ((bsz + SUBLANES - 1) // SUBLANES)
    cp = jnp.zeros((rows, d), F32).at[:bsz].set(c)
    tn = _pick(n, (1024, 512, 256, 128))
    out = pl.pallas_call(
        _ada_kernel,
        out_shape=jax.ShapeDtypeStruct((depth, rows, n), F32),
        grid=(depth, n // tn),
        in_specs=[
            pl.BlockSpec((rows, d), lambda l, j: (0, 0)),
            pl.BlockSpec((1, d, tn), lambda l, j: (l, 0, j)),
            pl.BlockSpec((1, 1, tn), lambda l, j: (l, 0, j)),
        ],
        out_specs=pl.BlockSpec((1, rows, tn), lambda l, j: (l, 0, j)),
        compiler_params=_params(("arbitrary", "arbitrary")),
        name="ada_mod",
    )(cp, ada_w, ada_b.reshape(depth, 1, n))
    return out[:, :bsz]


def _modulate(x_ref, sc_ref, sh_ref):
    return (x_ref[0] * (1.0 + sc_ref[0]) + sh_ref[0]).astype(BF16)


def _modmm_kernel(x_ref, sc_ref, sh_ref, w_ref, o_ref, h_scr):
    @pl.when(pl.program_id(2) == 0)
    def _():
        h_scr[...] = _modulate(x_ref, sc_ref, sh_ref)

    for rs in _row_halves(h_scr.shape[0]):
        o_ref[0, rs, :] = jnp.dot(h_scr[rs, :], w_ref[...], preferred_element_type=F32).astype(o_ref.dtype)


def _mod_matmul(x, sc, sh, w, out_dtype):
    bsz, seq, d = x.shape
    n = w.shape[1]
    tm = _pick(seq, (512, 256, 128))
    tn = n if 2 * d * n * w.dtype.itemsize <= RESIDENT_WEIGHT_BYTES else _pick(n, (1024, 512, 256, 128))
    return pl.pallas_call(
        _modmm_kernel,
        out_shape=jax.ShapeDtypeStruct((bsz, seq, n), out_dtype),
        grid=(bsz, seq // tm, n // tn),
        in_specs=[
            pl.BlockSpec((1, tm, d), lambda b, i, j: (b, i, 0)),
            pl.BlockSpec((1, 1, d), lambda b, i, j: (b, 0, 0)),
            pl.BlockSpec((1, 1, d), lambda b, i, j: (b, 0, 0)),
            pl.BlockSpec((d, tn), lambda b, i, j: (0, j)),
        ],
        out_specs=pl.BlockSpec((1, tm, tn), lambda b, i, j: (b, i, j)),
        scratch_shapes=[pltpu.VMEM((tm, d), BF16)],
        compiler_params=_params(("parallel", "parallel", "arbitrary")),
        name="mod_matmul",
    )(x, sc, sh, w)


def _s5_kernel(*refs, seg, width, lane_tiles):
    u_refs = refs[:lane_tiles]
    (bb_ref, cb_ref, lre_ref, lim_ref, ltre_ref, ltim_ref, d_ref, o_ref,
     xs_scr, carry_scr, cin_scr, up_scr) = refs[lane_tiles:]
    ns = xs_scr.shape[1] // 2

    @pl.when(pl.program_id(2) == 0)
    def _():
        carry_scr[...] = jnp.zeros_like(carry_scr)

    for t, u_ref in enumerate(u_refs):
        for i in range(seg):
            up_scr[t, i * SUBLANES:(i + 1) * SUBLANES, :] = u_ref[0, pl.ds(i, SUBLANES, stride=seg), :]
    u = jnp.concatenate([up_scr[t] for t in range(len(u_refs))], axis=1)
    halves = _row_halves(u.shape[0])
    for rs in halves:
        xs_scr[rs, :] = jnp.dot(u[rs].astype(BF16), bb_ref[0], preferred_element_type=F32)

    for part in range(ns // width):
        cr = slice(part * width, (part + 1) * width)
        ci = slice(ns + part * width, ns + (part + 1) * width)
        lr = jnp.broadcast_to(lre_ref[0, :, cr], (SUBLANES, width))
        li = jnp.broadcast_to(lim_ref[0, :, cr], (SUBLANES, width))

        def local_step(i, st, cr=cr, ci=ci, lr=lr, li=li):
            sr, si = st
            r0 = pl.multiple_of(i * SUBLANES, SUBLANES)
            nr = lr * sr - li * si + xs_scr[pl.ds(r0, SUBLANES), cr]
            ni = lr * si + li * sr + xs_scr[pl.ds(r0, SUBLANES), ci]
            xs_scr[pl.ds(r0, SUBLANES), cr] = nr
            xs_scr[pl.ds(r0, SUBLANES), ci] = ni
            return nr, ni

        zero = jnp.zeros((SUBLANES, width), F32)
        er, ei = lax.fori_loop(0, seg, local_step, (zero, zero), unroll=2)

        ltr = ltre_ref[0, :, cr]
        lti = ltim_ref[0, :, cr]
        c_r = carry_scr[:, cr]
        c_i = carry_scr[:, ci]
        for s in range(SUBLANES):
            cin_scr[s:s + 1, cr] = c_r
            cin_scr[s:s + 1, ci] = c_i
            e_r = er[s:s + 1, :]
            e_i = ei[s:s + 1, :]
            c_r, c_i = ltr * c_r - lti * c_i + e_r, ltr * c_i + lti * c_r + e_i
        carry_scr[:, cr] = c_r
        carry_scr[:, ci] = c_i

        def carry_step(i, st, cr=cr, ci=ci, lr=lr, li=li):
            pr, pi_ = st
            r0 = pl.multiple_of(i * SUBLANES, SUBLANES)
            nr = lr * pr - li * pi_
            ni = lr * pi_ + li * pr
            xs_scr[pl.ds(r0, SUBLANES), cr] += nr
            xs_scr[pl.ds(r0, SUBLANES), ci] += ni
            return nr, ni

        lax.fori_loop(0, seg, carry_step, (cin_scr[:, cr], cin_scr[:, ci]), unroll=2)

    for rs in halves:
        y = jnp.dot(xs_scr[rs, :].astype(BF16), cb_ref[0], preferred_element_type=F32)
        g = jax.nn.gelu(y + d_ref[0] * u[rs])
        for t in range(len(u_refs)):
            up_scr[t, rs, :] = g[:, t * LANES:(t + 1) * LANES]
    for t in range(len(u_refs)):
        for s in range(SUBLANES):
            o_ref[0, s * seg:(s + 1) * seg, t * LANES:(t + 1) * LANES] = (
                up_scr[t, pl.ds(s, seg, stride=SUBLANES), :].astype(o_ref.dtype))


def _cmul(ar, ai, br, bi):
    return ar * br - ai * bi, ar * bi + ai * br


def _s5_discretize(a_re, a_im, log_dt, b_re, b_im, c_re, c_im, seg):
    g, n = a_re.shape
    p = b_re.shape[-1]
    pg = S5_PACK_GROUPS
    packs = g // pg
    a_re, a_im = a_re.astype(F32), a_im.astype(F32)
    dt = jnp.exp(log_dt.astype(F32))[:, None]
    mag = jnp.exp(a_re * dt)
    lb_re, lb_im = mag * jnp.cos(a_im * dt), mag * jnp.sin(a_im * dt)
    den = a_re * a_re + a_im * a_im
    nr, ni = lb_re - 1.0, lb_im
    f_re = (nr * a_re + ni * a_im) / den
    f_im = (ni * a_re - nr * a_im) / den
    bb_re, bb_im = _cmul(f_re[..., None], f_im[..., None], b_re.astype(F32), b_im.astype(F32))
    lt_re, lt_im = lb_re, lb_im
    for _ in range(int(math.log2(seg))):
        lt_re, lt_im = _cmul(lt_re, lt_im, lt_re, lt_im)
    eye = jnp.eye(pg, dtype=F32)

    def blk_diag(m):
        r, c = m.shape[1:]
        blocks = m.reshape(packs, pg, r, 1, c) * eye[None, :, None, :, None]
        return blocks.reshape(packs, pg * r, pg * c)

    def in_blk(m):
        return blk_diag(jnp.swapaxes(m, 1, 2))

    def out_blk(m):
        return blk_diag(jnp.swapaxes(m, 1, 2))

    b_blk = jnp.concatenate([in_blk(bb_re), in_blk(bb_im)], axis=-1).astype(BF16)
    c_blk = jnp.concatenate([out_blk(c_re.astype(F32)), out_blk(-c_im.astype(F32))], axis=1).astype(BF16)

    def vec(m):
        return m.reshape(packs, 1, pg * n)

    return b_blk, c_blk, vec(lb_re), vec(lb_im), vec(lt_re), vec(lt_im)


def _s5_scan(u, prep, d_skip, seg):
    bsz, seq, d = u.shape
    b_blk, c_blk, lre, lim, ltre, ltim = prep
    packs, pw, ns2 = b_blk.shape
    ns = ns2 // 2
    rows = SUBLANES * seg
    width = _pick(ns, (1024, 512, 256, 128))
    lane_tiles = pw // LANES
    kern = functools.partial(_s5_kernel, seg=seg, width=width, lane_tiles=lane_tiles)
    vspec = pl.BlockSpec((1, 1, ns), lambda b, k, m: (k, 0, 0))
    u_specs = [pl.BlockSpec((1, rows, LANES), lambda b, k, m, t=t: (b, m, k * lane_tiles + t))
               for t in range(lane_tiles)]
    return pl.pallas_call(
        kern,
        out_shape=jax.ShapeDtypeStruct((bsz, seq, d), BF16),
        grid=(bsz, packs, seq // rows),
        in_specs=u_specs + [
            pl.BlockSpec((1, pw, ns2), lambda b, k, m: (k, 0, 0)),
            pl.BlockSpec((1, ns2, pw), lambda b, k, m: (k, 0, 0)),
            vspec, vspec, vspec, vspec,
            pl.BlockSpec((1, 1, pw), lambda b, k, m: (k, 0, 0)),
        ],
        out_specs=pl.BlockSpec((1, rows, pw), lambda b, k, m: (b, m, k)),
        scratch_shapes=[
            pltpu.VMEM((rows, ns2), F32),
            pltpu.VMEM((1, ns2), F32),
            pltpu.VMEM((SUBLANES, ns2), F32),
            pltpu.VMEM((lane_tiles, rows, LANES), F32),
        ],
        compiler_params=_params(("parallel", "parallel", "arbitrary")),
        name="s5_scan",
    )(*([u] * lane_tiles), b_blk, c_blk, lre, lim, ltre, ltim, d_skip.reshape(packs, 1, pw).astype(F32))


def _layer_norm_rows(tiles, lng_ref, lnb_ref, o_ref, d):
    tn = tiles[0].shape[1]
    tot = tiles[0].sum(axis=1, keepdims=True)
    for t in tiles[1:]:
        tot = tot + t.sum(axis=1, keepdims=True)
    mu = tot * (1.0 / d)
    sq = jnp.square(tiles[0] - mu).sum(axis=1, keepdims=True)
    for t in tiles[1:]:
        sq = sq + jnp.square(t - mu).sum(axis=1, keepdims=True)
    inv = lax.rsqrt(sq * (1.0 / d) + LN_EPS)
    for k, t in enumerate(tiles):
        cs = slice(k * tn, (k + 1) * tn)
        o_ref[0, :, cs] = ((t - mu) * inv * lng_ref[:, cs] + lnb_ref[:, cs]).astype(o_ref.dtype)


def _mmln_kernel(*refs, glu, nj, alpha, d):
    if glu:
        a_ref, w1_ref, w2_ref, b1_ref, b2_ref, x_ref, g_ref, lng_ref, lnb_ref, o_ref, r_scr = refs
    else:
        a_ref, w1_ref, x_ref, g_ref, lng_ref, lnb_ref, o_ref, r_scr = refs
    j = pl.program_id(2)
    for rs in _row_halves(a_ref.shape[1]):
        a = a_ref[0, rs, :]
        y = jnp.dot(a, w1_ref[...], preferred_element_type=F32)
        if glu:
            y = y + b1_ref[...]
            gate = jnp.dot(a, w2_ref[...], preferred_element_type=F32) + b2_ref[...]
            y = y * jax.nn.sigmoid(gate)
        r_scr[j, rs, :] = alpha * x_ref[0, rs, :] + (1.0 + g_ref[0]) * y

    @pl.when(j == nj - 1)
    def _():
        _layer_norm_rows([r_scr[t] for t in range(nj)], lng_ref, lnb_ref, o_ref, d)


def _matmul_res_ln(a, w, bias, xres, gate, ln_g, ln_b, alpha, glu):
    bsz, seq, k = a.shape
    d = xres.shape[-1]
    if glu:
        tm = _pick(seq, (1024, 512, 256, 128))
        tn = _pick(d, (512, 256, 128))
    else:
        tm = _pick(seq, (512, 256, 128))
        tn = d if 2 * k * d * w.dtype.itemsize <= RESIDENT_WEIGHT_BYTES else _pick(d, (512, 256, 128))
    nj = d // tn
    a_spec = pl.BlockSpec((1, tm, k), lambda b, i, j: (b, i, 0))
    w1_spec = pl.BlockSpec((k, tn), lambda b, i, j: (0, j))
    tail_specs = [
        pl.BlockSpec((1, tm, tn), lambda b, i, j: (b, i, j)),
        pl.BlockSpec((1, 1, tn), lambda b, i, j: (b, 0, j)),
        pl.BlockSpec((1, d), lambda b, i, j: (0, 0)),
        pl.BlockSpec((1, d), lambda b, i, j: (0, 0)),
    ]
    tail = (xres, gate, ln_g.reshape(1, d).astype(F32), ln_b.reshape(1, d).astype(F32))
    if glu:
        in_specs = [a_spec, w1_spec,
                    pl.BlockSpec((k, tn), lambda b, i, j: (0, j + nj)),
                    pl.BlockSpec((1, tn), lambda b, i, j: (0, j)),
                    pl.BlockSpec((1, tn), lambda b, i, j: (0, j + nj))] + tail_specs
        b2d = bias.reshape(1, 2 * d).astype(F32)
        args = (a, w, w, b2d, b2d) + tail
    else:
        in_specs = [a_spec, w1_spec] + tail_specs
        args = (a, w) + tail
    kern = functools.partial(_mmln_kernel, glu=glu, nj=nj, alpha=alpha, d=d)
    return pl.pallas_call(
        kern,
        out_shape=jax.ShapeDtypeStruct((bsz, seq, d), F32),
        grid=(bsz, seq // tm, nj),
        in_specs=in_specs,
        out_specs=pl.BlockSpec((1, tm, d), lambda b, i, j: (b, i, 0)),
        scratch_shapes=[pltpu.VMEM((nj, tm, tn), F32)],
        compiler_params=_params(("parallel", "parallel", "arbitrary")),
        name="matmul_res_ln",
    )(*args)


def _ffn_kernel(x_ref, sc_ref, sh_ref, g_ref, wg_ref, wu_ref, wo_ref, lng_ref, lnb_ref, o_ref,
                h_scr, acc_scr, *, nf, alpha, d, tn):
    f = pl.program_id(2)

    @pl.when(f == 0)
    def _():
        h_scr[...] = _modulate(x_ref, sc_ref, sh_ref)
        acc_scr[...] = jnp.zeros_like(acc_scr)

    h = h_scr[...]
    a_g = jnp.dot(h, wg_ref[...], preferred_element_type=F32)
    a_u = jnp.dot(h, wu_ref[...], preferred_element_type=F32)
    act = (jax.nn.silu(a_g) * a_u).astype(BF16)
    acc_scr[...] += jnp.dot(act, wo_ref[...], preferred_element_type=F32)

    @pl.when(f == nf - 1)
    def _():
        tiles = []
        for k in range(d // tn):
            cs = slice(k * tn, (k + 1) * tn)
            tiles.append(alpha * x_ref[0, :, cs] + (1.0 + g_ref[0, :, cs]) * acc_scr[:, cs])
        _layer_norm_rows(tiles, lng_ref, lnb_ref, o_ref, d)


def _ffn_res_ln(x, sc, sh, gate, w_in, w_out, ln_g, ln_b, alpha):
    bsz, seq, d = x.shape
    dff = w_out.shape[0]
    tm = _pick(seq, (512, 256, 128))
    tf = _pick(dff, (512, 256, 128))
    nf = dff // tf
    tn = _pick(d, (512, 256, 128))
    vec = pl.BlockSpec((1, 1, d), lambda b, i, f: (b, 0, 0))
    kern = functools.partial(_ffn_kernel, nf=nf, alpha=alpha, d=d, tn=tn)
    return pl.pallas_call(
        kern,
        out_shape=jax.ShapeDtypeStruct((bsz, seq, d), F32),
        grid=(bsz, seq // tm, nf),
        in_specs=[
            pl.BlockSpec((1, tm, d), lambda b, i, f: (b, i, 0)),
            vec, vec, vec,
            pl.BlockSpec((d, tf), lambda b, i, f: (0, f)),
            pl.BlockSpec((d, tf), lambda b, i, f: (0, f + nf)),
            pl.BlockSpec((tf, d), lambda b, i, f: (f, 0)),
            pl.BlockSpec((1, d), lambda b, i, f: (0, 0)),
            pl.BlockSpec((1, d), lambda b, i, f: (0, 0)),
        ],
        out_specs=pl.BlockSpec((1, tm, d), lambda b, i, f: (b, i, 0)),
        scratch_shapes=[pltpu.VMEM((tm, d), BF16), pltpu.VMEM((tm, d), F32)],
        compiler_params=_params(("parallel", "parallel", "arbitrary")),
        name="ffn_res_ln",
    )(x, sc, sh, gate, w_in, w_in, w_out, ln_g.reshape(1, d).astype(F32), ln_b.reshape(1, d).astype(F32))


def _qkv_kernel(x_ref, sc_ref, sh_ref, w_ref, cos_ref, sin_ref, o_ref, h_scr, *, tiles_per_tensor, q_scale):
    j = pl.program_id(2)

    @pl.when(j == 0)
    def _():
        h_scr[...] = _modulate(x_ref, sc_ref, sh_ref)

    tensor = j // tiles_per_tensor
    scale = jnp.where(tensor == 0, q_scale, 1.0)
    is_v = tensor == 2
    chunk = min(MXU_WIDTH, w_ref.shape[1])
    for rs in _row_halves(h_scr.shape[0]):
        a = jnp.where(is_v, 1.0, cos_ref[0, rs, :] * scale)
        b = jnp.where(is_v, 0.0, sin_ref[0, rs, :] * scale)
        for c in range(w_ref.shape[1] // chunk):
            acc = jnp.dot(h_scr[rs, :], w_ref[:, c * chunk:(c + 1) * chunk], preferred_element_type=F32)
            for hh in range(chunk // HEAD_DIM):
                xc = acc[:, hh * HEAD_DIM:(hh + 1) * HEAD_DIM]
                xc = xc * a + pltpu.roll(xc, HEAD_DIM // 2, 1) * b
                o_ref[0, c * (chunk // HEAD_DIM) + hh, rs, :] = xc.astype(o_ref.dtype)


def _qkv_proj(x, sc, sh, w_qkv, cos_t, sin_t):
    bsz, seq, d = x.shape
    tm = _pick(seq, (1024, 512, 256, 128))
    tn = _pick(d, (1024, 512, 256, 128))
    hpt = tn // HEAD_DIM
    kern = functools.partial(_qkv_kernel, tiles_per_tensor=d // tn, q_scale=HEAD_DIM ** -0.5 * math.log2(math.e))
    return pl.pallas_call(
        kern,
        out_shape=jax.ShapeDtypeStruct((bsz, 3 * d // HEAD_DIM, seq, HEAD_DIM), BF16),
        grid=(bsz, seq // tm, 3 * d // tn),
        in_specs=[
            pl.BlockSpec((1, tm, d), lambda b, i, j: (b, i, 0)),
            pl.BlockSpec((1, 1, d), lambda b, i, j: (b, 0, 0)),
            pl.BlockSpec((1, 1, d), lambda b, i, j: (b, 0, 0)),
            pl.BlockSpec((d, tn), lambda b, i, j: (0, j)),
            pl.BlockSpec((1, tm, HEAD_DIM), lambda b, i, j: (b, i, 0)),
            pl.BlockSpec((1, tm, HEAD_DIM), lambda b, i, j: (b, i, 0)),
        ],
        out_specs=pl.BlockSpec((1, hpt, tm, HEAD_DIM), lambda b, i, j: (b, j, i, 0)),
        scratch_shapes=[pltpu.VMEM((tm, d), BF16)],
        compiler_params=_params(("parallel", "parallel", "arbitrary")),
        name="dsa_qkv_proj",
    )(x, sc, sh, w_qkv, cos_t, sin_t)


def _idx_kernel(x_ref, sc_ref, sh_ref, w_ref, cos_ref, sin_ref, qi_ref, kia_ref, kib_ref, wi_ref,
                *, n_pair, idx_heads, w_scale):
    h = _modulate(x_ref, sc_ref, sh_ref)
    halves = _row_halves(h.shape[0])
    accs = [jnp.dot(h[rs], w_ref[...], preferred_element_type=F32) for rs in halves]
    lane = lax.broadcasted_iota(I32, (halves[0].stop, LANES), 1)
    first_half = (lane % IDX_DIM) < IDX_DIM // 2

    for rs, acc in zip(halves, accs):
        cos = cos_ref[0, rs, :]
        sin = sin_ref[0, rs, :]

        def rope(xc, cos=cos, sin=sin):
            partner = jnp.where(first_half, pltpu.roll(xc, LANES - IDX_DIM // 2, 1),
                                pltpu.roll(xc, IDX_DIM // 2, 1))
            return xc * cos + partner * sin

        for p in range(n_pair):
            cs = slice(p * LANES, (p + 1) * LANES)
            qi_ref[0, rs, cs] = rope(acc[:, cs]).astype(qi_ref.dtype)
        last = acc[:, n_pair * LANES:]
        ka = jnp.where(lane < IDX_DIM, rope(last), 0.0)
        kia_ref[0, rs, :] = ka.astype(kia_ref.dtype)
        kib_ref[0, rs, :] = pltpu.roll(ka, IDX_DIM, 1).astype(kib_ref.dtype)
        wi_ref[0, rs, :] = jnp.where(lane < idx_heads, pltpu.roll(last, LANES - IDX_DIM, 1), 0.0) * w_scale


def _idx_proj(x, sc, sh, w_idx, cos_t, sin_t, idx_heads):
    bsz, seq, d = x.shape
    n = w_idx.shape[1]
    n_pair = idx_heads // 2
    tm = _pick(seq, (512, 256, 128))
    kern = functools.partial(_idx_kernel, n_pair=n_pair, idx_heads=idx_heads,
                             w_scale=(idx_heads ** -0.5) * (IDX_DIM ** -0.5))
    row = lambda b, i: (b, i, 0)
    return pl.pallas_call(
        kern,
        out_shape=(
            jax.ShapeDtypeStruct((bsz, seq, n_pair * LANES), BF16),
            jax.ShapeDtypeStruct((bsz, seq, LANES), BF16),
            jax.ShapeDtypeStruct((bsz, seq, LANES), BF16),
            jax.ShapeDtypeStruct((bsz, seq, LANES), F32),
        ),
        grid=(bsz, seq // tm),
        in_specs=[
            pl.BlockSpec((1, tm, d), row),
            pl.BlockSpec((1, 1, d), lambda b, i: (b, 0, 0)),
            pl.BlockSpec((1, 1, d), lambda b, i: (b, 0, 0)),
            pl.BlockSpec((d, n), lambda b, i: (0, 0)),
            pl.BlockSpec((1, tm, LANES), row),
            pl.BlockSpec((1, tm, LANES), row),
        ],
        out_specs=(
            pl.BlockSpec((1, tm, n_pair * LANES), row),
            pl.BlockSpec((1, tm, LANES), row),
            pl.BlockSpec((1, tm, LANES), row),
            pl.BlockSpec((1, tm, LANES), row),
        ),
        compiler_params=_params(("parallel", "parallel")),
        name="dsa_idx_proj",
    )(x, sc, sh, w_idx, cos_t, sin_t)


def _bit_transpose32(words):
    a = list(words)
    j = WORD_BITS // 2
    mask = 0x0000FFFF
    while j:
        k = 0
        while k < WORD_BITS:
            t = (a[k] ^ lax.shift_right_logical(a[k + j], jnp.int32(j))) & jnp.int32(mask)
            a[k] = a[k] ^ t
            a[k + j] = a[k + j] ^ (t << j)
            k = (k + j + 1) & ~j
        j >>= 1
        if j:
            mask = (mask ^ (mask << j)) & 0xFFFFFFFF
    return a


def _dsa_kernel(qb_tab, kb_tab, q_ref, k_ref, v_ref, qi_ref, kia_ref, kib_ref, wi_ref, o_ref,
                keys_scr, planes_scr, thr_scr, tie_scr, ngt_scr, nge_scr,
                m_scr, l_scr, acc_scr, bias_scr, s0_scr, s1_scr, r0_scr, r1_scr,
                *, n_pair, k_top, idx_bits):
    heads, qb_rows, _ = acc_scr.shape
    kb_rows = keys_scr.shape[2]
    qb = qb_tab[pl.program_id(1)]
    kb = kb_tab[pl.program_id(1)]
    q0 = qb * qb_rows
    last_kb = (q0 + qb_rows - 1) // kb_rows
    n_chunk = last_kb + 1
    row = lax.broadcasted_iota(I32, (qb_rows, kb_rows), 0) + q0
    col = lax.broadcasted_iota(I32, (qb_rows, kb_rows), 1)
    nt = (((1,), (1,)), ((), ()))

    @pl.when(kb == 0)
    def _select():
        wi = wi_ref[0]

        def score_chunk(c, carry):
            k0 = pl.multiple_of(c * kb_rows, kb_rows)
            ka = kia_ref[0, pl.ds(k0, kb_rows), :]
            kbm = kib_ref[0, pl.ds(k0, kb_rows), :]
            sc = jnp.zeros((qb_rows, kb_rows), F32)
            for p in range(n_pair):
                qp = qi_ref[0, :, p * LANES:(p + 1) * LANES]
                sa = lax.dot_general(qp, ka, nt, preferred_element_type=F32)
                sb = lax.dot_general(qp, kbm, nt, preferred_element_type=F32)
                sc = sc + wi[:, 2 * p:2 * p + 1] * jnp.maximum(sa, 0.0)
                sc = sc + wi[:, 2 * p + 1:2 * p + 2] * jnp.maximum(sb, 0.0)
            bits = lax.bitcast_convert_type(sc, I32)
            key = bits ^ ((bits >> 31) & INT_MAX)
            keys_scr[c] = jnp.where(col + k0 <= row, key, INT_MIN)
            return carry

        lax.fori_loop(0, n_chunk, score_chunk, 0)

        def fill_chunk(c, carry):
            keys_scr[c] = jnp.full((qb_rows, kb_rows), INT_MIN, I32)
            return carry

        lax.fori_loop(n_chunk, keys_scr.shape[0], fill_chunk, 0)

        slices_per_chunk = kb_rows // LANES
        n_slices = keys_scr.shape[0] * slices_per_chunk
        n_sets = planes_scr.shape[2] // LANES

        def pack_rows(g, carry):
            r0 = pl.multiple_of(g * SUBLANES, SUBLANES)
            for st in range(n_sets):
                words = []
                for s in range(WORD_BITS):
                    sl = st * WORD_BITS + s
                    if sl < n_slices:
                        c, off = divmod(sl, slices_per_chunk)
                        words.append(keys_scr[c, pl.ds(r0, SUBLANES), off * LANES:(off + 1) * LANES])
                    else:
                        words.append(jnp.full((SUBLANES, LANES), INT_MIN, I32))
                words = _bit_transpose32(words)
                words[0] = ~words[0]
                for i in range(WORD_BITS):
                    planes_scr[i, pl.ds(r0, SUBLANES), st * LANES:(st + 1) * LANES] = words[i]
            return carry

        lax.fori_loop(0, qb_rows // SUBLANES, pack_rows, 0)

        lane_ones = jnp.ones((LANES, LANES), BF16)

        def row_total(pc):
            tot = pc[:, :LANES]
            for st in range(1, n_sets):
                tot = tot + pc[:, st * LANES:(st + 1) * LANES]
            return jnp.dot(tot.astype(F32).astype(BF16), lane_ones, preferred_element_type=F32)

        def all_sets(mask):
            return jnp.concatenate([mask] * n_sets, axis=1)

        def select_two_bits(i, carry):
            cand, above, prefix = carry
            hi = planes_scr[2 * i]
            lo = planes_scr[2 * i + 1]
            c1 = cand & hi
            c0 = cand ^ c1
            c11 = c1 & lo
            c10 = c1 ^ c11
            c01 = c0 & lo
            c00 = c0 ^ c01
            r11 = above + row_total(lax.population_count(c11))
            r10 = r11 + row_total(lax.population_count(c10))
            r01 = r10 + row_total(lax.population_count(c01))
            t11 = r11 >= k_top
            t10 = r10 >= k_top
            t01 = r01 >= k_top
            cand = jnp.where(all_sets(t11), c11,
                             jnp.where(all_sets(t10), c10, jnp.where(all_sets(t01), c01, c00)))
            above = jnp.where(t11, above, jnp.where(t10, r11, jnp.where(t01, r10, r01)))
            bits = jnp.where(t11, 3, jnp.where(t10, 2, jnp.where(t01, 1, 0)))
            prefix = prefix | (bits << (WORD_BITS - 2 - 2 * i))
            return cand, above, prefix

        cand, above, prefix = lax.fori_loop(
            0, WORD_BITS // 2, select_two_bits,
            (jnp.full((qb_rows, n_sets * LANES), -1, I32),
             jnp.zeros((qb_rows, LANES), F32), jnp.zeros((qb_rows, LANES), I32)))
        equal = row_total(lax.population_count(cand))
        thr_scr[...] = (prefix ^ INT_MIN)[:, :1]
        tie_scr[...] = jnp.full((qb_rows, 1), INT_MAX, I32)
        ngt_scr[...] = above[:, :1]
        nge_scr[...] = (above + equal)[:, :1]

        rg = min(qb_rows, SEARCH_ROWS)
        lane_col = lax.broadcasted_iota(I32, (rg, LANES), 1)

        for r in range(qb_rows // rg):
            rs = slice(r * rg, (r + 1) * rg)

            def count(pred, rs=rs):
                def body(c, acc):
                    for t in range(kb_rows // LANES):
                        kk = keys_scr[c, rs, t * LANES:(t + 1) * LANES]
                        idx = lane_col + (c * kb_rows + t * LANES)
                        acc = acc + jnp.where(pred(kk, idx), 1.0, 0.0)
                    return acc
                acc = lax.fori_loop(0, n_chunk, body, jnp.zeros((rg, LANES), F32))
                return jnp.sum(acc, axis=1, keepdims=True)

            def wide(v):
                return jnp.broadcast_to(v, (rg, LANES))

            @pl.when(jnp.max(nge_scr[rs, :]) > k_top)
            def _ties(count=count, rs=rs):
                thr_w = wide(thr_scr[rs, :])
                need = k_top - ngt_scr[rs, :]

                def tie_bit(i, jt):
                    cand = jt + (jnp.int32(1) << (idx_bits - 1 - i))
                    cand_w = wide(cand)
                    cnt = count(lambda kk, idx: (kk == thr_w) & (idx < cand_w))
                    return jnp.where(cnt < need, cand, jt)

                tie_scr[rs, :] = lax.fori_loop(0, idx_bits, tie_bit, jnp.zeros((rg, 1), I32))

        m_scr[...] = jnp.full(m_scr.shape, M_INIT, F32)
        l_scr[...] = jnp.zeros_like(l_scr)
        acc_scr[...] = jnp.zeros_like(acc_scr)

    def _attend():
        ones = jnp.ones((kb_rows, HEAD_DIM), BF16)

        kk = keys_scr[kb]
        thr = thr_scr[...]
        lim = jnp.minimum(tie_scr[...], row[:, :1])
        tie_bias = jnp.where(col + kb * kb_rows <= lim, 0.0, MASK_VALUE)
        bias_scr[...] = jnp.where(kk > thr, 0.0, jnp.where(kk == thr, tie_bias, MASK_VALUE))

        def logits(h, s_ref, r_ref):
            s = lax.dot_general(q_ref[0, h], k_ref[0, h], nt, preferred_element_type=F32) + bias_scr[...]
            s_ref[...] = s
            r_ref[...] = jnp.broadcast_to(jnp.max(s, axis=1, keepdims=True), r_ref.shape)

        def accumulate(h, s_ref, r_ref):
            m_old = m_scr[h]
            m_new = jnp.maximum(m_old, r_ref[...])
            alpha = jnp.exp2(m_old - m_new)
            p = jnp.concatenate(
                [jnp.exp2(s_ref[:, t * LANES:(t + 1) * LANES] - m_new).astype(BF16)
                 for t in range(kb_rows // LANES)], axis=1)
            v_ext = jnp.concatenate([v_ref[0, h], ones], axis=1)
            pv = jnp.dot(p, v_ext, preferred_element_type=F32)
            acc_scr[h] = alpha * acc_scr[h] + pv[:, :HEAD_DIM]
            l_scr[h] = alpha * l_scr[h] + pv[:, HEAD_DIM:]
            m_scr[h] = m_new

        bufs = ((s0_scr, r0_scr), (s1_scr, r1_scr))
        logits(0, *bufs[0])
        for h in range(heads):
            if h + 1 < heads:
                logits(h + 1, *bufs[(h + 1) % 2])
            accumulate(h, *bufs[h % 2])

    _attend()

    @pl.when(kb == last_kb)
    def _finish():
        for h in range(heads):
            o_ref[0, :, h * HEAD_DIM:(h + 1) * HEAD_DIM] = (acc_scr[h] / l_scr[h]).astype(o_ref.dtype)


def _dsa_attention(qkv, qi, kia, kib, wi, k_top):
    bsz, heads3, seq, _ = qkv.shape
    heads = heads3 // 3
    d = heads * HEAD_DIM
    n_pair = qi.shape[-1] // LANES
    qb_rows = _pick(seq, (256, 128))
    kb_rows = _pick(seq, (512, 256, 128))
    n_kb = seq // kb_rows
    n_sets = -(-seq // (WORD_BITS * LANES))

    pairs = [(i, j) for i in range(seq // qb_rows) for j in range((i * qb_rows + qb_rows - 1) // kb_rows + 1)]
    qb_tab = jnp.asarray([p[0] for p in pairs], I32)
    kb_tab = jnp.asarray([p[1] for p in pairs], I32)

    kern = functools.partial(_dsa_kernel, n_pair=n_pair, k_top=k_top, idx_bits=seq.bit_length())
    grid_spec = pltpu.PrefetchScalarGridSpec(
        num_scalar_prefetch=2,
        grid=(bsz, len(pairs)),
        in_specs=[
            pl.BlockSpec((1, heads, qb_rows, HEAD_DIM), lambda b, s, qt, kt: (b, 0, qt[s], 0)),
            pl.BlockSpec((1, heads, kb_rows, HEAD_DIM), lambda b, s, qt, kt: (b, 1, kt[s], 0)),
            pl.BlockSpec((1, heads, kb_rows, HEAD_DIM), lambda b, s, qt, kt: (b, 2, kt[s], 0)),
            pl.BlockSpec((1, qb_rows, n_pair * LANES), lambda b, s, qt, kt: (b, qt[s], 0)),
            pl.BlockSpec((1, seq, LANES), lambda b, s, qt, kt: (b, 0, 0)),
            pl.BlockSpec((1, seq, LANES), lambda b, s, qt, kt: (b, 0, 0)),
            pl.BlockSpec((1, qb_rows, LANES), lambda b, s, qt, kt: (b, qt[s], 0)),
        ],
        out_specs=pl.BlockSpec((1, qb_rows, d), lambda b, s, qt, kt: (b, qt[s], 0)),
        scratch_shapes=[
            pltpu.VMEM((n_kb, qb_rows, kb_rows), I32),
            pltpu.VMEM((WORD_BITS, qb_rows, n_sets * LANES), I32),
            pltpu.VMEM((qb_rows, 1), I32),
            pltpu.VMEM((qb_rows, 1), I32),
            pltpu.VMEM((qb_rows, 1), F32),
            pltpu.VMEM((qb_rows, 1), F32),
            pltpu.VMEM((heads, qb_rows, LANES), F32),
            pltpu.VMEM((heads, qb_rows, LANES), F32),
            pltpu.VMEM((heads, qb_rows, HEAD_DIM), F32),
            pltpu.VMEM((qb_rows, kb_rows), F32),
            pltpu.VMEM((qb_rows, kb_rows), F32),
            pltpu.VMEM((qb_rows, kb_rows), F32),
            pltpu.VMEM((qb_rows, LANES), F32),
            pltpu.VMEM((qb_rows, LANES), F32),
        ],
    )
    return pl.pallas_call(
        kern,
        out_shape=jax.ShapeDtypeStruct((bsz, seq, d), BF16),
        grid_spec=grid_spec,
        compiler_params=_params(("parallel", "arbitrary")),
        name="dsa_select_attend",
    )(qb_tab, kb_tab, qkv, qkv, qkv, qi, kia, kib, wi)


def _rope_tables(positions, dim, repeats):
    inv = 1.0 / (ROPE_THETA ** (jnp.arange(0, dim, 2, dtype=F32) / dim))
    ang = positions.astype(F32)[..., None] * inv
    cos, sin = jnp.cos(ang), jnp.sin(ang)
    cos_t = jnp.concatenate([cos, cos] * repeats, axis=-1)
    sin_t = jnp.concatenate([-sin, sin] * repeats, axis=-1)
    return cos_t, sin_t


def kernel(x, c, positions, ada_w, ada_b, ln_g, ln_b, s5_in_w, s5_a_re, s5_a_im, s5_log_dt, s5_b_re, s5_b_im, s5_c_re, s5_c_im, s5_d, s5_glu_w, s5_glu_b, dsa_in_w, dsa_out_w, ffn_w_in, ffn_w_out):
    bsz, seq, d = x.shape
    depth = ada_w.shape[0]
    alpha = (2.0 * depth) ** 0.25
    idx_heads = (dsa_in_w.shape[-1] - 3 * d - IDX_DIM) // (IDX_DIM + 1)
    k_top = min(TOPK_MAX, seq // 4)
    seg = min(S5_SEG, seq // SUBLANES)

    cos_h, sin_h = _rope_tables(positions, HEAD_DIM, 1)
    cos_i, sin_i = _rope_tables(positions, IDX_DIM, 2)
    mod = _ada_mod(c, ada_w, ada_b)

    for i in range(depth):
        sh1, sc1, g1, sh2, sc2, g2 = [m[:, None, :] for m in jnp.split(mod[i], 6, axis=-1)]
        j = i // 2
        if i % 2 == 0:
            prep =_s5_discretize(s5_a_re[j], s5_a_im[j], s5_log_dt[j], s5_b_re[j], s5_b_im[j],
                                  s5_c_re[j], s5_c_im[j], seg)
            u = _mod_matmul(x, sc1, sh1, s5_in_w[j].astype(BF16), F32)
            gl = _s5_scan(u, prep, s5_d[j], seg)
            x1 = _matmul_res_ln(gl, s5_glu_w[j].astype(BF16), s5_glu_b[j], x, g1,
                                ln_g[i, 0], ln_b[i, 0], alpha, glu=True)
        else:
            w = dsa_in_w[j]
            n_qi = idx_heads * IDX_DIM
            w_qkv = w[:, :3 * d].astype(BF16)
            pad = jnp.zeros((d, LANES - IDX_DIM - idx_heads), w.dtype)
            w_idx = jnp.concatenate(
                [w[:, 3 * d:3 * d + n_qi], w[:, 3 * d + n_qi + idx_heads:], w[:, 3 * d + n_qi:3 * d + n_qi + idx_heads], pad],
                axis=1).astype(BF16)
            qkv = _qkv_proj(x, sc1, sh1, w_qkv, cos_h, sin_h)
            qi, kia, kib, wi = _idx_proj(x, sc1, sh1, w_idx, cos_i, sin_i, idx_heads)
            att = _dsa_attention(qkv, qi, kia, kib, wi, k_top)
            x1 = _matmul_res_ln(att, dsa_out_w[j].astype(BF16), None, x, g1,
                                ln_g[i, 0], ln_b[i, 0], alpha, glu=False)
        x = _ffn_res_ln(x1, sc2, sh2, g2, ffn_w_in[i].astype(BF16), ffn_w_out[i].astype(BF16),
                        ln_g[i, 1], ln_b[i, 1], alpha)
    return x
```

```python
import functools
import math

import jax
import jax.numpy as jnp
from jax import lax
from jax.experimental import pallas as pl
from jax.experimental.pallas import tpu as pltpu

F32 = jnp.float32
BF16 = jnp.bfloat16
I32 = jnp.int32

S5_GROUP = 16
S5_STATE = 64
HEAD_DIM = 128
IDX_DIM = 64
TOPK_MAX = 256
ROPE_THETA = 10000.0
LN_EPS = 1e-5

LANES = 128
SUBLANES = 8
MXU_WIDTH = 256
VMEM_LIMIT_BYTES = 56 * 1024 * 1024
RESIDENT_WEIGHT_BYTES = 16 * 1024 * 1024

S5_PACK_GROUPS = 16
S5_SEG = 64

INT_MIN = -(2 ** 31)
INT_MAX = 2 ** 31 - 1
SEARCH_ROWS = 128
WORD_BITS = 32

MASK_VALUE = -2e30
M_INIT = -1e30


def _pick(n, cands):
    for c in cands:
        if n % c == 0:
            return c
    return n


def _row_halves(rows):
    half = rows // 2
    return (slice(0, half), slice(half, rows))


def _params(sem):
    return pltpu.CompilerParams(dimension_semantics=sem, vmem_limit_bytes=VMEM_LIMIT_BYTES)


def _ada_kernel(c_ref, w_ref, b_ref, o_ref):
    ca = jax.nn.silu(c_ref[...]).astype(BF16)
    o_ref[0] = jnp.dot(ca, w_ref[0].astype(BF16), preferred_element_type=F32) + b_ref[0]


def _ada_mod(c, ada_w, ada_b):
    bsz, d = c.shape
    depth, _, n = ada_w.shape
    rows = SUBLANES * ((bsz + SUBLANES - 1) // SUBLANES)
    cp = jnp.zeros((rows, d), F32).at[:bsz].set(c)
    tn = _pick(n, (1024, 512, 256, 128))
    out = pl.pallas_call(
        _ada_kernel,
        out_shape=jax.ShapeDtypeStruct((depth, rows, n), F32),
        grid=(depth, n // tn),
        in_specs=[
            pl.BlockSpec((rows, d), lambda l, j: (0, 0)),
            pl.BlockSpec((1, d, tn), lambda l, j: (l, 0, j)),
            pl.BlockSpec((1, 1, tn), lambda l, j: (l, 0, j)),
        ],
        out_specs=pl.BlockSpec((1, rows, tn), lambda l, j: (l, 0, j)),
        compiler_params=_params(("arbitrary", "arbitrary")),
        name="ada_mod",
    )(cp, ada_w, ada_b.reshape(depth, 1, n))
    return out[:, :bsz]


def _modulate(x_ref, sc_ref, sh_ref):
    return (x_ref[0] * (1.0 + sc_ref[0]) + sh_ref[0]).astype(BF16)


def _modmm_kernel(x_ref, sc_ref, sh_ref, w_ref, o_ref, h_scr):
    @pl.when(pl.program_id(2) == 0)
    def _():
        h_scr[...] = _modulate(x_ref, sc_ref, sh_ref)

    for rs in _row_halves(h_scr.shape[0]):
        o_ref[0, rs, :] = jnp.dot(h_scr[rs, :], w_ref[...], preferred_element_type=F32).astype(o_ref.dtype)


def _mod_matmul(x, sc, sh, w, out_dtype):
    bsz, seq, d = x.shape
    n = w.shape[1]
    tm = _pick(seq, (512, 256, 128))
    tn = n if 2 * d * n * w.dtype.itemsize <= RESIDENT_WEIGHT_BYTES else _pick(n, (1024, 512, 256, 128))
    return pl.pallas_call(
        _modmm_kernel,
        out_shape=jax.ShapeDtypeStruct((bsz, seq, n), out_dtype),
        grid=(bsz, seq // tm, n // tn),
        in_specs=[
            pl.BlockSpec((1, tm, d), lambda b, i, j: (b, i, 0)),
            pl.BlockSpec((1, 1, d), lambda b, i, j: (b, 0, 0)),
            pl.BlockSpec((1, 1, d), lambda b, i, j: (b, 0, 0)),
            pl.BlockSpec((d, tn), lambda b, i, j: (0, j)),
        ],
        out_specs=pl.BlockSpec((1, tm, tn), lambda b, i, j: (b, i, j)),
        scratch_shapes=[pltpu.VMEM((tm, d), BF16)],
        compiler_params=_params(("parallel", "parallel", "arbitrary")),
        name="mod_matmul",
    )(x, sc, sh, w)


def _s5_kernel(*refs, seg, width, lane_tiles):
    u_refs = refs[:lane_tiles]
    (bb_ref, cb_ref, lre_ref, lim_ref, ltre_ref, ltim_ref, d_ref, o_ref,
     xs_scr, carry_scr, cin_scr, up_scr) = refs[lane_tiles:]
    ns = xs_scr.shape[1] // 2

    @pl.when(pl.program_id(2) == 0)
    def _():
        carry_scr[...] = jnp.zeros_like(carry_scr)

    for t, u_ref in enumerate(u_refs):
        for i in range(seg):
            up_scr[t, i * SUBLANES:(i + 1) * SUBLANES, :] = u_ref[0, pl.ds(i, SUBLANES, stride=seg), :]
    u = jnp.concatenate([up_scr[t] for t in range(len(u_refs))], axis=1)
    halves = _row_halves(u.shape[0])
    for rs in halves:
        xs_scr[rs, :] = jnp.dot(u[rs].astype(BF16), bb_ref[0], preferred_element_type=F32)

    for part in range(ns // width):
        cr = slice(part * width, (part + 1) * width)
        ci = slice(ns + part * width, ns + (part + 1) * width)
        lr = jnp.broadcast_to(lre_ref[0, :, cr], (SUBLANES, width))
        li = jnp.broadcast_to(lim_ref[0, :, cr], (SUBLANES, width))

        def local_step(i, st, cr=cr, ci=ci, lr=lr, li=li):
            sr, si = st
            r0 = pl.multiple_of(i * SUBLANES, SUBLANES)
            nr = lr * sr - li * si + xs_scr[pl.ds(r0, SUBLANES), cr]
            ni = lr * si + li * sr + xs_scr[pl.ds(r0, SUBLANES), ci]
            xs_scr[pl.ds(r0, SUBLANES), cr] = nr
            xs_scr[pl.ds(r0, SUBLANES), ci] = ni
            return nr, ni

        zero = jnp.zeros((SUBLANES, width), F32)
        er, ei = lax.fori_loop(0, seg, local_step, (zero, zero), unroll=2)

        ltr = ltre_ref[0, :, cr]
        lti = ltim_ref[0, :, cr]
        c_r = carry_scr[:, cr]
        c_i = carry_scr[:, ci]
        for s in range(SUBLANES):
            cin_scr[s:s + 1, cr] = c_r
            cin_scr[s:s + 1, ci] = c_i
            e_r = er[s:s + 1, :]
            e_i = ei[s:s + 1, :]
            c_r, c_i = ltr * c_r - lti * c_i + e_r, ltr * c_i + lti * c_r + e_i
        carry_scr[:, cr] = c_r
        carry_scr[:, ci] = c_i

        def carry_step(i, st, cr=cr, ci=ci, lr=lr, li=li):
            pr, pi_ = st
            r0 = pl.multiple_of(i * SUBLANES, SUBLANES)
            nr = lr * pr - li * pi_
            ni = lr * pi_ + li * pr
            xs_scr[pl.ds(r0, SUBLANES), cr] += nr
            xs_scr[pl.ds(r0, SUBLANES), ci] += ni
            return nr, ni

        lax.fori_loop(0, seg, carry_step, (cin_scr[:, cr], cin_scr[:, ci]), unroll=2)

    for rs in halves:
        y = jnp.dot(xs_scr[rs, :].astype(BF16), cb_ref[0], preferred_element_type=F32)
        g = jax.nn.gelu(y + d_ref[0] * u[rs])
        for t in range(len(u_refs)):
            up_scr[t, rs, :] = g[:, t * LANES:(t + 1) * LANES]
    for t in range(len(u_refs)):
        for s in range(SUBLANES):
            o_ref[0, s * seg:(s + 1) * seg, t * LANES:(t + 1) * LANES] = (
                up_scr[t, pl.ds(s, seg, stride=SUBLANES), :].astype(o_ref.dtype))


def _cmul(ar, ai, br, bi):
    return ar * br - ai * bi, ar * bi + ai * br


def _s5_discretize(a_re, a_im, log_dt, b_re, b_im, c_re, c_im, seg):
    g, n = a_re.shape
    p = b_re.shape[-1]
    pg = S5_PACK_GROUPS
    packs = g // pg
    a_re, a_im = a_re.astype(F32), a_im.astype(F32)
    dt = jnp.exp(log_dt.astype(F32))[:, None]
    mag = jnp.exp(a_re * dt)
    lb_re, lb_im = mag * jnp.cos(a_im * dt), mag * jnp.sin(a_im * dt)
    den = a_re * a_re + a_im * a_im
    nr, ni = lb_re - 1.0, lb_im
    f_re = (nr * a_re + ni * a_im) / den
    f_im = (ni * a_re - nr * a_im) / den
    bb_re, bb_im = _cmul(f_re[..., None], f_im[..., None], b_re.astype(F32), b_im.astype(F32))
    lt_re, lt_im = lb_re, lb_im
    for _ in range(int(math.log2(seg))):
        lt_re, lt_im = _cmul(lt_re, lt_im, lt_re, lt_im)
    eye = jnp.eye(pg, dtype=F32)

    def blk_diag(m):
        r, c = m.shape[1:]
        blocks = m.reshape(packs, pg, r, 1, c) * eye[None, :, None, :, None]
        return blocks.reshape(packs, pg * r, pg * c)

    def in_blk(m):
        return blk_diag(jnp.swapaxes(m, 1, 2))

    def out_blk(m):
        return blk_diag(jnp.swapaxes(m, 1, 2))

    b_blk = jnp.concatenate([in_blk(bb_re), in_blk(bb_im)], axis=-1).astype(BF16)
    c_blk = jnp.concatenate([out_blk(c_re.astype(F32)), out_blk(-c_im.astype(F32))], axis=1).astype(BF16)

    def vec(m):
        return m.reshape(packs, 1, pg * n)

    return b_blk, c_blk, vec(lb_re), vec(lb_im), vec(lt_re), vec(lt_im)


def _s5_scan(u, prep, d_skip, seg):
    bsz, seq, d = u.shape
    b_blk, c_blk, lre, lim, ltre, ltim = prep
    packs, pw, ns2 = b_blk.shape
    ns = ns2 // 2
    rows = SUBLANES * seg
    width = _pick(ns, (1024, 512, 256, 128))
    lane_tiles = pw // LANES
    kern = functools.partial(_s5_kernel, seg=seg, width=width, lane_tiles=lane_tiles)
    vspec = pl.BlockSpec((1, 1, ns), lambda b, k, m: (k, 0, 0))
    u_specs = [pl.BlockSpec((1, rows, LANES), lambda b, k, m, t=t: (b, m, k * lane_tiles + t))
               for t in range(lane_tiles)]
    return pl.pallas_call(
        kern,
        out_shape=jax.ShapeDtypeStruct((bsz, seq, d), BF16),
        grid=(bsz, packs, seq // rows),
        in_specs=u_specs + [
            pl.BlockSpec((1, pw, ns2), lambda b, k, m: (k, 0, 0)),
            pl.BlockSpec((1, ns2, pw), lambda b, k, m: (k, 0, 0)),
            vspec, vspec, vspec, vspec,
            pl.BlockSpec((1, 1, pw), lambda b, k, m: (k, 0, 0)),
        ],
        out_specs=pl.BlockSpec((1, rows, pw), lambda b, k, m: (b, m, k)),
        scratch_shapes=[
            pltpu.VMEM((rows, ns2), F32),
            pltpu.VMEM((1, ns2), F32),
            pltpu.VMEM((SUBLANES, ns2), F32),
            pltpu.VMEM((lane_tiles, rows, LANES), F32),
        ],
        compiler_params=_params(("parallel", "parallel", "arbitrary")),
        name="s5_scan",
    )(*([u] * lane_tiles), b_blk, c_blk, lre, lim, ltre, ltim, d_skip.reshape(packs, 1, pw).astype(F32))


def _layer_norm_rows(tiles, lng_ref, lnb_ref, o_ref, d):
    tn = tiles[0].shape[1]
    tot = tiles[0].sum(axis=1, keepdims=True)
    for t in tiles[1:]:
        tot = tot + t.sum(axis=1, keepdims=True)
    mu = tot * (1.0 / d)
    sq = jnp.square(tiles[0] - mu).sum(axis=1, keepdims=True)
    for t in tiles[1:]:
        sq = sq + jnp.square(t - mu).sum(axis=1, keepdims=True)
    inv = lax.rsqrt(sq * (1.0 / d) + LN_EPS)
    for k, t in enumerate(tiles):
        cs = slice(k * tn, (k + 1) * tn)
        o_ref[0, :, cs] = ((t - mu) * inv * lng_ref[:, cs] + lnb_ref[:, cs]).astype(o_ref.dtype)


def _mmln_kernel(*refs, glu, nj, alpha, d):
    if glu:
        a_ref, w1_ref, w2_ref, b1_ref, b2_ref, x_ref, g_ref, lng_ref, lnb_ref, o_ref, r_scr = refs
    else:
        a_ref, w1_ref, x_ref, g_ref, lng_ref, lnb_ref, o_ref, r_scr = refs
    j = pl.program_id(2)
    for rs in _row_halves(a_ref.shape[1]):
        a = a_ref[0, rs, :]
        y = jnp.dot(a, w1_ref[...], preferred_element_type=F32)
        if glu:
            y = y + b1_ref[...]
            gate = jnp.dot(a, w2_ref[...], preferred_element_type=F32) + b2_ref[...]
            y = y * jax.nn.sigmoid(gate)
        r_scr[j, rs, :] = alpha * x_ref[0, rs, :] + (1.0 + g_ref[0]) * y

    @pl.when(j == nj - 1)
    def _():
        _layer_norm_rows([r_scr[t] for t in range(nj)], lng_ref, lnb_ref, o_ref, d)


def _matmul_res_ln(a, w, bias, xres, gate, ln_g, ln_b, alpha, glu):
    bsz, seq, k = a.shape
    d = xres.shape[-1]
    if glu:
        tm = _pick(seq, (1024, 512, 256, 128))
        tn = _pick(d, (512, 256, 128))
    else:
        tm = _pick(seq, (512, 256, 128))
        tn = d if 2 * k * d * w.dtype.itemsize <= RESIDENT_WEIGHT_BYTES else _pick(d, (512, 256, 128))
    nj = d // tn
    a_spec = pl.BlockSpec((1, tm, k), lambda b, i, j: (b, i, 0))
    w1_spec = pl.BlockSpec((k, tn), lambda b, i, j: (0, j))
    tail_specs = [
        pl.BlockSpec((1, tm, tn), lambda b, i, j: (b, i, j)),
        pl.BlockSpec((1, 1, tn), lambda b, i, j: (b, 0, j)),
        pl.BlockSpec((1, d), lambda b, i, j: (0, 0)),
        pl.BlockSpec((1, d), lambda b, i, j: (0, 0)),
    ]
    tail = (xres, gate, ln_g.reshape(1, d).astype(F32), ln_b.reshape(1, d).astype(F32))
    if glu:
        in_specs = [a_spec, w1_spec,
                    pl.BlockSpec((k, tn), lambda b, i, j: (0, j + nj)),
                    pl.BlockSpec((1, tn), lambda b, i, j: (0, j)),
                    pl.BlockSpec((1, tn), lambda b, i, j: (0, j + nj))] + tail_specs
        b2d = bias.reshape(1, 2 * d).astype(F32)
        args = (a, w, w, b2d, b2d) + tail
    else:
        in_specs = [a_spec, w1_spec] + tail_specs
        args = (a, w) + tail
    kern = functools.partial(_mmln_kernel, glu=glu, nj=nj, alpha=alpha, d=d)
    return pl.pallas_call(
        kern,
        out_shape=jax.ShapeDtypeStruct((bsz, seq, d), F32),
        grid=(bsz, seq // tm, nj),
        in_specs=in_specs,
        out_specs=pl.BlockSpec((1, tm, d), lambda b, i, j: (b, i, 0)),
        scratch_shapes=[pltpu.VMEM((nj, tm, tn), F32)],
        compiler_params=_params(("parallel", "parallel", "arbitrary")),
        name="matmul_res_ln",
    )(*args)


def _ffn_kernel(x_ref, sc_ref, sh_ref, g_ref, wg_ref, wu_ref, wo_ref, lng_ref, lnb_ref, o_ref,
                h_scr, acc_scr, *, nf, alpha, d, tn):
    f = pl.program_id(2)

    @pl.when(f == 0)
    def _():
        h_scr[...] = _modulate(x_ref, sc_ref, sh_ref)
        acc_scr[...] = jnp.zeros_like(acc_scr)

    h = h_scr[...]
    a_g = jnp.dot(h, wg_ref[...], preferred_element_type=F32)
    a_u = jnp.dot(h, wu_ref[...], preferred_element_type=F32)
    act = (jax.nn.silu(a_g) * a_u).astype(BF16)
    acc_scr[...] += jnp.dot(act, wo_ref[...], preferred_element_type=F32)

    @pl.when(f == nf - 1)
    def _():
        tiles = []
        for k in range(d // tn):
            cs = slice(k * tn, (k + 1) * tn)
            tiles.append(alpha * x_ref[0, :, cs] + (1.0 + g_ref[0, :, cs]) * acc_scr[:, cs])
        _layer_norm_rows(tiles, lng_ref, lnb_ref, o_ref, d)


def _ffn_res_ln(x, sc, sh, gate, w_in, w_out, ln_g, ln_b, alpha):
    bsz, seq, d = x.shape
    dff = w_out.shape[0]
    tm = _pick(seq, (512, 256, 128))
    tf = _pick(dff, (512, 256, 128))
    nf = dff // tf
    tn = _pick(d, (512, 256, 128))
    vec = pl.BlockSpec((1, 1, d), lambda b, i, f: (b, 0, 0))
    kern = functools.partial(_ffn_kernel, nf=nf, alpha=alpha, d=d, tn=tn)
    return pl.pallas_call(
        kern,
        out_shape=jax.ShapeDtypeStruct((bsz, seq, d), F32),
        grid=(bsz, seq // tm, nf),
        in_specs=[
            pl.BlockSpec((1, tm, d), lambda b, i, f: (b, i, 0)),
            vec, vec, vec,
            pl.BlockSpec((d, tf), lambda b, i, f: (0, f)),
            pl.BlockSpec((d, tf), lambda b, i, f: (0, f + nf)),
            pl.BlockSpec((tf, d), lambda b, i, f: (f, 0)),
            pl.BlockSpec((1, d), lambda b, i, f: (0, 0)),
            pl.BlockSpec((1, d), lambda b, i, f: (0, 0)),
        ],
        out_specs=pl.BlockSpec((1, tm, d), lambda b, i, f: (b, i, 0)),
        scratch_shapes=[pltpu.VMEM((tm, d), BF16), pltpu.VMEM((tm, d), F32)],
        compiler_params=_params(("parallel", "parallel", "arbitrary")),
        name="ffn_res_ln",
    )(x, sc, sh, gate, w_in, w_in, w_out, ln_g.reshape(1, d).astype(F32), ln_b.reshape(1, d).astype(F32))


def _qkv_kernel(x_ref, sc_ref, sh_ref, w_ref, cos_ref, sin_ref, o_ref, h_scr, *, tiles_per_tensor, q_scale):
    j = pl.program_id(2)

    @pl.when(j == 0)
    def _():
        h_scr[...] = _modulate(x_ref, sc_ref, sh_ref)

    tensor = j // tiles_per_tensor
    scale = jnp.where(tensor == 0, q_scale, 1.0)
    is_v = tensor == 2
    chunk = min(MXU_WIDTH, w_ref.shape[1])
    for rs in _row_halves(h_scr.shape[0]):
        a = jnp.where(is_v, 1.0, cos_ref[0, rs, :] * scale)
        b = jnp.where(is_v, 0.0, sin_ref[0, rs, :] * scale)
        for c in range(w_ref.shape[1] // chunk):
            acc = jnp.dot(h_scr[rs, :], w_ref[:, c * chunk:(c + 1) * chunk], preferred_element_type=F32)
            for hh in range(chunk // HEAD_DIM):
                xc = acc[:, hh * HEAD_DIM:(hh + 1) * HEAD_DIM]
                xc = xc * a + pltpu.roll(xc, HEAD_DIM // 2, 1) * b
                o_ref[0, c * (chunk // HEAD_DIM) + hh, rs, :] = xc.astype(o_ref.dtype)


def _qkv_proj(x, sc, sh, w_qkv, cos_t, sin_t):
    bsz, seq, d = x.shape
    tm = _pick(seq, (1024, 512, 256, 128))
    tn = _pick(d, (1024, 512, 256, 128))
    hpt = tn // HEAD_DIM
    kern = functools.partial(_qkv_kernel, tiles_per_tensor=d // tn, q_scale=HEAD_DIM ** -0.5 * math.log2(math.e))
    return pl.pallas_call(
        kern,
        out_shape=jax.ShapeDtypeStruct((bsz, 3 * d // HEAD_DIM, seq, HEAD_DIM), BF16),
        grid=(bsz, seq // tm, 3 * d // tn),
        in_specs=[
            pl.BlockSpec((1, tm, d), lambda b, i, j: (b, i, 0)),
            pl.BlockSpec((1, 1, d), lambda b, i, j: (b, 0, 0)),
            pl.BlockSpec((1, 1, d), lambda b, i, j: (b, 0, 0)),
            pl.BlockSpec((d, tn), lambda b, i, j: (0, j)),
            pl.BlockSpec((1, tm, HEAD_DIM), lambda b, i, j: (b, i, 0)),
            pl.BlockSpec((1, tm, HEAD_DIM), lambda b, i, j: (b, i, 0)),
        ],
        out_specs=pl.BlockSpec((1, hpt, tm, HEAD_DIM), lambda b, i, j: (b, j, i, 0)),
        scratch_shapes=[pltpu.VMEM((tm, d), BF16)],
        compiler_params=_params(("parallel", "parallel", "arbitrary")),
        name="dsa_qkv_proj",
    )(x, sc, sh, w_qkv, cos_t, sin_t)


def _idx_kernel(x_ref, sc_ref, sh_ref, w_ref, cos_ref, sin_ref, qi_ref, kia_ref, kib_ref, wi_ref,
                *, n_pair, idx_heads, w_scale):
    h = _modulate(x_ref, sc_ref, sh_ref)
    halves = _row_halves(h.shape[0])
    accs = [jnp.dot(h[rs], w_ref[...], preferred_element_type=F32) for rs in halves]
    lane = lax.broadcasted_iota(I32, (halves[0].stop, LANES), 1)
    first_half = (lane % IDX_DIM) < IDX_DIM // 2

    for rs, acc in zip(halves, accs):
        cos = cos_ref[0, rs, :]
        sin = sin_ref[0, rs, :]

        def rope(xc, cos=cos, sin=sin):
            partner = jnp.where(first_half, pltpu.roll(xc, LANES - IDX_DIM // 2, 1),
                                pltpu.roll(xc, IDX_DIM // 2, 1))
            return xc * cos + partner * sin

        for p in range(n_pair):
            cs = slice(p * LANES, (p + 1) * LANES)
            qi_ref[0, rs, cs] = rope(acc[:, cs]).astype(qi_ref.dtype)
        last = acc[:, n_pair * LANES:]
        ka = jnp.where(lane < IDX_DIM, rope(last), 0.0)
        kia_ref[0, rs, :] = ka.astype(kia_ref.dtype)
        kib_ref[0, rs, :] = pltpu.roll(ka, IDX_DIM, 1).astype(kib_ref.dtype)
        wi_ref[0, rs, :] = jnp.where(lane < idx_heads, pltpu.roll(last, LANES - IDX_DIM, 1), 0.0) * w_scale


def _idx_proj(x, sc, sh, w_idx, cos_t, sin_t, idx_heads):
    bsz, seq, d = x.shape
    n = w_idx.shape[1]
    n_pair = idx_heads // 2
    tm = _pick(seq, (512, 256, 128))
    kern = functools.partial(_idx_kernel, n_pair=n_pair, idx_heads=idx_heads,
                             w_scale=(idx_heads ** -0.5) * (IDX_DIM ** -0.5))
    row = lambda b, i: (b, i, 0)
    return pl.pallas_call(
        kern,
        out_shape=(
            jax.ShapeDtypeStruct((bsz, seq, n_pair * LANES), BF16),
            jax.ShapeDtypeStruct((bsz, seq, LANES), BF16),
            jax.ShapeDtypeStruct((bsz, seq, LANES), BF16),
            jax.ShapeDtypeStruct((bsz, seq, LANES), F32),
        ),
        grid=(bsz, seq // tm),
        in_specs=[
            pl.BlockSpec((1, tm, d), row),
            pl.BlockSpec((1, 1, d), lambda b, i: (b, 0, 0)),
            pl.BlockSpec((1, 1, d), lambda b, i: (b, 0, 0)),
            pl.BlockSpec((d, n), lambda b, i: (0, 0)),
            pl.BlockSpec((1, tm, LANES), row),
            pl.BlockSpec((1, tm, LANES), row),
        ],
        out_specs=(
            pl.BlockSpec((1, tm, n_pair * LANES), row),
            pl.BlockSpec((1, tm, LANES), row),
            pl.BlockSpec((1, tm, LANES), row),
            pl.BlockSpec((1, tm, LANES), row),
        ),
        compiler_params=_params(("parallel", "parallel")),
        name="dsa_idx_proj",
    )(x, sc, sh, w_idx, cos_t, sin_t)


def _bit_transpose32(words):
    a = list(words)
    j = WORD_BITS // 2
    mask = 0x0000FFFF
    while j:
        k = 0
        while k < WORD_BITS:
            t = (a[k] ^ lax.shift_right_logical(a[k + j], jnp.int32(j))) & jnp.int32(mask)
            a[k] = a[k] ^ t
            a[k + j] = a[k + j] ^ (t << j)
            k = (k + j + 1) & ~j
        j >>= 1
        if j:
            mask = (mask ^ (mask << j)) & 0xFFFFFFFF
    return a


def _dsa_kernel(qb_tab, kb_tab, q_ref, k_ref, v_ref, qi_ref, kia_ref, kib_ref, wi_ref, o_ref,
                keys_scr, planes_scr, thr_scr, tie_scr, ngt_scr, nge_scr,
                m_scr, l_scr, acc_scr, bias_scr, s0_scr, s1_scr, r0_scr, r1_scr,
                *, n_pair, k_top, idx_bits):
    heads, qb_rows, _ = acc_scr.shape
    kb_rows = keys_scr.shape[2]
    qb = qb_tab[pl.program_id(1)]
    kb = kb_tab[pl.program_id(1)]
    q0 = qb * qb_rows
    last_kb = (q0 + qb_rows - 1) // kb_rows
    n_chunk = last_kb + 1
    row = lax.broadcasted_iota(I32, (qb_rows, kb_rows), 0) + q0
    col = lax.broadcasted_iota(I32, (qb_rows, kb_rows), 1)
    nt = (((1,), (1,)), ((), ()))

    @pl.when(kb == 0)
    def _select():
        wi = wi_ref[0]

        def score_chunk(c, carry):
            k0 = pl.multiple_of(c * kb_rows, kb_rows)
            ka = kia_ref[0, pl.ds(k0, kb_rows), :]
            kbm = kib_ref[0, pl.ds(k0, kb_rows), :]
            sc = jnp.zeros((qb_rows, kb_rows), F32)
            for p in range(n_pair):
                qp = qi_ref[0, :, p * LANES:(p + 1) * LANES]
                sa = lax.dot_general(qp, ka, nt, preferred_element_type=F32)
                sb = lax.dot_general(qp, kbm, nt, preferred_element_type=F32)
                sc = sc + wi[:, 2 * p:2 * p + 1] * jnp.maximum(sa, 0.0)
                sc = sc + wi[:, 2 * p + 1:2 * p + 2] * jnp.maximum(sb, 0.0)
            bits = lax.bitcast_convert_type(sc, I32)
            key = bits ^ ((bits >> 31) & INT_MAX)
            keys_scr[c] = jnp.where(col + k0 <= row, key, INT_MIN)
            return carry

        lax.fori_loop(0, n_chunk, score_chunk, 0)

        slices_per_chunk = kb_rows // LANES
        chunks_per_set = WORD_BITS // slices_per_chunk
        n_slices = keys_scr.shape[0] * slices_per_chunk
        n_sets = planes_scr.shape[2] // LANES
        live_sets = (n_chunk + chunks_per_set - 1) // chunks_per_set

        def fill_chunk(c, carry):
            keys_scr[c] = jnp.full((qb_rows, kb_rows), INT_MIN, I32)
            return carry

        lax.fori_loop(n_chunk, jnp.minimum(live_sets * chunks_per_set, keys_scr.shape[0]), fill_chunk, 0)

        def pack_rows(g, carry, st):
            r0 = pl.multiple_of(g * SUBLANES, SUBLANES)
            words = []
            for s in range(WORD_BITS):
                sl = st * WORD_BITS + s
                if sl < n_slices:
                    c, off = divmod(sl, slices_per_chunk)
                    words.append(keys_scr[c, pl.ds(r0, SUBLANES), off * LANES:(off + 1) * LANES])
                else:
                    words.append(jnp.full((SUBLANES, LANES), INT_MIN, I32))
            words = _bit_transpose32(words)
            words[0] = ~words[0]
            for i in range(WORD_BITS):
                planes_scr[i, pl.ds(r0, SUBLANES), st * LANES:(st + 1) * LANES] = words[i]
            return carry

        for st in range(n_sets):
            @pl.when(st < live_sets)
            def _(st=st):
                lax.fori_loop(0, qb_rows // SUBLANES, functools.partial(pack_rows, st=st), 0)

            @pl.when(st >= live_sets)
            def _(st=st):
                planes_scr[:, :, st * LANES:(st + 1) * LANES] = jnp.zeros((WORD_BITS, qb_rows, LANES), I32)

        lane_ones = jnp.ones((LANES, LANES), BF16)

        def row_total(pc):
            tot = pc[:, :LANES]
            for st in range(1, n_sets):
                tot = tot + pc[:, st * LANES:(st + 1) * LANES]
            return jnp.dot(tot.astype(F32).astype(BF16), lane_ones, preferred_element_type=F32)

        def all_sets(mask):
            return jnp.concatenate([mask] * n_sets, axis=1)

        def select_two_bits(i, carry):
            cand, above, prefix = carry
            hi = planes_scr[2 * i]
            lo = planes_scr[2 * i + 1]
            c1 = cand & hi
            c0 = cand ^ c1
            c11 = c1 & lo
            c10 = c1 ^ c11
            c01 = c0 & lo
            c00 = c0 ^ c01
            r11 = above + row_total(lax.population_count(c11))
            r10 = r11 + row_total(lax.population_count(c10))
            r01 = r10 + row_total(lax.population_count(c01))
            t11 = r11 >= k_top
            t10 = r10 >= k_top
            t01 = r01 >= k_top
            cand = jnp.where(all_sets(t11), c11,
                             jnp.where(all_sets(t10), c10, jnp.where(all_sets(t01), c01, c00)))
            above = jnp.where(t11, above, jnp.where(t10, r11, jnp.where(t01, r10, r01)))
            bits = jnp.where(t11, 3, jnp.where(t10, 2, jnp.where(t01, 1, 0)))
            prefix = prefix | (bits << (WORD_BITS - 2 - 2 * i))
            return cand, above, prefix

        cand, above, prefix = lax.fori_loop(
            0, WORD_BITS // 2, select_two_bits,
            (jnp.concatenate([jnp.full((qb_rows, LANES), jnp.where(st < live_sets, -1, 0), I32)
                              for st in range(n_sets)], axis=1),
             jnp.zeros((qb_rows, LANES), F32), jnp.zeros((qb_rows, LANES), I32)))
        equal = row_total(lax.population_count(cand))
        thr_scr[...] = (prefix ^ INT_MIN)[:, :1]
        tie_scr[...] = jnp.full((qb_rows, 1), INT_MAX, I32)
        ngt_scr[...] = above[:, :1]
        nge_scr[...] = (above + equal)[:, :1]

        rg = min(qb_rows, SEARCH_ROWS)
        lane_col = lax.broadcasted_iota(I32, (rg, LANES), 1)

        for r in range(qb_rows // rg):
            rs = slice(r * rg, (r + 1) * rg)

            def count(pred, rs=rs):
                def body(c, acc):
                    for t in range(kb_rows // LANES):
                        kk = keys_scr[c, rs, t * LANES:(t + 1) * LANES]
                        idx = lane_col + (c * kb_rows + t * LANES)
                        acc = acc + jnp.where(pred(kk, idx), 1.0, 0.0)
                    return acc
                acc = lax.fori_loop(0, n_chunk, body, jnp.zeros((rg, LANES), F32))
                return jnp.sum(acc, axis=1, keepdims=True)

            def wide(v):
                return jnp.broadcast_to(v, (rg, LANES))

            @pl.when(jnp.max(nge_scr[rs, :]) > k_top)
            def _ties(count=count, rs=rs):
                thr_w = wide(thr_scr[rs, :])
                need = k_top - ngt_scr[rs, :]

                def tie_bit(i, jt):
                    cand = jt + (jnp.int32(1) << (idx_bits - 1 - i))
                    cand_w = wide(cand)
                    cnt = count(lambda kk, idx: (kk == thr_w) & (idx < cand_w))
                    return jnp.where(cnt < need, cand, jt)

                tie_scr[rs, :] = lax.fori_loop(0, idx_bits, tie_bit, jnp.zeros((rg, 1), I32))

        m_scr[...] = jnp.full(m_scr.shape, M_INIT, F32)
        l_scr[...] = jnp.zeros_like(l_scr)
        acc_scr[...] = jnp.zeros_like(acc_scr)

    def _attend():
        ones = jnp.ones((kb_rows, HEAD_DIM), BF16)

        kk = keys_scr[kb]
        thr = thr_scr[...]
        lim = jnp.minimum(tie_scr[...], row[:, :1])
        tie_bias = jnp.where(col + kb * kb_rows <= lim, 0.0, MASK_VALUE)
        bias_scr[...] = jnp.where(kk > thr, 0.0, jnp.where(kk == thr, tie_bias, MASK_VALUE))

        def logits(h, s_ref, r_ref):
            s = lax.dot_general(q_ref[0, h], k_ref[0, h], nt, preferred_element_type=F32) + bias_scr[...]
            s_ref[...] = s
            r_ref[...] = jnp.broadcast_to(jnp.max(s, axis=1, keepdims=True), r_ref.shape)

        def accumulate(h, s_ref, r_ref):
            m_old = m_scr[h]
            m_new = jnp.maximum(m_old, r_ref[...])
            alpha = jnp.exp2(m_old - m_new)
            p = jnp.concatenate(
                [jnp.exp2(s_ref[:, t * LANES:(t + 1) * LANES] - m_new).astype(BF16)
                 for t in range(kb_rows // LANES)], axis=1)
            v_ext = jnp.concatenate([v_ref[0, h], ones], axis=1)
            pv = jnp.dot(p, v_ext, preferred_element_type=F32)
            acc_scr[h] = alpha * acc_scr[h] + pv[:, :HEAD_DIM]
            l_scr[h] = alpha * l_scr[h] + pv[:, HEAD_DIM:]
            m_scr[h] = m_new

        bufs = ((s0_scr, r0_scr), (s1_scr, r1_scr))
        logits(0, *bufs[0])
        for h in range(heads):
            if h + 1 < heads:
                logits(h + 1, *bufs[(h + 1) % 2])
            accumulate(h, *bufs[h % 2])

    _attend()

    @pl.when(kb == last_kb)
    def _finish():
        for h in range(heads):
            o_ref[0, :, h * HEAD_DIM:(h + 1) * HEAD_DIM] = (acc_scr[h] / l_scr[h]).astype(o_ref.dtype)


def _dsa_attention(qkv, qi, kia, kib, wi, k_top):
    bsz, heads3, seq, _ = qkv.shape
    heads = heads3 // 3
    d = heads * HEAD_DIM
    n_pair = qi.shape[-1] // LANES
    qb_rows = _pick(seq, (256, 128))
    kb_rows = _pick(seq, (512, 256, 128))
    n_kb = seq // kb_rows
    n_sets = -(-seq // (WORD_BITS * LANES))

    pairs = [(i, j) for i in range(seq // qb_rows) for j in range((i * qb_rows + qb_rows - 1) // kb_rows + 1)]
    qb_tab = jnp.asarray([p[0] for p in pairs], I32)
    kb_tab = jnp.asarray([p[1] for p in pairs], I32)

    kern = functools.partial(_dsa_kernel, n_pair=n_pair, k_top=k_top, idx_bits=seq.bit_length())
    grid_spec = pltpu.PrefetchScalarGridSpec(
        num_scalar_prefetch=2,
        grid=(bsz, len(pairs)),
        in_specs=[
            pl.BlockSpec((1, heads, qb_rows, HEAD_DIM), lambda b, s, qt, kt: (b, 0, qt[s], 0)),
            pl.BlockSpec((1, heads, kb_rows, HEAD_DIM), lambda b, s, qt, kt: (b, 1, kt[s], 0)),
            pl.BlockSpec((1, heads, kb_rows, HEAD_DIM), lambda b, s, qt, kt: (b, 2, kt[s], 0)),
            pl.BlockSpec((1, qb_rows, n_pair * LANES), lambda b, s, qt, kt: (b, qt[s], 0)),
            pl.BlockSpec((1, seq, LANES), lambda b, s, qt, kt: (b, 0, 0)),
            pl.BlockSpec((1, seq, LANES), lambda b, s, qt, kt: (b, 0, 0)),
            pl.BlockSpec((1, qb_rows, LANES), lambda b, s, qt, kt: (b, qt[s], 0)),
        ],
        out_specs=pl.BlockSpec((1, qb_rows, d), lambda b, s, qt, kt: (b, qt[s], 0)),
        scratch_shapes=[
            pltpu.VMEM((n_kb, qb_rows, kb_rows), I32),
            pltpu.VMEM((WORD_BITS, qb_rows, n_sets * LANES), I32),
            pltpu.VMEM((qb_rows, 1), I32),
            pltpu.VMEM((qb_rows, 1), I32),
            pltpu.VMEM((qb_rows, 1), F32),
            pltpu.VMEM((qb_rows, 1), F32),
            pltpu.VMEM((heads, qb_rows, LANES), F32),
            pltpu.VMEM((heads, qb_rows, LANES), F32),
            pltpu.VMEM((heads, qb_rows, HEAD_DIM), F32),
            pltpu.VMEM((qb_rows, kb_rows), F32),
            pltpu.VMEM((qb_rows, kb_rows), F32),
            pltpu.VMEM((qb_rows, kb_rows), F32),
            pltpu.VMEM((qb_rows, LANES), F32),
            pltpu.VMEM((qb_rows, LANES), F32),
        ],
    )
    return pl.pallas_call(
        kern,
        out_shape=jax.ShapeDtypeStruct((bsz, seq, d), BF16),
        grid_spec=grid_spec,
        compiler_params=_params(("parallel", "arbitrary")),
        name="dsa_select_attend",
    )(qb_tab, kb_tab, qkv, qkv, qkv, qi, kia, kib, wi)


def _rope_tables(positions, dim, repeats):
    inv = 1.0 / (ROPE_THETA ** (jnp.arange(0, dim, 2, dtype=F32) / dim))
    ang = positions.astype(F32)[..., None] * inv
    cos, sin = jnp.cos(ang), jnp.sin(ang)
    cos_t = jnp.concatenate([cos, cos] * repeats, axis=-1)
    sin_t = jnp.concatenate([-sin, sin] * repeats, axis=-1)
    return cos_t, sin_t


def kernel(x, c, positions, ada_w, ada_b, ln_g, ln_b, s5_in_w, s5_a_re, s5_a_im, s5_log_dt, s5_b_re, s5_b_im, s5_c_re, s5_c_im, s5_d, s5_glu_w, s5_glu_b, dsa_in_w, dsa_out_w, ffn_w_in, ffn_w_out):
    bsz, seq, d = x.shape
    depth = ada_w.shape[0]
    alpha = (2.0 * depth) ** 0.25
    idx_heads = (dsa_in_w.shape[-1] - 3 * d - IDX_DIM) // (IDX_DIM + 1)
    k_top = min(TOPK_MAX, seq // 4)
    seg = min(S5_SEG, seq // SUBLANES)

    cos_h, sin_h = _rope_tables(positions, HEAD_DIM, 1)
    cos_i, sin_i = _rope_tables(positions, IDX_DIM, 2)
    mod = _ada_mod(c, ada_w, ada_b)

    for i in range(depth):
        sh1, sc1, g1, sh2, sc2, g2 = [m[:, None, :] for m in jnp.split(mod[i], 6, axis=-1)]
        j = i // 2
        if i % 2 == 0:
            prep =_s5_discretize(s5_a_re[j], s5_a_im[j], s5_log_dt[j], s5_b_re[j], s5_b_im[j],
                                  s5_c_re[j], s5_c_im[j], seg)
            u = _mod_matmul(x, sc1, sh1, s5_in_w[j].astype(BF16), F32)
            gl = _s5_scan(u, prep, s5_d[j], seg)
            x1 = _matmul_res_ln(gl, s5_glu_w[j].astype(BF16), s5_glu_b[j], x, g1,
                                ln_g[i, 0], ln_b[i, 0], alpha, glu=True)
        else:
            w = dsa_in_w[j]
            n_qi = idx_heads * IDX_DIM
            w_qkv = w[:, :3 * d].astype(BF16)
            pad = jnp.zeros((d, LANES - IDX_DIM - idx_heads), w.dtype)
            w_idx = jnp.concatenate(
                [w[:, 3 * d:3 * d + n_qi], w[:, 3 * d + n_qi + idx_heads:], w[:, 3 * d + n_qi:3 * d + n_qi + idx_heads], pad],
                axis=1).astype(BF16)
            qkv = _qkv_proj(x, sc1, sh1, w_qkv, cos_h, sin_h)
            qi, kia, kib, wi = _idx_proj(x, sc1, sh1, w_idx, cos_i, sin_i, idx_heads)
            att = _dsa_attention(qkv, qi, kia, kib, wi, k_top)
            x1 = _matmul_res_ln(att, dsa_out_w[j].astype(BF16), None, x, g1,
                                ln_g[i, 0], ln_b[i, 0], alpha, glu=False)
        x = _ffn_res_ln(x1, sc2, sh2, g2, ffn_w_in[i].astype(BF16), ffn_w_out[i].astype(BF16),
                        ln_g[i, 1], ln_b[i, 1], alpha)
    return x
```

```python
import functools
import math

import jax
import jax.numpy as jnp
from jax import lax
from jax.experimental import pallas as pl
from jax.experimental.pallas import tpu as pltpu

F32 = jnp.float32
BF16 = jnp.bfloat16
I32 = jnp.int32

S5_GROUP = 16
S5_STATE = 64
HEAD_DIM = 128
IDX_DIM = 64
TOPK_MAX = 256
ROPE_THETA = 10000.0
LN_EPS = 1e-5

LANES = 128
SUBLANES = 8
MXU_WIDTH = 256
VMEM_LIMIT_BYTES = 56 * 1024 * 1024
RESIDENT_WEIGHT_BYTES = 16 * 1024 * 1024

S5_PACK_GROUPS = 16
S5_SEG = 64

INT_MIN = -(2 ** 31)
INT_MAX = 2 ** 31 - 1
SEARCH_ROWS = 128
WORD_BITS = 32

MASK_VALUE = -2e30
M_INIT = -1e30


def _pick(n, cands):
    for c in cands:
        if n % c == 0:
            return c
    return n


def _row_halves(rows):
    half = rows // 2
    return (slice(0, half), slice(half, rows))


def _params(sem):
    return pltpu.CompilerParams(dimension_semantics=sem, vmem_limit_bytes=VMEM_LIMIT_BYTES)


def _ada_kernel(c_ref, w_ref, b_ref, o_ref):
    ca = jax.nn.silu(c_ref[...]).astype(BF16)
    o_ref[0] = jnp.dot(ca, w_ref[0].astype(BF16), preferred_element_type=F32) + b_ref[0]


def _ada_mod(c, ada_w, ada_b):
    bsz, d = c.shape
    depth, _, n = ada_w.shape
    rows = SUBLANES * ((bsz + SUBLANES - 1) // SUBLANES)
    cp = jnp.zeros((rows, d), F32).at[:bsz].set(c)
    tn = _pick(n, (1024, 512, 256, 128))
    out = pl.pallas_call(
        _ada_kernel,
        out_shape=jax.ShapeDtypeStruct((depth, rows, n), F32),
        grid=(depth, n // tn),
        in_specs=[
            pl.BlockSpec((rows, d), lambda l, j: (0, 0)),
            pl.BlockSpec((1, d, tn), lambda l, j: (l, 0, j)),
            pl.BlockSpec((1, 1, tn), lambda l, j: (l, 0, j)),
        ],
        out_specs=pl.BlockSpec((1, rows, tn), lambda l, j: (l, 0, j)),
        compiler_params=_params(("arbitrary", "arbitrary")),
        name="ada_mod",
    )(cp, ada_w, ada_b.reshape(depth, 1, n))
    return out[:, :bsz]


def _modulate(x_ref, sc_ref, sh_ref):
    return (x_ref[0] * (1.0 + sc_ref[0]) + sh_ref[0]).astype(BF16)


def _modmm_kernel(x_ref, sc_ref, sh_ref, w_ref, o_ref, h_scr):
    @pl.when(pl.program_id(2) == 0)
    def _():
        h_scr[...] = _modulate(x_ref, sc_ref, sh_ref)

    for rs in _row_halves(h_scr.shape[0]):
        o_ref[0, rs, :] = jnp.dot(h_scr[rs, :], w_ref[...], preferred_element_type=F32).astype(o_ref.dtype)


def _mod_matmul(x, sc, sh, w, out_dtype):
    bsz, seq, d = x.shape
    n = w.shape[1]
    tm = _pick(seq, (512, 256, 128))
    tn = n if 2 * d * n * w.dtype.itemsize <= RESIDENT_WEIGHT_BYTES else _pick(n, (1024, 512, 256, 128))
    return pl.pallas_call(
        _modmm_kernel,
        out_shape=jax.ShapeDtypeStruct((bsz, seq, n), out_dtype),
        grid=(bsz, seq // tm, n // tn),
        in_specs=[
            pl.BlockSpec((1, tm, d), lambda b, i, j: (b, i, 0)),
            pl.BlockSpec((1, 1, d), lambda b, i, j: (b, 0, 0)),
            pl.BlockSpec((1, 1, d), lambda b, i, j: (b, 0, 0)),
            pl.BlockSpec((d, tn), lambda b, i, j: (0, j)),
        ],
        out_specs=pl.BlockSpec((1, tm, tn), lambda b, i, j: (b, i, j)),
        scratch_shapes=[pltpu.VMEM((tm, d), BF16)],
        compiler_params=_params(("parallel", "parallel", "arbitrary")),
        name="mod_matmul",
    )(x, sc, sh, w)


def _s5_kernel(*refs, seg, width, lane_tiles):
    u_refs = refs[:lane_tiles]
    (bb_ref, cb_ref, lre_ref, lim_ref, ltre_ref, ltim_ref, d_ref, o_ref,
     xs_scr, carry_scr, cin_scr, up_scr) = refs[lane_tiles:]
    ns = xs_scr.shape[1] // 2

    @pl.when(pl.program_id(2) == 0)
    def _():
        carry_scr[...] = jnp.zeros_like(carry_scr)

    for t, u_ref in enumerate(u_refs):
        for i in range(seg):
            up_scr[t, i * SUBLANES:(i + 1) * SUBLANES, :] = u_ref[0, pl.ds(i, SUBLANES, stride=seg), :]
    u = jnp.concatenate([up_scr[t] for t in range(len(u_refs))], axis=1)
    halves = _row_halves(u.shape[0])
    for rs in halves:
        xs_scr[rs, :] = jnp.dot(u[rs].astype(BF16), bb_ref[0], preferred_element_type=F32)

    for part in range(ns // width):
        cr = slice(part * width, (part + 1) * width)
        ci = slice(ns + part * width, ns + (part + 1) * width)
        lr = jnp.broadcast_to(lre_ref[0, :, cr], (SUBLANES, width))
        li = jnp.broadcast_to(lim_ref[0, :, cr], (SUBLANES, width))

        def local_step(i, st, cr=cr, ci=ci, lr=lr, li=li):
            sr, si = st
            r0 = pl.multiple_of(i * SUBLANES, SUBLANES)
            nr = lr * sr - li * si + xs_scr[pl.ds(r0, SUBLANES), cr]
            ni = lr * si + li * sr + xs_scr[pl.ds(r0, SUBLANES), ci]
            xs_scr[pl.ds(r0, SUBLANES), cr] = nr
            xs_scr[pl.ds(r0, SUBLANES), ci] = ni
            return nr, ni

        zero = jnp.zeros((SUBLANES, width), F32)
        er, ei = lax.fori_loop(0, seg, local_step, (zero, zero), unroll=2)

        ltr = ltre_ref[0, :, cr]
        lti = ltim_ref[0, :, cr]
        c_r = carry_scr[:, cr]
        c_i = carry_scr[:, ci]
        for s in range(SUBLANES):
            cin_scr[s:s + 1, cr] = c_r
            cin_scr[s:s + 1, ci] = c_i
            e_r = er[s:s + 1, :]
            e_i = ei[s:s + 1, :]
            c_r, c_i = ltr * c_r - lti * c_i + e_r, ltr * c_i + lti * c_r + e_i
        carry_scr[:, cr] = c_r
        carry_scr[:, ci] = c_i

        def carry_step(i, st, cr=cr, ci=ci, lr=lr, li=li):
            pr, pi_ = st
            r0 = pl.multiple_of(i * SUBLANES, SUBLANES)
            nr = lr * pr - li * pi_
            ni = lr * pi_ + li * pr
            xs_scr[pl.ds(r0, SUBLANES), cr] += nr
            xs_scr[pl.ds(r0, SUBLANES), ci] += ni
            return nr, ni

        lax.fori_loop(0, seg, carry_step, (cin_scr[:, cr], cin_scr[:, ci]), unroll=2)

    for rs in halves:
        y = jnp.dot(xs_scr[rs, :].astype(BF16), cb_ref[0], preferred_element_type=F32)
        g = jax.nn.gelu(y + d_ref[0] * u[rs])
        for t in range(len(u_refs)):
            up_scr[t, rs, :] = g[:, t * LANES:(t + 1) * LANES]
    for t in range(len(u_refs)):
        for s in range(SUBLANES):
            o_ref[0, s * seg:(s + 1) * seg, t * LANES:(t + 1) * LANES] = (
                up_scr[t, pl.ds(s, seg, stride=SUBLANES), :].astype(o_ref.dtype))


def _cmul(ar, ai, br, bi):
    return ar * br - ai * bi, ar * bi + ai * br


def _s5_discretize(a_re, a_im, log_dt, b_re, b_im, c_re, c_im, seg):
    g, n = a_re.shape
    p = b_re.shape[-1]
    pg = S5_PACK_GROUPS
    packs = g // pg
    a_re, a_im = a_re.astype(F32), a_im.astype(F32)
    dt = jnp.exp(log_dt.astype(F32))[:, None]
    mag = jnp.exp(a_re * dt)
    lb_re, lb_im = mag * jnp.cos(a_im * dt), mag * jnp.sin(a_im * dt)
    den = a_re * a_re + a_im * a_im
    nr, ni = lb_re - 1.0, lb_im
    f_re = (nr * a_re + ni * a_im) / den
    f_im = (ni * a_re - nr * a_im) / den
    bb_re, bb_im = _cmul(f_re[..., None], f_im[..., None], b_re.astype(F32), b_im.astype(F32))
    lt_re, lt_im = lb_re, lb_im
    for _ in range(int(math.log2(seg))):
        lt_re, lt_im = _cmul(lt_re, lt_im, lt_re, lt_im)
    on_diag = (jnp.arange(pg * p)[:, None] // p) == (jnp.arange(pg * n)[None, :] // n)

    def blk_diag(m):
        tiled = jnp.tile(m.astype(BF16).reshape(packs, pg * p, n), (1, 1, pg))
        return jnp.where(on_diag[None], tiled, 0)

    b_blk = jnp.concatenate([blk_diag(jnp.swapaxes(bb_re, 1, 2)), blk_diag(jnp.swapaxes(bb_im, 1, 2))], axis=-1)
    c_blk = jnp.swapaxes(jnp.concatenate([blk_diag(c_re), blk_diag(-c_im)], axis=-1), 1, 2)

    def vec(m):
        return m.reshape(packs, 1, pg * n)

    return b_blk, c_blk, vec(lb_re), vec(lb_im), vec(lt_re), vec(lt_im)


def _s5_scan(u, prep, d_skip, seg):
    bsz, seq, d = u.shape
    b_blk, c_blk, lre, lim, ltre, ltim = prep
    packs, pw, ns2 = b_blk.shape
    ns = ns2 // 2
    rows = SUBLANES * seg
    width = _pick(ns, (1024, 512, 256, 128))
    lane_tiles = pw // LANES
    kern = functools.partial(_s5_kernel, seg=seg, width=width, lane_tiles=lane_tiles)
    vspec = pl.BlockSpec((1, 1, ns), lambda b, k, m: (k, 0, 0))
    u_specs = [pl.BlockSpec((1, rows, LANES), lambda b, k, m, t=t: (b, m, k * lane_tiles + t))
               for t in range(lane_tiles)]
    return pl.pallas_call(
        kern,
        out_shape=jax.ShapeDtypeStruct((bsz, seq, d), BF16),
        grid=(bsz, packs, seq // rows),
        in_specs=u_specs + [
            pl.BlockSpec((1, pw, ns2), lambda b, k, m: (k, 0, 0)),
            pl.BlockSpec((1, ns2, pw), lambda b, k, m: (k, 0, 0)),
            vspec, vspec, vspec, vspec,
            pl.BlockSpec((1, 1, pw), lambda b, k, m: (k, 0, 0)),
        ],
        out_specs=pl.BlockSpec((1, rows, pw), lambda b, k, m: (b, m, k)),
        scratch_shapes=[
            pltpu.VMEM((rows, ns2), F32),
            pltpu.VMEM((1, ns2), F32),
            pltpu.VMEM((SUBLANES, ns2), F32),
            pltpu.VMEM((lane_tiles, rows, LANES), F32),
        ],
        compiler_params=_params(("parallel", "parallel", "arbitrary")),
        name="s5_scan",
    )(*([u] * lane_tiles), b_blk, c_blk, lre, lim, ltre, ltim, d_skip.reshape(packs, 1, pw).astype(F32))


def _layer_norm_rows(tiles, lng_ref, lnb_ref, o_ref, d):
    tn = tiles[0].shape[1]
    tot = tiles[0].sum(axis=1, keepdims=True)
    for t in tiles[1:]:
        tot = tot + t.sum(axis=1, keepdims=True)
    mu = tot * (1.0 / d)
    sq = jnp.square(tiles[0] - mu).sum(axis=1, keepdims=True)
    for t in tiles[1:]:
        sq = sq + jnp.square(t - mu).sum(axis=1, keepdims=True)
    inv = lax.rsqrt(sq * (1.0 / d) + LN_EPS)
    for k, t in enumerate(tiles):
        cs = slice(k * tn, (k + 1) * tn)
        o_ref[0, :, cs] = ((t - mu) * inv * lng_ref[:, cs] + lnb_ref[:, cs]).astype(o_ref.dtype)


def _mmln_kernel(*refs, glu, nj, alpha, d):
    if glu:
        a_ref, w1_ref, w2_ref, b1_ref, b2_ref, x_ref, g_ref, lng_ref, lnb_ref, o_ref, r_scr = refs
    else:
        a_ref, w1_ref, x_ref, g_ref, lng_ref, lnb_ref, o_ref, r_scr = refs
    j = pl.program_id(2)
    for rs in _row_halves(a_ref.shape[1]):
        a = a_ref[0, rs, :]
        y = jnp.dot(a, w1_ref[...], preferred_element_type=F32)
        if glu:
            y = y + b1_ref[...]
            gate = jnp.dot(a, w2_ref[...], preferred_element_type=F32) + b2_ref[...]
            y = y * jax.nn.sigmoid(gate)
        r_scr[j, rs, :] = alpha * x_ref[0, rs, :] + (1.0 + g_ref[0]) * y

    @pl.when(j == nj - 1)
    def _():
        _layer_norm_rows([r_scr[t] for t in range(nj)], lng_ref, lnb_ref, o_ref, d)


def _matmul_res_ln(a, w, bias, xres, gate, ln_g, ln_b, alpha, glu):
    bsz, seq, k = a.shape
    d = xres.shape[-1]
    if glu:
        tm = _pick(seq, (1024, 512, 256, 128))
        tn = _pick(d, (512, 256, 128))
    else:
        tm = _pick(seq, (512, 256, 128))
        tn = d if 2 * k * d * w.dtype.itemsize <= RESIDENT_WEIGHT_BYTES else _pick(d, (512, 256, 128))
    nj = d // tn
    a_spec = pl.BlockSpec((1, tm, k), lambda b, i, j: (b, i, 0))
    w1_spec = pl.BlockSpec((k, tn), lambda b, i, j: (0, j))
    tail_specs = [
        pl.BlockSpec((1, tm, tn), lambda b, i, j: (b, i, j)),
        pl.BlockSpec((1, 1, tn), lambda b, i, j: (b, 0, j)),
        pl.BlockSpec((1, d), lambda b, i, j: (0, 0)),
        pl.BlockSpec((1, d), lambda b, i, j: (0, 0)),
    ]
    tail = (xres, gate, ln_g.reshape(1, d).astype(F32), ln_b.reshape(1, d).astype(F32))
    if glu:
        in_specs = [a_spec, w1_spec,
                    pl.BlockSpec((k, tn), lambda b, i, j: (0, j + nj)),
                    pl.BlockSpec((1, tn), lambda b, i, j: (0, j)),
                    pl.BlockSpec((1, tn), lambda b, i, j: (0, j + nj))] + tail_specs
        b2d = bias.reshape(1, 2 * d).astype(F32)
        args = (a, w, w, b2d, b2d) + tail
    else:
        in_specs = [a_spec, w1_spec] + tail_specs
        args = (a, w) + tail
    kern = functools.partial(_mmln_kernel, glu=glu, nj=nj, alpha=alpha, d=d)
    return pl.pallas_call(
        kern,
        out_shape=jax.ShapeDtypeStruct((bsz, seq, d), F32),
        grid=(bsz, seq // tm, nj),
        in_specs=in_specs,
        out_specs=pl.BlockSpec((1, tm, d), lambda b, i, j: (b, i, 0)),
        scratch_shapes=[pltpu.VMEM((nj, tm, tn), F32)],
        compiler_params=_params(("parallel", "parallel", "arbitrary")),
        name="matmul_res_ln",
    )(*args)


def _ffn_kernel(x_ref, sc_ref, sh_ref, g_ref, wg_ref, wu_ref, wo_ref, lng_ref, lnb_ref, o_ref,
                h_scr, acc_scr, *, nf, alpha, d, tn):
    f = pl.program_id(2)

    @pl.when(f == 0)
    def _():
        h_scr[...] = _modulate(x_ref, sc_ref, sh_ref)
        acc_scr[...] = jnp.zeros_like(acc_scr)

    h = h_scr[...]
    a_g = jnp.dot(h, wg_ref[...], preferred_element_type=F32)
    a_u = jnp.dot(h, wu_ref[...], preferred_element_type=F32)
    act = (jax.nn.silu(a_g) * a_u).astype(BF16)
    acc_scr[...] += jnp.dot(act, wo_ref[...], preferred_element_type=F32)

    @pl.when(f == nf - 1)
    def _():
        tiles = []
        for k in range(d // tn):
            cs = slice(k * tn, (k + 1) * tn)
            tiles.append(alpha * x_ref[0, :, cs] + (1.0 + g_ref[0, :, cs]) * acc_scr[:, cs])
        _layer_norm_rows(tiles, lng_ref, lnb_ref, o_ref, d)


def _ffn_res_ln(x, sc, sh, gate, w_in, w_out, ln_g, ln_b, alpha):
    bsz, seq, d = x.shape
    dff = w_out.shape[0]
    tm = _pick(seq, (512, 256, 128))
    tf = _pick(dff, (512, 256, 128))
    nf = dff // tf
    tn = _pick(d, (512, 256, 128))
    vec = pl.BlockSpec((1, 1, d), lambda b, i, f: (b, 0, 0))
    kern = functools.partial(_ffn_kernel, nf=nf, alpha=alpha, d=d, tn=tn)
    return pl.pallas_call(
        kern,
        out_shape=jax.ShapeDtypeStruct((bsz, seq, d), F32),
        grid=(bsz, seq // tm, nf),
        in_specs=[
            pl.BlockSpec((1, tm, d), lambda b, i, f: (b, i, 0)),
            vec, vec, vec,
            pl.BlockSpec((d, tf), lambda b, i, f: (0, f)),
            pl.BlockSpec((d, tf), lambda b, i, f: (0, f + nf)),
            pl.BlockSpec((tf, d), lambda b, i, f: (f, 0)),
            pl.BlockSpec((1, d), lambda b, i, f: (0, 0)),
            pl.BlockSpec((1, d), lambda b, i, f: (0, 0)),
        ],
        out_specs=pl.BlockSpec((1, tm, d), lambda b, i, f: (b, i, 0)),
        scratch_shapes=[pltpu.VMEM((tm, d), BF16), pltpu.VMEM((tm, d), F32)],
        compiler_params=_params(("parallel", "parallel", "arbitrary")),
        name="ffn_res_ln",
    )(x, sc, sh, gate, w_in, w_in, w_out, ln_g.reshape(1, d).astype(F32), ln_b.reshape(1, d).astype(F32))


def _qkv_kernel(x_ref, sc_ref, sh_ref, w_ref, cos_ref, sin_ref, o_ref, h_scr, *, tiles_per_tensor, q_scale):
    j = pl.program_id(2)

    @pl.when(j == 0)
    def _():
        h_scr[...] = _modulate(x_ref, sc_ref, sh_ref)

    tensor = j // tiles_per_tensor
    scale = jnp.where(tensor == 0, q_scale, 1.0)
    is_v = tensor == 2
    chunk = min(MXU_WIDTH, w_ref.shape[1])
    for rs in _row_halves(h_scr.shape[0]):
        a = jnp.where(is_v, 1.0, cos_ref[0, rs, :] * scale)
        b = jnp.where(is_v, 0.0, sin_ref[0, rs, :] * scale)
        for c in range(w_ref.shape[1] // chunk):
            acc = jnp.dot(h_scr[rs, :], w_ref[:, c * chunk:(c + 1) * chunk], preferred_element_type=F32)
            for hh in range(chunk // HEAD_DIM):
                xc = acc[:, hh * HEAD_DIM:(hh + 1) * HEAD_DIM]
                xc = xc * a + pltpu.roll(xc, HEAD_DIM // 2, 1) * b
                o_ref[0, c * (chunk // HEAD_DIM) + hh, rs, :] = xc.astype(o_ref.dtype)


def _qkv_proj(x, sc, sh, w_qkv, cos_t, sin_t):
    bsz, seq, d = x.shape
    tm = _pick(seq, (1024, 512, 256, 128))
    tn = _pick(d, (1024, 512, 256, 128))
    hpt = tn // HEAD_DIM
    kern = functools.partial(_qkv_kernel, tiles_per_tensor=d // tn, q_scale=HEAD_DIM ** -0.5 * math.log2(math.e))
    return pl.pallas_call(
        kern,
        out_shape=jax.ShapeDtypeStruct((bsz, 3 * d // HEAD_DIM, seq, HEAD_DIM), BF16),
        grid=(bsz, seq // tm, 3 * d // tn),
        in_specs=[
            pl.BlockSpec((1, tm, d), lambda b, i, j: (b, i, 0)),
            pl.BlockSpec((1, 1, d), lambda b, i, j: (b, 0, 0)),
            pl.BlockSpec((1, 1, d), lambda b, i, j: (b, 0, 0)),
            pl.BlockSpec((d, tn), lambda b, i, j: (0, j)),
            pl.BlockSpec((1, tm, HEAD_DIM), lambda b, i, j: (b, i, 0)),
            pl.BlockSpec((1, tm, HEAD_DIM), lambda b, i, j: (b, i, 0)),
        ],
        out_specs=pl.BlockSpec((1, hpt, tm, HEAD_DIM), lambda b, i, j: (b, j, i, 0)),
        scratch_shapes=[pltpu.VMEM((tm, d), BF16)],
        compiler_params=_params(("parallel", "parallel", "arbitrary")),
        name="dsa_qkv_proj",
    )(x, sc, sh, w_qkv, cos_t, sin_t)


def _idx_kernel(x_ref, sc_ref, sh_ref, w_ref, cos_ref, sin_ref, qi_ref, kia_ref, kib_ref, wi_ref,
                *, n_pair, idx_heads, w_scale):
    h = _modulate(x_ref, sc_ref, sh_ref)
    halves = _row_halves(h.shape[0])
    accs = [jnp.dot(h[rs], w_ref[...], preferred_element_type=F32) for rs in halves]
    lane = lax.broadcasted_iota(I32, (halves[0].stop, LANES), 1)
    first_half = (lane % IDX_DIM) < IDX_DIM // 2

    for rs, acc in zip(halves, accs):
        cos = cos_ref[0, rs, :]
        sin = sin_ref[0, rs, :]

        def rope(xc, cos=cos, sin=sin):
            partner = jnp.where(first_half, pltpu.roll(xc, LANES - IDX_DIM // 2, 1),
                                pltpu.roll(xc, IDX_DIM // 2, 1))
            return xc * cos + partner * sin

        for p in range(n_pair):
            cs = slice(p * LANES, (p + 1) * LANES)
            qi_ref[0, rs, cs] = rope(acc[:, cs]).astype(qi_ref.dtype)
        last = acc[:, n_pair * LANES:]
        ka = jnp.where(lane < IDX_DIM, rope(last), 0.0)
        kia_ref[0, rs, :] = ka.astype(kia_ref.dtype)
        kib_ref[0, rs, :] = pltpu.roll(ka, IDX_DIM, 1).astype(kib_ref.dtype)
        wi_ref[0, rs, :] = jnp.where(lane < idx_heads, pltpu.roll(last, LANES - IDX_DIM, 1), 0.0) * w_scale


def _idx_proj(x, sc, sh, w_idx, cos_t, sin_t, idx_heads):
    bsz, seq, d = x.shape
    n = w_idx.shape[1]
    n_pair = idx_heads // 2
    tm = _pick(seq, (512, 256, 128))
    kern = functools.partial(_idx_kernel, n_pair=n_pair, idx_heads=idx_heads,
                             w_scale=(idx_heads ** -0.5) * (IDX_DIM ** -0.5))
    row = lambda b, i: (b, i, 0)
    return pl.pallas_call(
        kern,
        out_shape=(
            jax.ShapeDtypeStruct((bsz, seq, n_pair * LANES), BF16),
            jax.ShapeDtypeStruct((bsz, seq, LANES), BF16),
            jax.ShapeDtypeStruct((bsz, seq, LANES), BF16),
            jax.ShapeDtypeStruct((bsz, seq, LANES), F32),
        ),
        grid=(bsz, seq // tm),
        in_specs=[
            pl.BlockSpec((1, tm, d), row),
            pl.BlockSpec((1, 1, d), lambda b, i: (b, 0, 0)),
            pl.BlockSpec((1, 1, d), lambda b, i: (b, 0, 0)),
            pl.BlockSpec((d, n), lambda b, i: (0, 0)),
            pl.BlockSpec((1, tm, LANES), row),
            pl.BlockSpec((1, tm, LANES), row),
        ],
        out_specs=(
            pl.BlockSpec((1, tm, n_pair * LANES), row),
            pl.BlockSpec((1, tm, LANES), row),
            pl.BlockSpec((1, tm, LANES), row),
            pl.BlockSpec((1, tm, LANES), row),
        ),
        compiler_params=_params(("parallel", "parallel")),
        name="dsa_idx_proj",
    )(x, sc, sh, w_idx, cos_t, sin_t)


def _bit_transpose32(words):
    a = list(words)
    j = WORD_BITS // 2
    mask = 0x0000FFFF
    while j:
        k = 0
        while k < WORD_BITS:
            t = (a[k] ^ lax.shift_right_logical(a[k + j], jnp.int32(j))) & jnp.int32(mask)
            a[k] = a[k] ^ t
            a[k + j] = a[k + j] ^ (t << j)
            k = (k + j + 1) & ~j
        j >>= 1
        if j:
            mask = (mask ^ (mask << j)) & 0xFFFFFFFF
    return a


def _dsa_kernel(qb_tab, kb_tab, q_ref, k_ref, v_ref, qi_ref, kia_ref, kib_ref, wi_ref, o_ref,
                keys_scr, planes_scr, thr_scr, tie_scr, ngt_scr, nge_scr,
                m_scr, l_scr, acc_scr, bias_scr, s0_scr, s1_scr, r0_scr, r1_scr,
                *, n_pair, k_top, idx_bits):
    heads, qb_rows, _ = acc_scr.shape
    kb_rows = keys_scr.shape[2]
    qb = qb_tab[pl.program_id(1)]
    kb = kb_tab[pl.program_id(1)]
    q0 = qb * qb_rows
    last_kb = (q0 + qb_rows - 1) // kb_rows
    n_chunk = last_kb + 1
    row = lax.broadcasted_iota(I32, (qb_rows, kb_rows), 0) + q0
    col = lax.broadcasted_iota(I32, (qb_rows, kb_rows), 1)
    nt = (((1,), (1,)), ((), ()))

    @pl.when(kb == 0)
    def _select():
        wi = wi_ref[0]

        def score_chunk(c, carry):
            k0 = pl.multiple_of(c * kb_rows, kb_rows)
            ka = kia_ref[0, pl.ds(k0, kb_rows), :]
            kbm = kib_ref[0, pl.ds(k0, kb_rows), :]
            sc = jnp.zeros((qb_rows, kb_rows), F32)
            for p in range(n_pair):
                qp = qi_ref[0, :, p * LANES:(p + 1) * LANES]
                sa = lax.dot_general(qp, ka, nt, preferred_element_type=F32)
                sb = lax.dot_general(qp, kbm, nt, preferred_element_type=F32)
                sc = sc + wi[:, 2 * p:2 * p + 1] * jnp.maximum(sa, 0.0)
                sc = sc + wi[:, 2 * p + 1:2 * p + 2] * jnp.maximum(sb, 0.0)
            bits = lax.bitcast_convert_type(sc, I32)
            key = bits ^ ((bits >> 31) & INT_MAX)
            keys_scr[c] = jnp.where(col + k0 <= row, key, INT_MIN)
            return carry

        lax.fori_loop(0, n_chunk, score_chunk, 0)

        slices_per_chunk = kb_rows // LANES
        chunks_per_set = WORD_BITS // slices_per_chunk
        n_slices = keys_scr.shape[0] * slices_per_chunk
        n_sets = planes_scr.shape[2] // LANES
        live_sets = (n_chunk + chunks_per_set - 1) // chunks_per_set

        def fill_chunk(c, carry):
            keys_scr[c] = jnp.full((qb_rows, kb_rows), INT_MIN, I32)
            return carry

        lax.fori_loop(n_chunk, jnp.minimum(live_sets * chunks_per_set, keys_scr.shape[0]), fill_chunk, 0)

        def pack_rows(g, carry, st):
            r0 = pl.multiple_of(g * SUBLANES, SUBLANES)
            words = []
            for s in range(WORD_BITS):
                sl = st * WORD_BITS + s
                if sl < n_slices:
                    c, off = divmod(sl, slices_per_chunk)
                    words.append(keys_scr[c, pl.ds(r0, SUBLANES), off * LANES:(off + 1) * LANES])
                else:
                    words.append(jnp.full((SUBLANES, LANES), INT_MIN, I32))
            words = _bit_transpose32(words)
            words[0] = ~words[0]
            for i in range(WORD_BITS):
                planes_scr[i, pl.ds(r0, SUBLANES), st * LANES:(st + 1) * LANES] = words[i]
            return carry

        for st in range(n_sets):
            @pl.when(st < live_sets)
            def _(st=st):
                lax.fori_loop(0, qb_rows // SUBLANES, functools.partial(pack_rows, st=st), 0)

            @pl.when(st >= live_sets)
            def _(st=st):
                planes_scr[:, :, st * LANES:(st + 1) * LANES] = jnp.zeros((WORD_BITS, qb_rows, LANES), I32)

        lane_ones = jnp.ones((LANES, LANES), BF16)

        def row_total(pc):
            tot = pc[:, :LANES]
            for st in range(1, n_sets):
                tot = tot + pc[:, st * LANES:(st + 1) * LANES]
            return jnp.dot(tot.astype(F32).astype(BF16), lane_ones, preferred_element_type=F32)

        def all_sets(mask):
            return jnp.concatenate([mask] * n_sets, axis=1)

        def select_two_bits(i, carry):
            cand, above, prefix = carry
            hi = planes_scr[2 * i]
            lo = planes_scr[2 * i + 1]
            c1 = cand & hi
            c0 = cand ^ c1
            c11 = c1 & lo
            c10 = c1 ^ c11
            c01 = c0 & lo
            c00 = c0 ^ c01
            r11 = above + row_total(lax.population_count(c11))
            r10 = r11 + row_total(lax.population_count(c10))
            r01 = r10 + row_total(lax.population_count(c01))
            t11 = r11 >= k_top
            t10 = r10 >= k_top
            t01 = r01 >= k_top
            cand = jnp.where(all_sets(t11), c11,
                             jnp.where(all_sets(t10), c10, jnp.where(all_sets(t01), c01, c00)))
            above = jnp.where(t11, above, jnp.where(t10, r11, jnp.where(t01, r10, r01)))
            bits = jnp.where(t11, 3, jnp.where(t10, 2, jnp.where(t01, 1, 0)))
            prefix = prefix | (bits << (WORD_BITS - 2 - 2 * i))
            return cand, above, prefix

        cand, above, prefix = lax.fori_loop(
            0, WORD_BITS // 2, select_two_bits,
            (jnp.concatenate([jnp.full((qb_rows, LANES), jnp.where(st < live_sets, -1, 0), I32)
                              for st in range(n_sets)], axis=1),
             jnp.zeros((qb_rows, LANES), F32), jnp.zeros((qb_rows, LANES), I32)))
        equal = row_total(lax.population_count(cand))
        thr_scr[...] = (prefix ^ INT_MIN)[:, :1]
        tie_scr[...] = jnp.full((qb_rows, 1), INT_MAX, I32)
        ngt_scr[...] = above[:, :1]
        nge_scr[...] = (above + equal)[:, :1]

        rg = min(qb_rows, SEARCH_ROWS)
        lane_col = lax.broadcasted_iota(I32, (rg, LANES), 1)

        for r in range(qb_rows // rg):
            rs = slice(r * rg, (r + 1) * rg)

            def count(pred, rs=rs):
                def body(c, acc):
                    for t in range(kb_rows // LANES):
                        kk = keys_scr[c, rs, t * LANES:(t + 1) * LANES]
                        idx = lane_col + (c * kb_rows + t * LANES)
                        acc = acc + jnp.where(pred(kk, idx), 1.0, 0.0)
                    return acc
                acc = lax.fori_loop(0, n_chunk, body, jnp.zeros((rg, LANES), F32))
                return jnp.sum(acc, axis=1, keepdims=True)

            def wide(v):
                return jnp.broadcast_to(v, (rg, LANES))

            @pl.when(jnp.max(nge_scr[rs, :]) > k_top)
            def _ties(count=count, rs=rs):
                thr_w = wide(thr_scr[rs, :])
                need = k_top - ngt_scr[rs, :]

                def tie_bit(i, jt):
                    cand = jt + (jnp.int32(1) << (idx_bits - 1 - i))
                    cand_w = wide(cand)
                    cnt = count(lambda kk, idx: (kk == thr_w) & (idx < cand_w))
                    return jnp.where(cnt < need, cand, jt)

                tie_scr[rs, :] = lax.fori_loop(0, idx_bits, tie_bit, jnp.zeros((rg, 1), I32))

        m_scr[...] = jnp.full(m_scr.shape, M_INIT, F32)
        l_scr[...] = jnp.zeros_like(l_scr)
        acc_scr[...] = jnp.zeros_like(acc_scr)

    def _attend():
        ones = jnp.ones((kb_rows, HEAD_DIM), BF16)

        kk = keys_scr[kb]
        thr = thr_scr[...]
        lim = jnp.minimum(tie_scr[...], row[:, :1])
        tie_bias = jnp.where(col + kb * kb_rows <= lim, 0.0, MASK_VALUE)
        bias_scr[...] = jnp.where(kk > thr, 0.0, jnp.where(kk == thr, tie_bias, MASK_VALUE))

        def logits(h, s_ref, r_ref):
            s = lax.dot_general(q_ref[0, h], k_ref[0, h], nt, preferred_element_type=F32) + bias_scr[...]
            s_ref[...] = s
            r_ref[...] = jnp.broadcast_to(jnp.max(s, axis=1, keepdims=True), r_ref.shape)

        def accumulate(h, s_ref, r_ref):
            m_old = m_scr[h]
            m_new = jnp.maximum(m_old, r_ref[...])
            alpha = jnp.exp2(m_old - m_new)
            p = jnp.concatenate(
                [jnp.exp2(s_ref[:, t * LANES:(t + 1) * LANES] - m_new).astype(BF16)
                 for t in range(kb_rows // LANES)], axis=1)
            v_ext = jnp.concatenate([v_ref[0, h], ones], axis=1)
            pv = jnp.dot(p, v_ext, preferred_element_type=F32)
            acc_scr[h] = alpha * acc_scr[h] + pv[:, :HEAD_DIM]
            l_scr[h] = alpha * l_scr[h] + pv[:, HEAD_DIM:]
            m_scr[h] = m_new

        bufs = ((s0_scr, r0_scr), (s1_scr, r1_scr))
        logits(0, *bufs[0])
        for h in range(heads):
            if h + 1 < heads:
                logits(h + 1, *bufs[(h + 1) % 2])
            accumulate(h, *bufs[h % 2])

    _attend()

    @pl.when(kb == last_kb)
    def _finish():
        for h in range(heads):
            o_ref[0, :, h * HEAD_DIM:(h + 1) * HEAD_DIM] = (acc_scr[h] / l_scr[h]).astype(o_ref.dtype)


def _dsa_attention(qkv, qi, kia, kib, wi, k_top):
    bsz, heads3, seq, _ = qkv.shape
    heads = heads3 // 3
    d = heads * HEAD_DIM
    n_pair = qi.shape[-1] // LANES
    qb_rows = _pick(seq, (256, 128))
    kb_rows = _pick(seq, (512, 256, 128))
    n_kb = seq // kb_rows
    n_sets = -(-seq // (WORD_BITS * LANES))

    pairs = [(i, j) for i in range(seq // qb_rows) for j in range((i * qb_rows + qb_rows - 1) // kb_rows + 1)]
    qb_tab = jnp.asarray([p[0] for p in pairs], I32)
    kb_tab = jnp.asarray([p[1] for p in pairs], I32)

    kern = functools.partial(_dsa_kernel, n_pair=n_pair, k_top=k_top, idx_bits=seq.bit_length())
    grid_spec = pltpu.PrefetchScalarGridSpec(
        num_scalar_prefetch=2,
        grid=(bsz, len(pairs)),
        in_specs=[
            pl.BlockSpec((1, heads, qb_rows, HEAD_DIM), lambda b, s, qt, kt: (b, 0, qt[s], 0)),
            pl.BlockSpec((1, heads, kb_rows, HEAD_DIM), lambda b, s, qt, kt: (b, 1, kt[s], 0)),
            pl.BlockSpec((1, heads, kb_rows, HEAD_DIM), lambda b, s, qt, kt: (b, 2, kt[s], 0)),
            pl.BlockSpec((1, qb_rows, n_pair * LANES), lambda b, s, qt, kt: (b, qt[s], 0)),
            pl.BlockSpec((1, seq, LANES), lambda b, s, qt, kt: (b, 0, 0)),
            pl.BlockSpec((1, seq, LANES), lambda b, s, qt, kt: (b, 0, 0)),
            pl.BlockSpec((1, qb_rows, LANES), lambda b, s, qt, kt: (b, qt[s], 0)),
        ],
        out_specs=pl.BlockSpec((1, qb_rows, d), lambda b, s, qt, kt: (b, qt[s], 0)),
        scratch_shapes=[
            pltpu.VMEM((n_kb, qb_rows, kb_rows), I32),
            pltpu.VMEM((WORD_BITS, qb_rows, n_sets * LANES), I32),
            pltpu.VMEM((qb_rows, 1), I32),
            pltpu.VMEM((qb_rows, 1), I32),
            pltpu.VMEM((qb_rows, 1), F32),
            pltpu.VMEM((qb_rows, 1), F32),
            pltpu.VMEM((heads, qb_rows, LANES), F32),
            pltpu.VMEM((heads, qb_rows, LANES), F32),
            pltpu.VMEM((heads, qb_rows, HEAD_DIM), F32),
            pltpu.VMEM((qb_rows, kb_rows), F32),
            pltpu.VMEM((qb_rows, kb_rows), F32),
            pltpu.VMEM((qb_rows, kb_rows), F32),
            pltpu.VMEM((qb_rows, LANES), F32),
            pltpu.VMEM((qb_rows, LANES), F32),
        ],
    )
    return pl.pallas_call(
        kern,
        out_shape=jax.ShapeDtypeStruct((bsz, seq, d), BF16),
        grid_spec=grid_spec,
        compiler_params=_params(("parallel", "arbitrary")),
        name="dsa_select_attend",
    )(qb_tab, kb_tab, qkv, qkv, qkv, qi, kia, kib, wi)


def _rope_tables(positions, dim, repeats):
    inv = 1.0 / (ROPE_THETA ** (jnp.arange(0, dim, 2, dtype=F32) / dim))
    ang = positions.astype(F32)[..., None] * inv
    cos, sin = jnp.cos(ang), jnp.sin(ang)
    cos_t = jnp.concatenate([cos, cos] * repeats, axis=-1)
    sin_t = jnp.concatenate([-sin, sin] * repeats, axis=-1)
    return cos_t, sin_t


def kernel(x, c, positions, ada_w, ada_b, ln_g, ln_b, s5_in_w, s5_a_re, s5_a_im, s5_log_dt, s5_b_re, s5_b_im, s5_c_re, s5_c_im, s5_d, s5_glu_w, s5_glu_b, dsa_in_w, dsa_out_w, ffn_w_in, ffn_w_out):
    bsz, seq, d = x.shape
    depth = ada_w.shape[0]
    alpha = (2.0 * depth) ** 0.25
    idx_heads = (dsa_in_w.shape[-1] - 3 * d - IDX_DIM) // (IDX_DIM + 1)
    k_top = min(TOPK_MAX, seq // 4)
    seg = min(S5_SEG, seq // SUBLANES)

    cos_h, sin_h = _rope_tables(positions, HEAD_DIM, 1)
    cos_i, sin_i = _rope_tables(positions, IDX_DIM, 2)
    mod = _ada_mod(c, ada_w, ada_b)

    for i in range(depth):
        sh1, sc1, g1, sh2, sc2, g2 = [m[:, None, :] for m in jnp.split(mod[i], 6, axis=-1)]
        j = i // 2
        if i % 2 == 0:
            prep =_s5_discretize(s5_a_re[j], s5_a_im[j], s5_log_dt[j], s5_b_re[j], s5_b_im[j],
                                  s5_c_re[j], s5_c_im[j], seg)
            u = _mod_matmul(x, sc1, sh1, s5_in_w[j].astype(BF16), F32)
            gl = _s5_scan(u, prep, s5_d[j], seg)
            x1 = _matmul_res_ln(gl, s5_glu_w[j].astype(BF16), s5_glu_b[j], x, g1,
                                ln_g[i, 0], ln_b[i, 0], alpha, glu=True)
        else:
            w = dsa_in_w[j]
            n_qi = idx_heads * IDX_DIM
            w_qkv = w[:, :3 * d].astype(BF16)
            pad = jnp.zeros((d, LANES - IDX_DIM - idx_heads), w.dtype)
            w_idx = jnp.concatenate(
                [w[:, 3 * d:3 * d + n_qi], w[:, 3 * d + n_qi + idx_heads:], w[:, 3 * d + n_qi:3 * d + n_qi + idx_heads], pad],
                axis=1).astype(BF16)
            qkv = _qkv_proj(x, sc1, sh1, w_qkv, cos_h, sin_h)
            qi, kia, kib, wi = _idx_proj(x, sc1, sh1, w_idx, cos_i, sin_i, idx_heads)
            att = _dsa_attention(qkv, qi, kia, kib, wi, k_top)
            x1 = _matmul_res_ln(att, dsa_out_w[j].astype(BF16), None, x, g1,
                                ln_g[i, 0], ln_b[i, 0], alpha, glu=False)
        x = _ffn_res_ln(x1, sc2, sh2, g2, ffn_w_in[i].astype(BF16), ffn_w_out[i].astype(BF16),
                        ln_g[i, 1], ln_b[i, 1], alpha)
    return x
```

```python
import functools
import math

import jax
import jax.numpy as jnp
from jax import lax
from jax.experimental import pallas as pl
from jax.experimental.pallas import tpu as pltpu

F32 = jnp.float32
BF16 = jnp.bfloat16
I32 = jnp.int32

S5_GROUP = 16
S5_STATE = 64
HEAD_DIM = 128
IDX_DIM = 64
TOPK_MAX = 256
ROPE_THETA = 10000.0
LN_EPS = 1e-5

LANES = 128
SUBLANES = 8
MXU_WIDTH = 256
VMEM_LIMIT_BYTES = 56 * 1024 * 1024
RESIDENT_WEIGHT_BYTES = 16 * 1024 * 1024

S5_PACK_GROUPS = 16
S5_SEG = 64

INT_MIN = -(2 ** 31)
INT_MAX = 2 ** 31 - 1
SEARCH_ROWS = 128
WORD_BITS = 32

MASK_VALUE = -2e30
M_INIT = -1e30


def _pick(n, cands):
    for c in cands:
        if n % c == 0:
            return c
    return n


def _row_halves(rows):
    half = rows // 2
    return (slice(0, half), slice(half, rows))


def _params(sem):
    return pltpu.CompilerParams(dimension_semantics=sem, vmem_limit_bytes=VMEM_LIMIT_BYTES)


def _ada_kernel(c_ref, w_ref, b_ref, o_ref):
    ca = jax.nn.silu(c_ref[...]).astype(BF16)
    o_ref[0] = jnp.dot(ca, w_ref[0].astype(BF16), preferred_element_type=F32) + b_ref[0]


def _ada_mod(c, ada_w, ada_b):
    bsz, d = c.shape
    depth, _, n = ada_w.shape
    rows = SUBLANES * ((bsz + SUBLANES - 1) // SUBLANES)
    cp = jnp.zeros((rows, d), F32).at[:bsz].set(c)
    tn = _pick(n, (1024, 512, 256, 128))
    out = pl.pallas_call(
        _ada_kernel,
        out_shape=jax.ShapeDtypeStruct((depth, rows, n), F32),
        grid=(depth, n // tn),
        in_specs=[
            pl.BlockSpec((rows, d), lambda l, j: (0, 0)),
            pl.BlockSpec((1, d, tn), lambda l, j: (l, 0, j)),
            pl.BlockSpec((1, 1, tn), lambda l, j: (l, 0, j)),
        ],
        out_specs=pl.BlockSpec((1, rows, tn), lambda l, j: (l, 0, j)),
        compiler_params=_params(("arbitrary", "arbitrary")),
        name="ada_mod",
    )(cp, ada_w, ada_b.reshape(depth, 1, n))
    return out[:, :bsz]


def _modulate(x_ref, sc_ref, sh_ref):
    return (x_ref[0] * (1.0 + sc_ref[0]) + sh_ref[0]).astype(BF16)


def _modmm_kernel(x_ref, sc_ref, sh_ref, w_ref, o_ref, h_scr):
    @pl.when(pl.program_id(2) == 0)
    def _():
        h_scr[...] = _modulate(x_ref, sc_ref, sh_ref)

    for rs in _row_halves(h_scr.shape[0]):
        o_ref[0, rs, :] = jnp.dot(h_scr[rs, :], w_ref[...], preferred_element_type=F32).astype(o_ref.dtype)


def _mod_matmul(x, sc, sh, w, out_dtype):
    bsz, seq, d = x.shape
    n = w.shape[1]
    tm = _pick(seq, (512, 256, 128))
    tn = n if 2 * d * n * w.dtype.itemsize <= RESIDENT_WEIGHT_BYTES else _pick(n, (1024, 512, 256, 128))
    return pl.pallas_call(
        _modmm_kernel,
        out_shape=jax.ShapeDtypeStruct((bsz, seq, n), out_dtype),
        grid=(bsz, seq // tm, n // tn),
        in_specs=[
            pl.BlockSpec((1, tm, d), lambda b, i, j: (b, i, 0)),
            pl.BlockSpec((1, 1, d), lambda b, i, j: (b, 0, 0)),
            pl.BlockSpec((1, 1, d), lambda b, i, j: (b, 0, 0)),
            pl.BlockSpec((d, tn), lambda b, i, j: (0, j)),
        ],
        out_specs=pl.BlockSpec((1, tm, tn), lambda b, i, j: (b, i, j)),
        scratch_shapes=[pltpu.VMEM((tm, d), BF16)],
        compiler_params=_params(("parallel", "parallel", "arbitrary")),
        name="mod_matmul",
    )(x, sc, sh, w)


def _s5_kernel(*refs, seg, width, lane_tiles):
    u_refs = refs[:lane_tiles]
    (bb_ref, cb_ref, lre_ref, lim_ref, ltre_ref, ltim_ref, d_ref, o_ref,
     xs_scr, carry_scr, cin_scr, up_scr) = refs[lane_tiles:]
    ns = xs_scr.shape[1] // 2

    @pl.when(pl.program_id(2) == 0)
    def _():
        carry_scr[...] = jnp.zeros_like(carry_scr)

    for t, u_ref in enumerate(u_refs):
        for i in range(seg):
            up_scr[t, i * SUBLANES:(i + 1) * SUBLANES, :] = u_ref[0, pl.ds(i, SUBLANES, stride=seg), :]
    u = jnp.concatenate([up_scr[t] for t in range(len(u_refs))], axis=1)
    halves = _row_halves(u.shape[0])
    for rs in halves:
        xs_scr[rs, :] = jnp.dot(u[rs].astype(BF16), bb_ref[0], preferred_element_type=F32)

    for part in range(ns // width):
        cr = slice(part * width, (part + 1) * width)
        ci = slice(ns + part * width, ns + (part + 1) * width)
        lr = jnp.broadcast_to(lre_ref[0, :, cr], (SUBLANES, width))
        li = jnp.broadcast_to(lim_ref[0, :, cr], (SUBLANES, width))

        def local_step(i, st, cr=cr, ci=ci, lr=lr, li=li):
            sr, si = st
            r0 = pl.multiple_of(i * SUBLANES, SUBLANES)
            nr = lr * sr - li * si + xs_scr[pl.ds(r0, SUBLANES), cr]
            ni = lr * si + li * sr + xs_scr[pl.ds(r0, SUBLANES), ci]
            xs_scr[pl.ds(r0, SUBLANES), cr] = nr
            xs_scr[pl.ds(r0, SUBLANES), ci] = ni
            return nr, ni

        zero = jnp.zeros((SUBLANES, width), F32)
        er, ei = lax.fori_loop(0, seg, local_step, (zero, zero), unroll=2)

        ltr = ltre_ref[0, :, cr]
        lti = ltim_ref[0, :, cr]
        c_r = carry_scr[:, cr]
        c_i = carry_scr[:, ci]
        for s in range(SUBLANES):
            cin_scr[s:s + 1, cr] = c_r
            cin_scr[s:s + 1, ci] = c_i
            e_r = er[s:s + 1, :]
            e_i = ei[s:s + 1, :]
            c_r, c_i = ltr * c_r - lti * c_i + e_r, ltr * c_i + lti * c_r + e_i
        carry_scr[:, cr] = c_r
        carry_scr[:, ci] = c_i

        def carry_step(i, st, cr=cr, ci=ci, lr=lr, li=li):
            pr, pi_ = st
            r0 = pl.multiple_of(i * SUBLANES, SUBLANES)
            nr = lr * pr - li * pi_
            ni = lr * pi_ + li * pr
            xs_scr[pl.ds(r0, SUBLANES), cr] += nr
            xs_scr[pl.ds(r0, SUBLANES), ci] += ni
            return nr, ni

        lax.fori_loop(0, seg, carry_step, (cin_scr[:, cr], cin_scr[:, ci]), unroll=2)

    for rs in halves:
        y = jnp.dot(xs_scr[rs, :].astype(BF16), cb_ref[0], preferred_element_type=F32)
        g = jax.nn.gelu(y + d_ref[0] * u[rs])
        for t in range(len(u_refs)):
            up_scr[t, rs, :] = g[:, t * LANES:(t + 1) * LANES]
    for t in range(len(u_refs)):
        for s in range(SUBLANES):
            o_ref[0, s * seg:(s + 1) * seg, t * LANES:(t + 1) * LANES] = (
                up_scr[t, pl.ds(s, seg, stride=SUBLANES), :].astype(o_ref.dtype))


def _cmul(ar, ai, br, bi):
    return ar * br - ai * bi, ar * bi + ai * br


def _s5_discretize(a_re, a_im, log_dt, b_re, b_im, c_re, c_im, seg):
    g, n = a_re.shape
    p = b_re.shape[-1]
    pg = S5_PACK_GROUPS
    packs = g // pg
    a_re, a_im = a_re.astype(F32), a_im.astype(F32)
    dt = jnp.exp(log_dt.astype(F32))[:, None]
    mag = jnp.exp(a_re * dt)
    lb_re, lb_im = mag * jnp.cos(a_im * dt), mag * jnp.sin(a_im * dt)
    den = a_re * a_re + a_im * a_im
    nr, ni = lb_re - 1.0, lb_im
    f_re = (nr * a_re + ni * a_im) / den
    f_im = (ni * a_re - nr * a_im) / den
    bb_re, bb_im = _cmul(f_re[..., None], f_im[..., None], b_re.astype(F32), b_im.astype(F32))
    lt_re, lt_im = lb_re, lb_im
    for _ in range(int(math.log2(seg))):
        lt_re, lt_im = _cmul(lt_re, lt_im, lt_re, lt_im)
    on_diag = (jnp.arange(pg * p)[:, None] // p) == (jnp.arange(pg * n)[None, :] // n)

    def blk_diag(m):
        tiled = jnp.tile(m.astype(BF16).reshape(packs, pg * p, n), (1, 1, pg))
        return jnp.where(on_diag[None], tiled, 0)

    b_blk = jnp.concatenate([blk_diag(jnp.swapaxes(bb_re, 1, 2)), blk_diag(jnp.swapaxes(bb_im, 1, 2))], axis=-1)
    c_blk = jnp.swapaxes(jnp.concatenate([blk_diag(c_re), blk_diag(-c_im)], axis=-1), 1, 2)

    def vec(m):
        return m.reshape(packs, 1, pg * n)

    return b_blk, c_blk, vec(lb_re), vec(lb_im), vec(lt_re), vec(lt_im)


def _s5_scan(u, prep, d_skip, seg):
    bsz, seq, d = u.shape
    b_blk, c_blk, lre, lim, ltre, ltim = prep
    packs, pw, ns2 = b_blk.shape
    ns = ns2 // 2
    rows = SUBLANES * seg
    width = _pick(ns, (1024, 512, 256, 128))
    lane_tiles = pw // LANES
    kern = functools.partial(_s5_kernel, seg=seg, width=width, lane_tiles=lane_tiles)
    vspec = pl.BlockSpec((1, 1, ns), lambda b, k, m: (k, 0, 0))
    u_specs = [pl.BlockSpec((1, rows, LANES), lambda b, k, m, t=t: (b, m, k * lane_tiles + t))
               for t in range(lane_tiles)]
    return pl.pallas_call(
        kern,
        out_shape=jax.ShapeDtypeStruct((bsz, seq, d), BF16),
        grid=(bsz, packs, seq // rows),
        in_specs=u_specs + [
            pl.BlockSpec((1, pw, ns2), lambda b, k, m: (k, 0, 0)),
            pl.BlockSpec((1, ns2, pw), lambda b, k, m: (k, 0, 0)),
            vspec, vspec, vspec, vspec,
            pl.BlockSpec((1, 1, pw), lambda b, k, m: (k, 0, 0)),
        ],
        out_specs=pl.BlockSpec((1, rows, pw), lambda b, k, m: (b, m, k)),
        scratch_shapes=[
            pltpu.VMEM((rows, ns2), F32),
            pltpu.VMEM((1, ns2), F32),
            pltpu.VMEM((SUBLANES, ns2), F32),
            pltpu.VMEM((lane_tiles, rows, LANES), F32),
        ],
        compiler_params=_params(("parallel", "parallel", "arbitrary")),
        name="s5_scan",
    )(*([u] * lane_tiles), b_blk, c_blk, lre, lim, ltre, ltim, d_skip.reshape(packs, 1, pw).astype(F32))


def _layer_norm_rows(tiles, lng_ref, lnb_ref, o_ref, d):
    tn = tiles[0].shape[1]
    tot = tiles[0].sum(axis=1, keepdims=True)
    for t in tiles[1:]:
        tot = tot + t.sum(axis=1, keepdims=True)
    mu = tot * (1.0 / d)
    sq = jnp.square(tiles[0] - mu).sum(axis=1, keepdims=True)
    for t in tiles[1:]:
        sq = sq + jnp.square(t - mu).sum(axis=1, keepdims=True)
    inv = lax.rsqrt(sq * (1.0 / d) + LN_EPS)
    for k, t in enumerate(tiles):
        cs = slice(k * tn, (k + 1) * tn)
        o_ref[0, :, cs] = ((t - mu) * inv * lng_ref[:, cs] + lnb_ref[:, cs]).astype(o_ref.dtype)


def _mmln_kernel(*refs, glu, nj, alpha, d):
    if glu:
        a_ref, w1_ref, w2_ref, b1_ref, b2_ref, x_ref, g_ref, lng_ref, lnb_ref, o_ref, r_scr = refs
    else:
        a_ref, w1_ref, x_ref, g_ref, lng_ref, lnb_ref, o_ref, r_scr = refs
    j = pl.program_id(2)
    for rs in _row_halves(a_ref.shape[1]):
        a = a_ref[0, rs, :]
        y = jnp.dot(a, w1_ref[...], preferred_element_type=F32)
        if glu:
            y = y + b1_ref[...]
            gate = jnp.dot(a, w2_ref[...], preferred_element_type=F32) + b2_ref[...]
            y = y * jax.nn.sigmoid(gate)
        r_scr[j, rs, :] = alpha * x_ref[0, rs, :] + (1.0 + g_ref[0]) * y

    @pl.when(j == nj - 1)
    def _():
        _layer_norm_rows([r_scr[t] for t in range(nj)], lng_ref, lnb_ref, o_ref, d)


def _matmul_res_ln(a, w, bias, xres, gate, ln_g, ln_b, alpha, glu):
    bsz, seq, k = a.shape
    d = xres.shape[-1]
    if glu:
        tm = _pick(seq, (1024, 512, 256, 128))
        tn = _pick(d, (512, 256, 128))
    else:
        tm = _pick(seq, (512, 256, 128))
        tn = d if 2 * k * d * w.dtype.itemsize <= RESIDENT_WEIGHT_BYTES else _pick(d, (512, 256, 128))
    nj = d // tn
    a_spec = pl.BlockSpec((1, tm, k), lambda b, i, j: (b, i, 0))
    w1_spec = pl.BlockSpec((k, tn), lambda b, i, j: (0, j))
    tail_specs = [
        pl.BlockSpec((1, tm, tn), lambda b, i, j: (b, i, j)),
        pl.BlockSpec((1, 1, tn), lambda b, i, j: (b, 0, j)),
        pl.BlockSpec((1, d), lambda b, i, j: (0, 0)),
        pl.BlockSpec((1, d), lambda b, i, j: (0, 0)),
    ]
    tail = (xres, gate, ln_g.reshape(1, d).astype(F32), ln_b.reshape(1, d).astype(F32))
    if glu:
        in_specs = [a_spec, w1_spec,
                    pl.BlockSpec((k, tn), lambda b, i, j: (0, j + nj)),
                    pl.BlockSpec((1, tn), lambda b, i, j: (0, j)),
                    pl.BlockSpec((1, tn), lambda b, i, j: (0, j + nj))] + tail_specs
        b2d = bias.reshape(1, 2 * d).astype(F32)
        args = (a, w, w, b2d, b2d) + tail
    else:
        in_specs = [a_spec, w1_spec] + tail_specs
        args = (a, w) + tail
    kern = functools.partial(_mmln_kernel, glu=glu, nj=nj, alpha=alpha, d=d)
    return pl.pallas_call(
        kern,
        out_shape=jax.ShapeDtypeStruct((bsz, seq, d), F32),
        grid=(bsz, seq // tm, nj),
        in_specs=in_specs,
        out_specs=pl.BlockSpec((1, tm, d), lambda b, i, j: (b, i, 0)),
        scratch_shapes=[pltpu.VMEM((nj, tm, tn), F32)],
        compiler_params=_params(("parallel", "parallel", "arbitrary")),
        name="matmul_res_ln",
    )(*args)


def _ffn_kernel(x_ref, sc_ref, sh_ref, g_ref, wg_ref, wu_ref, wo_ref, lng_ref, lnb_ref, o_ref,
                h_scr, acc_scr, *, nf, alpha, d, tn):
    f = pl.program_id(2)

    @pl.when(f == 0)
    def _():
        h_scr[...] = _modulate(x_ref, sc_ref, sh_ref)
        acc_scr[...] = jnp.zeros_like(acc_scr)

    h = h_scr[...]
    a_g = jnp.dot(h, wg_ref[0], preferred_element_type=F32)
    a_u = jnp.dot(h, wu_ref[0], preferred_element_type=F32)
    act = (jax.nn.silu(a_g) * a_u).astype(BF16)
    acc_scr[...] += jnp.dot(act, wo_ref[0], preferred_element_type=F32)

    @pl.when(f == nf - 1)
    def _():
        tiles = []
        for k in range(d // tn):
            cs = slice(k * tn, (k + 1) * tn)
            tiles.append(alpha * x_ref[0, :, cs] + (1.0 + g_ref[0, :, cs]) * acc_scr[:, cs])
        _layer_norm_rows(tiles, lng_ref, lnb_ref, o_ref, d)


def _ffn_res_ln(x, sc, sh, gate, w_in, w_out, layer, ln_g, ln_b, alpha):
    bsz, seq, d = x.shape
    dff = w_out.shape[1]
    tm = _pick(seq, (512, 256, 128))
    tf = _pick(dff, (512, 256, 128))
    nf = dff // tf
    tn = _pick(d, (512, 256, 128))
    vec = pl.BlockSpec((1, 1, d), lambda b, i, f: (b, 0, 0))
    kern = functools.partial(_ffn_kernel, nf=nf, alpha=alpha, d=d, tn=tn)
    return pl.pallas_call(
        kern,
        out_shape=jax.ShapeDtypeStruct((bsz, seq, d), F32),
        grid=(bsz, seq // tm, nf),
        in_specs=[
            pl.BlockSpec((1, tm, d), lambda b, i, f: (b, i, 0)),
            vec, vec, vec,
            pl.BlockSpec((1, d, tf), lambda b, i, f: (layer, 0, f)),
            pl.BlockSpec((1, d, tf), lambda b, i, f: (layer, 0, f + nf)),
            pl.BlockSpec((1, tf, d), lambda b, i, f: (layer, f, 0)),
            pl.BlockSpec((1, d), lambda b, i, f: (0, 0)),
            pl.BlockSpec((1, d), lambda b, i, f: (0, 0)),
        ],
        out_specs=pl.BlockSpec((1, tm, d), lambda b, i, f: (b, i, 0)),
        scratch_shapes=[pltpu.VMEM((tm, d), BF16), pltpu.VMEM((tm, d), F32)],
        compiler_params=_params(("parallel", "parallel", "arbitrary")),
        name="ffn_res_ln",
    )(x, sc, sh, gate, w_in, w_in, w_out, ln_g.reshape(1, d).astype(F32), ln_b.reshape(1, d).astype(F32))


def _qkv_kernel(x_ref, sc_ref, sh_ref, w_ref, cos_ref, sin_ref, o_ref, h_scr, *, tiles_per_tensor, q_scale):
    j = pl.program_id(2)

    @pl.when(j == 0)
    def _():
        h_scr[...] = _modulate(x_ref, sc_ref, sh_ref)

    tensor = j // tiles_per_tensor
    scale = jnp.where(tensor == 0, q_scale, 1.0)
    is_v = tensor == 2
    chunk = min(MXU_WIDTH, w_ref.shape[1])
    for rs in _row_halves(h_scr.shape[0]):
        a = jnp.where(is_v, 1.0, cos_ref[0, rs, :] * scale)
        b = jnp.where(is_v, 0.0, sin_ref[0, rs, :] * scale)
        for c in range(w_ref.shape[1] // chunk):
            acc = jnp.dot(h_scr[rs, :], w_ref[:, c * chunk:(c + 1) * chunk], preferred_element_type=F32)
            for hh in range(chunk // HEAD_DIM):
                xc = acc[:, hh * HEAD_DIM:(hh + 1) * HEAD_DIM]
                xc = xc * a + pltpu.roll(xc, HEAD_DIM // 2, 1) * b
                o_ref[0, c * (chunk // HEAD_DIM) + hh, rs, :] = xc.astype(o_ref.dtype)


def _qkv_proj(x, sc, sh, w_qkv, cos_t, sin_t):
    bsz, seq, d = x.shape
    tm = _pick(seq, (1024, 512, 256, 128))
    tn = _pick(d, (1024, 512, 256, 128))
    hpt = tn // HEAD_DIM
    kern = functools.partial(_qkv_kernel, tiles_per_tensor=d // tn, q_scale=HEAD_DIM ** -0.5 * math.log2(math.e))
    return pl.pallas_call(
        kern,
        out_shape=jax.ShapeDtypeStruct((bsz, 3 * d // HEAD_DIM, seq, HEAD_DIM), BF16),
        grid=(bsz, seq // tm, 3 * d // tn),
        in_specs=[
            pl.BlockSpec((1, tm, d), lambda b, i, j: (b, i, 0)),
            pl.BlockSpec((1, 1, d), lambda b, i, j: (b, 0, 0)),
            pl.BlockSpec((1, 1, d), lambda b, i, j: (b, 0, 0)),
            pl.BlockSpec((d, tn), lambda b, i, j: (0, j)),
            pl.BlockSpec((1, tm, HEAD_DIM), lambda b, i, j: (b, i, 0)),
            pl.BlockSpec((1, tm, HEAD_DIM), lambda b, i, j: (b, i, 0)),
        ],
        out_specs=pl.BlockSpec((1, hpt, tm, HEAD_DIM), lambda b, i, j: (b, j, i, 0)),
        scratch_shapes=[pltpu.VMEM((tm, d), BF16)],
        compiler_params=_params(("parallel", "parallel", "arbitrary")),
        name="dsa_qkv_proj",
    )(x, sc, sh, w_qkv, cos_t, sin_t)


def _idx_kernel(x_ref, sc_ref, sh_ref, w_ref, cos_ref, sin_ref, qi_ref, kia_ref, kib_ref, wi_ref,
                *, n_pair, idx_heads, w_scale):
    h = _modulate(x_ref, sc_ref, sh_ref)
    halves = _row_halves(h.shape[0])
    accs = [jnp.dot(h[rs], w_ref[...], preferred_element_type=F32) for rs in halves]
    lane = lax.broadcasted_iota(I32, (halves[0].stop, LANES), 1)
    first_half = (lane % IDX_DIM) < IDX_DIM // 2

    for rs, acc in zip(halves, accs):
        cos = cos_ref[0, rs, :]
        sin = sin_ref[0, rs, :]

        def rope(xc, cos=cos, sin=sin):
            partner = jnp.where(first_half, pltpu.roll(xc, LANES - IDX_DIM // 2, 1),
                                pltpu.roll(xc, IDX_DIM // 2, 1))
            return xc * cos + partner * sin

        for p in range(n_pair):
            cs = slice(p * LANES, (p + 1) * LANES)
            qi_ref[0, rs, cs] = rope(acc[:, cs]).astype(qi_ref.dtype)
        last = acc[:, n_pair * LANES:]
        ka = jnp.where(lane < IDX_DIM, rope(last), 0.0)
        kia_ref[0, rs, :] = ka.astype(kia_ref.dtype)
        kib_ref[0, rs, :] = pltpu.roll(ka, IDX_DIM, 1).astype(kib_ref.dtype)
        wi_ref[0, rs, :] = jnp.where(lane < idx_heads, pltpu.roll(last, LANES - IDX_DIM, 1), 0.0) * w_scale


def _idx_proj(x, sc, sh, w_idx, cos_t, sin_t, idx_heads):
    bsz, seq, d = x.shape
    n = w_idx.shape[1]
    n_pair = idx_heads // 2
    tm = _pick(seq, (512, 256, 128))
    kern = functools.partial(_idx_kernel, n_pair=n_pair, idx_heads=idx_heads,
                             w_scale=(idx_heads ** -0.5) * (IDX_DIM ** -0.5))
    row = lambda b, i: (b, i, 0)
    return pl.pallas_call(
        kern,
        out_shape=(
            jax.ShapeDtypeStruct((bsz, seq, n_pair * LANES), BF16),
            jax.ShapeDtypeStruct((bsz, seq, LANES), BF16),
            jax.ShapeDtypeStruct((bsz, seq, LANES), BF16),
            jax.ShapeDtypeStruct((bsz, seq, LANES), F32),
        ),
        grid=(bsz, seq // tm),
        in_specs=[
            pl.BlockSpec((1, tm, d), row),
            pl.BlockSpec((1, 1, d), lambda b, i: (b, 0, 0)),
            pl.BlockSpec((1, 1, d), lambda b, i: (b, 0, 0)),
            pl.BlockSpec((d, n), lambda b, i: (0, 0)),
            pl.BlockSpec((1, tm, LANES), row),
            pl.BlockSpec((1, tm, LANES), row),
        ],
        out_specs=(
            pl.BlockSpec((1, tm, n_pair * LANES), row),
            pl.BlockSpec((1, tm, LANES), row),
            pl.BlockSpec((1, tm, LANES), row),
            pl.BlockSpec((1, tm, LANES), row),
        ),
        compiler_params=_params(("parallel", "parallel")),
        name="dsa_idx_proj",
    )(x, sc, sh, w_idx, cos_t, sin_t)


def _bit_transpose32(words):
    a = list(words)
    j = WORD_BITS // 2
    mask = 0x0000FFFF
    while j:
        k = 0
        while k < WORD_BITS:
            t = (a[k] ^ lax.shift_right_logical(a[k + j], jnp.int32(j))) & jnp.int32(mask)
            a[k] = a[k] ^ t
            a[k + j] = a[k + j] ^ (t << j)
            k = (k + j + 1) & ~j
        j >>= 1
        if j:
            mask = (mask ^ (mask << j)) & 0xFFFFFFFF
    return a


def _dsa_kernel(qb_tab, kb_tab, q_ref, k_ref, v_ref, qi_ref, kia_ref, kib_ref, wi_ref, o_ref,
                keys_scr, planes_scr, thr_scr, tie_scr, ngt_scr, nge_scr,
                m_scr, l_scr, acc_scr, bias_scr, s0_scr, s1_scr, r0_scr, r1_scr,
                *, n_pair, k_top, idx_bits):
    heads, qb_rows, _ = acc_scr.shape
    kb_rows = keys_scr.shape[2]
    qb = qb_tab[pl.program_id(1)]
    kb = kb_tab[pl.program_id(1)]
    q0 = qb * qb_rows
    last_kb = (q0 + qb_rows - 1) // kb_rows
    n_chunk = last_kb + 1
    row = lax.broadcasted_iota(I32, (qb_rows, kb_rows), 0) + q0
    col = lax.broadcasted_iota(I32, (qb_rows, kb_rows), 1)
    nt = (((1,), (1,)), ((), ()))

    @pl.when(kb == 0)
    def _select():
        wi = wi_ref[0]

        def score_chunk(c, carry):
            k0 = pl.multiple_of(c * kb_rows, kb_rows)
            ka = kia_ref[0, pl.ds(k0, kb_rows), :]
            kbm = kib_ref[0, pl.ds(k0, kb_rows), :]
            sc = jnp.zeros((qb_rows, kb_rows), F32)
            for p in range(n_pair):
                qp = qi_ref[0, :, p * LANES:(p + 1) * LANES]
                sa = lax.dot_general(qp, ka, nt, preferred_element_type=F32)
                sb = lax.dot_general(qp, kbm, nt, preferred_element_type=F32)
                sc = sc + wi[:, 2 * p:2 * p + 1] * jnp.maximum(sa, 0.0)
                sc = sc + wi[:, 2 * p + 1:2 * p + 2] * jnp.maximum(sb, 0.0)
            bits = lax.bitcast_convert_type(sc, I32)
            key = bits ^ ((bits >> 31) & INT_MAX)
            keys_scr[c] = jnp.where(col + k0 <= row, key, INT_MIN)
            return carry

        lax.fori_loop(0, n_chunk, score_chunk, 0)

        slices_per_chunk = kb_rows // LANES
        chunks_per_set = WORD_BITS // slices_per_chunk
        n_slices = keys_scr.shape[0] * slices_per_chunk
        n_sets = planes_scr.shape[2] // LANES
        live_sets = (n_chunk + chunks_per_set - 1) // chunks_per_set

        def fill_chunk(c, carry):
            keys_scr[c] = jnp.full((qb_rows, kb_rows), INT_MIN, I32)
            return carry

        lax.fori_loop(n_chunk, jnp.minimum(live_sets * chunks_per_set, keys_scr.shape[0]), fill_chunk, 0)

        def pack_rows(g, carry, st):
            r0 = pl.multiple_of(g * SUBLANES, SUBLANES)
            words = []
            for s in range(WORD_BITS):
                sl = st * WORD_BITS + s
                if sl < n_slices:
                    c, off = divmod(sl, slices_per_chunk)
                    words.append(keys_scr[c, pl.ds(r0, SUBLANES), off * LANES:(off + 1) * LANES])
                else:
                    words.append(jnp.full((SUBLANES, LANES), INT_MIN, I32))
            words = _bit_transpose32(words)
            words[0] = ~words[0]
            for i in range(WORD_BITS):
                planes_scr[i, pl.ds(r0, SUBLANES), st * LANES:(st + 1) * LANES] = words[i]
            return carry

        for st in range(n_sets):
            @pl.when(st < live_sets)
            def _(st=st):
                lax.fori_loop(0, qb_rows // SUBLANES, functools.partial(pack_rows, st=st), 0)

            @pl.when(st >= live_sets)
            def _(st=st):
                planes_scr[:, :, st * LANES:(st + 1) * LANES] = jnp.zeros((WORD_BITS, qb_rows, LANES), I32)

        lane_ones = jnp.ones((LANES, LANES), BF16)

        def row_total(pc):
            tot = pc[:, :LANES]
            for st in range(1, n_sets):
                tot = tot + pc[:, st * LANES:(st + 1) * LANES]
            return jnp.dot(tot.astype(F32).astype(BF16), lane_ones, preferred_element_type=F32)

        def all_sets(mask):
            return jnp.concatenate([mask] * n_sets, axis=1)

        def select_two_bits(i, carry):
            cand, above, prefix = carry
            hi = planes_scr[2 * i]
            lo = planes_scr[2 * i + 1]
            c1 = cand & hi
            c0 = cand ^ c1
            c11 = c1 & lo
            c10 = c1 ^ c11
            c01 = c0 & lo
            c00 = c0 ^ c01
            r11 = above + row_total(lax.population_count(c11))
            r10 = r11 + row_total(lax.population_count(c10))
            r01 = r10 + row_total(lax.population_count(c01))
            t11 = r11 >= k_top
            t10 = r10 >= k_top
            t01 = r01 >= k_top
            cand = jnp.where(all_sets(t11), c11,
                             jnp.where(all_sets(t10), c10, jnp.where(all_sets(t01), c01, c00)))
            above = jnp.where(t11, above, jnp.where(t10, r11, jnp.where(t01, r10, r01)))
            bits = jnp.where(t11, 3, jnp.where(t10, 2, jnp.where(t01, 1, 0)))
            prefix = prefix | (bits << (WORD_BITS - 2 - 2 * i))
            return cand, above, prefix

        cand, above, prefix = lax.fori_loop(
            0, WORD_BITS // 2, select_two_bits,
            (jnp.concatenate([jnp.full((qb_rows, LANES), jnp.where(st < live_sets, -1, 0), I32)
                              for st in range(n_sets)], axis=1),
             jnp.zeros((qb_rows, LANES), F32), jnp.zeros((qb_rows, LANES), I32)))
        equal = row_total(lax.population_count(cand))
        thr_scr[...] = (prefix ^ INT_MIN)[:, :1]
        tie_scr[...] = jnp.full((qb_rows, 1), INT_MAX, I32)
        ngt_scr[...] = above[:, :1]
        nge_scr[...] = (above + equal)[:, :1]

        rg = min(qb_rows, SEARCH_ROWS)
        lane_col = lax.broadcasted_iota(I32, (rg, LANES), 1)

        for r in range(qb_rows // rg):
            rs = slice(r * rg, (r + 1) * rg)

            def count(pred, rs=rs):
                def body(c, acc):
                    for t in range(kb_rows // LANES):
                        kk = keys_scr[c, rs, t * LANES:(t + 1) * LANES]
                        idx = lane_col + (c * kb_rows + t * LANES)
                        acc = acc + jnp.where(pred(kk, idx), 1.0, 0.0)
                    return acc
                acc = lax.fori_loop(0, n_chunk, body, jnp.zeros((rg, LANES), F32))
                return jnp.sum(acc, axis=1, keepdims=True)

            def wide(v):
                return jnp.broadcast_to(v, (rg, LANES))

            @pl.when(jnp.max(nge_scr[rs, :]) > k_top)
            def _ties(count=count, rs=rs):
                thr_w = wide(thr_scr[rs, :])
                need = k_top - ngt_scr[rs, :]

                def tie_bit(i, jt):
                    cand = jt + (jnp.int32(1) << (idx_bits - 1 - i))
                    cand_w = wide(cand)
                    cnt = count(lambda kk, idx: (kk == thr_w) & (idx < cand_w))
                    return jnp.where(cnt < need, cand, jt)

                tie_scr[rs, :] = lax.fori_loop(0, idx_bits, tie_bit, jnp.zeros((rg, 1), I32))

        m_scr[...] = jnp.full(m_scr.shape, M_INIT, F32)
        l_scr[...] = jnp.zeros_like(l_scr)
        acc_scr[...] = jnp.zeros_like(acc_scr)

    def _attend():
        ones = jnp.ones((kb_rows, HEAD_DIM), BF16)

        kk = keys_scr[kb]
        thr = thr_scr[...]
        lim = jnp.minimum(tie_scr[...], row[:, :1])
        tie_bias = jnp.where(col + kb * kb_rows <= lim, 0.0, MASK_VALUE)
        bias_scr[...] = jnp.where(kk > thr, 0.0, jnp.where(kk == thr, tie_bias, MASK_VALUE))

        def logits(h, s_ref, r_ref):
            s = lax.dot_general(q_ref[0, h], k_ref[0, h], nt, preferred_element_type=F32) + bias_scr[...]
            s_ref[...] = s
            r_ref[...] = jnp.broadcast_to(jnp.max(s, axis=1, keepdims=True), r_ref.shape)

        def accumulate(h, s_ref, r_ref):
            m_old = m_scr[h]
            m_new = jnp.maximum(m_old, r_ref[...])
            alpha = jnp.exp2(m_old - m_new)
            p = jnp.concatenate(
                [jnp.exp2(s_ref[:, t * LANES:(t + 1) * LANES] - m_new).astype(BF16)
                 for t in range(kb_rows // LANES)], axis=1)
            v_ext = jnp.concatenate([v_ref[0, h], ones], axis=1)
            pv = jnp.dot(p, v_ext, preferred_element_type=F32)
            acc_scr[h] = alpha * acc_scr[h] + pv[:, :HEAD_DIM]
            l_scr[h] = alpha * l_scr[h] + pv[:, HEAD_DIM:]
            m_scr[h] = m_new

        bufs = ((s0_scr, r0_scr), (s1_scr, r1_scr))
        logits(0, *bufs[0])
        for h in range(heads):
            if h + 1 < heads:
                logits(h + 1, *bufs[(h + 1) % 2])
            accumulate(h, *bufs[h % 2])

    _attend()

    @pl.when(kb == last_kb)
    def _finish():
        for h in range(heads):
            o_ref[0, :, h * HEAD_DIM:(h + 1) * HEAD_DIM] = (acc_scr[h] / l_scr[h]).astype(o_ref.dtype)


def _dsa_attention(qkv, qi, kia, kib, wi, k_top):
    bsz, heads3, seq, _ = qkv.shape
    heads = heads3 // 3
    d = heads * HEAD_DIM
    n_pair = qi.shape[-1] // LANES
    qb_rows = _pick(seq, (256, 128))
    kb_rows = _pick(seq, (512, 256, 128))
    n_kb = seq // kb_rows
    n_sets = -(-seq // (WORD_BITS * LANES))

    pairs = [(i, j) for i in range(seq // qb_rows) for j in range((i * qb_rows + qb_rows - 1) // kb_rows + 1)]
    qb_tab = jnp.asarray([p[0] for p in pairs], I32)
    kb_tab = jnp.asarray([p[1] for p in pairs], I32)

    kern = functools.partial(_dsa_kernel, n_pair=n_pair, k_top=k_top, idx_bits=seq.bit_length())
    grid_spec = pltpu.PrefetchScalarGridSpec(
        num_scalar_prefetch=2,
        grid=(bsz, len(pairs)),
        in_specs=[
            pl.BlockSpec((1, heads, qb_rows, HEAD_DIM), lambda b, s, qt, kt: (b, 0, qt[s], 0)),
            pl.BlockSpec((1, heads, kb_rows, HEAD_DIM), lambda b, s, qt, kt: (b, 1, kt[s], 0)),
            pl.BlockSpec((1, heads, kb_rows, HEAD_DIM), lambda b, s, qt, kt: (b, 2, kt[s], 0)),
            pl.BlockSpec((1, qb_rows, n_pair * LANES), lambda b, s, qt, kt: (b, qt[s], 0)),
            pl.BlockSpec((1, seq, LANES), lambda b, s, qt, kt: (b, 0, 0)),
            pl.BlockSpec((1, seq, LANES), lambda b, s, qt, kt: (b, 0, 0)),
            pl.BlockSpec((1, qb_rows, LANES), lambda b, s, qt, kt: (b, qt[s], 0)),
        ],
        out_specs=pl.BlockSpec((1, qb_rows, d), lambda b, s, qt, kt: (b, qt[s], 0)),
        scratch_shapes=[
            pltpu.VMEM((n_kb, qb_rows, kb_rows), I32),
            pltpu.VMEM((WORD_BITS, qb_rows, n_sets * LANES), I32),
            pltpu.VMEM((qb_rows, 1), I32),
            pltpu.VMEM((qb_rows, 1), I32),
            pltpu.VMEM((qb_rows, 1), F32),
            pltpu.VMEM((qb_rows, 1), F32),
            pltpu.VMEM((heads, qb_rows, LANES), F32),
            pltpu.VMEM((heads, qb_rows, LANES), F32),
            pltpu.VMEM((heads, qb_rows, HEAD_DIM), F32),
            pltpu.VMEM((qb_rows, kb_rows), F32),
            pltpu.VMEM((qb_rows, kb_rows), F32),
            pltpu.VMEM((qb_rows, kb_rows), F32),
            pltpu.VMEM((qb_rows, LANES), F32),
            pltpu.VMEM((qb_rows, LANES), F32),
        ],
    )
    return pl.pallas_call(
        kern,
        out_shape=jax.ShapeDtypeStruct((bsz, seq, d), BF16),
        grid_spec=grid_spec,
        compiler_params=_params(("parallel", "arbitrary")),
        name="dsa_select_attend",
    )(qb_tab, kb_tab, qkv, qkv, qkv, qi, kia, kib, wi)


def _rope_tables(positions, dim):
    inv = 1.0 / (ROPE_THETA ** (jnp.arange(0, dim, 2, dtype=F32) / dim))
    half = dim // 2
    lane = jnp.arange(LANES)
    inv_t = inv[lane % half]
    sign = jnp.where((lane % dim) < half, -1.0, 1.0).astype(F32)
    ang = positions.astype(F32)[..., None] * inv_t
    return jnp.cos(ang), jnp.sin(ang) * sign


def kernel(x, c, positions, ada_w, ada_b, ln_g, ln_b, s5_in_w, s5_a_re, s5_a_im, s5_log_dt, s5_b_re, s5_b_im, s5_c_re, s5_c_im, s5_d, s5_glu_w, s5_glu_b, dsa_in_w, dsa_out_w, ffn_w_in, ffn_w_out):
    bsz, seq, d = x.shape
    depth = ada_w.shape[0]
    alpha = (2.0 * depth) ** 0.25
    idx_heads = (dsa_in_w.shape[-1] - 3 * d - IDX_DIM) // (IDX_DIM + 1)
    k_top = min(TOPK_MAX, seq // 4)
    seg = min(S5_SEG, seq // SUBLANES)

    cos_h, sin_h = _rope_tables(positions, HEAD_DIM)
    cos_i, sin_i = _rope_tables(positions, IDX_DIM)
    mod = _ada_mod(c, ada_w, ada_b)
    ffn_in = ffn_w_in.astype(BF16)
    ffn_out = ffn_w_out.astype(BF16)

    for i in range(depth):
        sh1, sc1, g1, sh2, sc2, g2 = [m[:, None, :] for m in jnp.split(mod[i], 6, axis=-1)]
        j = i // 2
        if i % 2 == 0:
            prep = _s5_discretize(s5_a_re[j], s5_a_im[j], s5_log_dt[j], s5_b_re[j], s5_b_im[j],
                                  s5_c_re[j], s5_c_im[j], seg)
            u = _mod_matmul(x, sc1, sh1, s5_in_w[j].astype(BF16), F32)
            gl = _s5_scan(u, prep, s5_d[j], seg)
            x1 = _matmul_res_ln(gl, s5_glu_w[j].astype(BF16), s5_glu_b[j], x, g1,
                                ln_g[i, 0], ln_b[i, 0], alpha, glu=True)
        else:
            w = dsa_in_w[j].astype(BF16)
            n_qi = idx_heads * IDX_DIM
            pad = jnp.zeros((d, LANES - IDX_DIM - idx_heads), w.dtype)
            w_idx = jnp.concatenate(
                [w[:, 3 * d:3 * d + n_qi], w[:, 3 * d + n_qi + idx_heads:], w[:, 3 * d + n_qi:3 * d + n_qi + idx_heads], pad],
                axis=1)
            qkv = _qkv_proj(x, sc1, sh1, w, cos_h, sin_h)
            qi, kia, kib, wi = _idx_proj(x, sc1, sh1, w_idx, cos_i, sin_i, idx_heads)
            att = _dsa_attention(qkv, qi, kia, kib, wi, k_top)
            x1 = _matmul_res_ln(att, dsa_out_w[j].astype(BF16), None, x, g1,
                                ln_g[i, 0], ln_b[i, 0], alpha, glu=False)
        x = _ffn_res_ln(x1, sc2, sh2, g2, ffn_in, ffn_out, i, ln_g[i, 1], ln_b[i, 1], alpha)
    return x
```

```python
import functools
import math

import jax
import jax.numpy as jnp
from jax import lax
from jax.experimental import pallas as pl
from jax.experimental.pallas import tpu as pltpu

F32 = jnp.float32
BF16 = jnp.bfloat16
I32 = jnp.int32

S5_GROUP = 16
S5_STATE = 64
HEAD_DIM = 128
IDX_DIM = 64
TOPK_MAX = 256
ROPE_THETA = 10000.0
LN_EPS = 1e-5

LANES = 128
SUBLANES = 8
MXU_WIDTH = 256
VMEM_LIMIT_BYTES = 56 * 1024 * 1024
RESIDENT_WEIGHT_BYTES = 16 * 1024 * 1024

S5_PACK_GROUPS = 16
S5_SEG = 64

INT_MIN = -(2 ** 31)
INT_MAX = 2 ** 31 - 1
SEARCH_ROWS = 128
WORD_BITS = 32

MASK_VALUE = -2e30
M_INIT = -1e30


def _pick(n, cands):
    for c in cands:
        if n % c == 0:
            return c
    return n


def _row_halves(rows):
    half = rows // 2
    return (slice(0, half), slice(half, rows))


def _params(sem):
    return pltpu.CompilerParams(dimension_semantics=sem, vmem_limit_bytes=VMEM_LIMIT_BYTES)


def _ada_kernel(c_ref, w_ref, b_ref, o_ref):
    ca = jax.nn.silu(c_ref[...]).astype(BF16)
    o_ref[0] = jnp.dot(ca, w_ref[0].astype(BF16), preferred_element_type=F32) + b_ref[0]


def _ada_mod(c, ada_w, ada_b):
    bsz, d = c.shape
    depth, _, n = ada_w.shape
    rows = SUBLANES * ((bsz + SUBLANES - 1) // SUBLANES)
    cp = jnp.zeros((rows, d), F32).at[:bsz].set(c)
    tn = _pick(n, (1024, 512, 256, 128))
    out = pl.pallas_call(
        _ada_kernel,
        out_shape=jax.ShapeDtypeStruct((depth, rows, n), F32),
        grid=(depth, n // tn),
        in_specs=[
            pl.BlockSpec((rows, d), lambda l, j: (0, 0)),
            pl.BlockSpec((1, d, tn), lambda l, j: (l, 0, j)),
            pl.BlockSpec((1, 1, tn), lambda l, j: (l, 0, j)),
        ],
        out_specs=pl.BlockSpec((1, rows, tn), lambda l, j: (l, 0, j)),
        compiler_params=_params(("arbitrary", "arbitrary")),
        name="ada_mod",
    )(cp, ada_w, ada_b.reshape(depth, 1, n))
    return out[:, :bsz]


def _modulate(x_ref, sc_ref, sh_ref):
    return (x_ref[0] * (1.0 + sc_ref[0]) + sh_ref[0]).astype(BF16)


def _modmm_kernel(x_ref, sc_ref, sh_ref, w_ref, o_ref, h_scr):
    @pl.when(pl.program_id(2) == 0)
    def _():
        h_scr[...] = _modulate(x_ref, sc_ref, sh_ref)

    for rs in _row_halves(h_scr.shape[0]):
        o_ref[0, rs, :] = jnp.dot(h_scr[rs, :], w_ref[...], preferred_element_type=F32).astype(o_ref.dtype)


def _mod_matmul(x, sc, sh, w, out_dtype):
    bsz, seq, d = x.shape
    n = w.shape[1]
    tm = _pick(seq, (512, 256, 128))
    tn = n if 2 * d * n * w.dtype.itemsize <= RESIDENT_WEIGHT_BYTES else _pick(n, (1024, 512, 256, 128))
    return pl.pallas_call(
        _modmm_kernel,
        out_shape=jax.ShapeDtypeStruct((bsz, seq, n), out_dtype),
        grid=(bsz, seq // tm, n // tn),
        in_specs=[
            pl.BlockSpec((1, tm, d), lambda b, i, j: (b, i, 0)),
            pl.BlockSpec((1, 1, d), lambda b, i, j: (b, 0, 0)),
            pl.BlockSpec((1, 1, d), lambda b, i, j: (b, 0, 0)),
            pl.BlockSpec((d, tn), lambda b, i, j: (0, j)),
        ],
        out_specs=pl.BlockSpec((1, tm, tn), lambda b, i, j: (b, i, j)),
        scratch_shapes=[pltpu.VMEM((tm, d), BF16)],
        compiler_params=_params(("parallel", "parallel", "arbitrary")),
        name="mod_matmul",
    )(x, sc, sh, w)


def _s5_kernel(*refs, seg, width, lane_tiles):
    u_refs = refs[:lane_tiles]
    (bb_ref, cb_ref, lre_ref, lim_ref, ltre_ref, ltim_ref, d_ref, o_ref,
     xs_scr, carry_scr, cin_scr, up_scr) = refs[lane_tiles:]
    ns = xs_scr.shape[1] // 2

    @pl.when(pl.program_id(2) == 0)
    def _():
        carry_scr[...] = jnp.zeros_like(carry_scr)

    for t, u_ref in enumerate(u_refs):
        for i in range(seg):
            up_scr[t, i * SUBLANES:(i + 1) * SUBLANES, :] = u_ref[0, pl.ds(i, SUBLANES, stride=seg), :]
    u = jnp.concatenate([up_scr[t] for t in range(len(u_refs))], axis=1)
    halves = _row_halves(u.shape[0])
    for rs in halves:
        xs_scr[rs, :] = jnp.dot(u[rs].astype(BF16), bb_ref[0], preferred_element_type=F32)

    for part in range(ns // width):
        cr = slice(part * width, (part + 1) * width)
        ci = slice(ns + part * width, ns + (part + 1) * width)
        lr = jnp.broadcast_to(lre_ref[0, :, cr], (SUBLANES, width))
        li = jnp.broadcast_to(lim_ref[0, :, cr], (SUBLANES, width))

        def local_step(i, st, cr=cr, ci=ci, lr=lr, li=li):
            sr, si = st
            r0 = pl.multiple_of(i * SUBLANES, SUBLANES)
            nr = lr * sr - li * si + xs_scr[pl.ds(r0, SUBLANES), cr]
            ni = lr * si + li * sr + xs_scr[pl.ds(r0, SUBLANES), ci]
            xs_scr[pl.ds(r0, SUBLANES), cr] = nr
            xs_scr[pl.ds(r0, SUBLANES), ci] = ni
            return nr, ni

        zero = jnp.zeros((SUBLANES, width), F32)
        er, ei = lax.fori_loop(0, seg, local_step, (zero, zero), unroll=2)

        ltr = ltre_ref[0, :, cr]
        lti = ltim_ref[0, :, cr]
        c_r = carry_scr[:, cr]
        c_i = carry_scr[:, ci]
        for s in range(SUBLANES):
            cin_scr[s:s + 1, cr] = c_r
            cin_scr[s:s + 1, ci] = c_i
            e_r = er[s:s + 1, :]
            e_i = ei[s:s + 1, :]
            c_r, c_i = ltr * c_r - lti * c_i + e_r, ltr * c_i + lti * c_r + e_i
        carry_scr[:, cr] = c_r
        carry_scr[:, ci] = c_i

        def carry_step(i, st, cr=cr, ci=ci, lr=lr, li=li):
            pr, pi_ = st
            r0 = pl.multiple_of(i * SUBLANES, SUBLANES)
            nr = lr * pr - li * pi_
            ni = lr * pi_ + li * pr
            xs_scr[pl.ds(r0, SUBLANES), cr] += nr
            xs_scr[pl.ds(r0, SUBLANES), ci] += ni
            return nr, ni

        lax.fori_loop(0, seg, carry_step, (cin_scr[:, cr], cin_scr[:, ci]), unroll=2)

    for rs in halves:
        y = jnp.dot(xs_scr[rs, :].astype(BF16), cb_ref[0], preferred_element_type=F32)
        g = jax.nn.gelu(y + d_ref[0] * u[rs])
        for t in range(len(u_refs)):
            up_scr[t, rs, :] = g[:, t * LANES:(t + 1) * LANES]
    for t in range(len(u_refs)):
        for s in range(SUBLANES):
            o_ref[0, s * seg:(s + 1) * seg, t * LANES:(t + 1) * LANES] = (
                up_scr[t, pl.ds(s, seg, stride=SUBLANES), :].astype(o_ref.dtype))


def _cmul(ar, ai, br, bi):
    return ar * br - ai * bi, ar * bi + ai * br


def _s5_discretize(a_re, a_im, log_dt, b_re, b_im, c_re, c_im, seg):
    g, n = a_re.shape
    p = b_re.shape[-1]
    pg = S5_PACK_GROUPS
    packs = g // pg
    a_re, a_im = a_re.astype(F32), a_im.astype(F32)
    dt = jnp.exp(log_dt.astype(F32))[:, None]
    mag = jnp.exp(a_re * dt)
    lb_re, lb_im = mag * jnp.cos(a_im * dt), mag * jnp.sin(a_im * dt)
    den = a_re * a_re + a_im * a_im
    nr, ni = lb_re - 1.0, lb_im
    f_re = (nr * a_re + ni * a_im) / den
    f_im = (ni * a_re - nr * a_im) / den
    bb_re, bb_im = _cmul(f_re[..., None], f_im[..., None], b_re.astype(F32), b_im.astype(F32))
    lt_re, lt_im = lb_re, lb_im
    for _ in range(int(math.log2(seg))):
        lt_re, lt_im = _cmul(lt_re, lt_im, lt_re, lt_im)
    on_diag = (jnp.arange(pg * p)[:, None] // p) == (jnp.arange(pg * n)[None, :] // n)

    def blk_diag(m):
        tiled = jnp.tile(m.astype(BF16).reshape(packs, pg * p, n), (1, 1, pg))
        return jnp.where(on_diag[None], tiled, 0)

    b_blk = jnp.concatenate([blk_diag(jnp.swapaxes(bb_re, 1, 2)), blk_diag(jnp.swapaxes(bb_im, 1, 2))], axis=-1)
    c_blk = jnp.swapaxes(jnp.concatenate([blk_diag(c_re), blk_diag(-c_im)], axis=-1), 1, 2)

    def vec(m):
        return m.reshape(packs, 1, pg * n)

    return b_blk, c_blk, vec(lb_re), vec(lb_im), vec(lt_re), vec(lt_im)


def _s5_scan(u, prep, d_skip, seg):
    bsz, seq, d = u.shape
    b_blk, c_blk, lre, lim, ltre, ltim = prep
    packs, pw, ns2 = b_blk.shape
    ns = ns2 // 2
    rows = SUBLANES * seg
    width = _pick(ns, (1024, 512, 256, 128))
    lane_tiles = pw // LANES
    kern = functools.partial(_s5_kernel, seg=seg, width=width, lane_tiles=lane_tiles)
    vspec = pl.BlockSpec((1, 1, ns), lambda b, k, m: (k, 0, 0))
    u_specs = [pl.BlockSpec((1, rows, LANES), lambda b, k, m, t=t: (b, m, k * lane_tiles + t))
               for t in range(lane_tiles)]
    return pl.pallas_call(
        kern,
        out_shape=jax.ShapeDtypeStruct((bsz, seq, d), BF16),
        grid=(bsz, packs, seq // rows),
        in_specs=u_specs + [
            pl.BlockSpec((1, pw, ns2), lambda b, k, m: (k, 0, 0)),
            pl.BlockSpec((1, ns2, pw), lambda b, k, m: (k, 0, 0)),
            vspec, vspec, vspec, vspec,
            pl.BlockSpec((1, 1, pw), lambda b, k, m: (k, 0, 0)),
        ],
        out_specs=pl.BlockSpec((1, rows, pw), lambda b, k, m: (b, m, k)),
        scratch_shapes=[
            pltpu.VMEM((rows, ns2), F32),
            pltpu.VMEM((1, ns2), F32),
            pltpu.VMEM((SUBLANES, ns2), F32),
            pltpu.VMEM((lane_tiles, rows, LANES), F32),
        ],
        compiler_params=_params(("parallel", "parallel", "arbitrary")),
        name="s5_scan",
    )(*([u] * lane_tiles), b_blk, c_blk, lre, lim, ltre, ltim, d_skip.reshape(packs, 1, pw).astype(F32))


def _layer_norm_rows(tiles, lng_ref, lnb_ref, o_ref, d):
    tn = tiles[0].shape[1]
    tot = tiles[0].sum(axis=1, keepdims=True)
    for t in tiles[1:]:
        tot = tot + t.sum(axis=1, keepdims=True)
    mu = tot * (1.0 / d)
    sq = jnp.square(tiles[0] - mu).sum(axis=1, keepdims=True)
    for t in tiles[1:]:
        sq = sq + jnp.square(t - mu).sum(axis=1, keepdims=True)
    inv = lax.rsqrt(sq * (1.0 / d) + LN_EPS)
    for k, t in enumerate(tiles):
        cs = slice(k * tn, (k + 1) * tn)
        o_ref[0, :, cs] = ((t - mu) * inv * lng_ref[:, cs] + lnb_ref[:, cs]).astype(o_ref.dtype)


def _mmln_kernel(*refs, glu, nj, alpha, d):
    if glu:
        a_ref, w1_ref, w2_ref, b1_ref, b2_ref, x_ref, g_ref, lng_ref, lnb_ref, o_ref, r_scr = refs
    else:
        a_ref, w1_ref, x_ref, g_ref, lng_ref, lnb_ref, o_ref, r_scr = refs
    j = pl.program_id(2)
    for rs in _row_halves(a_ref.shape[1]):
        a = a_ref[0, rs, :]
        y = jnp.dot(a, w1_ref[...], preferred_element_type=F32)
        if glu:
            y = y + b1_ref[...]
            gate = jnp.dot(a, w2_ref[...], preferred_element_type=F32) + b2_ref[...]
            y = y * jax.nn.sigmoid(gate)
        r_scr[j, rs, :] = alpha * x_ref[0, rs, :] + (1.0 + g_ref[0]) * y

    @pl.when(j == nj - 1)
    def _():
        _layer_norm_rows([r_scr[t] for t in range(nj)], lng_ref, lnb_ref, o_ref, d)


def _matmul_res_ln(a, w, bias, xres, gate, ln_g, ln_b, alpha, glu):
    bsz, seq, k = a.shape
    d = xres.shape[-1]
    if glu:
        tm = _pick(seq, (1024, 512, 256, 128))
        tn = _pick(d, (512, 256, 128))
    else:
        tm = _pick(seq, (512, 256, 128))
        tn = d if 2 * k * d * w.dtype.itemsize <= RESIDENT_WEIGHT_BYTES else _pick(d, (512, 256, 128))
    nj = d // tn
    a_spec = pl.BlockSpec((1, tm, k), lambda b, i, j: (b, i, 0))
    w1_spec = pl.BlockSpec((k, tn), lambda b, i, j: (0, j))
    tail_specs = [
        pl.BlockSpec((1, tm, tn), lambda b, i, j: (b, i, j)),
        pl.BlockSpec((1, 1, tn), lambda b, i, j: (b, 0, j)),
        pl.BlockSpec((1, d), lambda b, i, j: (0, 0)),
        pl.BlockSpec((1, d), lambda b, i, j: (0, 0)),
    ]
    tail = (xres, gate, ln_g.reshape(1, d).astype(F32), ln_b.reshape(1, d).astype(F32))
    if glu:
        in_specs = [a_spec, w1_spec,
                    pl.BlockSpec((k, tn), lambda b, i, j: (0, j + nj)),
                    pl.BlockSpec((1, tn), lambda b, i, j: (0, j)),
                    pl.BlockSpec((1, tn), lambda b, i, j: (0, j + nj))] + tail_specs
        b2d = bias.reshape(1, 2 * d).astype(F32)
        args = (a, w, w, b2d, b2d) + tail
    else:
        in_specs = [a_spec, w1_spec] + tail_specs
        args = (a, w) + tail
    kern = functools.partial(_mmln_kernel, glu=glu, nj=nj, alpha=alpha, d=d)
    return pl.pallas_call(
        kern,
        out_shape=jax.ShapeDtypeStruct((bsz, seq, d), F32),
        grid=(bsz, seq // tm, nj),
        in_specs=in_specs,
        out_specs=pl.BlockSpec((1, tm, d), lambda b, i, j: (b, i, 0)),
        scratch_shapes=[pltpu.VMEM((nj, tm, tn), F32)],
        compiler_params=_params(("parallel", "parallel", "arbitrary")),
        name="matmul_res_ln",
    )(*args)


def _ffn_kernel(x_ref, xp_ref, sc_ref, sh_ref, g_ref, wg_ref, wu_ref, wo_ref, lng_ref, lnb_ref, o_ref,
                h_scr, acc_a, acc_b, *, n_tiles, alpha, d, tn):
    i = pl.program_id(1)
    f = pl.program_id(2)

    def products(acc_ref, first):
        h = h_scr[...]
        a_g = jnp.dot(h, wg_ref[0], preferred_element_type=F32)
        a_u = jnp.dot(h, wu_ref[0], preferred_element_type=F32)
        act = (jax.nn.silu(a_g) * a_u).astype(BF16)
        part = jnp.dot(act, wo_ref[0], preferred_element_type=F32)
        if first:
            acc_ref[...] = part
        else:
            acc_ref[...] += part

    def finish(acc_ref):
        tiles = []
        for k in range(d // tn):
            cs = slice(k * tn, (k + 1) * tn)
            tiles.append(alpha * xp_ref[0, :, cs] + (1.0 + g_ref[0, :, cs]) * acc_ref[:, cs])
        _layer_norm_rows(tiles, lng_ref, lnb_ref, o_ref, d)

    for parity, (cur, prev) in enumerate(((acc_a, acc_b), (acc_b, acc_a))):
        mine = (i % 2) == parity

        @pl.when(mine & (f == 0) & (i == 0))
        def _(cur=cur):
            h_scr[...] = _modulate(x_ref, sc_ref, sh_ref)
            products(cur, True)

        @pl.when(mine & (f == 0) & (i > 0) & (i < n_tiles))
        def _(cur=cur, prev=prev):
            h_scr[...] = _modulate(x_ref, sc_ref, sh_ref)
            products(cur, True)
            finish(prev)

        @pl.when(mine & (f == 0) & (i == n_tiles))
        def _(prev=prev):
            finish(prev)

        @pl.when(mine & (f > 0) & (i < n_tiles))
        def _(cur=cur):
            products(cur, False)


def _ffn_res_ln(x, sc, sh, gate, w_in, w_out, layer, ln_g, ln_b, alpha):
    bsz, seq, d = x.shape
    dff = w_out.shape[1]
    tm = _pick(seq, (512, 256, 128))
    tf = _pick(dff, (512, 256, 128))
    nf = dff // tf
    n_tiles = seq // tm
    tn = _pick(d, (512, 256, 128))
    vec = pl.BlockSpec((1, 1, d), lambda b, i, f: (b, 0, 0))

    def col(i, f):
        return jnp.where(i == n_tiles, nf - 1, f)

    kern = functools.partial(_ffn_kernel, n_tiles=n_tiles, alpha=alpha, d=d, tn=tn)
    return pl.pallas_call(
        kern,
        out_shape=jax.ShapeDtypeStruct((bsz, seq, d), F32),
        grid=(bsz, n_tiles + 1, nf),
        in_specs=[
            pl.BlockSpec((1, tm, d), lambda b, i, f: (b, jnp.minimum(i, n_tiles - 1), 0)),
            pl.BlockSpec((1, tm, d), lambda b, i, f: (b, jnp.maximum(i - 1, 0), 0)),
            vec, vec, vec,
            pl.BlockSpec((1, d, tf), lambda b, i, f: (layer, 0, col(i, f))),
            pl.BlockSpec((1, d, tf), lambda b, i, f: (layer, 0, col(i, f) + nf)),
            pl.BlockSpec((1, tf, d), lambda b, i, f: (layer, col(i, f), 0)),
            pl.BlockSpec((1, d), lambda b, i, f: (0, 0)),
            pl.BlockSpec((1, d), lambda b, i, f: (0, 0)),
        ],
        out_specs=pl.BlockSpec((1, tm, d), lambda b, i, f: (b, jnp.maximum(i - 1, 0), 0)),
        scratch_shapes=[pltpu.VMEM((tm, d), BF16), pltpu.VMEM((tm, d), F32), pltpu.VMEM((tm, d), F32)],
        compiler_params=_params(("parallel", "arbitrary", "arbitrary")),
        name="ffn_res_ln",
    )(x, x, sc, sh, gate, w_in, w_in, w_out, ln_g.reshape(1, d).astype(F32), ln_b.reshape(1, d).astype(F32))


def _qkv_kernel(x_ref, sc_ref, sh_ref, w_ref, cos_ref, sin_ref, o_ref, h_scr, *, tiles_per_tensor, q_scale):
    j = pl.program_id(2)

    @pl.when(j == 0)
    def _():
        h_scr[...] = _modulate(x_ref, sc_ref, sh_ref)

    tensor = j // tiles_per_tensor
    scale = jnp.where(tensor == 0, q_scale, 1.0)
    is_v = tensor == 2
    chunk = min(MXU_WIDTH, w_ref.shape[1])
    for rs in _row_halves(h_scr.shape[0]):
        a = jnp.where(is_v, 1.0, cos_ref[0, rs, :] * scale)
        b = jnp.where(is_v, 0.0, sin_ref[0, rs, :] * scale)
        for c in range(w_ref.shape[1] // chunk):
            acc = jnp.dot(h_scr[rs, :], w_ref[:, c * chunk:(c + 1) * chunk], preferred_element_type=F32)
            for hh in range(chunk // HEAD_DIM):
                xc = acc[:, hh * HEAD_DIM:(hh + 1) * HEAD_DIM]
                xc = xc * a + pltpu.roll(xc, HEAD_DIM // 2, 1) * b
                o_ref[0, c * (chunk // HEAD_DIM) + hh, rs, :] = xc.astype(o_ref.dtype)


def _qkv_proj(x, sc, sh, w_qkv, cos_t, sin_t):
    bsz, seq, d = x.shape
    tm = _pick(seq, (1024, 512, 256, 128))
    tn = _pick(d, (1024, 512, 256, 128))
    hpt = tn // HEAD_DIM
    kern = functools.partial(_qkv_kernel, tiles_per_tensor=d // tn, q_scale=HEAD_DIM ** -0.5 * math.log2(math.e))
    return pl.pallas_call(
        kern,
        out_shape=jax.ShapeDtypeStruct((bsz, 3 * d // HEAD_DIM, seq, HEAD_DIM), BF16),
        grid=(bsz, seq // tm, 3 * d // tn),
        in_specs=[
            pl.BlockSpec((1, tm, d), lambda b, i, j: (b, i, 0)),
            pl.BlockSpec((1, 1, d), lambda b, i, j: (b, 0, 0)),
            pl.BlockSpec((1, 1, d), lambda b, i, j: (b, 0, 0)),
            pl.BlockSpec((d, tn), lambda b, i, j: (0, j)),
            pl.BlockSpec((1, tm, HEAD_DIM), lambda b, i, j: (b, i, 0)),
            pl.BlockSpec((1, tm, HEAD_DIM), lambda b, i, j: (b, i, 0)),
        ],
        out_specs=pl.BlockSpec((1, hpt, tm, HEAD_DIM), lambda b, i, j: (b, j, i, 0)),
        scratch_shapes=[pltpu.VMEM((tm, d), BF16)],
        compiler_params=_params(("parallel", "parallel", "arbitrary")),
        name="dsa_qkv_proj",
    )(x, sc, sh, w_qkv, cos_t, sin_t)


def _idx_kernel(x_ref, sc_ref, sh_ref, w_ref, cos_ref, sin_ref, qi_ref, kia_ref, kib_ref, wi_ref,
                *, n_pair, idx_heads, w_scale):
    h = _modulate(x_ref, sc_ref, sh_ref)
    halves = _row_halves(h.shape[0])
    accs = [jnp.dot(h[rs], w_ref[...], preferred_element_type=F32) for rs in halves]
    lane = lax.broadcasted_iota(I32, (halves[0].stop, LANES), 1)
    first_half = (lane % IDX_DIM) < IDX_DIM // 2

    for rs, acc in zip(halves, accs):
        cos = cos_ref[0, rs, :]
        sin = sin_ref[0, rs, :]

        def rope(xc, cos=cos, sin=sin):
            partner = jnp.where(first_half, pltpu.roll(xc, LANES - IDX_DIM // 2, 1),
                                pltpu.roll(xc, IDX_DIM // 2, 1))
            return xc * cos + partner * sin

        for p in range(n_pair):
            cs = slice(p * LANES, (p + 1) * LANES)
            qi_ref[0, rs, cs] = rope(acc[:, cs]).astype(qi_ref.dtype)
        last = acc[:, n_pair * LANES:]
        ka = jnp.where(lane < IDX_DIM, rope(last), 0.0)
        kia_ref[0, rs, :] = ka.astype(kia_ref.dtype)
        kib_ref[0, rs, :] = pltpu.roll(ka, IDX_DIM, 1).astype(kib_ref.dtype)
        wi_ref[0, rs, :] = jnp.where(lane < idx_heads, pltpu.roll(last, LANES - IDX_DIM, 1), 0.0) * w_scale


def _idx_proj(x, sc, sh, w_idx, cos_t, sin_t, idx_heads):
    bsz, seq, d = x.shape
    n = w_idx.shape[1]
    n_pair = idx_heads // 2
    tm = _pick(seq, (512, 256, 128))
    kern = functools.partial(_idx_kernel, n_pair=n_pair, idx_heads=idx_heads,
                             w_scale=(idx_heads ** -0.5) * (IDX_DIM ** -0.5))
    row = lambda b, i: (b, i, 0)
    return pl.pallas_call(
        kern,
        out_shape=(
            jax.ShapeDtypeStruct((bsz, seq, n_pair * LANES), BF16),
            jax.ShapeDtypeStruct((bsz, seq, LANES), BF16),
            jax.ShapeDtypeStruct((bsz, seq, LANES), BF16),
            jax.ShapeDtypeStruct((bsz, seq, LANES), F32),
        ),
        grid=(bsz, seq // tm),
        in_specs=[
            pl.BlockSpec((1, tm, d), row),
            pl.BlockSpec((1, 1, d), lambda b, i: (b, 0, 0)),
            pl.BlockSpec((1, 1, d), lambda b, i: (b, 0, 0)),
            pl.BlockSpec((d, n), lambda b, i: (0, 0)),
            pl.BlockSpec((1, tm, LANES), row),
            pl.BlockSpec((1, tm, LANES), row),
        ],
        out_specs=(
            pl.BlockSpec((1, tm, n_pair * LANES), row),
            pl.BlockSpec((1, tm, LANES), row),
            pl.BlockSpec((1, tm, LANES), row),
            pl.BlockSpec((1, tm, LANES), row),
        ),
        compiler_params=_params(("parallel", "parallel")),
        name="dsa_idx_proj",
    )(x, sc, sh, w_idx, cos_t, sin_t)


def _bit_transpose32(words):
    a = list(words)
    j = WORD_BITS // 2
    mask = 0x0000FFFF
    while j:
        k = 0
        while k < WORD_BITS:
            t = (a[k] ^ lax.shift_right_logical(a[k + j], jnp.int32(j))) & jnp.int32(mask)
            a[k] = a[k] ^ t
            a[k + j] = a[k + j] ^ (t << j)
            k = (k + j + 1) & ~j
        j >>= 1
        if j:
            mask = (mask ^ (mask << j)) & 0xFFFFFFFF
    return a


def _dsa_kernel(qb_tab, kb_tab, q_ref, k_ref, v_ref, qi_ref, kia_ref, kib_ref, wi_ref, o_ref,
                keys_scr, planes_scr, thr_scr, tie_scr, ngt_scr, nge_scr,
                m_scr, l_scr, acc_scr, bias_scr, s0_scr, s1_scr, r0_scr, r1_scr,
                *, n_pair, k_top, idx_bits):
    heads, qb_rows, _ = acc_scr.shape
    kb_rows = keys_scr.shape[2]
    qb = qb_tab[pl.program_id(1)]
    kb = kb_tab[pl.program_id(1)]
    q0 = qb * qb_rows
    last_kb = (q0 + qb_rows - 1) // kb_rows
    n_chunk = last_kb + 1
    row = lax.broadcasted_iota(I32, (qb_rows, kb_rows), 0) + q0
    col = lax.broadcasted_iota(I32, (qb_rows, kb_rows), 1)
    nt = (((1,), (1,)), ((), ()))

    @pl.when(kb == 0)
    def _select():
        wi = wi_ref[0]

        def score_chunk(c, carry):
            k0 = pl.multiple_of(c * kb_rows, kb_rows)
            ka = kia_ref[0, pl.ds(k0, kb_rows), :]
            kbm = kib_ref[0, pl.ds(k0, kb_rows), :]
            sc = jnp.zeros((qb_rows, kb_rows), F32)
            for p in range(n_pair):
                qp = qi_ref[0, :, p * LANES:(p + 1) * LANES]
                sa = lax.dot_general(qp, ka, nt, preferred_element_type=F32)
                sb = lax.dot_general(qp, kbm, nt, preferred_element_type=F32)
                sc = sc + wi[:, 2 * p:2 * p + 1] * jnp.maximum(sa, 0.0)
                sc = sc + wi[:, 2 * p + 1:2 * p + 2] * jnp.maximum(sb, 0.0)
            bits = lax.bitcast_convert_type(sc, I32)
            key = bits ^ ((bits >> 31) & INT_MAX)
            keys_scr[c] = jnp.where(col + k0 <= row, key, INT_MIN)
            return carry

        lax.fori_loop(0, n_chunk, score_chunk, 0)

        slices_per_chunk = kb_rows // LANES
        chunks_per_set = WORD_BITS // slices_per_chunk
        n_slices = keys_scr.shape[0] * slices_per_chunk
        n_sets = planes_scr.shape[2] // LANES
        live_sets = (n_chunk + chunks_per_set - 1) // chunks_per_set

        def fill_chunk(c, carry):
            keys_scr[c] = jnp.full((qb_rows, kb_rows), INT_MIN, I32)
            return carry

        lax.fori_loop(n_chunk, jnp.minimum(live_sets * chunks_per_set, keys_scr.shape[0]), fill_chunk, 0)

        def pack_rows(g, carry, st):
            r0 = pl.multiple_of(g * SUBLANES, SUBLANES)
            words = []
            for s in range(WORD_BITS):
                sl = st * WORD_BITS + s
                if sl < n_slices:
                    c, off = divmod(sl, slices_per_chunk)
                    words.append(keys_scr[c, pl.ds(r0, SUBLANES), off * LANES:(off + 1) * LANES])
                else:
                    words.append(jnp.full((SUBLANES, LANES), INT_MIN, I32))
            words = _bit_transpose32(words)
            words[0] = ~words[0]
            for i in range(WORD_BITS):
                planes_scr[i, pl.ds(r0, SUBLANES), st * LANES:(st + 1) * LANES] = words[i]
            return carry

        for st in range(n_sets):
            @pl.when(st < live_sets)
            def _(st=st):
                lax.fori_loop(0, qb_rows // SUBLANES, functools.partial(pack_rows, st=st), 0)

            @pl.when(st >= live_sets)
            def _(st=st):
                planes_scr[:, :, st * LANES:(st + 1) * LANES] = jnp.zeros((WORD_BITS, qb_rows, LANES), I32)

        lane_ones = jnp.ones((LANES, LANES), BF16)

        def row_total(pc):
            tot = pc[:, :LANES]
            for st in range(1, n_sets):
                tot = tot + pc[:, st * LANES:(st + 1) * LANES]
            return jnp.dot(tot.astype(F32).astype(BF16), lane_ones, preferred_element_type=F32)

        def all_sets(mask):
            return jnp.concatenate([mask] * n_sets, axis=1)

        def select_two_bits(i, carry):
            cand, above, prefix = carry
            hi = planes_scr[2 * i]
            lo = planes_scr[2 * i + 1]
            c1 = cand & hi
            c0 = cand ^ c1
            c11 = c1 & lo
            c10 = c1 ^ c11
            c01 = c0 & lo
            c00 = c0 ^ c01
            r11 = above + row_total(lax.population_count(c11))
            r10 = r11 + row_total(lax.population_count(c10))
            r01 = r10 + row_total(lax.population_count(c01))
            t11 = r11 >= k_top
            t10 = r10 >= k_top
            t01 = r01 >= k_top
            cand = jnp.where(all_sets(t11), c11,
                             jnp.where(all_sets(t10), c10, jnp.where(all_sets(t01), c01, c00)))
            above = jnp.where(t11, above, jnp.where(t10, r11, jnp.where(t01, r10, r01)))
            bits = jnp.where(t11, 3, jnp.where(t10, 2, jnp.where(t01, 1, 0)))
            prefix = prefix | (bits << (WORD_BITS - 2 - 2 * i))
            return cand, above, prefix

        cand, above, prefix = lax.fori_loop(
            0, WORD_BITS // 2, select_two_bits,
            (jnp.concatenate([jnp.full((qb_rows, LANES), jnp.where(st < live_sets, -1, 0), I32)
                              for st in range(n_sets)], axis=1),
             jnp.zeros((qb_rows, LANES), F32), jnp.zeros((qb_rows, LANES), I32)))
        equal = row_total(lax.population_count(cand))
        thr_scr[...] = (prefix ^ INT_MIN)[:, :1]
        tie_scr[...] = jnp.full((qb_rows, 1), INT_MAX, I32)
        ngt_scr[...] = above[:, :1]
        nge_scr[...] = (above + equal)[:, :1]

        rg = min(qb_rows, SEARCH_ROWS)
        lane_col = lax.broadcasted_iota(I32, (rg, LANES), 1)

        for r in range(qb_rows // rg):
            rs = slice(r * rg, (r + 1) * rg)

            def count(pred, rs=rs):
                def body(c, acc):
                    for t in range(kb_rows // LANES):
                        kk = keys_scr[c, rs, t * LANES:(t + 1) * LANES]
                        idx = lane_col + (c * kb_rows + t * LANES)
                        acc = acc + jnp.where(pred(kk, idx), 1.0, 0.0)
                    return acc
                acc = lax.fori_loop(0, n_chunk, body, jnp.zeros((rg, LANES), F32))
                return jnp.sum(acc, axis=1, keepdims=True)

            def wide(v):
                return jnp.broadcast_to(v, (rg, LANES))

            @pl.when(jnp.max(nge_scr[rs, :]) > k_top)
            def _ties(count=count, rs=rs):
                thr_w = wide(thr_scr[rs, :])
                need = k_top - ngt_scr[rs, :]

                def tie_bit(i, jt):
                    cand = jt + (jnp.int32(1) << (idx_bits - 1 - i))
                    cand_w = wide(cand)
                    cnt = count(lambda kk, idx: (kk == thr_w) & (idx < cand_w))
                    return jnp.where(cnt < need, cand, jt)

                tie_scr[rs, :] = lax.fori_loop(0, idx_bits, tie_bit, jnp.zeros((rg, 1), I32))

        m_scr[...] = jnp.full(m_scr.shape, M_INIT, F32)
        l_scr[...] = jnp.zeros_like(l_scr)
        acc_scr[...] = jnp.zeros_like(acc_scr)

    def _attend():
        ones = jnp.ones((kb_rows, HEAD_DIM), BF16)

        kk = keys_scr[kb]
        thr = thr_scr[...]
        lim = jnp.minimum(tie_scr[...], row[:, :1])
        tie_bias = jnp.where(col + kb * kb_rows <= lim, 0.0, MASK_VALUE)
        bias_scr[...] = jnp.where(kk > thr, 0.0, jnp.where(kk == thr, tie_bias, MASK_VALUE))

        def logits(h, s_ref, r_ref):
            s = lax.dot_general(q_ref[0, h], k_ref[0, h], nt, preferred_element_type=F32) + bias_scr[...]
            s_ref[...] = s
            r_ref[...] = jnp.broadcast_to(jnp.max(s, axis=1, keepdims=True), r_ref.shape)

        def accumulate(h, s_ref, r_ref):
            m_old = m_scr[h]
            m_new = jnp.maximum(m_old, r_ref[...])
            alpha = jnp.exp2(m_old - m_new)
            p = jnp.concatenate(
                [jnp.exp2(s_ref[:, t * LANES:(t + 1) * LANES] - m_new).astype(BF16)
                 for t in range(kb_rows // LANES)], axis=1)
            v_ext = jnp.concatenate([v_ref[0, h], ones], axis=1)
            pv = jnp.dot(p, v_ext, preferred_element_type=F32)
            acc_scr[h] = alpha * acc_scr[h] + pv[:, :HEAD_DIM]
            l_scr[h] = alpha * l_scr[h] + pv[:, HEAD_DIM:]
            m_scr[h] = m_new

        bufs = ((s0_scr, r0_scr), (s1_scr, r1_scr))
        logits(0, *bufs[0])
        for h in range(heads):
            if h + 1 < heads:
                logits(h + 1, *bufs[(h + 1) % 2])
            accumulate(h, *bufs[h % 2])

    _attend()

    @pl.when(kb == last_kb)
    def _finish():
        for h in range(heads):
            o_ref[0, :, h * HEAD_DIM:(h + 1) * HEAD_DIM] = (acc_scr[h] / l_scr[h]).astype(o_ref.dtype)


def _dsa_attention(qkv, qi, kia, kib, wi, k_top):
    bsz, heads3, seq, _ = qkv.shape
    heads = heads3 // 3
    d = heads * HEAD_DIM
    n_pair = qi.shape[-1] // LANES
    qb_rows = _pick(seq, (256, 128))
    kb_rows = _pick(seq, (512, 256, 128))
    n_kb = seq // kb_rows
    n_sets = -(-seq // (WORD_BITS * LANES))

    pairs = [(i, j) for i in range(seq // qb_rows) for j in range((i * qb_rows + qb_rows - 1) // kb_rows + 1)]
    qb_tab = jnp.asarray([p[0] for p in pairs], I32)
    kb_tab = jnp.asarray([p[1] for p in pairs], I32)

    kern = functools.partial(_dsa_kernel, n_pair=n_pair, k_top=k_top, idx_bits=seq.bit_length())
    grid_spec = pltpu.PrefetchScalarGridSpec(
        num_scalar_prefetch=2,
        grid=(bsz, len(pairs)),
        in_specs=[
            pl.BlockSpec((1, heads, qb_rows, HEAD_DIM), lambda b, s, qt, kt: (b, 0, qt[s], 0)),
            pl.BlockSpec((1, heads, kb_rows, HEAD_DIM), lambda b, s, qt, kt: (b, 1, kt[s], 0)),
            pl.BlockSpec((1, heads, kb_rows, HEAD_DIM), lambda b, s, qt, kt: (b, 2, kt[s], 0)),
            pl.BlockSpec((1, qb_rows, n_pair * LANES), lambda b, s, qt, kt: (b, qt[s], 0)),
            pl.BlockSpec((1, seq, LANES), lambda b, s, qt, kt: (b, 0, 0)),
            pl.BlockSpec((1, seq, LANES), lambda b, s, qt, kt: (b, 0, 0)),
            pl.BlockSpec((1, qb_rows, LANES), lambda b, s, qt, kt: (b, qt[s], 0)),
        ],
        out_specs=pl.BlockSpec((1, qb_rows, d), lambda b, s, qt, kt: (b, qt[s], 0)),
        scratch_shapes=[
            pltpu.VMEM((n_kb, qb_rows, kb_rows), I32),
            pltpu.VMEM((WORD_BITS, qb_rows, n_sets * LANES), I32),
            pltpu.VMEM((qb_rows, 1), I32),
            pltpu.VMEM((qb_rows, 1), I32),
            pltpu.VMEM((qb_rows, 1), F32),
            pltpu.VMEM((qb_rows, 1), F32),
            pltpu.VMEM((heads, qb_rows, LANES), F32),
            pltpu.VMEM((heads, qb_rows, LANES), F32),
            pltpu.VMEM((heads, qb_rows, HEAD_DIM), F32),
            pltpu.VMEM((qb_rows, kb_rows), F32),
            pltpu.VMEM((qb_rows, kb_rows), F32),
            pltpu.VMEM((qb_rows, kb_rows), F32),
            pltpu.VMEM((qb_rows, LANES), F32),
            pltpu.VMEM((qb_rows, LANES), F32),
        ],
    )
    return pl.pallas_call(
        kern,
        out_shape=jax.ShapeDtypeStruct((bsz, seq, d), BF16),
        grid_spec=grid_spec,
        compiler_params=_params(("parallel", "arbitrary")),
        name="dsa_select_attend",
    )(qb_tab, kb_tab, qkv, qkv, qkv, qi, kia, kib, wi)


def _rope_tables(positions, dim):
    inv = 1.0 / (ROPE_THETA ** (jnp.arange(0, dim, 2, dtype=F32) / dim))
    half = dim // 2
    lane = jnp.arange(LANES)
    inv_t = inv[lane % half]
    sign = jnp.where((lane % dim) < half, -1.0, 1.0).astype(F32)
    ang = positions.astype(F32)[..., None] * inv_t
    return jnp.cos(ang), jnp.sin(ang) * sign


def kernel(x, c, positions, ada_w, ada_b, ln_g, ln_b, s5_in_w, s5_a_re, s5_a_im, s5_log_dt, s5_b_re, s5_b_im, s5_c_re, s5_c_im, s5_d, s5_glu_w, s5_glu_b, dsa_in_w, dsa_out_w, ffn_w_in, ffn_w_out):
    bsz, seq, d = x.shape
    depth = ada_w.shape[0]
    alpha = (2.0 * depth) ** 0.25
    idx_heads = (dsa_in_w.shape[-1] - 3 * d - IDX_DIM) // (IDX_DIM + 1)
    k_top = min(TOPK_MAX, seq // 4)
    seg = min(S5_SEG, seq // SUBLANES)

    cos_h, sin_h = _rope_tables(positions, HEAD_DIM)
    cos_i, sin_i = _rope_tables(positions, IDX_DIM)
    mod = _ada_mod(c, ada_w, ada_b)
    ffn_in = ffn_w_in.astype(BF16)
    ffn_out = ffn_w_out.astype(BF16)

    for i in range(depth):
        sh1, sc1, g1, sh2, sc2, g2 = [m[:, None, :] for m in jnp.split(mod[i], 6, axis=-1)]
        j = i // 2
        if i % 2 == 0:
            prep = _s5_discretize(s5_a_re[j], s5_a_im[j], s5_log_dt[j], s5_b_re[j], s5_b_im[j],
                                  s5_c_re[j], s5_c_im[j], seg)
            u = _mod_matmul(x, sc1, sh1, s5_in_w[j].astype(BF16), F32)
            gl = _s5_scan(u, prep, s5_d[j], seg)
            x1 = _matmul_res_ln(gl, s5_glu_w[j].astype(BF16), s5_glu_b[j], x, g1,
                                ln_g[i, 0], ln_b[i, 0], alpha, glu=True)
        else:
            w = dsa_in_w[j].astype(BF16)
            n_qi = idx_heads * IDX_DIM
            pad = jnp.zeros((d, LANES - IDX_DIM - idx_heads), w.dtype)
            w_idx = jnp.concatenate(
                [w[:, 3 * d:3 * d + n_qi], w[:, 3 * d + n_qi + idx_heads:], w[:, 3 * d + n_qi:3 * d + n_qi + idx_heads], pad],
                axis=1)
            qkv = _qkv_proj(x, sc1, sh1, w, cos_h, sin_h)
            qi, kia, kib, wi = _idx_proj(x, sc1, sh1, w_idx, cos_i, sin_i, idx_heads)
            att = _dsa_attention(qkv, qi, kia, kib, wi, k_top)
            x1 = _matmul_res_ln(att, dsa_out_w[j].astype(BF16), None, x, g1,
                                ln_g[i, 0], ln_b[i, 0], alpha, glu=False)
        x = _ffn_res_ln(x1, sc2, sh2, g2, ffn_in, ffn_out, i, ln_g[i, 1], ln_b[i, 1], alpha)
    return x
```

```python
import functools
import math

import jax
import jax.numpy as jnp
from jax import lax
from jax.experimental import pallas as pl
from jax.experimental.pallas import tpu as pltpu

F32 = jnp.float32
BF16 = jnp.bfloat16
I32 = jnp.int32

S5_GROUP = 16
S5_STATE = 64
HEAD_DIM = 128
IDX_DIM = 64
TOPK_MAX = 256
ROPE_THETA = 10000.0
LN_EPS = 1e-5

LANES = 128
SUBLANES = 8
MXU_WIDTH = 256
VMEM_LIMIT_BYTES = 56 * 1024 * 1024
RESIDENT_WEIGHT_BYTES = 16 * 1024 * 1024

S5_PACK_GROUPS = 16
S5_SEG = 128

INT_MIN = -(2 ** 31)
INT_MAX = 2 ** 31 - 1
SEARCH_ROWS = 128
WORD_BITS = 32

MASK_VALUE = -2e30
M_INIT = -1e30


def _pick(n, cands):
    for c in cands:
        if n % c == 0:
            return c
    return n


def _row_halves(rows):
    half = rows // 2
    return (slice(0, half), slice(half, rows))


def _params(sem):
    return pltpu.CompilerParams(dimension_semantics=sem, vmem_limit_bytes=VMEM_LIMIT_BYTES)


def _ada_kernel(c_ref, w_ref, b_ref, o_ref):
    ca = jax.nn.silu(c_ref[...]).astype(BF16)
    o_ref[0] = jnp.dot(ca, w_ref[0].astype(BF16), preferred_element_type=F32) + b_ref[0]


def _ada_mod(c, ada_w, ada_b):
    bsz, d = c.shape
    depth, _, n = ada_w.shape
    rows = SUBLANES * ((bsz + SUBLANES - 1) // SUBLANES)
    cp = jnp.zeros((rows, d), F32).at[:bsz].set(c)
    tn = _pick(n, (1024, 512, 256, 128))
    out = pl.pallas_call(
        _ada_kernel,
        out_shape=jax.ShapeDtypeStruct((depth, rows, n), F32),
        grid=(depth, n // tn),
        in_specs=[
            pl.BlockSpec((rows, d), lambda l, j: (0, 0)),
            pl.BlockSpec((1, d, tn), lambda l, j: (l, 0, j)),
            pl.BlockSpec((1, 1, tn), lambda l, j: (l, 0, j)),
        ],
        out_specs=pl.BlockSpec((1, rows, tn), lambda l, j: (l, 0, j)),
        compiler_params=_params(("arbitrary", "arbitrary")),
        name="ada_mod",
    )(cp, ada_w, ada_b.reshape(depth, 1, n))
    return out[:, :bsz]


def _modulate(x_ref, sc_ref, sh_ref):
    return (x_ref[0] * (1.0 + sc_ref[0]) + sh_ref[0]).astype(BF16)


def _modmm_kernel(x_ref, sc_ref, sh_ref, w_ref, o_ref, h_scr):
    @pl.when(pl.program_id(2) == 0)
    def _():
        h_scr[...] = _modulate(x_ref, sc_ref, sh_ref)

    for rs in _row_halves(h_scr.shape[0]):
        o_ref[0, rs, :] = jnp.dot(h_scr[rs, :], w_ref[...], preferred_element_type=F32).astype(o_ref.dtype)


def _mod_matmul(x, sc, sh, w, out_dtype):
    bsz, seq, d = x.shape
    n = w.shape[1]
    tm = _pick(seq, (512, 256, 128))
    tn = n if 2 * d * n * w.dtype.itemsize <= RESIDENT_WEIGHT_BYTES else _pick(n, (1024, 512, 256, 128))
    return pl.pallas_call(
        _modmm_kernel,
        out_shape=jax.ShapeDtypeStruct((bsz, seq, n), out_dtype),
        grid=(bsz, seq // tm, n // tn),
        in_specs=[
            pl.BlockSpec((1, tm, d), lambda b, i, j: (b, i, 0)),
            pl.BlockSpec((1, 1, d), lambda b, i, j: (b, 0, 0)),
            pl.BlockSpec((1, 1, d), lambda b, i, j: (b, 0, 0)),
            pl.BlockSpec((d, tn), lambda b, i, j: (0, j)),
        ],
        out_specs=pl.BlockSpec((1, tm, tn), lambda b, i, j: (b, i, j)),
        scratch_shapes=[pltpu.VMEM((tm, d), BF16)],
        compiler_params=_params(("parallel", "parallel", "arbitrary")),
        name="mod_matmul",
    )(x, sc, sh, w)


def _s5_kernel(*refs, seg, width, lane_tiles):
    u_refs = refs[:lane_tiles]
    (bb_ref, cb_ref, lre_ref, lim_ref, ltre_ref, ltim_ref, d_ref, o_ref,
     xs_scr, carry_scr, cin_scr, up_scr) = refs[lane_tiles:]
    ns = xs_scr.shape[1] // 2

    @pl.when(pl.program_id(2) == 0)
    def _():
        carry_scr[...] = jnp.zeros_like(carry_scr)

    for t, u_ref in enumerate(u_refs):
        for i in range(seg):
            up_scr[t, i * SUBLANES:(i + 1) * SUBLANES, :] = u_ref[0, pl.ds(i, SUBLANES, stride=seg), :]
    u = jnp.concatenate([up_scr[t] for t in range(len(u_refs))], axis=1)
    halves = _row_halves(u.shape[0])
    for rs in halves:
        xs_scr[rs, :] = jnp.dot(u[rs].astype(BF16), bb_ref[0], preferred_element_type=F32)

    for part in range(ns // width):
        cr = slice(part * width, (part + 1) * width)
        ci = slice(ns + part * width, ns + (part + 1) * width)
        lr = jnp.broadcast_to(lre_ref[0, :, cr], (SUBLANES, width))
        li = jnp.broadcast_to(lim_ref[0, :, cr], (SUBLANES, width))

        def local_step(i, st, cr=cr, ci=ci, lr=lr, li=li):
            sr, si = st
            r0 = pl.multiple_of(i * SUBLANES, SUBLANES)
            nr = lr * sr - li * si + xs_scr[pl.ds(r0, SUBLANES), cr]
            ni = lr * si + li * sr + xs_scr[pl.ds(r0, SUBLANES), ci]
            xs_scr[pl.ds(r0, SUBLANES), cr] = nr
            xs_scr[pl.ds(r0, SUBLANES), ci] = ni
            return nr, ni

        zero = jnp.zeros((SUBLANES, width), F32)
        er, ei = lax.fori_loop(0, seg, local_step, (zero, zero), unroll=2)

        ltr = ltre_ref[0, :, cr]
        lti = ltim_ref[0, :, cr]
        c_r = carry_scr[:, cr]
        c_i = carry_scr[:, ci]
        for s in range(SUBLANES):
            cin_scr[s:s + 1, cr] = c_r
            cin_scr[s:s + 1, ci] = c_i
            e_r = er[s:s + 1, :]
            e_i = ei[s:s + 1, :]
            c_r, c_i = ltr * c_r - lti * c_i + e_r, ltr * c_i + lti * c_r + e_i
        carry_scr[:, cr] = c_r
        carry_scr[:, ci] = c_i

        def carry_step(i, st, cr=cr, ci=ci, lr=lr, li=li):
            pr, pi_ = st
            r0 = pl.multiple_of(i * SUBLANES, SUBLANES)
            nr = lr * pr - li * pi_
            ni = lr * pi_ + li * pr
            xs_scr[pl.ds(r0, SUBLANES), cr] += nr
            xs_scr[pl.ds(r0, SUBLANES), ci] += ni
            return nr, ni

        lax.fori_loop(0, seg, carry_step, (cin_scr[:, cr], cin_scr[:, ci]), unroll=2)

    for rs in halves:
        y = jnp.dot(xs_scr[rs, :].astype(BF16), cb_ref[0], preferred_element_type=F32)
        g = jax.nn.gelu(y + d_ref[0] * u[rs])
        for t in range(len(u_refs)):
            up_scr[t, rs, :] = g[:, t * LANES:(t + 1) * LANES]
    for t in range(len(u_refs)):
        for s in range(SUBLANES):
            o_ref[0, s * seg:(s + 1) * seg, t * LANES:(t + 1) * LANES] = (
                up_scr[t, pl.ds(s, seg, stride=SUBLANES), :].astype(o_ref.dtype))


def _cmul(ar, ai, br, bi):
    return ar * br - ai * bi, ar * bi + ai * br


def _s5_discretize(a_re, a_im, log_dt, b_re, b_im, c_re, c_im, seg):
    g, n = a_re.shape
    p = b_re.shape[-1]
    pg = S5_PACK_GROUPS
    packs = g // pg
    a_re, a_im = a_re.astype(F32), a_im.astype(F32)
    dt = jnp.exp(log_dt.astype(F32))[:, None]
    mag = jnp.exp(a_re * dt)
    lb_re, lb_im = mag * jnp.cos(a_im * dt), mag * jnp.sin(a_im * dt)
    den = a_re * a_re + a_im * a_im
    nr, ni = lb_re - 1.0, lb_im
    f_re = (nr * a_re + ni * a_im) / den
    f_im = (ni * a_re - nr * a_im) / den
    bb_re, bb_im = _cmul(f_re[..., None], f_im[..., None], b_re.astype(F32), b_im.astype(F32))
    lt_re, lt_im = lb_re, lb_im
    for _ in range(int(math.log2(seg))):
        lt_re, lt_im = _cmul(lt_re, lt_im, lt_re, lt_im)
    on_diag = (jnp.arange(pg * p)[:, None] // p) == (jnp.arange(pg * n)[None, :] // n)

    def blk_diag(m):
        tiled = jnp.tile(m.astype(BF16).reshape(packs, pg * p, n), (1, 1, pg))
        return jnp.where(on_diag[None], tiled, 0)

    b_blk = jnp.concatenate([blk_diag(jnp.swapaxes(bb_re, 1, 2)), blk_diag(jnp.swapaxes(bb_im, 1, 2))], axis=-1)
    c_blk = jnp.swapaxes(jnp.concatenate([blk_diag(c_re), blk_diag(-c_im)], axis=-1), 1, 2)

    def vec(m):
        return m.reshape(packs, 1, pg * n)

    return b_blk, c_blk, vec(lb_re), vec(lb_im), vec(lt_re), vec(lt_im)


def _s5_scan(u, prep, d_skip, seg):
    bsz, seq, d = u.shape
    b_blk, c_blk, lre, lim, ltre, ltim = prep
    packs, pw, ns2 = b_blk.shape
    ns = ns2 // 2
    rows = SUBLANES * seg
    width = _pick(ns, (1024, 512, 256, 128))
    lane_tiles = pw // LANES
    kern = functools.partial(_s5_kernel, seg=seg, width=width, lane_tiles=lane_tiles)
    vspec = pl.BlockSpec((1, 1, ns), lambda b, k, m: (k, 0, 0))
    u_specs = [pl.BlockSpec((1, rows, LANES), lambda b, k, m, t=t: (b, m, k * lane_tiles + t))
               for t in range(lane_tiles)]
    return pl.pallas_call(
        kern,
        out_shape=jax.ShapeDtypeStruct((bsz, seq, d), BF16),
        grid=(bsz, packs, seq // rows),
        in_specs=u_specs + [
            pl.BlockSpec((1, pw, ns2), lambda b, k, m: (k, 0, 0)),
            pl.BlockSpec((1, ns2, pw), lambda b, k, m: (k, 0, 0)),
            vspec, vspec, vspec, vspec,
            pl.BlockSpec((1, 1, pw), lambda b, k, m: (k, 0, 0)),
        ],
        out_specs=pl.BlockSpec((1, rows, pw), lambda b, k, m: (b, m, k)),
        scratch_shapes=[
            pltpu.VMEM((rows, ns2), F32),
            pltpu.VMEM((1, ns2), F32),
            pltpu.VMEM((SUBLANES, ns2), F32),
            pltpu.VMEM((lane_tiles, rows, LANES), F32),
        ],
        compiler_params=_params(("parallel", "parallel", "arbitrary")),
        name="s5_scan",
    )(*([u] * lane_tiles), b_blk, c_blk, lre, lim, ltre, ltim, d_skip.reshape(packs, 1, pw).astype(F32))


def _layer_norm_rows(tiles, lng_ref, lnb_ref, o_ref, d):
    tn = tiles[0].shape[1]
    tot = tiles[0].sum(axis=1, keepdims=True)
    for t in tiles[1:]:
        tot = tot + t.sum(axis=1, keepdims=True)
    mu = tot * (1.0 / d)
    sq = jnp.square(tiles[0] - mu).sum(axis=1, keepdims=True)
    for t in tiles[1:]:
        sq = sq + jnp.square(t - mu).sum(axis=1, keepdims=True)
    inv = lax.rsqrt(sq * (1.0 / d) + LN_EPS)
    for k, t in enumerate(tiles):
        cs = slice(k * tn, (k + 1) * tn)
        o_ref[0, :, cs] = ((t - mu) * inv * lng_ref[:, cs] + lnb_ref[:, cs]).astype(o_ref.dtype)


def _mmln_kernel(*refs, glu, nj, alpha, d):
    if glu:
        a_ref, w1_ref, w2_ref, b1_ref, b2_ref, x_ref, g_ref, lng_ref, lnb_ref, o_ref, r_scr = refs
    else:
        a_ref, w1_ref, x_ref, g_ref, lng_ref, lnb_ref, o_ref, r_scr = refs
    j = pl.program_id(2)
    for rs in _row_halves(a_ref.shape[1]):
        a = a_ref[0, rs, :]
        y = jnp.dot(a, w1_ref[...], preferred_element_type=F32)
        if glu:
            y = y + b1_ref[...]
            gate = jnp.dot(a, w2_ref[...], preferred_element_type=F32) + b2_ref[...]
            y = y * jax.nn.sigmoid(gate)
        r_scr[j, rs, :] = alpha * x_ref[0, rs, :] + (1.0 + g_ref[0]) * y

    @pl.when(j == nj - 1)
    def _():
        _layer_norm_rows([r_scr[t] for t in range(nj)], lng_ref, lnb_ref, o_ref, d)


def _matmul_res_ln(a, w, bias, xres, gate, ln_g, ln_b, alpha, glu):
    bsz, seq, k = a.shape
    d = xres.shape[-1]
    if glu:
        tm = _pick(seq, (1024, 512, 256, 128))
        tn = _pick(d, (512, 256, 128))
    else:
        tm = _pick(seq, (512, 256, 128))
        tn = d if 2 * k * d * w.dtype.itemsize <= RESIDENT_WEIGHT_BYTES else _pick(d, (512, 256, 128))
    nj = d // tn
    a_spec = pl.BlockSpec((1, tm, k), lambda b, i, j: (b, i, 0))
    w1_spec = pl.BlockSpec((k, tn), lambda b, i, j: (0, j))
    tail_specs = [
        pl.BlockSpec((1, tm, tn), lambda b, i, j: (b, i, j)),
        pl.BlockSpec((1, 1, tn), lambda b, i, j: (b, 0, j)),
        pl.BlockSpec((1, d), lambda b, i, j: (0, 0)),
        pl.BlockSpec((1, d), lambda b, i, j: (0, 0)),
    ]
    tail = (xres, gate, ln_g.reshape(1, d).astype(F32), ln_b.reshape(1, d).astype(F32))
    if glu:
        in_specs = [a_spec, w1_spec,
                    pl.BlockSpec((k, tn), lambda b, i, j: (0, j + nj)),
                    pl.BlockSpec((1, tn), lambda b, i, j: (0, j)),
                    pl.BlockSpec((1, tn), lambda b, i, j: (0, j + nj))] + tail_specs
        b2d = bias.reshape(1, 2 * d).astype(F32)
        args = (a, w, w, b2d, b2d) + tail
    else:
        in_specs = [a_spec, w1_spec] + tail_specs
        args = (a, w) + tail
    kern = functools.partial(_mmln_kernel, glu=glu, nj=nj, alpha=alpha, d=d)
    return pl.pallas_call(
        kern,
        out_shape=jax.ShapeDtypeStruct((bsz, seq, d), F32),
        grid=(bsz, seq // tm, nj),
        in_specs=in_specs,
        out_specs=pl.BlockSpec((1, tm, d), lambda b, i, j: (b, i, 0)),
        scratch_shapes=[pltpu.VMEM((nj, tm, tn), F32)],
        compiler_params=_params(("parallel", "parallel", "arbitrary")),
        name="matmul_res_ln",
    )(*args)


def _ffn_kernel(x_ref, sc_ref, sh_ref, g_ref, wg_ref, wu_ref, wo_ref, lng_ref, lnb_ref, o_ref,
                h_scr, acc_scr, *, nf, alpha, d, tn):
    f = pl.program_id(2)

    @pl.when(f == 0)
    def _():
        h_scr[...] = _modulate(x_ref, sc_ref, sh_ref)
        acc_scr[...] = jnp.zeros_like(acc_scr)

    h = h_scr[...]
    a_g = jnp.dot(h, wg_ref[0], preferred_element_type=F32)
    a_u = jnp.dot(h, wu_ref[0], preferred_element_type=F32)
    act = (jax.nn.silu(a_g) * a_u).astype(BF16)
    acc_scr[...] += jnp.dot(act, wo_ref[0], preferred_element_type=F32)

    @pl.when(f == nf - 1)
    def _():
        tiles = []
        for k in range(d // tn):
            cs = slice(k * tn, (k + 1) * tn)
            tiles.append(alpha * x_ref[0, :, cs] + (1.0 + g_ref[0, :, cs]) * acc_scr[:, cs])
        _layer_norm_rows(tiles, lng_ref, lnb_ref, o_ref, d)


def _ffn_res_ln(x, sc, sh, gate, w_in, w_out, layer, ln_g, ln_b, alpha):
    bsz, seq, d = x.shape
    dff = w_out.shape[1]
    tm = _pick(seq, (512, 256, 128))
    tf = _pick(dff, (512, 256, 128))
    nf = dff // tf
    tn = _pick(d, (512, 256, 128))
    vec = pl.BlockSpec((1, 1, d), lambda b, i, f: (b, 0, 0))
    kern = functools.partial(_ffn_kernel, nf=nf, alpha=alpha, d=d, tn=tn)
    return pl.pallas_call(
        kern,
        out_shape=jax.ShapeDtypeStruct((bsz, seq, d), F32),
        grid=(bsz, seq // tm, nf),
        in_specs=[
            pl.BlockSpec((1, tm, d), lambda b, i, f: (b, i, 0)),
            vec, vec, vec,
            pl.BlockSpec((1, d, tf), lambda b, i, f: (layer, 0, f)),
            pl.BlockSpec((1, d, tf), lambda b, i, f: (layer, 0, f + nf)),
            pl.BlockSpec((1, tf, d), lambda b, i, f: (layer, f, 0)),
            pl.BlockSpec((1, d), lambda b, i, f: (0, 0)),
            pl.BlockSpec((1, d), lambda b, i, f: (0, 0)),
        ],
        out_specs=pl.BlockSpec((1, tm, d), lambda b, i, f: (b, i, 0)),
        scratch_shapes=[pltpu.VMEM((tm, d), BF16), pltpu.VMEM((tm, d), F32)],
        compiler_params=_params(("parallel", "parallel", "arbitrary")),
        name="ffn_res_ln",
    )(x, sc, sh, gate, w_in, w_in, w_out, ln_g.reshape(1, d).astype(F32), ln_b.reshape(1, d).astype(F32))


def _qkv_kernel(x_ref, sc_ref, sh_ref, w_ref, cos_ref, sin_ref, o_ref, h_scr, *, tiles_per_tensor, q_scale):
    j = pl.program_id(2)

    @pl.when(j == 0)
    def _():
        h_scr[...] = _modulate(x_ref, sc_ref, sh_ref)

    tensor = j // tiles_per_tensor
    scale = jnp.where(tensor == 0, q_scale, 1.0)
    is_v = tensor == 2
    chunk = min(MXU_WIDTH, w_ref.shape[1])
    for rs in _row_halves(h_scr.shape[0]):
        a = jnp.where(is_v, 1.0, cos_ref[0, rs, :] * scale)
        b = jnp.where(is_v, 0.0, sin_ref[0, rs, :] * scale)
        for c in range(w_ref.shape[1] // chunk):
            acc = jnp.dot(h_scr[rs, :], w_ref[:, c * chunk:(c + 1) * chunk], preferred_element_type=F32)
            for hh in range(chunk // HEAD_DIM):
                xc = acc[:, hh * HEAD_DIM:(hh + 1) * HEAD_DIM]
                xc = xc * a + pltpu.roll(xc, HEAD_DIM // 2, 1) * b
                o_ref[0, c * (chunk // HEAD_DIM) + hh, rs, :] = xc.astype(o_ref.dtype)


def _qkv_proj(x, sc, sh, w_qkv, cos_t, sin_t):
    bsz, seq, d = x.shape
    tm = _pick(seq, (1024, 512, 256, 128))
    tn = _pick(d, (1024, 512, 256, 128))
    hpt = tn // HEAD_DIM
    kern = functools.partial(_qkv_kernel, tiles_per_tensor=d // tn, q_scale=HEAD_DIM ** -0.5 * math.log2(math.e))
    return pl.pallas_call(
        kern,
        out_shape=jax.ShapeDtypeStruct((bsz, 3 * d // HEAD_DIM, seq, HEAD_DIM), BF16),
        grid=(bsz, seq // tm, 3 * d // tn),
        in_specs=[
            pl.BlockSpec((1, tm, d), lambda b, i, j: (b, i, 0)),
            pl.BlockSpec((1, 1, d), lambda b, i, j: (b, 0, 0)),
            pl.BlockSpec((1, 1, d), lambda b, i, j: (b, 0, 0)),
            pl.BlockSpec((d, tn), lambda b, i, j: (0, j)),
            pl.BlockSpec((1, tm, HEAD_DIM), lambda b, i, j: (b, i, 0)),
            pl.BlockSpec((1, tm, HEAD_DIM), lambda b, i, j: (b, i, 0)),
        ],
        out_specs=pl.BlockSpec((1, hpt, tm, HEAD_DIM), lambda b, i, j: (b, j, i, 0)),
        scratch_shapes=[pltpu.VMEM((tm, d), BF16)],
        compiler_params=_params(("parallel", "parallel", "arbitrary")),
        name="dsa_qkv_proj",
    )(x, sc, sh, w_qkv, cos_t, sin_t)


def _idx_kernel(x_ref, sc_ref, sh_ref, w_ref, cos_ref, sin_ref, qi_ref, kia_ref, kib_ref, wi_ref,
                *, n_pair, idx_heads, w_scale):
    h = _modulate(x_ref, sc_ref, sh_ref)
    halves = _row_halves(h.shape[0])
    accs = [jnp.dot(h[rs], w_ref[...], preferred_element_type=F32) for rs in halves]
    lane = lax.broadcasted_iota(I32, (halves[0].stop, LANES), 1)
    first_half = (lane % IDX_DIM) < IDX_DIM // 2

    for rs, acc in zip(halves, accs):
        cos = cos_ref[0, rs, :]
        sin = sin_ref[0, rs, :]

        def rope(xc, cos=cos, sin=sin):
            partner = jnp.where(first_half, pltpu.roll(xc, LANES - IDX_DIM // 2, 1),
                                pltpu.roll(xc, IDX_DIM // 2, 1))
            return xc * cos + partner * sin

        for p in range(n_pair):
            cs = slice(p * LANES, (p + 1) * LANES)
            qi_ref[0, rs, cs] = rope(acc[:, cs]).astype(qi_ref.dtype)
        last = acc[:, n_pair * LANES:]
        ka = jnp.where(lane < IDX_DIM, rope(last), 0.0)
        kia_ref[0, rs, :] = ka.astype(kia_ref.dtype)
        kib_ref[0, rs, :] = pltpu.roll(ka, IDX_DIM, 1).astype(kib_ref.dtype)
        wi_ref[0, rs, :] = jnp.where(lane < idx_heads, pltpu.roll(last, LANES - IDX_DIM, 1), 0.0) * w_scale


def _idx_proj(x, sc, sh, w_idx, cos_t, sin_t, idx_heads):
    bsz, seq, d = x.shape
    n = w_idx.shape[1]
    n_pair = idx_heads // 2
    tm = _pick(seq, (512, 256, 128))
    kern = functools.partial(_idx_kernel, n_pair=n_pair, idx_heads=idx_heads,
                             w_scale=(idx_heads ** -0.5) * (IDX_DIM ** -0.5))
    row = lambda b, i: (b, i, 0)
    return pl.pallas_call(
        kern,
        out_shape=(
            jax.ShapeDtypeStruct((bsz, seq, n_pair * LANES), BF16),
            jax.ShapeDtypeStruct((bsz, seq, LANES), BF16),
            jax.ShapeDtypeStruct((bsz, seq, LANES), BF16),
            jax.ShapeDtypeStruct((bsz, seq, LANES), F32),
        ),
        grid=(bsz, seq // tm),
        in_specs=[
            pl.BlockSpec((1, tm, d), row),
            pl.BlockSpec((1, 1, d), lambda b, i: (b, 0, 0)),
            pl.BlockSpec((1, 1, d), lambda b, i: (b, 0, 0)),
            pl.BlockSpec((d, n), lambda b, i: (0, 0)),
            pl.BlockSpec((1, tm, LANES), row),
            pl.BlockSpec((1, tm, LANES), row),
        ],
        out_specs=(
            pl.BlockSpec((1, tm, n_pair * LANES), row),
            pl.BlockSpec((1, tm, LANES), row),
            pl.BlockSpec((1, tm, LANES), row),
            pl.BlockSpec((1, tm, LANES), row),
        ),
        compiler_params=_params(("parallel", "parallel")),
        name="dsa_idx_proj",
    )(x, sc, sh, w_idx, cos_t, sin_t)


def _bit_transpose32(words):
    a = list(words)
    j = WORD_BITS // 2
    mask = 0x0000FFFF
    while j:
        k = 0
        while k < WORD_BITS:
            t = (a[k] ^ lax.shift_right_logical(a[k + j], jnp.int32(j))) & jnp.int32(mask)
            a[k] = a[k] ^ t
            a[k + j] = a[k + j] ^ (t << j)
            k = (k + j + 1) & ~j
        j >>= 1
        if j:
            mask = (mask ^ (mask << j)) & 0xFFFFFFFF
    return a


def _dsa_kernel(qb_tab, kb_tab, q_ref, k_ref, v_ref, qi_ref, kia_ref, kib_ref, wi_ref, o_ref,
                keys_scr, planes_scr, thr_scr, tie_scr, ngt_scr, nge_scr,
                m_scr, l_scr, acc_scr, bias_scr, s0_scr, s1_scr, r0_scr, r1_scr,
                *, n_pair, k_top, idx_bits):
    heads, qb_rows, _ = acc_scr.shape
    kb_rows = keys_scr.shape[2]
    qb = qb_tab[pl.program_id(1)]
    kb = kb_tab[pl.program_id(1)]
    q0 = qb * qb_rows
    last_kb = (q0 + qb_rows - 1) // kb_rows
    n_chunk = last_kb + 1
    row = lax.broadcasted_iota(I32, (qb_rows, kb_rows), 0) + q0
    col = lax.broadcasted_iota(I32, (qb_rows, kb_rows), 1)
    nt = (((1,), (1,)), ((), ()))

    @pl.when(kb == 0)
    def _select():
        wi = wi_ref[0]

        def score_chunk(c, carry):
            k0 = pl.multiple_of(c * kb_rows, kb_rows)
            ka = kia_ref[0, pl.ds(k0, kb_rows), :]
            kbm = kib_ref[0, pl.ds(k0, kb_rows), :]
            sc = jnp.zeros((qb_rows, kb_rows), F32)
            for p in range(n_pair):
                qp = qi_ref[0, :, p * LANES:(p + 1) * LANES]
                sa = lax.dot_general(qp, ka, nt, preferred_element_type=F32)
                sb = lax.dot_general(qp, kbm, nt, preferred_element_type=F32)
                sc = sc + wi[:, 2 * p:2 * p + 1] * jnp.maximum(sa, 0.0)
                sc = sc + wi[:, 2 * p + 1:2 * p + 2] * jnp.maximum(sb, 0.0)
            bits = lax.bitcast_convert_type(sc, I32)
            key = bits ^ ((bits >> 31) & INT_MAX)
            keys_scr[c] = jnp.where(col + k0 <= row, key, INT_MIN)
            return carry

        lax.fori_loop(0, n_chunk, score_chunk, 0)

        slices_per_chunk = kb_rows // LANES
        chunks_per_set = WORD_BITS // slices_per_chunk
        n_slices = keys_scr.shape[0] * slices_per_chunk
        n_sets = planes_scr.shape[2] // LANES
        live_sets = (n_chunk + chunks_per_set - 1) // chunks_per_set

        def fill_chunk(c, carry):
            keys_scr[c] = jnp.full((qb_rows, kb_rows), INT_MIN, I32)
            return carry

        lax.fori_loop(n_chunk, jnp.minimum(live_sets * chunks_per_set, keys_scr.shape[0]), fill_chunk, 0)

        def pack_rows(g, carry, st):
            r0 = pl.multiple_of(g * SUBLANES, SUBLANES)
            words = []
            for s in range(WORD_BITS):
                sl = st * WORD_BITS + s
                if sl < n_slices:
                    c, off = divmod(sl, slices_per_chunk)
                    words.append(keys_scr[c, pl.ds(r0, SUBLANES), off * LANES:(off + 1) * LANES])
                else:
                    words.append(jnp.full((SUBLANES, LANES), INT_MIN, I32))
            words = _bit_transpose32(words)
            words[0] = ~words[0]
            for i in range(WORD_BITS):
                planes_scr[i, pl.ds(r0, SUBLANES), st * LANES:(st + 1) * LANES] = words[i]
            return carry

        for st in range(n_sets):
            @pl.when(st < live_sets)
            def _(st=st):
                lax.fori_loop(0, qb_rows // SUBLANES, functools.partial(pack_rows, st=st), 0)

            @pl.when(st >= live_sets)
            def _(st=st):
                planes_scr[:, :, st * LANES:(st + 1) * LANES] = jnp.zeros((WORD_BITS, qb_rows, LANES), I32)

        lane_ones = jnp.ones((LANES, LANES), BF16)

        def row_total(pc):
            tot = pc[:, :LANES]
            for st in range(1, n_sets):
                tot = tot + pc[:, st * LANES:(st + 1) * LANES]
            return jnp.dot(tot.astype(F32).astype(BF16), lane_ones, preferred_element_type=F32)

        def all_sets(mask):
            return jnp.concatenate([mask] * n_sets, axis=1)

        def select_two_bits(i, carry):
            cand, above, prefix = carry
            hi = planes_scr[2 * i]
            lo = planes_scr[2 * i + 1]
            c1 = cand & hi
            c0 = cand ^ c1
            c11 = c1 & lo
            c10 = c1 ^ c11
            c01 = c0 & lo
            c00 = c0 ^ c01
            r11 = above + row_total(lax.population_count(c11))
            r10 = r11 + row_total(lax.population_count(c10))
            r01 = r10 + row_total(lax.population_count(c01))
            t11 = r11 >= k_top
            t10 = r10 >= k_top
            t01 = r01 >= k_top
            cand = jnp.where(all_sets(t11), c11,
                             jnp.where(all_sets(t10), c10, jnp.where(all_sets(t01), c01, c00)))
            above = jnp.where(t11, above, jnp.where(t10, r11, jnp.where(t01, r10, r01)))
            bits = jnp.where(t11, 3, jnp.where(t10, 2, jnp.where(t01, 1, 0)))
            prefix = prefix | (bits << (WORD_BITS - 2 - 2 * i))
            return cand, above, prefix

        cand, above, prefix = lax.fori_loop(
            0, WORD_BITS // 2, select_two_bits,
            (jnp.concatenate([jnp.full((qb_rows, LANES), jnp.where(st < live_sets, -1, 0), I32)
                              for st in range(n_sets)], axis=1),
             jnp.zeros((qb_rows, LANES), F32), jnp.zeros((qb_rows, LANES), I32)))
        equal = row_total(lax.population_count(cand))
        thr_scr[...] = (prefix ^ INT_MIN)[:, :1]
        tie_scr[...] = jnp.full((qb_rows, 1), INT_MAX, I32)
        ngt_scr[...] = above[:, :1]
        nge_scr[...] = (above + equal)[:, :1]

        rg = min(qb_rows, SEARCH_ROWS)
        lane_col = lax.broadcasted_iota(I32, (rg, LANES), 1)

        for r in range(qb_rows // rg):
            rs = slice(r * rg, (r + 1) * rg)

            def count(pred, rs=rs):
                def body(c, acc):
                    for t in range(kb_rows // LANES):
                        kk = keys_scr[c, rs, t * LANES:(t + 1) * LANES]
                        idx = lane_col + (c * kb_rows + t * LANES)
                        acc = acc + jnp.where(pred(kk, idx), 1.0, 0.0)
                    return acc
                acc = lax.fori_loop(0, n_chunk, body, jnp.zeros((rg, LANES), F32))
                return jnp.sum(acc, axis=1, keepdims=True)

            def wide(v):
                return jnp.broadcast_to(v, (rg, LANES))

            @pl.when(jnp.max(nge_scr[rs, :]) > k_top)
            def _ties(count=count, rs=rs):
                thr_w = wide(thr_scr[rs, :])
                need = k_top - ngt_scr[rs, :]

                def tie_bit(i, jt):
                    cand = jt + (jnp.int32(1) << (idx_bits - 1 - i))
                    cand_w = wide(cand)
                    cnt = count(lambda kk, idx: (kk == thr_w) & (idx < cand_w))
                    return jnp.where(cnt < need, cand, jt)

                tie_scr[rs, :] = lax.fori_loop(0, idx_bits, tie_bit, jnp.zeros((rg, 1), I32))

        m_scr[...] = jnp.full(m_scr.shape, M_INIT, F32)
        l_scr[...] = jnp.zeros_like(l_scr)
        acc_scr[...] = jnp.zeros_like(acc_scr)

    def _attend():
        ones = jnp.ones((kb_rows, HEAD_DIM), BF16)

        kk = keys_scr[kb]
        thr = thr_scr[...]
        lim = jnp.minimum(tie_scr[...], row[:, :1])
        tie_bias = jnp.where(col + kb * kb_rows <= lim, 0.0, MASK_VALUE)
        bias_scr[...] = jnp.where(kk > thr, 0.0, jnp.where(kk == thr, tie_bias, MASK_VALUE))

        def logits(h, s_ref, r_ref):
            s = lax.dot_general(q_ref[0, h], k_ref[0, h], nt, preferred_element_type=F32) + bias_scr[...]
            s_ref[...] = s
            r_ref[...] = jnp.broadcast_to(jnp.max(s, axis=1, keepdims=True), r_ref.shape)

        def accumulate(h, s_ref, r_ref):
            m_old = m_scr[h]
            m_new = jnp.maximum(m_old, r_ref[...])
            alpha = jnp.exp2(m_old - m_new)
            p = jnp.concatenate(
                [jnp.exp2(s_ref[:, t * LANES:(t + 1) * LANES] - m_new).astype(BF16)
                 for t in range(kb_rows // LANES)], axis=1)
            v_ext = jnp.concatenate([v_ref[0, h], ones], axis=1)
            pv = jnp.dot(p, v_ext, preferred_element_type=F32)
            acc_scr[h] = alpha * acc_scr[h] + pv[:, :HEAD_DIM]
            l_scr[h] = alpha * l_scr[h] + pv[:, HEAD_DIM:]
            m_scr[h] = m_new

        bufs = ((s0_scr, r0_scr), (s1_scr, r1_scr))
        logits(0, *bufs[0])
        for h in range(heads):
            if h + 1 < heads:
                logits(h + 1, *bufs[(h + 1) % 2])
            accumulate(h, *bufs[h % 2])

    _attend()

    @pl.when(kb == last_kb)
    def _finish():
        for h in range(heads):
            o_ref[0, :, h * HEAD_DIM:(h + 1) * HEAD_DIM] = (acc_scr[h] / l_scr[h]).astype(o_ref.dtype)


def _dsa_attention(qkv, qi, kia, kib, wi, k_top):
    bsz, heads3, seq, _ = qkv.shape
    heads = heads3 // 3
    d = heads * HEAD_DIM
    n_pair = qi.shape[-1] // LANES
    qb_rows = _pick(seq, (256, 128))
    kb_rows = _pick(seq, (512, 256, 128))
    n_kb = seq // kb_rows
    n_sets = -(-seq // (WORD_BITS * LANES))

    pairs = [(i, j) for i in range(seq // qb_rows) for j in range((i * qb_rows + qb_rows - 1) // kb_rows + 1)]
    qb_tab = jnp.asarray([p[0] for p in pairs], I32)
    kb_tab = jnp.asarray([p[1] for p in pairs], I32)

    kern = functools.partial(_dsa_kernel, n_pair=n_pair, k_top=k_top, idx_bits=seq.bit_length())
    grid_spec = pltpu.PrefetchScalarGridSpec(
        num_scalar_prefetch=2,
        grid=(bsz, len(pairs)),
        in_specs=[
            pl.BlockSpec((1, heads, qb_rows, HEAD_DIM), lambda b, s, qt, kt: (b, 0, qt[s], 0)),
            pl.BlockSpec((1, heads, kb_rows, HEAD_DIM), lambda b, s, qt, kt: (b, 1, kt[s], 0)),
            pl.BlockSpec((1, heads, kb_rows, HEAD_DIM), lambda b, s, qt, kt: (b, 2, kt[s], 0)),
            pl.BlockSpec((1, qb_rows, n_pair * LANES), lambda b, s, qt, kt: (b, qt[s], 0)),
            pl.BlockSpec((1, seq, LANES), lambda b, s, qt, kt: (b, 0, 0)),
            pl.BlockSpec((1, seq, LANES), lambda b, s, qt, kt: (b, 0, 0)),
            pl.BlockSpec((1, qb_rows, LANES), lambda b, s, qt, kt: (b, qt[s], 0)),
        ],
        out_specs=pl.BlockSpec((1, qb_rows, d), lambda b, s, qt, kt: (b, qt[s], 0)),
        scratch_shapes=[
            pltpu.VMEM((n_kb, qb_rows, kb_rows), I32),
            pltpu.VMEM((WORD_BITS, qb_rows, n_sets * LANES), I32),
            pltpu.VMEM((qb_rows, 1), I32),
            pltpu.VMEM((qb_rows, 1), I32),
            pltpu.VMEM((qb_rows, 1), F32),
            pltpu.VMEM((qb_rows, 1), F32),
            pltpu.VMEM((heads, qb_rows, LANES), F32),
            pltpu.VMEM((heads, qb_rows, LANES), F32),
            pltpu.VMEM((heads, qb_rows, HEAD_DIM), F32),
            pltpu.VMEM((qb_rows, kb_rows), F32),
            pltpu.VMEM((qb_rows, kb_rows), F32),
            pltpu.VMEM((qb_rows, kb_rows), F32),
            pltpu.VMEM((qb_rows, LANES), F32),
            pltpu.VMEM((qb_rows, LANES), F32),
        ],
    )
    return pl.pallas_call(
        kern,
        out_shape=jax.ShapeDtypeStruct((bsz, seq, d), BF16),
        grid_spec=grid_spec,
        compiler_params=_params(("parallel", "arbitrary")),
        name="dsa_select_attend",
    )(qb_tab, kb_tab, qkv, qkv, qkv, qi, kia, kib, wi)


def _rope_tables(positions, dim):
    inv = 1.0 / (ROPE_THETA ** (jnp.arange(0, dim, 2, dtype=F32) / dim))
    half = dim // 2
    lane = jnp.arange(LANES)
    inv_t = inv[lane % half]
    sign = jnp.where((lane % dim) < half, -1.0, 1.0).astype(F32)
    ang = positions.astype(F32)[..., None] * inv_t
    return jnp.cos(ang), jnp.sin(ang) * sign


def kernel(x, c, positions, ada_w, ada_b, ln_g, ln_b, s5_in_w, s5_a_re, s5_a_im, s5_log_dt, s5_b_re, s5_b_im, s5_c_re, s5_c_im, s5_d, s5_glu_w, s5_glu_b, dsa_in_w, dsa_out_w, ffn_w_in, ffn_w_out):
    bsz, seq, d = x.shape
    depth = ada_w.shape[0]
    alpha = (2.0 * depth) ** 0.25
    idx_heads = (dsa_in_w.shape[-1] - 3 * d - IDX_DIM) // (IDX_DIM + 1)
    k_top = min(TOPK_MAX, seq // 4)
    seg = min(S5_SEG, seq // SUBLANES)

    cos_h, sin_h = _rope_tables(positions, HEAD_DIM)
    cos_i, sin_i = _rope_tables(positions, IDX_DIM)
    mod = _ada_mod(c, ada_w, ada_b)
    ffn_in = ffn_w_in.astype(BF16)
    ffn_out = ffn_w_out.astype(BF16)

    for i in range(depth):
        sh1, sc1, g1, sh2, sc2, g2 = [m[:, None, :] for m in jnp.split(mod[i], 6, axis=-1)]
        j = i // 2
        if i % 2 == 0:
            prep = _s5_discretize(s5_a_re[j], s5_a_im[j], s5_log_dt[j], s5_b_re[j], s5_b_im[j],
                                  s5_c_re[j], s5_c_im[j], seg)
            u = _mod_matmul(x, sc1, sh1, s5_in_w[j].astype(BF16), F32)
            gl = _s5_scan(u, prep, s5_d[j], seg)
            x1 = _matmul_res_ln(gl, s5_glu_w[j].astype(BF16), s5_glu_b[j], x, g1,
                                ln_g[i, 0], ln_b[i, 0], alpha, glu=True)
        else:
            w = dsa_in_w[j].astype(BF16)
            n_qi = idx_heads * IDX_DIM
            pad = jnp.zeros((d, LANES - IDX_DIM - idx_heads), w.dtype)
            w_idx = jnp.concatenate(
                [w[:, 3 * d:3 * d + n_qi], w[:, 3 * d + n_qi + idx_heads:], w[:, 3 * d + n_qi:3 * d + n_qi + idx_heads], pad],
                axis=1)
            qkv = _qkv_proj(x, sc1, sh1, w, cos_h, sin_h)
            qi, kia, kib, wi = _idx_proj(x, sc1, sh1, w_idx, cos_i, sin_i, idx_heads)
            att = _dsa_attention(qkv, qi, kia, kib, wi, k_top)
            x1 = _matmul_res_ln(att, dsa_out_w[j].astype(BF16), None, x, g1,
                                ln_g[i, 0], ln_b[i, 0], alpha, glu=False)
        x = _ffn_res_ln(x1, sc2, sh2, g2, ffn_in, ffn_out, i, ln_g[i, 1], ln_b[i, 1], alpha)
    return x
```

```python
import functools
import math

import jax
import jax.numpy as jnp
from jax import lax
from jax.experimental import pallas as pl
from jax.experimental.pallas import tpu as pltpu

F32 = jnp.float32
BF16 = jnp.bfloat16
I32 = jnp.int32

S5_GROUP = 16
S5_STATE = 64
HEAD_DIM = 128
IDX_DIM = 64
TOPK_MAX = 256
ROPE_THETA = 10000.0
LN_EPS = 1e-5

LANES = 128
SUBLANES = 8
MXU_WIDTH = 256
VMEM_LIMIT_BYTES = 56 * 1024 * 1024
RESIDENT_WEIGHT_BYTES = 16 * 1024 * 1024

S5_PACK_GROUPS = 16
S5_SEG = 128

INT_MIN = -(2 ** 31)
INT_MAX = 2 ** 31 - 1
SEARCH_ROWS = 128
WORD_BITS = 32

MASK_VALUE = -2e30
M_INIT = -1e30


def _pick(n, cands):
    for c in cands:
        if n % c == 0:
            return c
    return n


def _row_halves(rows):
    half = rows // 2
    return (slice(0, half), slice(half, rows))


def _params(sem):
    return pltpu.CompilerParams(dimension_semantics=sem, vmem_limit_bytes=VMEM_LIMIT_BYTES)


def _ada_kernel(c_ref, w_ref, b_ref, o_ref):
    ca = jax.nn.silu(c_ref[...]).astype(BF16)
    o_ref[0] = jnp.dot(ca, w_ref[0].astype(BF16), preferred_element_type=F32) + b_ref[0]


def _ada_mod(c, ada_w, ada_b):
    bsz, d = c.shape
    depth, _, n = ada_w.shape
    rows = SUBLANES * ((bsz + SUBLANES - 1) // SUBLANES)
    cp = jnp.zeros((rows, d), F32).at[:bsz].set(c)
    tn = _pick(n, (1024, 512, 256, 128))
    out = pl.pallas_call(
        _ada_kernel,
        out_shape=jax.ShapeDtypeStruct((depth, rows, n), F32),
        grid=(depth, n // tn),
        in_specs=[
            pl.BlockSpec((rows, d), lambda l, j: (0, 0)),
            pl.BlockSpec((1, d, tn), lambda l, j: (l, 0, j)),
            pl.BlockSpec((1, 1, tn), lambda l, j: (l, 0, j)),
        ],
        out_specs=pl.BlockSpec((1, rows, tn), lambda l, j: (l, 0, j)),
        compiler_params=_params(("arbitrary", "arbitrary")),
        name="ada_mod",
    )(cp, ada_w, ada_b.reshape(depth, 1, n))
    return out[:, :bsz]


def _modulate(x_ref, sc_ref, sh_ref):
    return (x_ref[0] * (1.0 + sc_ref[0]) + sh_ref[0]).astype(BF16)


def _modmm_kernel(x_ref, sc_ref, sh_ref, w_ref, o_ref, h_scr):
    @pl.when(pl.program_id(2) == 0)
    def _():
        h_scr[...] = _modulate(x_ref, sc_ref, sh_ref)

    for rs in _row_halves(h_scr.shape[0]):
        o_ref[0, rs, :] = jnp.dot(h_scr[rs, :], w_ref[...], preferred_element_type=F32).astype(o_ref.dtype)


def _mod_matmul(x, sc, sh, w, out_dtype):
    bsz, seq, d = x.shape
    n = w.shape[1]
    tm = _pick(seq, (512, 256, 128))
    tn = n if 2 * d * n * w.dtype.itemsize <= RESIDENT_WEIGHT_BYTES else _pick(n, (1024, 512, 256, 128))
    return pl.pallas_call(
        _modmm_kernel,
        out_shape=jax.ShapeDtypeStruct((bsz, seq, n), out_dtype),
        grid=(bsz, seq // tm, n // tn),
        in_specs=[
            pl.BlockSpec((1, tm, d), lambda b, i, j: (b, i, 0)),
            pl.BlockSpec((1, 1, d), lambda b, i, j: (b, 0, 0)),
            pl.BlockSpec((1, 1, d), lambda b, i, j: (b, 0, 0)),
            pl.BlockSpec((d, tn), lambda b, i, j: (0, j)),
        ],
        out_specs=pl.BlockSpec((1, tm, tn), lambda b, i, j: (b, i, j)),
        scratch_shapes=[pltpu.VMEM((tm, d), BF16)],
        compiler_params=_params(("parallel", "parallel", "arbitrary")),
        name="mod_matmul",
    )(x, sc, sh, w)


def _s5_kernel(*refs, seg, width, lane_tiles):
    u_refs = refs[:lane_tiles]
    (bb_ref, cb_ref, lre_ref, lim_ref, ltre_ref, ltim_ref, d_ref, o_ref,
     xs_scr, carry_scr, cin_scr, up_scr) = refs[lane_tiles:]
    ns = xs_scr.shape[1] // 2

    @pl.when(pl.program_id(2) == 0)
    def _():
        carry_scr[...] = jnp.zeros_like(carry_scr)

    for t, u_ref in enumerate(u_refs):
        for i in range(seg):
            up_scr[t, i * SUBLANES:(i + 1) * SUBLANES, :] = u_ref[0, pl.ds(i, SUBLANES, stride=seg), :]
    u = jnp.concatenate([up_scr[t] for t in range(len(u_refs))], axis=1)
    halves = _row_halves(u.shape[0])
    for rs in halves:
        xs_scr[rs, :] = jnp.dot(u[rs].astype(BF16), bb_ref[0], preferred_element_type=F32)

    for part in range(ns // width):
        cr = slice(part * width, (part + 1) * width)
        ci = slice(ns + part * width, ns + (part + 1) * width)
        lr = jnp.broadcast_to(lre_ref[0, :, cr], (SUBLANES, width))
        li = jnp.broadcast_to(lim_ref[0, :, cr], (SUBLANES, width))

        def local_step(i, st, cr=cr, ci=ci, lr=lr, li=li):
            sr, si = st
            r0 = pl.multiple_of(i * SUBLANES, SUBLANES)
            nr = lr * sr - li * si + xs_scr[pl.ds(r0, SUBLANES), cr]
            ni = lr * si + li * sr + xs_scr[pl.ds(r0, SUBLANES), ci]
            xs_scr[pl.ds(r0, SUBLANES), cr] = nr
            xs_scr[pl.ds(r0, SUBLANES), ci] = ni
            return nr, ni

        zero = jnp.zeros((SUBLANES, width), F32)
        er, ei = lax.fori_loop(0, seg, local_step, (zero, zero), unroll=2)

        ltr = ltre_ref[0, :, cr]
        lti = ltim_ref[0, :, cr]
        c_r = carry_scr[:, cr]
        c_i = carry_scr[:, ci]
        for s in range(SUBLANES):
            cin_scr[s:s + 1, cr] = c_r
            cin_scr[s:s + 1, ci] = c_i
            e_r = er[s:s + 1, :]
            e_i = ei[s:s + 1, :]
            c_r, c_i = ltr * c_r - lti * c_i + e_r, ltr * c_i + lti * c_r + e_i
        carry_scr[:, cr] = c_r
        carry_scr[:, ci] = c_i

        def carry_step(i, st, cr=cr, ci=ci, lr=lr, li=li):
            pr, pi_ = st
            r0 = pl.multiple_of(i * SUBLANES, SUBLANES)
            nr = lr * pr - li * pi_
            ni = lr * pi_ + li * pr
            xs_scr[pl.ds(r0, SUBLANES), cr] += nr
            xs_scr[pl.ds(r0, SUBLANES), ci] += ni
            return nr, ni

        lax.fori_loop(0, seg, carry_step, (cin_scr[:, cr], cin_scr[:, ci]), unroll=2)

    for rs in halves:
        y = jnp.dot(xs_scr[rs, :].astype(BF16), cb_ref[0], preferred_element_type=F32)
        g = jax.nn.gelu(y + d_ref[0] * u[rs])
        for t in range(len(u_refs)):
            up_scr[t, rs, :] = g[:, t * LANES:(t + 1) * LANES]
    for t in range(len(u_refs)):
        for s in range(SUBLANES):
            o_ref[0, s * seg:(s + 1) * seg, t * LANES:(t + 1) * LANES] = (
                up_scr[t, pl.ds(s, seg, stride=SUBLANES), :].astype(o_ref.dtype))


def _cmul(ar, ai, br, bi):
    return ar * br - ai * bi, ar * bi + ai * br


def _s5_discretize(a_re, a_im, log_dt, b_re, b_im, c_re, c_im, seg):
    g, n = a_re.shape
    p = b_re.shape[-1]
    pg = S5_PACK_GROUPS
    packs = g // pg
    a_re, a_im = a_re.astype(F32), a_im.astype(F32)
    dt = jnp.exp(log_dt.astype(F32))[:, None]
    mag = jnp.exp(a_re * dt)
    lb_re, lb_im = mag * jnp.cos(a_im * dt), mag * jnp.sin(a_im * dt)
    den = a_re * a_re + a_im * a_im
    nr, ni = lb_re - 1.0, lb_im
    f_re = (nr * a_re + ni * a_im) / den
    f_im = (ni * a_re - nr * a_im) / den
    bb_re, bb_im = _cmul(f_re[..., None], f_im[..., None], b_re.astype(F32), b_im.astype(F32))
    lt_re, lt_im = lb_re, lb_im
    for _ in range(int(math.log2(seg))):
        lt_re, lt_im = _cmul(lt_re, lt_im, lt_re, lt_im)
    on_diag = (jnp.arange(pg * p)[:, None] // p) == (jnp.arange(pg * n)[None, :] // n)

    def blk_diag(m):
        tiled = jnp.tile(m.astype(BF16).reshape(packs, pg * p, n), (1, 1, pg))
        return jnp.where(on_diag[None], tiled, 0)

    b_blk = jnp.concatenate([blk_diag(jnp.swapaxes(bb_re, 1, 2)), blk_diag(jnp.swapaxes(bb_im, 1, 2))], axis=-1)
    c_blk = jnp.swapaxes(jnp.concatenate([blk_diag(c_re), blk_diag(-c_im)], axis=-1), 1, 2)

    def vec(m):
        return m.reshape(packs, 1, pg * n)

    return b_blk, c_blk, vec(lb_re), vec(lb_im), vec(lt_re), vec(lt_im)


def _s5_scan(u, prep, d_skip, seg):
    bsz, seq, d = u.shape
    b_blk, c_blk, lre, lim, ltre, ltim = prep
    packs, pw, ns2 = b_blk.shape
    ns = ns2 // 2
    rows = SUBLANES * seg
    width = _pick(ns, (1024, 512, 256, 128))
    lane_tiles = pw // LANES
    kern = functools.partial(_s5_kernel, seg=seg, width=width, lane_tiles=lane_tiles)
    vspec = pl.BlockSpec((1, 1, ns), lambda b, k, m: (k, 0, 0))
    u_specs = [pl.BlockSpec((1, rows, LANES), lambda b, k, m, t=t: (b, m, k * lane_tiles + t))
               for t in range(lane_tiles)]
    return pl.pallas_call(
        kern,
        out_shape=jax.ShapeDtypeStruct((bsz, seq, d), BF16),
        grid=(bsz, packs, seq // rows),
        in_specs=u_specs + [
            pl.BlockSpec((1, pw, ns2), lambda b, k, m: (k, 0, 0)),
            pl.BlockSpec((1, ns2, pw), lambda b, k, m: (k, 0, 0)),
            vspec, vspec, vspec, vspec,
            pl.BlockSpec((1, 1, pw), lambda b, k, m: (k, 0, 0)),
        ],
        out_specs=pl.BlockSpec((1, rows, pw), lambda b, k, m: (b, m, k)),
        scratch_shapes=[
            pltpu.VMEM((rows, ns2), F32),
            pltpu.VMEM((1, ns2), F32),
            pltpu.VMEM((SUBLANES, ns2), F32),
            pltpu.VMEM((lane_tiles, rows, LANES), F32),
        ],
        compiler_params=_params(("parallel", "parallel", "arbitrary")),
        name="s5_scan",
    )(*([u] * lane_tiles), b_blk, c_blk, lre, lim, ltre, ltim, d_skip.reshape(packs, 1, pw).astype(F32))


def _layer_norm_rows(tiles, lng_ref, lnb_ref, o_ref, d):
    tn = tiles[0].shape[1]
    tot = tiles[0].sum(axis=1, keepdims=True)
    for t in tiles[1:]:
        tot = tot + t.sum(axis=1, keepdims=True)
    mu = tot * (1.0 / d)
    sq = jnp.square(tiles[0] - mu).sum(axis=1, keepdims=True)
    for t in tiles[1:]:
        sq = sq + jnp.square(t - mu).sum(axis=1, keepdims=True)
    inv = lax.rsqrt(sq * (1.0 / d) + LN_EPS)
    for k, t in enumerate(tiles):
        cs = slice(k * tn, (k + 1) * tn)
        o_ref[0, :, cs] = ((t - mu) * inv * lng_ref[:, cs] + lnb_ref[:, cs]).astype(o_ref.dtype)


def _mmln_kernel(*refs, glu, nj, alpha, d):
    if glu:
        a_ref, w1_ref, w2_ref, b1_ref, b2_ref, x_ref, g_ref, lng_ref, lnb_ref, o_ref, r_scr = refs
    else:
        a_ref, w1_ref, x_ref, g_ref, lng_ref, lnb_ref, o_ref, r_scr = refs
    j = pl.program_id(2)
    for rs in _row_halves(a_ref.shape[1]):
        a = a_ref[0, rs, :]
        y = jnp.dot(a, w1_ref[...], preferred_element_type=F32)
        if glu:
            y = y + b1_ref[...]
            gate = jnp.dot(a, w2_ref[...], preferred_element_type=F32) + b2_ref[...]
            y = y * jax.nn.sigmoid(gate)
        r_scr[j, rs, :] = alpha * x_ref[0, rs, :] + (1.0 + g_ref[0]) * y

    @pl.when(j == nj - 1)
    def _():
        _layer_norm_rows([r_scr[t] for t in range(nj)], lng_ref, lnb_ref, o_ref, d)


def _matmul_res_ln(a, w, bias, xres, gate, ln_g, ln_b, alpha, glu):
    bsz, seq, k = a.shape
    d = xres.shape[-1]
    if glu:
        tm = _pick(seq, (1024, 512, 256, 128))
        tn = _pick(d, (512, 256, 128))
    else:
        tm = _pick(seq, (512, 256, 128))
        tn = d if 2 * k * d * w.dtype.itemsize <= RESIDENT_WEIGHT_BYTES else _pick(d, (512, 256, 128))
    nj = d // tn
    a_spec = pl.BlockSpec((1, tm, k), lambda b, i, j: (b, i, 0))
    w1_spec = pl.BlockSpec((k, tn), lambda b, i, j: (0, j))
    tail_specs = [
        pl.BlockSpec((1, tm, tn), lambda b, i, j: (b, i, j)),
        pl.BlockSpec((1, 1, tn), lambda b, i, j: (b, 0, j)),
        pl.BlockSpec((1, d), lambda b, i, j: (0, 0)),
        pl.BlockSpec((1, d), lambda b, i, j: (0, 0)),
    ]
    tail = (xres, gate, ln_g.reshape(1, d).astype(F32), ln_b.reshape(1, d).astype(F32))
    if glu:
        in_specs = [a_spec, w1_spec,
                    pl.BlockSpec((k, tn), lambda b, i, j: (0, j + nj)),
                    pl.BlockSpec((1, tn), lambda b, i, j: (0, j)),
                    pl.BlockSpec((1, tn), lambda b, i, j: (0, j + nj))] + tail_specs
        b2d = bias.reshape(1, 2 * d).astype(F32)
        args = (a, w, w, b2d, b2d) + tail
    else:
        in_specs = [a_spec, w1_spec] + tail_specs
        args = (a, w) + tail
    kern = functools.partial(_mmln_kernel, glu=glu, nj=nj, alpha=alpha, d=d)
    return pl.pallas_call(
        kern,
        out_shape=jax.ShapeDtypeStruct((bsz, seq, d), F32),
        grid=(bsz, seq // tm, nj),
        in_specs=in_specs,
        out_specs=pl.BlockSpec((1, tm, d), lambda b, i, j: (b, i, 0)),
        scratch_shapes=[pltpu.VMEM((nj, tm, tn), F32)],
        compiler_params=_params(("parallel", "parallel", "arbitrary")),
        name="matmul_res_ln",
    )(*args)


def _ffn_kernel(x_ref, sc_ref, sh_ref, g_ref, wg_ref, wu_ref, wo_ref, lng_ref, lnb_ref, o_ref,
                h_scr, acc_scr, *, nf, alpha, d, tn):
    f = pl.program_id(2)

    @pl.when(f == 0)
    def _():
        h_scr[...] = _modulate(x_ref, sc_ref, sh_ref)
        acc_scr[...] = jnp.zeros_like(acc_scr)

    h = h_scr[...]
    a_g = jnp.dot(h, wg_ref[0], preferred_element_type=F32)
    a_u = jnp.dot(h, wu_ref[0], preferred_element_type=F32)
    act = (jax.nn.silu(a_g) * a_u).astype(BF16)
    acc_scr[...] += jnp.dot(act, wo_ref[0], preferred_element_type=F32)

    @pl.when(f == nf - 1)
    def _():
        tiles = []
        for k in range(d // tn):
            cs = slice(k * tn, (k + 1) * tn)
            tiles.append(alpha * x_ref[0, :, cs] + (1.0 + g_ref[0, :, cs]) * acc_scr[:, cs])
        _layer_norm_rows(tiles, lng_ref, lnb_ref, o_ref, d)


def _ffn_res_ln(x, sc, sh, gate, w_in, w_out, layer, ln_g, ln_b, alpha):
    bsz, seq, d = x.shape
    dff = w_out.shape[1]
    tm = _pick(seq, (512, 256, 128))
    tf = _pick(dff, (512, 256, 128))
    nf = dff // tf
    tn = _pick(d, (512, 256, 128))
    vec = pl.BlockSpec((1, 1, d), lambda b, i, f: (b, 0, 0))
    kern = functools.partial(_ffn_kernel, nf=nf, alpha=alpha, d=d, tn=tn)
    return pl.pallas_call(
        kern,
        out_shape=jax.ShapeDtypeStruct((bsz, seq, d), F32),
        grid=(bsz, seq // tm, nf),
        in_specs=[
            pl.BlockSpec((1, tm, d), lambda b, i, f: (b, i, 0)),
            vec, vec, vec,
            pl.BlockSpec((1, d, tf), lambda b, i, f: (layer, 0, f)),
            pl.BlockSpec((1, d, tf), lambda b, i, f: (layer, 0, f + nf)),
            pl.BlockSpec((1, tf, d), lambda b, i, f: (layer, f, 0)),
            pl.BlockSpec((1, d), lambda b, i, f: (0, 0)),
            pl.BlockSpec((1, d), lambda b, i, f: (0, 0)),
        ],
        out_specs=pl.BlockSpec((1, tm, d), lambda b, i, f: (b, i, 0)),
        scratch_shapes=[pltpu.VMEM((tm, d), BF16), pltpu.VMEM((tm, d), F32)],
        compiler_params=_params(("parallel", "parallel", "arbitrary")),
        name="ffn_res_ln",
    )(x, sc, sh, gate, w_in, w_in, w_out, ln_g.reshape(1, d).astype(F32), ln_b.reshape(1, d).astype(F32))


def _qkv_kernel(x_ref, sc_ref, sh_ref, w_ref, cos_ref, sin_ref, o_ref, h_scr, *, tiles_per_tensor, q_scale):
    j = pl.program_id(2)

    @pl.when(j == 0)
    def _():
        h_scr[...] = _modulate(x_ref, sc_ref, sh_ref)

    tensor = j // tiles_per_tensor
    scale = jnp.where(tensor == 0, q_scale, 1.0)
    is_v = tensor == 2
    chunk = min(MXU_WIDTH, w_ref.shape[1])
    for rs in _row_halves(h_scr.shape[0]):
        a = jnp.where(is_v, 1.0, cos_ref[0, rs, :] * scale)
        b = jnp.where(is_v, 0.0, sin_ref[0, rs, :] * scale)
        for c in range(w_ref.shape[1] // chunk):
            acc = jnp.dot(h_scr[rs, :], w_ref[:, c * chunk:(c + 1) * chunk], preferred_element_type=F32)
            for hh in range(chunk // HEAD_DIM):
                xc = acc[:, hh * HEAD_DIM:(hh + 1) * HEAD_DIM]
                xc = xc * a + pltpu.roll(xc, HEAD_DIM // 2, 1) * b
                o_ref[0, c * (chunk // HEAD_DIM) + hh, rs, :] = xc.astype(o_ref.dtype)


def _qkv_proj(x, sc, sh, w_qkv, cos_t, sin_t):
    bsz, seq, d = x.shape
    tm = _pick(seq, (1024, 512, 256, 128))
    tn = _pick(d, (2048, 1024, 512, 256, 128))
    hpt = tn // HEAD_DIM
    kern = functools.partial(_qkv_kernel, tiles_per_tensor=d // tn, q_scale=HEAD_DIM ** -0.5 * math.log2(math.e))
    return pl.pallas_call(
        kern,
        out_shape=jax.ShapeDtypeStruct((bsz, 3 * d // HEAD_DIM, seq, HEAD_DIM), BF16),
        grid=(bsz, seq // tm, 3 * d // tn),
        in_specs=[
            pl.BlockSpec((1, tm, d), lambda b, i, j: (b, i, 0)),
            pl.BlockSpec((1, 1, d), lambda b, i, j: (b, 0, 0)),
            pl.BlockSpec((1, 1, d), lambda b, i, j: (b, 0, 0)),
            pl.BlockSpec((d, tn), lambda b, i, j: (0, j)),
            pl.BlockSpec((1, tm, HEAD_DIM), lambda b, i, j: (b, i, 0)),
            pl.BlockSpec((1, tm, HEAD_DIM), lambda b, i, j: (b, i, 0)),
        ],
        out_specs=pl.BlockSpec((1, hpt, tm, HEAD_DIM), lambda b, i, j: (b, j, i, 0)),
        scratch_shapes=[pltpu.VMEM((tm, d), BF16)],
        compiler_params=_params(("parallel", "parallel", "arbitrary")),
        name="dsa_qkv_proj",
    )(x, sc, sh, w_qkv, cos_t, sin_t)


def _idx_kernel(x_ref, sc_ref, sh_ref, w_ref, cos_ref, sin_ref, qi_ref, kia_ref, kib_ref, wi_ref,
                *, n_pair, idx_heads, w_scale):
    h = _modulate(x_ref, sc_ref, sh_ref)
    halves = _row_halves(h.shape[0])
    accs = [jnp.dot(h[rs], w_ref[...], preferred_element_type=F32) for rs in halves]
    lane = lax.broadcasted_iota(I32, (halves[0].stop, LANES), 1)
    first_half = (lane % IDX_DIM) < IDX_DIM // 2

    for rs, acc in zip(halves, accs):
        cos = cos_ref[0, rs, :]
        sin = sin_ref[0, rs, :]

        def rope(xc, cos=cos, sin=sin):
            partner = jnp.where(first_half, pltpu.roll(xc, LANES - IDX_DIM // 2, 1),
                                pltpu.roll(xc, IDX_DIM // 2, 1))
            return xc * cos + partner * sin

        for p in range(n_pair):
            cs = slice(p * LANES, (p + 1) * LANES)
            qi_ref[0, rs, cs] = rope(acc[:, cs]).astype(qi_ref.dtype)
        last = acc[:, n_pair * LANES:]
        ka = jnp.where(lane < IDX_DIM, rope(last), 0.0)
        kia_ref[0, rs, :] = ka.astype(kia_ref.dtype)
        kib_ref[0, rs, :] = pltpu.roll(ka, IDX_DIM, 1).astype(kib_ref.dtype)
        wi_ref[0, rs, :] = jnp.where(lane < idx_heads, pltpu.roll(last, LANES - IDX_DIM, 1), 0.0) * w_scale


def _idx_proj(x, sc, sh, w_idx, cos_t, sin_t, idx_heads):
    bsz, seq, d = x.shape
    n = w_idx.shape[1]
    n_pair = idx_heads // 2
    tm = _pick(seq, (1024, 512, 256, 128))
    kern = functools.partial(_idx_kernel, n_pair=n_pair, idx_heads=idx_heads,
                             w_scale=(idx_heads ** -0.5) * (IDX_DIM ** -0.5))
    row = lambda b, i: (b, i, 0)
    return pl.pallas_call(
        kern,
        out_shape=(
            jax.ShapeDtypeStruct((bsz, seq, n_pair * LANES), BF16),
            jax.ShapeDtypeStruct((bsz, seq, LANES), BF16),
            jax.ShapeDtypeStruct((bsz, seq, LANES), BF16),
            jax.ShapeDtypeStruct((bsz, seq, LANES), F32),
        ),
        grid=(bsz, seq // tm),
        in_specs=[
            pl.BlockSpec((1, tm, d), row),
            pl.BlockSpec((1, 1, d), lambda b, i: (b, 0, 0)),
            pl.BlockSpec((1, 1, d), lambda b, i: (b, 0, 0)),
            pl.BlockSpec((d, n), lambda b, i: (0, 0)),
            pl.BlockSpec((1, tm, LANES), row),
            pl.BlockSpec((1, tm, LANES), row),
        ],
        out_specs=(
            pl.BlockSpec((1, tm, n_pair * LANES), row),
            pl.BlockSpec((1, tm, LANES), row),
            pl.BlockSpec((1, tm, LANES), row),
            pl.BlockSpec((1, tm, LANES), row),
        ),
        compiler_params=_params(("parallel", "parallel")),
        name="dsa_idx_proj",
    )(x, sc, sh, w_idx, cos_t, sin_t)


def _bit_transpose32(words):
    a = list(words)
    j = WORD_BITS // 2
    mask = 0x0000FFFF
    while j:
        k = 0
        while k < WORD_BITS:
            t = (a[k] ^ lax.shift_right_logical(a[k + j], jnp.int32(j))) & jnp.int32(mask)
            a[k] = a[k] ^ t
            a[k + j] = a[k + j] ^ (t << j)
            k = (k + j + 1) & ~j
        j >>= 1
        if j:
            mask = (mask ^ (mask << j)) & 0xFFFFFFFF
    return a


def _dsa_kernel(qb_tab, kb_tab, q_ref, k_ref, v_ref, qi_ref, kia_ref, kib_ref, wi_ref, o_ref,
                keys_scr, planes_scr, thr_scr, tie_scr, ngt_scr, nge_scr,
                m_scr, l_scr, acc_scr, bias_scr, s0_scr, s1_scr, r0_scr, r1_scr,
                *, n_pair, k_top, idx_bits):
    heads, qb_rows, _ = acc_scr.shape
    kb_rows = keys_scr.shape[2]
    qb = qb_tab[pl.program_id(1)]
    kb = kb_tab[pl.program_id(1)]
    q0 = qb * qb_rows
    last_kb = (q0 + qb_rows - 1) // kb_rows
    n_chunk = last_kb + 1
    row = lax.broadcasted_iota(I32, (qb_rows, kb_rows), 0) + q0
    col = lax.broadcasted_iota(I32, (qb_rows, kb_rows), 1)
    nt = (((1,), (1,)), ((), ()))

    @pl.when(kb == 0)
    def _select():
        wi = wi_ref[0]

        def score_chunk(c, carry):
            k0 = pl.multiple_of(c * kb_rows, kb_rows)
            ka = kia_ref[0, pl.ds(k0, kb_rows), :]
            kbm = kib_ref[0, pl.ds(k0, kb_rows), :]
            sc = jnp.zeros((qb_rows, kb_rows), F32)
            for p in range(n_pair):
                qp = qi_ref[0, :, p * LANES:(p + 1) * LANES]
                sa = lax.dot_general(qp, ka, nt, preferred_element_type=F32)
                sb = lax.dot_general(qp, kbm, nt, preferred_element_type=F32)
                sc = sc + wi[:, 2 * p:2 * p + 1] * jnp.maximum(sa, 0.0)
                sc = sc + wi[:, 2 * p + 1:2 * p + 2] * jnp.maximum(sb, 0.0)
            bits = lax.bitcast_convert_type(sc, I32)
            key = bits ^ ((bits >> 31) & INT_MAX)
            keys_scr[c] = jnp.where(col + k0 <= row, key, INT_MIN)
            return carry

        lax.fori_loop(0, n_chunk, score_chunk, 0)

        slices_per_chunk = kb_rows // LANES
        chunks_per_set = WORD_BITS // slices_per_chunk
        n_slices = keys_scr.shape[0] * slices_per_chunk
        n_sets = planes_scr.shape[2] // LANES
        live_sets = (n_chunk + chunks_per_set - 1) // chunks_per_set

        def fill_chunk(c, carry):
            keys_scr[c] = jnp.full((qb_rows, kb_rows), INT_MIN, I32)
            return carry

        lax.fori_loop(n_chunk, jnp.minimum(live_sets * chunks_per_set, keys_scr.shape[0]), fill_chunk, 0)

        def pack_rows(g, carry, st):
            r0 = pl.multiple_of(g * SUBLANES, SUBLANES)
            words = []
            for s in range(WORD_BITS):
                sl = st * WORD_BITS + s
                if sl < n_slices:
                    c, off = divmod(sl, slices_per_chunk)
                    words.append(keys_scr[c, pl.ds(r0, SUBLANES), off * LANES:(off + 1) * LANES])
                else:
                    words.append(jnp.full((SUBLANES, LANES), INT_MIN, I32))
            words = _bit_transpose32(words)
            words[0] = ~words[0]
            for i in range(WORD_BITS):
                planes_scr[i, pl.ds(r0, SUBLANES), st * LANES:(st + 1) * LANES] = words[i]
            return carry

        for st in range(n_sets):
            @pl.when(st < live_sets)
            def _(st=st):
                lax.fori_loop(0, qb_rows // SUBLANES, functools.partial(pack_rows, st=st), 0)

            @pl.when(st >= live_sets)
            def _(st=st):
                planes_scr[:, :, st * LANES:(st + 1) * LANES] = jnp.zeros((WORD_BITS, qb_rows, LANES), I32)

        lane_ones = jnp.ones((LANES, LANES), BF16)

        def row_total(pc):
            tot = pc[:, :LANES]
            for st in range(1, n_sets):
                tot = tot + pc[:, st * LANES:(st + 1) * LANES]
            return jnp.dot(tot.astype(F32).astype(BF16), lane_ones, preferred_element_type=F32)

        def all_sets(mask):
            return jnp.concatenate([mask] * n_sets, axis=1)

        def select_two_bits(i, carry):
            cand, above, prefix = carry
            hi = planes_scr[2 * i]
            lo = planes_scr[2 * i + 1]
            c1 = cand & hi
            c0 = cand ^ c1
            c11 = c1 & lo
            c10 = c1 ^ c11
            c01 = c0 & lo
            c00 = c0 ^ c01
            r11 = above + row_total(lax.population_count(c11))
            r10 = r11 + row_total(lax.population_count(c10))
            r01 = r10 + row_total(lax.population_count(c01))
            t11 = r11 >= k_top
            t10 = r10 >= k_top
            t01 = r01 >= k_top
            cand = jnp.where(all_sets(t11), c11,
                             jnp.where(all_sets(t10), c10, jnp.where(all_sets(t01), c01, c00)))
            above = jnp.where(t11, above, jnp.where(t10, r11, jnp.where(t01, r10, r01)))
            bits = jnp.where(t11, 3, jnp.where(t10, 2, jnp.where(t01, 1, 0)))
            prefix = prefix | (bits << (WORD_BITS - 2 - 2 * i))
            return cand, above, prefix

        cand, above, prefix = lax.fori_loop(
            0, WORD_BITS // 2, select_two_bits,
            (jnp.concatenate([jnp.full((qb_rows, LANES), jnp.where(st < live_sets, -1, 0), I32)
                              for st in range(n_sets)], axis=1),
             jnp.zeros((qb_rows, LANES), F32), jnp.zeros((qb_rows, LANES), I32)))
        equal = row_total(lax.population_count(cand))
        thr_scr[...] = (prefix ^ INT_MIN)[:, :1]
        tie_scr[...] = jnp.full((qb_rows, 1), INT_MAX, I32)
        ngt_scr[...] = above[:, :1]
        nge_scr[...] = (above + equal)[:, :1]

        rg = min(qb_rows, SEARCH_ROWS)
        lane_col = lax.broadcasted_iota(I32, (rg, LANES), 1)

        for r in range(qb_rows // rg):
            rs = slice(r * rg, (r + 1) * rg)

            def count(pred, rs=rs):
                def body(c, acc):
                    for t in range(kb_rows // LANES):
                        kk = keys_scr[c, rs, t * LANES:(t + 1) * LANES]
                        idx = lane_col + (c * kb_rows + t * LANES)
                        acc = acc + jnp.where(pred(kk, idx), 1.0, 0.0)
                    return acc
                acc = lax.fori_loop(0, n_chunk, body, jnp.zeros((rg, LANES), F32))
                return jnp.sum(acc, axis=1, keepdims=True)

            def wide(v):
                return jnp.broadcast_to(v, (rg, LANES))

            @pl.when(jnp.max(nge_scr[rs, :]) > k_top)
            def _ties(count=count, rs=rs):
                thr_w = wide(thr_scr[rs, :])
                need = k_top - ngt_scr[rs, :]

                def tie_bit(i, jt):
                    cand = jt + (jnp.int32(1) << (idx_bits - 1 - i))
                    cand_w = wide(cand)
                    cnt = count(lambda kk, idx: (kk == thr_w) & (idx < cand_w))
                    return jnp.where(cnt < need, cand, jt)

                tie_scr[rs, :] = lax.fori_loop(0, idx_bits, tie_bit, jnp.zeros((rg, 1), I32))

        m_scr[...] = jnp.full(m_scr.shape, M_INIT, F32)
        l_scr[...] = jnp.zeros_like(l_scr)
        acc_scr[...] = jnp.zeros_like(acc_scr)

    def _attend():
        ones = jnp.ones((kb_rows, HEAD_DIM), BF16)

        kk = keys_scr[kb]
        thr = thr_scr[...]
        lim = jnp.minimum(tie_scr[...], row[:, :1])
        tie_bias = jnp.where(col + kb * kb_rows <= lim, 0.0, MASK_VALUE)
        bias_scr[...] = jnp.where(kk > thr, 0.0, jnp.where(kk == thr, tie_bias, MASK_VALUE))

        def logits(h, s_ref, r_ref):
            s = lax.dot_general(q_ref[0, h], k_ref[0, h], nt, preferred_element_type=F32) + bias_scr[...]
            s_ref[...] = s
            r_ref[...] = jnp.broadcast_to(jnp.max(s, axis=1, keepdims=True), r_ref.shape)

        def accumulate(h, s_ref, r_ref):
            m_old = m_scr[h]
            m_new = jnp.maximum(m_old, r_ref[...])
            alpha = jnp.exp2(m_old - m_new)
            p = jnp.concatenate(
                [jnp.exp2(s_ref[:, t * LANES:(t + 1) * LANES] - m_new).astype(BF16)
                 for t in range(kb_rows // LANES)], axis=1)
            v_ext = jnp.concatenate([v_ref[0, h], ones], axis=1)
            pv = jnp.dot(p, v_ext, preferred_element_type=F32)
            acc_scr[h] = alpha * acc_scr[h] + pv[:, :HEAD_DIM]
            l_scr[h] = alpha * l_scr[h] + pv[:, HEAD_DIM:]
            m_scr[h] = m_new

        bufs = ((s0_scr, r0_scr), (s1_scr, r1_scr))
        logits(0, *bufs[0])
        for h in range(heads):
            if h + 1 < heads:
                logits(h + 1, *bufs[(h + 1) % 2])
            accumulate(h, *bufs[h % 2])

    _attend()

    @pl.when(kb == last_kb)
    def _finish():
        for h in range(heads):
            o_ref[0, :, h * HEAD_DIM:(h + 1) * HEAD_DIM] = (acc_scr[h] / l_scr[h]).astype(o_ref.dtype)


def _dsa_attention(qkv, qi, kia, kib, wi, k_top):
    bsz, heads3, seq, _ = qkv.shape
    heads = heads3 // 3
    d = heads * HEAD_DIM
    n_pair = qi.shape[-1] // LANES
    qb_rows = _pick(seq, (256, 128))
    kb_rows = _pick(seq, (512, 256, 128))
    n_kb = seq // kb_rows
    n_sets = -(-seq // (WORD_BITS * LANES))

    pairs = [(i, j) for i in range(seq // qb_rows) for j in range((i * qb_rows + qb_rows - 1) // kb_rows + 1)]
    qb_tab = jnp.asarray([p[0] for p in pairs], I32)
    kb_tab = jnp.asarray([p[1] for p in pairs], I32)

    kern = functools.partial(_dsa_kernel, n_pair=n_pair, k_top=k_top, idx_bits=seq.bit_length())
    grid_spec = pltpu.PrefetchScalarGridSpec(
        num_scalar_prefetch=2,
        grid=(bsz, len(pairs)),
        in_specs=[
            pl.BlockSpec((1, heads, qb_rows, HEAD_DIM), lambda b, s, qt, kt: (b, 0, qt[s], 0)),
            pl.BlockSpec((1, heads, kb_rows, HEAD_DIM), lambda b, s, qt, kt: (b, 1, kt[s], 0)),
            pl.BlockSpec((1, heads, kb_rows, HEAD_DIM), lambda b, s, qt, kt: (b, 2, kt[s], 0)),
            pl.BlockSpec((1, qb_rows, n_pair * LANES), lambda b, s, qt, kt: (b, qt[s], 0)),
            pl.BlockSpec((1, seq, LANES), lambda b, s, qt, kt: (b, 0, 0)),
            pl.BlockSpec((1, seq, LANES), lambda b, s, qt, kt: (b, 0, 0)),
            pl.BlockSpec((1, qb_rows, LANES), lambda b, s, qt, kt: (b, qt[s], 0)),
        ],
        out_specs=pl.BlockSpec((1, qb_rows, d), lambda b, s, qt, kt: (b, qt[s], 0)),
        scratch_shapes=[
            pltpu.VMEM((n_kb, qb_rows, kb_rows), I32),
            pltpu.VMEM((WORD_BITS, qb_rows, n_sets * LANES), I32),
            pltpu.VMEM((qb_rows, 1), I32),
            pltpu.VMEM((qb_rows, 1), I32),
            pltpu.VMEM((qb_rows, 1), F32),
            pltpu.VMEM((qb_rows, 1), F32),
            pltpu.VMEM((heads, qb_rows, LANES), F32),
            pltpu.VMEM((heads, qb_rows, LANES), F32),
            pltpu.VMEM((heads, qb_rows, HEAD_DIM), F32),
            pltpu.VMEM((qb_rows, kb_rows), F32),
            pltpu.VMEM((qb_rows, kb_rows), F32),
            pltpu.VMEM((qb_rows, kb_rows), F32),
            pltpu.VMEM((qb_rows, LANES), F32),
            pltpu.VMEM((qb_rows, LANES), F32),
        ],
    )
    return pl.pallas_call(
        kern,
        out_shape=jax.ShapeDtypeStruct((bsz, seq, d), BF16),
        grid_spec=grid_spec,
        compiler_params=_params(("parallel", "arbitrary")),
        name="dsa_select_attend",
    )(qb_tab, kb_tab, qkv, qkv, qkv, qi, kia, kib, wi)


def _rope_tables(positions, dim):
    inv = 1.0 / (ROPE_THETA ** (jnp.arange(0, dim, 2, dtype=F32) / dim))
    half = dim // 2
    lane = jnp.arange(LANES)
    inv_t = inv[lane % half]
    sign = jnp.where((lane % dim) < half, -1.0, 1.0).astype(F32)
    ang = positions.astype(F32)[..., None] * inv_t
    return jnp.cos(ang), jnp.sin(ang) * sign


def kernel(x, c, positions, ada_w, ada_b, ln_g, ln_b, s5_in_w, s5_a_re, s5_a_im, s5_log_dt, s5_b_re, s5_b_im, s5_c_re, s5_c_im, s5_d, s5_glu_w, s5_glu_b, dsa_in_w, dsa_out_w, ffn_w_in, ffn_w_out):
    bsz, seq, d = x.shape
    depth = ada_w.shape[0]
    alpha = (2.0 * depth) ** 0.25
    idx_heads = (dsa_in_w.shape[-1] - 3 * d - IDX_DIM) // (IDX_DIM + 1)
    k_top = min(TOPK_MAX, seq // 4)
    seg = min(S5_SEG, seq // SUBLANES)

    cos_h, sin_h = _rope_tables(positions, HEAD_DIM)
    cos_i, sin_i = _rope_tables(positions, IDX_DIM)
    mod = _ada_mod(c, ada_w, ada_b)
    ffn_in = ffn_w_in.astype(BF16)
    ffn_out = ffn_w_out.astype(BF16)

    for i in range(depth):
        sh1, sc1, g1, sh2, sc2, g2 = [m[:, None, :] for m in jnp.split(mod[i], 6, axis=-1)]
        j = i // 2
        if i % 2 == 0:
            prep = _s5_discretize(s5_a_re[j], s5_a_im[j], s5_log_dt[j], s5_b_re[j], s5_b_im[j],
                                  s5_c_re[j], s5_c_im[j], seg)
            u = _mod_matmul(x, sc1, sh1, s5_in_w[j].astype(BF16), F32)
            gl = _s5_scan(u, prep, s5_d[j], seg)
            x1 = _matmul_res_ln(gl, s5_glu_w[j].astype(BF16), s5_glu_b[j], x, g1,
                                ln_g[i, 0], ln_b[i, 0], alpha, glu=True)
        else:
            w = dsa_in_w[j].astype(BF16)
            n_qi = idx_heads * IDX_DIM
            pad = jnp.zeros((d, LANES - IDX_DIM - idx_heads), w.dtype)
            w_idx = jnp.concatenate(
                [w[:, 3 * d:3 * d + n_qi], w[:, 3 * d + n_qi + idx_heads:], w[:, 3 * d + n_qi:3 * d + n_qi + idx_heads], pad],
                axis=1)
            qkv = _qkv_proj(x, sc1, sh1, w, cos_h, sin_h)
            qi, kia, kib, wi = _idx_proj(x, sc1, sh1, w_idx, cos_i, sin_i, idx_heads)
            att = _dsa_attention(qkv, qi, kia, kib, wi, k_top)
            x1 = _matmul_res_ln(att, dsa_out_w[j].astype(BF16), None, x, g1,
                                ln_g[i, 0], ln_b[i, 0], alpha, glu=False)
        x = _ffn_res_ln(x1, sc2, sh2, g2, ffn_in, ffn_out, i, ln_g[i, 1], ln_b[i, 1], alpha)
    return x
```

```python
import functools
import math

import jax
import jax.numpy as jnp
from jax import lax
from jax.experimental import pallas as pl
from jax.experimental.pallas import tpu as pltpu

F32 = jnp.float32
BF16 = jnp.bfloat16
I32 = jnp.int32

S5_GROUP = 16
S5_STATE = 64
HEAD_DIM = 128
IDX_DIM = 64
TOPK_MAX = 256
ROPE_THETA = 10000.0
LN_EPS = 1e-5

LANES = 128
SUBLANES = 8
MXU_WIDTH = 256
VMEM_LIMIT_BYTES = 56 * 1024 * 1024
RESIDENT_WEIGHT_BYTES = 16 * 1024 * 1024

S5_PACK_GROUPS = 16
S5_SEG = 256

INT_MIN = -(2 ** 31)
INT_MAX = 2 ** 31 - 1
SEARCH_ROWS = 128
WORD_BITS = 32

MASK_VALUE = -2e30
M_INIT = -1e30


def _pick(n, cands):
    for c in cands:
        if n % c == 0:
            return c
    return n


def _row_halves(rows):
    half = rows // 2
    return (slice(0, half), slice(half, rows))


def _params(sem):
    return pltpu.CompilerParams(dimension_semantics=sem, vmem_limit_bytes=VMEM_LIMIT_BYTES)


def _ada_kernel(c_ref, w_ref, b_ref, o_ref):
    ca = jax.nn.silu(c_ref[...]).astype(BF16)
    o_ref[0] = jnp.dot(ca, w_ref[0].astype(BF16), preferred_element_type=F32) + b_ref[0]


def _ada_mod(c, ada_w, ada_b):
    bsz, d = c.shape
    depth, _, n = ada_w.shape
    rows = SUBLANES * ((bsz + SUBLANES - 1) // SUBLANES)
    cp = jnp.zeros((rows, d), F32).at[:bsz].set(c)
    tn = _pick(n, (1024, 512, 256, 128))
    out = pl.pallas_call(
        _ada_kernel,
        out_shape=jax.ShapeDtypeStruct((depth, rows, n), F32),
        grid=(depth, n // tn),
        in_specs=[
            pl.BlockSpec((rows, d), lambda l, j: (0, 0)),
            pl.BlockSpec((1, d, tn), lambda l, j: (l, 0, j)),
            pl.BlockSpec((1, 1, tn), lambda l, j: (l, 0, j)),
        ],
        out_specs=pl.BlockSpec((1, rows, tn), lambda l, j: (l, 0, j)),
        compiler_params=_params(("arbitrary", "arbitrary")),
        name="ada_mod",
    )(cp, ada_w, ada_b.reshape(depth, 1, n))
    return out[:, :bsz]


def _modulate(x_ref, sc_ref, sh_ref):
    return (x_ref[0] * (1.0 + sc_ref[0]) + sh_ref[0]).astype(BF16)


def _modmm_kernel(x_ref, sc_ref, sh_ref, w_ref, o_ref, h_scr):
    @pl.when(pl.program_id(2) == 0)
    def _():
        h_scr[...] = _modulate(x_ref, sc_ref, sh_ref)

    for rs in _row_halves(h_scr.shape[0]):
        o_ref[0, rs, :] = jnp.dot(h_scr[rs, :], w_ref[...], preferred_element_type=F32).astype(o_ref.dtype)


def _mod_matmul(x, sc, sh, w, out_dtype):
    bsz, seq, d = x.shape
    n = w.shape[1]
    tm = _pick(seq, (512, 256, 128))
    tn = n if 2 * d * n * w.dtype.itemsize <= RESIDENT_WEIGHT_BYTES else _pick(n, (1024, 512, 256, 128))
    return pl.pallas_call(
        _modmm_kernel,
        out_shape=jax.ShapeDtypeStruct((bsz, seq, n), out_dtype),
        grid=(bsz, seq // tm, n // tn),
        in_specs=[
            pl.BlockSpec((1, tm, d), lambda b, i, j: (b, i, 0)),
            pl.BlockSpec((1, 1, d), lambda b, i, j: (b, 0, 0)),
            pl.BlockSpec((1, 1, d), lambda b, i, j: (b, 0, 0)),
            pl.BlockSpec((d, tn), lambda b, i, j: (0, j)),
        ],
        out_specs=pl.BlockSpec((1, tm, tn), lambda b, i, j: (b, i, j)),
        scratch_shapes=[pltpu.VMEM((tm, d), BF16)],
        compiler_params=_params(("parallel", "parallel", "arbitrary")),
        name="mod_matmul",
    )(x, sc, sh, w)


def _s5_kernel(*refs, seg, width, lane_tiles):
    u_refs = refs[:lane_tiles]
    (bb_ref, cb_ref, lre_ref, lim_ref, ltre_ref, ltim_ref, d_ref, o_ref,
     xs_scr, carry_scr, cin_scr, up_scr) = refs[lane_tiles:]
    ns = xs_scr.shape[1] // 2

    @pl.when(pl.program_id(2) == 0)
    def _():
        carry_scr[...] = jnp.zeros_like(carry_scr)

    for t, u_ref in enumerate(u_refs):
        for i in range(seg):
            up_scr[t, i * SUBLANES:(i + 1) * SUBLANES, :] = u_ref[0, pl.ds(i, SUBLANES, stride=seg), :]
    u = jnp.concatenate([up_scr[t] for t in range(len(u_refs))], axis=1)
    halves = _row_halves(u.shape[0])
    for rs in halves:
        xs_scr[rs, :] = jnp.dot(u[rs].astype(BF16), bb_ref[0], preferred_element_type=F32)

    for part in range(ns // width):
        cr = slice(part * width, (part + 1) * width)
        ci = slice(ns + part * width, ns + (part + 1) * width)
        lr = jnp.broadcast_to(lre_ref[0, :, cr], (SUBLANES, width))
        li = jnp.broadcast_to(lim_ref[0, :, cr], (SUBLANES, width))

        def local_step(i, st, cr=cr, ci=ci, lr=lr, li=li):
            sr, si = st
            r0 = pl.multiple_of(i * SUBLANES, SUBLANES)
            nr = lr * sr - li * si + xs_scr[pl.ds(r0, SUBLANES), cr]
            ni = lr * si + li * sr + xs_scr[pl.ds(r0, SUBLANES), ci]
            xs_scr[pl.ds(r0, SUBLANES), cr] = nr
            xs_scr[pl.ds(r0, SUBLANES), ci] = ni
            return nr, ni

        zero = jnp.zeros((SUBLANES, width), F32)
        er, ei = lax.fori_loop(0, seg, local_step, (zero, zero), unroll=2)

        ltr = ltre_ref[0, :, cr]
        lti = ltim_ref[0, :, cr]
        c_r = carry_scr[:, cr]
        c_i = carry_scr[:, ci]
        for s in range(SUBLANES):
            cin_scr[s:s + 1, cr] = c_r
            cin_scr[s:s + 1, ci] = c_i
            e_r = er[s:s + 1, :]
            e_i = ei[s:s + 1, :]
            c_r, c_i = ltr * c_r - lti * c_i + e_r, ltr * c_i + lti * c_r + e_i
        carry_scr[:, cr] = c_r
        carry_scr[:, ci] = c_i

        def carry_step(i, st, cr=cr, ci=ci, lr=lr, li=li):
            pr, pi_ = st
            r0 = pl.multiple_of(i * SUBLANES, SUBLANES)
            nr = lr * pr - li * pi_
            ni = lr * pi_ + li * pr
            xs_scr[pl.ds(r0, SUBLANES), cr] += nr
            xs_scr[pl.ds(r0, SUBLANES), ci] += ni
            return nr, ni

        lax.fori_loop(0, seg, carry_step, (cin_scr[:, cr], cin_scr[:, ci]), unroll=2)

    for rs in halves:
        y = jnp.dot(xs_scr[rs, :].astype(BF16), cb_ref[0], preferred_element_type=F32)
        g = jax.nn.gelu(y + d_ref[0] * u[rs])
        for t in range(len(u_refs)):
            up_scr[t, rs, :] = g[:, t * LANES:(t + 1) * LANES]
    for t in range(len(u_refs)):
        for s in range(SUBLANES):
            o_ref[0, s * seg:(s + 1) * seg, t * LANES:(t + 1) * LANES] = (
                up_scr[t, pl.ds(s, seg, stride=SUBLANES), :].astype(o_ref.dtype))


def _cmul(ar, ai, br, bi):
    return ar * br - ai * bi, ar * bi + ai * br


def _s5_discretize(a_re, a_im, log_dt, b_re, b_im, c_re, c_im, seg):
    g, n = a_re.shape
    p = b_re.shape[-1]
    pg = S5_PACK_GROUPS
    packs = g // pg
    a_re, a_im = a_re.astype(F32), a_im.astype(F32)
    dt = jnp.exp(log_dt.astype(F32))[:, None]
    mag = jnp.exp(a_re * dt)
    lb_re, lb_im = mag * jnp.cos(a_im * dt), mag * jnp.sin(a_im * dt)
    den = a_re * a_re + a_im * a_im
    nr, ni = lb_re - 1.0, lb_im
    f_re = (nr * a_re + ni * a_im) / den
    f_im = (ni * a_re - nr * a_im) / den
    bb_re, bb_im = _cmul(f_re[..., None], f_im[..., None], b_re.astype(F32), b_im.astype(F32))
    lt_re, lt_im = lb_re, lb_im
    for _ in range(int(math.log2(seg))):
        lt_re, lt_im = _cmul(lt_re, lt_im, lt_re, lt_im)
    on_diag = (jnp.arange(pg * p)[:, None] // p) == (jnp.arange(pg * n)[None, :] // n)

    def blk_diag(m):
        tiled = jnp.tile(m.astype(BF16).reshape(packs, pg * p, n), (1, 1, pg))
        return jnp.where(on_diag[None], tiled, 0)

    b_blk = jnp.concatenate([blk_diag(jnp.swapaxes(bb_re, 1, 2)), blk_diag(jnp.swapaxes(bb_im, 1, 2))], axis=-1)
    c_blk = jnp.swapaxes(jnp.concatenate([blk_diag(c_re), blk_diag(-c_im)], axis=-1), 1, 2)

    def vec(m):
        return m.reshape(packs, 1, pg * n)

    return b_blk, c_blk, vec(lb_re), vec(lb_im), vec(lt_re), vec(lt_im)


def _s5_scan(u, prep, d_skip, seg):
    bsz, seq, d = u.shape
    b_blk, c_blk, lre, lim, ltre, ltim = prep
    packs, pw, ns2 = b_blk.shape
    ns = ns2 // 2
    rows = SUBLANES * seg
    width = _pick(ns, (1024, 512, 256, 128))
    lane_tiles = pw // LANES
    kern = functools.partial(_s5_kernel, seg=seg, width=width, lane_tiles=lane_tiles)
    vspec = pl.BlockSpec((1, 1, ns), lambda b, k, m: (k, 0, 0))
    u_specs = [pl.BlockSpec((1, rows, LANES), lambda b, k, m, t=t: (b, m, k * lane_tiles + t))
               for t in range(lane_tiles)]
    return pl.pallas_call(
        kern,
        out_shape=jax.ShapeDtypeStruct((bsz, seq, d), BF16),
        grid=(bsz, packs, seq // rows),
        in_specs=u_specs + [
            pl.BlockSpec((1, pw, ns2), lambda b, k, m: (k, 0, 0)),
            pl.BlockSpec((1, ns2, pw), lambda b, k, m: (k, 0, 0)),
            vspec, vspec, vspec, vspec,
            pl.BlockSpec((1, 1, pw), lambda b, k, m: (k, 0, 0)),
        ],
        out_specs=pl.BlockSpec((1, rows, pw), lambda b, k, m: (b, m, k)),
        scratch_shapes=[
            pltpu.VMEM((rows, ns2), F32),
            pltpu.VMEM((1, ns2), F32),
            pltpu.VMEM((SUBLANES, ns2), F32),
            pltpu.VMEM((lane_tiles, rows, LANES), F32),
        ],
        compiler_params=_params(("parallel", "parallel", "arbitrary")),
        name="s5_scan",
    )(*([u] * lane_tiles), b_blk, c_blk, lre, lim, ltre, ltim, d_skip.reshape(packs, 1, pw).astype(F32))


def _layer_norm_rows(tiles, lng_ref, lnb_ref, o_ref, d):
    tn = tiles[0].shape[1]
    tot = tiles[0].sum(axis=1, keepdims=True)
    for t in tiles[1:]:
        tot = tot + t.sum(axis=1, keepdims=True)
    mu = tot * (1.0 / d)
    sq = jnp.square(tiles[0] - mu).sum(axis=1, keepdims=True)
    for t in tiles[1:]:
        sq = sq + jnp.square(t - mu).sum(axis=1, keepdims=True)
    inv = lax.rsqrt(sq * (1.0 / d) + LN_EPS)
    for k, t in enumerate(tiles):
        cs = slice(k * tn, (k + 1) * tn)
        o_ref[0, :, cs] = ((t - mu) * inv * lng_ref[:, cs] + lnb_ref[:, cs]).astype(o_ref.dtype)


def _mmln_kernel(*refs, glu, nj, alpha, d):
    if glu:
        a_ref, w1_ref, w2_ref, b1_ref, b2_ref, x_ref, g_ref, lng_ref, lnb_ref, o_ref, r_scr = refs
    else:
        a_ref, w1_ref, x_ref, g_ref, lng_ref, lnb_ref, o_ref, r_scr = refs
    j = pl.program_id(2)
    for rs in _row_halves(a_ref.shape[1]):
        a = a_ref[0, rs, :]
        y = jnp.dot(a, w1_ref[...], preferred_element_type=F32)
        if glu:
            y = y + b1_ref[...]
            gate = jnp.dot(a, w2_ref[...], preferred_element_type=F32) + b2_ref[...]
            y = y * jax.nn.sigmoid(gate)
        r_scr[j, rs, :] = alpha * x_ref[0, rs, :] + (1.0 + g_ref[0]) * y

    @pl.when(j == nj - 1)
    def _():
        _layer_norm_rows([r_scr[t] for t in range(nj)], lng_ref, lnb_ref, o_ref, d)


def _matmul_res_ln(a, w, bias, xres, gate, ln_g, ln_b, alpha, glu):
    bsz, seq, k = a.shape
    d = xres.shape[-1]
    if glu:
        tm = _pick(seq, (1024, 512, 256, 128))
        tn = _pick(d, (512, 256, 128))
    else:
        tm = _pick(seq, (512, 256, 128))
        tn = d if 2 * k * d * w.dtype.itemsize <= RESIDENT_WEIGHT_BYTES else _pick(d, (512, 256, 128))
    nj = d // tn
    a_spec = pl.BlockSpec((1, tm, k), lambda b, i, j: (b, i, 0))
    w1_spec = pl.BlockSpec((k, tn), lambda b, i, j: (0, j))
    tail_specs = [
        pl.BlockSpec((1, tm, tn), lambda b, i, j: (b, i, j)),
        pl.BlockSpec((1, 1, tn), lambda b, i, j: (b, 0, j)),
        pl.BlockSpec((1, d), lambda b, i, j: (0, 0)),
        pl.BlockSpec((1, d), lambda b, i, j: (0, 0)),
    ]
    tail = (xres, gate, ln_g.reshape(1, d).astype(F32), ln_b.reshape(1, d).astype(F32))
    if glu:
        in_specs = [a_spec, w1_spec,
                    pl.BlockSpec((k, tn), lambda b, i, j: (0, j + nj)),
                    pl.BlockSpec((1, tn), lambda b, i, j: (0, j)),
                    pl.BlockSpec((1, tn), lambda b, i, j: (0, j + nj))] + tail_specs
        b2d = bias.reshape(1, 2 * d).astype(F32)
        args = (a, w, w, b2d, b2d) + tail
    else:
        in_specs = [a_spec, w1_spec] + tail_specs
        args = (a, w) + tail
    kern = functools.partial(_mmln_kernel, glu=glu, nj=nj, alpha=alpha, d=d)
    return pl.pallas_call(
        kern,
        out_shape=jax.ShapeDtypeStruct((bsz, seq, d), F32),
        grid=(bsz, seq // tm, nj),
        in_specs=in_specs,
        out_specs=pl.BlockSpec((1, tm, d), lambda b, i, j: (b, i, 0)),
        scratch_shapes=[pltpu.VMEM((nj, tm, tn), F32)],
        compiler_params=_params(("parallel", "parallel", "arbitrary")),
        name="matmul_res_ln",
    )(*args)


def _ffn_kernel(x_ref, sc_ref, sh_ref, g_ref, wg_ref, wu_ref, wo_ref, lng_ref, lnb_ref, o_ref,
                h_scr, acc_scr, *, nf, alpha, d, tn):
    f = pl.program_id(2)

    @pl.when(f == 0)
    def _():
        h_scr[...] = _modulate(x_ref, sc_ref, sh_ref)
        acc_scr[...] = jnp.zeros_like(acc_scr)

    h = h_scr[...]
    a_g = jnp.dot(h, wg_ref[0], preferred_element_type=F32)
    a_u = jnp.dot(h, wu_ref[0], preferred_element_type=F32)
    act = (jax.nn.silu(a_g) * a_u).astype(BF16)
    acc_scr[...] += jnp.dot(act, wo_ref[0], preferred_element_type=F32)

    @pl.when(f == nf - 1)
    def _():
        tiles = []
        for k in range(d // tn):
            cs = slice(k * tn, (k + 1) * tn)
            tiles.append(alpha * x_ref[0, :, cs] + (1.0 + g_ref[0, :, cs]) * acc_scr[:, cs])
        _layer_norm_rows(tiles, lng_ref, lnb_ref, o_ref, d)


def _ffn_res_ln(x, sc, sh, gate, w_in, w_out, layer, ln_g, ln_b, alpha):
    bsz, seq, d = x.shape
    dff = w_out.shape[1]
    tm = _pick(seq, (512, 256, 128))
    tf = _pick(dff, (512, 256, 128))
    nf = dff // tf
    tn = _pick(d, (512, 256, 128))
    vec = pl.BlockSpec((1, 1, d), lambda b, i, f: (b, 0, 0))
    kern = functools.partial(_ffn_kernel, nf=nf, alpha=alpha, d=d, tn=tn)
    return pl.pallas_call(
        kern,
        out_shape=jax.ShapeDtypeStruct((bsz, seq, d), F32),
        grid=(bsz, seq // tm, nf),
        in_specs=[
            pl.BlockSpec((1, tm, d), lambda b, i, f: (b, i, 0)),
            vec, vec, vec,
            pl.BlockSpec((1, d, tf), lambda b, i, f: (layer, 0, f)),
            pl.BlockSpec((1, d, tf), lambda b, i, f: (layer, 0, f + nf)),
            pl.BlockSpec((1, tf, d), lambda b, i, f: (layer, f, 0)),
            pl.BlockSpec((1, d), lambda b, i, f: (0, 0)),
            pl.BlockSpec((1, d), lambda b, i, f: (0, 0)),
        ],
        out_specs=pl.BlockSpec((1, tm, d), lambda b, i, f: (b, i, 0)),
        scratch_shapes=[pltpu.VMEM((tm, d), BF16), pltpu.VMEM((tm, d), F32)],
        compiler_params=_params(("parallel", "parallel", "arbitrary")),
        name="ffn_res_ln",
    )(x, sc, sh, gate, w_in, w_in, w_out, ln_g.reshape(1, d).astype(F32), ln_b.reshape(1, d).astype(F32))


def _qkv_kernel(x_ref, sc_ref, sh_ref, w_ref, cos_ref, sin_ref, o_ref, h_scr, *, tiles_per_tensor, q_scale):
    j = pl.program_id(2)

    @pl.when(j == 0)
    def _():
        h_scr[...] = _modulate(x_ref, sc_ref, sh_ref)

    tensor = j // tiles_per_tensor
    scale = jnp.where(tensor == 0, q_scale, 1.0)
    is_v = tensor == 2
    chunk = min(MXU_WIDTH, w_ref.shape[1])
    for rs in _row_halves(h_scr.shape[0]):
        a = jnp.where(is_v, 1.0, cos_ref[0, rs, :] * scale)
        b = jnp.where(is_v, 0.0, sin_ref[0, rs, :] * scale)
        for c in range(w_ref.shape[1] // chunk):
            acc = jnp.dot(h_scr[rs, :], w_ref[:, c * chunk:(c + 1) * chunk], preferred_element_type=F32)
            for hh in range(chunk // HEAD_DIM):
                xc = acc[:, hh * HEAD_DIM:(hh + 1) * HEAD_DIM]
                xc = xc * a + pltpu.roll(xc, HEAD_DIM // 2, 1) * b
                o_ref[0, c * (chunk // HEAD_DIM) + hh, rs, :] = xc.astype(o_ref.dtype)


def _qkv_proj(x, sc, sh, w_qkv, cos_t, sin_t):
    bsz, seq, d = x.shape
    tm = _pick(seq, (1024, 512, 256, 128))
    tn = _pick(d, (2048, 1024, 512, 256, 128))
    hpt = tn // HEAD_DIM
    kern = functools.partial(_qkv_kernel, tiles_per_tensor=d // tn, q_scale=HEAD_DIM ** -0.5 * math.log2(math.e))
    return pl.pallas_call(
        kern,
        out_shape=jax.ShapeDtypeStruct((bsz, 3 * d // HEAD_DIM, seq, HEAD_DIM), BF16),
        grid=(bsz, seq // tm, 3 * d // tn),
        in_specs=[
            pl.BlockSpec((1, tm, d), lambda b, i, j: (b, i, 0)),
            pl.BlockSpec((1, 1, d), lambda b, i, j: (b, 0, 0)),
            pl.BlockSpec((1, 1, d), lambda b, i, j: (b, 0, 0)),
            pl.BlockSpec((d, tn), lambda b, i, j: (0, j)),
            pl.BlockSpec((1, tm, HEAD_DIM), lambda b, i, j: (b, i, 0)),
            pl.BlockSpec((1, tm, HEAD_DIM), lambda b, i, j: (b, i, 0)),
        ],
        out_specs=pl.BlockSpec((1, hpt, tm, HEAD_DIM), lambda b, i, j: (b, j, i, 0)),
        scratch_shapes=[pltpu.VMEM((tm, d), BF16)],
        compiler_params=_params(("parallel", "parallel", "arbitrary")),
        name="dsa_qkv_proj",
    )(x, sc, sh, w_qkv, cos_t, sin_t)


def _idx_kernel(x_ref, sc_ref, sh_ref, w_ref, cos_ref, sin_ref, qi_ref, kia_ref, kib_ref, wi_ref,
                *, n_pair, idx_heads, w_scale):
    h = _modulate(x_ref, sc_ref, sh_ref)
    halves = _row_halves(h.shape[0])
    accs = [jnp.dot(h[rs], w_ref[...], preferred_element_type=F32) for rs in halves]
    lane = lax.broadcasted_iota(I32, (halves[0].stop, LANES), 1)
    first_half = (lane % IDX_DIM) < IDX_DIM // 2

    for rs, acc in zip(halves, accs):
        cos = cos_ref[0, rs, :]
        sin = sin_ref[0, rs, :]

        def rope(xc, cos=cos, sin=sin):
            partner = jnp.where(first_half, pltpu.roll(xc, LANES - IDX_DIM // 2, 1),
                                pltpu.roll(xc, IDX_DIM // 2, 1))
            return xc * cos + partner * sin

        for p in range(n_pair):
            cs = slice(p * LANES, (p + 1) * LANES)
            qi_ref[0, rs, cs] = rope(acc[:, cs]).astype(qi_ref.dtype)
        last = acc[:, n_pair * LANES:]
        ka = jnp.where(lane < IDX_DIM, rope(last), 0.0)
        kia_ref[0, rs, :] = ka.astype(kia_ref.dtype)
        kib_ref[0, rs, :] = pltpu.roll(ka, IDX_DIM, 1).astype(kib_ref.dtype)
        wi_ref[0, rs, :] = jnp.where(lane < idx_heads, pltpu.roll(last, LANES - IDX_DIM, 1), 0.0) * w_scale


def _idx_proj(x, sc, sh, w_idx, cos_t, sin_t, idx_heads):
    bsz, seq, d = x.shape
    n = w_idx.shape[1]
    n_pair = idx_heads // 2
    tm = _pick(seq, (1024, 512, 256, 128))
    kern = functools.partial(_idx_kernel, n_pair=n_pair, idx_heads=idx_heads,
                             w_scale=(idx_heads ** -0.5) * (IDX_DIM ** -0.5))
    row = lambda b, i: (b, i, 0)
    return pl.pallas_call(
        kern,
        out_shape=(
            jax.ShapeDtypeStruct((bsz, seq, n_pair * LANES), BF16),
            jax.ShapeDtypeStruct((bsz, seq, LANES), BF16),
            jax.ShapeDtypeStruct((bsz, seq, LANES), BF16),
            jax.ShapeDtypeStruct((bsz, seq, LANES), F32),
        ),
        grid=(bsz, seq // tm),
        in_specs=[
            pl.BlockSpec((1, tm, d), row),
            pl.BlockSpec((1, 1, d), lambda b, i: (b, 0, 0)),
            pl.BlockSpec((1, 1, d), lambda b, i: (b, 0, 0)),
            pl.BlockSpec((d, n), lambda b, i: (0, 0)),
            pl.BlockSpec((1, tm, LANES), row),
            pl.BlockSpec((1, tm, LANES), row),
        ],
        out_specs=(
            pl.BlockSpec((1, tm, n_pair * LANES), row),
            pl.BlockSpec((1, tm, LANES), row),
            pl.BlockSpec((1, tm, LANES), row),
            pl.BlockSpec((1, tm, LANES), row),
        ),
        compiler_params=_params(("parallel", "parallel")),
        name="dsa_idx_proj",
    )(x, sc, sh, w_idx, cos_t, sin_t)


def _bit_transpose32(words):
    a = list(words)
    j = WORD_BITS // 2
    mask = 0x0000FFFF
    while j:
        k = 0
        while k < WORD_BITS:
            t = (a[k] ^ lax.shift_right_logical(a[k + j], jnp.int32(j))) & jnp.int32(mask)
            a[k] = a[k] ^ t
            a[k + j] = a[k + j] ^ (t << j)
            k = (k + j + 1) & ~j
        j >>= 1
        if j:
            mask = (mask ^ (mask << j)) & 0xFFFFFFFF
    return a


def _dsa_kernel(qb_tab, kb_tab, q_ref, k_ref, v_ref, qi_ref, kia_ref, kib_ref, wi_ref, o_ref,
                keys_scr, planes_scr, thr_scr, tie_scr, ngt_scr, nge_scr,
                m_scr, l_scr, acc_scr, bias_scr, s0_scr, s1_scr, r0_scr, r1_scr,
                *, n_pair, k_top, idx_bits):
    heads, qb_rows, _ = acc_scr.shape
    kb_rows = keys_scr.shape[2]
    qb = qb_tab[pl.program_id(1)]
    kb = kb_tab[pl.program_id(1)]
    q0 = qb * qb_rows
    last_kb = (q0 + qb_rows - 1) // kb_rows
    n_chunk = last_kb + 1
    row = lax.broadcasted_iota(I32, (qb_rows, kb_rows), 0) + q0
    col = lax.broadcasted_iota(I32, (qb_rows, kb_rows), 1)
    nt = (((1,), (1,)), ((), ()))

    @pl.when(kb == 0)
    def _select():
        wi = wi_ref[0]

        def score_chunk(c, carry):
            k0 = pl.multiple_of(c * kb_rows, kb_rows)
            ka = kia_ref[0, pl.ds(k0, kb_rows), :]
            kbm = kib_ref[0, pl.ds(k0, kb_rows), :]
            sc = jnp.zeros((qb_rows, kb_rows), F32)
            for p in range(n_pair):
                qp = qi_ref[0, :, p * LANES:(p + 1) * LANES]
                sa = lax.dot_general(qp, ka, nt, preferred_element_type=F32)
                sb = lax.dot_general(qp, kbm, nt, preferred_element_type=F32)
                sc = sc + wi[:, 2 * p:2 * p + 1] * jnp.maximum(sa, 0.0)
                sc = sc + wi[:, 2 * p + 1:2 * p + 2] * jnp.maximum(sb, 0.0)
            bits = lax.bitcast_convert_type(sc, I32)
            key = bits ^ ((bits >> 31) & INT_MAX)
            keys_scr[c] = jnp.where(col + k0 <= row, key, INT_MIN)
            return carry

        lax.fori_loop(0, n_chunk, score_chunk, 0)

        slices_per_chunk = kb_rows // LANES
        chunks_per_set = WORD_BITS // slices_per_chunk
        n_slices = keys_scr.shape[0] * slices_per_chunk
        n_sets = planes_scr.shape[2] // LANES
        live_sets = (n_chunk + chunks_per_set - 1) // chunks_per_set

        def fill_chunk(c, carry):
            keys_scr[c] = jnp.full((qb_rows, kb_rows), INT_MIN, I32)
            return carry

        lax.fori_loop(n_chunk, jnp.minimum(live_sets * chunks_per_set, keys_scr.shape[0]), fill_chunk, 0)

        def pack_rows(g, carry, st):
            r0 = pl.multiple_of(g * SUBLANES, SUBLANES)
            words = []
            for s in range(WORD_BITS):
                sl = st * WORD_BITS + s
                if sl < n_slices:
                    c, off = divmod(sl, slices_per_chunk)
                    words.append(keys_scr[c, pl.ds(r0, SUBLANES), off * LANES:(off + 1) * LANES])
                else:
                    words.append(jnp.full((SUBLANES, LANES), INT_MIN, I32))
            words = _bit_transpose32(words)
            words[0] = ~words[0]
            for i in range(WORD_BITS):
                planes_scr[i, pl.ds(r0, SUBLANES), st * LANES:(st + 1) * LANES] = words[i]
            return carry

        for st in range(n_sets):
            @pl.when(st < live_sets)
            def _(st=st):
                lax.fori_loop(0, qb_rows // SUBLANES, functools.partial(pack_rows, st=st), 0)

            @pl.when(st >= live_sets)
            def _(st=st):
                planes_scr[:, :, st * LANES:(st + 1) * LANES] = jnp.zeros((WORD_BITS, qb_rows, LANES), I32)

        lane_ones = jnp.ones((LANES, LANES), BF16)

        def row_total(pc):
            tot = pc[:, :LANES]
            for st in range(1, n_sets):
                tot = tot + pc[:, st * LANES:(st + 1) * LANES]
            return jnp.dot(tot.astype(F32).astype(BF16), lane_ones, preferred_element_type=F32)

        def all_sets(mask):
            return jnp.concatenate([mask] * n_sets, axis=1)

        def select_two_bits(i, carry):
            cand, above, prefix = carry
            hi = planes_scr[2 * i]
            lo = planes_scr[2 * i + 1]
            c1 = cand & hi
            c0 = cand ^ c1
            c11 = c1 & lo
            c10 = c1 ^ c11
            c01 = c0 & lo
            c00 = c0 ^ c01
            r11 = above + row_total(lax.population_count(c11))
            r10 = r11 + row_total(lax.population_count(c10))
            r01 = r10 + row_total(lax.population_count(c01))
            t11 = r11 >= k_top
            t10 = r10 >= k_top
            t01 = r01 >= k_top
            cand = jnp.where(all_sets(t11), c11,
                             jnp.where(all_sets(t10), c10, jnp.where(all_sets(t01), c01, c00)))
            above = jnp.where(t11, above, jnp.where(t10, r11, jnp.where(t01, r10, r01)))
            bits = jnp.where(t11, 3, jnp.where(t10, 2, jnp.where(t01, 1, 0)))
            prefix = prefix | (bits << (WORD_BITS - 2 - 2 * i))
            return cand, above, prefix

        cand, above, prefix = lax.fori_loop(
            0, WORD_BITS // 2, select_two_bits,
            (jnp.concatenate([jnp.full((qb_rows, LANES), jnp.where(st < live_sets, -1, 0), I32)
                              for st in range(n_sets)], axis=1),
             jnp.zeros((qb_rows, LANES), F32), jnp.zeros((qb_rows, LANES), I32)))
        equal = row_total(lax.population_count(cand))
        thr_scr[...] = (prefix ^ INT_MIN)[:, :1]
        tie_scr[...] = jnp.full((qb_rows, 1), INT_MAX, I32)
        ngt_scr[...] = above[:, :1]
        nge_scr[...] = (above + equal)[:, :1]

        rg = min(qb_rows, SEARCH_ROWS)
        lane_col = lax.broadcasted_iota(I32, (rg, LANES), 1)

        for r in range(qb_rows // rg):
            rs = slice(r * rg, (r + 1) * rg)

            def count(pred, rs=rs):
                def body(c, acc):
                    for t in range(kb_rows // LANES):
                        kk = keys_scr[c, rs, t * LANES:(t + 1) * LANES]
                        idx = lane_col + (c * kb_rows + t * LANES)
                        acc = acc + jnp.where(pred(kk, idx), 1.0, 0.0)
                    return acc
                acc = lax.fori_loop(0, n_chunk, body, jnp.zeros((rg, LANES), F32))
                return jnp.sum(acc, axis=1, keepdims=True)

            def wide(v):
                return jnp.broadcast_to(v, (rg, LANES))

            @pl.when(jnp.max(nge_scr[rs, :]) > k_top)
            def _ties(count=count, rs=rs):
                thr_w = wide(thr_scr[rs, :])
                need = k_top - ngt_scr[rs, :]

                def tie_bit(i, jt):
                    cand = jt + (jnp.int32(1) << (idx_bits - 1 - i))
                    cand_w = wide(cand)
                    cnt = count(lambda kk, idx: (kk == thr_w) & (idx < cand_w))
                    return jnp.where(cnt < need, cand, jt)

                tie_scr[rs, :] = lax.fori_loop(0, idx_bits, tie_bit, jnp.zeros((rg, 1), I32))

        m_scr[...] = jnp.full(m_scr.shape, M_INIT, F32)
        l_scr[...] = jnp.zeros_like(l_scr)
        acc_scr[...] = jnp.zeros_like(acc_scr)

    def _attend():
        ones = jnp.ones((kb_rows, HEAD_DIM), BF16)

        kk = keys_scr[kb]
        thr = thr_scr[...]
        lim = jnp.minimum(tie_scr[...], row[:, :1])
        tie_bias = jnp.where(col + kb * kb_rows <= lim, 0.0, MASK_VALUE)
        bias_scr[...] = jnp.where(kk > thr, 0.0, jnp.where(kk == thr, tie_bias, MASK_VALUE))

        def logits(h, s_ref, r_ref):
            s = lax.dot_general(q_ref[0, h], k_ref[0, h], nt, preferred_element_type=F32) + bias_scr[...]
            s_ref[...] = s
            r_ref[...] = jnp.broadcast_to(jnp.max(s, axis=1, keepdims=True), r_ref.shape)

        def accumulate(h, s_ref, r_ref):
            m_old = m_scr[h]
            m_new = jnp.maximum(m_old, r_ref[...])
            alpha = jnp.exp2(m_old - m_new)
            p = jnp.concatenate(
                [jnp.exp2(s_ref[:, t * LANES:(t + 1) * LANES] - m_new).astype(BF16)
                 for t in range(kb_rows // LANES)], axis=1)
            v_ext = jnp.concatenate([v_ref[0, h], ones], axis=1)
            pv = jnp.dot(p, v_ext, preferred_element_type=F32)
            acc_scr[h] = alpha * acc_scr[h] + pv[:, :HEAD_DIM]
            l_scr[h] = alpha * l_scr[h] + pv[:, HEAD_DIM:]
            m_scr[h] = m_new

        bufs = ((s0_scr, r0_scr), (s1_scr, r1_scr))
        logits(0, *bufs[0])
        for h in range(heads):
            if h + 1 < heads:
                logits(h + 1, *bufs[(h + 1) % 2])
            accumulate(h, *bufs[h % 2])

    _attend()

    @pl.when(kb == last_kb)
    def _finish():
        for h in range(heads):
            o_ref[0, :, h * HEAD_DIM:(h + 1) * HEAD_DIM] = (acc_scr[h] / l_scr[h]).astype(o_ref.dtype)


def _dsa_attention(qkv, qi, kia, kib, wi, k_top):
    bsz, heads3, seq, _ = qkv.shape
    heads = heads3 // 3
    d = heads * HEAD_DIM
    n_pair = qi.shape[-1] // LANES
    qb_rows = _pick(seq, (256, 128))
    kb_rows = _pick(seq, (512, 256, 128))
    n_kb = seq // kb_rows
    n_sets = -(-seq // (WORD_BITS * LANES))

    pairs = [(i, j) for i in range(seq // qb_rows) for j in range((i * qb_rows + qb_rows - 1) // kb_rows + 1)]
    qb_tab = jnp.asarray([p[0] for p in pairs], I32)
    kb_tab = jnp.asarray([p[1] for p in pairs], I32)

    kern = functools.partial(_dsa_kernel, n_pair=n_pair, k_top=k_top, idx_bits=seq.bit_length())
    grid_spec = pltpu.PrefetchScalarGridSpec(
        num_scalar_prefetch=2,
        grid=(bsz, len(pairs)),
        in_specs=[
            pl.BlockSpec((1, heads, qb_rows, HEAD_DIM), lambda b, s, qt, kt: (b, 0, qt[s], 0)),
            pl.BlockSpec((1, heads, kb_rows, HEAD_DIM), lambda b, s, qt, kt: (b, 1, kt[s], 0)),
            pl.BlockSpec((1, heads, kb_rows, HEAD_DIM), lambda b, s, qt, kt: (b, 2, kt[s], 0)),
            pl.BlockSpec((1, qb_rows, n_pair * LANES), lambda b, s, qt, kt: (b, qt[s], 0)),
            pl.BlockSpec((1, seq, LANES), lambda b, s, qt, kt: (b, 0, 0)),
            pl.BlockSpec((1, seq, LANES), lambda b, s, qt, kt: (b, 0, 0)),
            pl.BlockSpec((1, qb_rows, LANES), lambda b, s, qt, kt: (b, qt[s], 0)),
        ],
        out_specs=pl.BlockSpec((1, qb_rows, d), lambda b, s, qt, kt: (b, qt[s], 0)),
        scratch_shapes=[
            pltpu.VMEM((n_kb, qb_rows, kb_rows), I32),
            pltpu.VMEM((WORD_BITS, qb_rows, n_sets * LANES), I32),
            pltpu.VMEM((qb_rows, 1), I32),
            pltpu.VMEM((qb_rows, 1), I32),
            pltpu.VMEM((qb_rows, 1), F32),
            pltpu.VMEM((qb_rows, 1), F32),
            pltpu.VMEM((heads, qb_rows, LANES), F32),
            pltpu.VMEM((heads, qb_rows, LANES), F32),
            pltpu.VMEM((heads, qb_rows, HEAD_DIM), F32),
            pltpu.VMEM((qb_rows, kb_rows), F32),
            pltpu.VMEM((qb_rows, kb_rows), F32),
            pltpu.VMEM((qb_rows, kb_rows), F32),
            pltpu.VMEM((qb_rows, LANES), F32),
            pltpu.VMEM((qb_rows, LANES), F32),
        ],
    )
    return pl.pallas_call(
        kern,
        out_shape=jax.ShapeDtypeStruct((bsz, seq, d), BF16),
        grid_spec=grid_spec,
        compiler_params=_params(("parallel", "arbitrary")),
        name="dsa_select_attend",
    )(qb_tab, kb_tab, qkv, qkv, qkv, qi, kia, kib, wi)


def _rope_tables(positions, dim):
    inv = 1.0 / (ROPE_THETA ** (jnp.arange(0, dim, 2, dtype=F32) / dim))
    half = dim // 2
    lane = jnp.arange(LANES)
    inv_t = inv[lane % half]
    sign = jnp.where((lane % dim) < half, -1.0, 1.0).astype(F32)
    ang = positions.astype(F32)[..., None] * inv_t
    return jnp.cos(ang), jnp.sin(ang) * sign


def kernel(x, c, positions, ada_w, ada_b, ln_g, ln_b, s5_in_w, s5_a_re, s5_a_im, s5_log_dt, s5_b_re, s5_b_im, s5_c_re, s5_c_im, s5_d, s5_glu_w, s5_glu_b, dsa_in_w, dsa_out_w, ffn_w_in, ffn_w_out):
    bsz, seq, d = x.shape
    depth = ada_w.shape[0]
    alpha = (2.0 * depth) ** 0.25
    idx_heads = (dsa_in_w.shape[-1] - 3 * d - IDX_DIM) // (IDX_DIM + 1)
    k_top = min(TOPK_MAX, seq // 4)
    seg = min(S5_SEG, seq // SUBLANES)

    cos_h, sin_h = _rope_tables(positions, HEAD_DIM)
    cos_i, sin_i = _rope_tables(positions, IDX_DIM)
    mod = _ada_mod(c, ada_w, ada_b)
    ffn_in = ffn_w_in.astype(BF16)
    ffn_out = ffn_w_out.astype(BF16)

    for i in range(depth):
        sh1, sc1, g1, sh2, sc2, g2 = [m[:, None, :] for m in jnp.split(mod[i], 6, axis=-1)]
        j = i // 2
        if i % 2 == 0:
            prep = _s5_discretize(s5_a_re[j], s5_a_im[j], s5_log_dt[j], s5_b_re[j], s5_b_im[j],
                                  s5_c_re[j], s5_c_im[j], seg)
            u = _mod_matmul(x, sc1, sh1, s5_in_w[j].astype(BF16), F32)
            gl = _s5_scan(u, prep, s5_d[j], seg)
            x1 = _matmul_res_ln(gl, s5_glu_w[j].astype(BF16), s5_glu_b[j], x, g1,
                                ln_g[i, 0], ln_b[i, 0], alpha, glu=True)
        else:
            w = dsa_in_w[j].astype(BF16)
            n_qi = idx_heads * IDX_DIM
            pad = jnp.zeros((d, LANES - IDX_DIM - idx_heads), w.dtype)
            w_idx = jnp.concatenate(
                [w[:, 3 * d:3 * d + n_qi], w[:, 3 * d + n_qi + idx_heads:], w[:, 3 * d + n_qi:3 * d + n_qi + idx_heads], pad],
                axis=1)
            qkv = _qkv_proj(x, sc1, sh1, w, cos_h, sin_h)
            qi, kia, kib, wi = _idx_proj(x, sc1, sh1, w_idx, cos_i, sin_i, idx_heads)
            att = _dsa_attention(qkv, qi, kia, kib, wi, k_top)
            x1 = _matmul_res_ln(att, dsa_out_w[j].astype(BF16), None, x, g1,
                                ln_g[i, 0], ln_b[i, 0], alpha, glu=False)
        x = _ffn_res_ln(x1, sc2, sh2, g2, ffn_in, ffn_out, i, ln_g[i, 1], ln_b[i, 1], alpha)
    return x
```

```python
import functools
import math

import jax
import jax.numpy as jnp
from jax import lax
from jax.experimental import pallas as pl
from jax.experimental.pallas import tpu as pltpu

F32 = jnp.float32
BF16 = jnp.bfloat16
I32 = jnp.int32

S5_GROUP = 16
S5_STATE = 64
HEAD_DIM = 128
IDX_DIM = 64
TOPK_MAX = 256
ROPE_THETA = 10000.0
LN_EPS = 1e-5

LANES = 128
SUBLANES = 8
MXU_WIDTH = 256
VMEM_LIMIT_BYTES = 56 * 1024 * 1024
RESIDENT_WEIGHT_BYTES = 16 * 1024 * 1024

S5_PACK_GROUPS = 16
S5_SEG = 256

INT_MIN = -(2 ** 31)
INT_MAX = 2 ** 31 - 1
SEARCH_ROWS = 128
WORD_BITS = 32

MASK_VALUE = -2e30
M_INIT = -1e30


def _pick(n, cands):
    for c in cands:
        if n % c == 0:
            return c
    return n


def _row_halves(rows):
    half = rows // 2
    return (slice(0, half), slice(half, rows))


def _params(sem):
    return pltpu.CompilerParams(dimension_semantics=sem, vmem_limit_bytes=VMEM_LIMIT_BYTES)


def _ada_kernel(c_ref, w_ref, b_ref, o_ref):
    ca = jax.nn.silu(c_ref[...]).astype(BF16)
    o_ref[0] = jnp.dot(ca, w_ref[0].astype(BF16), preferred_element_type=F32) + b_ref[0]


def _ada_mod(c, ada_w, ada_b):
    bsz, d = c.shape
    depth, _, n = ada_w.shape
    rows = SUBLANES * ((bsz + SUBLANES - 1) // SUBLANES)
    cp = jnp.zeros((rows, d), F32).at[:bsz].set(c)
    tn = _pick(n, (1024, 512, 256, 128))
    out = pl.pallas_call(
        _ada_kernel,
        out_shape=jax.ShapeDtypeStruct((depth, rows, n), F32),
        grid=(depth, n // tn),
        in_specs=[
            pl.BlockSpec((rows, d), lambda l, j: (0, 0)),
            pl.BlockSpec((1, d, tn), lambda l, j: (l, 0, j)),
            pl.BlockSpec((1, 1, tn), lambda l, j: (l, 0, j)),
        ],
        out_specs=pl.BlockSpec((1, rows, tn), lambda l, j: (l, 0, j)),
        compiler_params=_params(("arbitrary", "arbitrary")),
        name="ada_mod",
    )(cp, ada_w, ada_b.reshape(depth, 1, n))
    return out[:, :bsz]


def _modulate(x_ref, sc_ref, sh_ref):
    return (x_ref[0] * (1.0 + sc_ref[0]) + sh_ref[0]).astype(BF16)


def _modmm_kernel(x_ref, sc_ref, sh_ref, w_ref, o_ref, h_scr):
    @pl.when(pl.program_id(2) == 0)
    def _():
        h_scr[...] = _modulate(x_ref, sc_ref, sh_ref)

    for rs in _row_halves(h_scr.shape[0]):
        o_ref[0, rs, :] = jnp.dot(h_scr[rs, :], w_ref[...], preferred_element_type=F32).astype(o_ref.dtype)


def _mod_matmul(x, sc, sh, w, out_dtype):
    bsz, seq, d = x.shape
    n = w.shape[1]
    tm = _pick(seq, (512, 256, 128))
    tn = n if 2 * d * n * w.dtype.itemsize <= RESIDENT_WEIGHT_BYTES else _pick(n, (1024, 512, 256, 128))
    return pl.pallas_call(
        _modmm_kernel,
        out_shape=jax.ShapeDtypeStruct((bsz, seq, n), out_dtype),
        grid=(bsz, seq // tm, n // tn),
        in_specs=[
            pl.BlockSpec((1, tm, d), lambda b, i, j: (b, i, 0)),
            pl.BlockSpec((1, 1, d), lambda b, i, j: (b, 0, 0)),
            pl.BlockSpec((1, 1, d), lambda b, i, j: (b, 0, 0)),
            pl.BlockSpec((d, tn), lambda b, i, j: (0, j)),
        ],
        out_specs=pl.BlockSpec((1, tm, tn), lambda b, i, j: (b, i, j)),
        scratch_shapes=[pltpu.VMEM((tm, d), BF16)],
        compiler_params=_params(("parallel", "parallel", "arbitrary")),
        name="mod_matmul",
    )(x, sc, sh, w)


def _s5_kernel(*refs, seg, width, lane_tiles):
    u_refs = refs[:lane_tiles]
    (bb_ref, cb_ref, lre_ref, lim_ref, ltre_ref, ltim_ref, d_ref, o_ref,
     xs_scr, carry_scr, cin_scr, up_scr) = refs[lane_tiles:]
    ns = xs_scr.shape[1] // 2

    @pl.when(pl.program_id(2) == 0)
    def _():
        carry_scr[...] = jnp.zeros_like(carry_scr)

    for t, u_ref in enumerate(u_refs):
        for i in range(seg):
            up_scr[t, i * SUBLANES:(i + 1) * SUBLANES, :] = u_ref[0, pl.ds(i, SUBLANES, stride=seg), :]
    u = jnp.concatenate([up_scr[t] for t in range(len(u_refs))], axis=1)
    halves = _row_halves(u.shape[0])
    for rs in halves:
        xs_scr[rs, :] = jnp.dot(u[rs].astype(BF16), bb_ref[0], preferred_element_type=F32)

    for part in range(ns // width):
        cr = slice(part * width, (part + 1) * width)
        ci = slice(ns + part * width, ns + (part + 1) * width)
        lr = jnp.broadcast_to(lre_ref[0, :, cr], (SUBLANES, width))
        li = jnp.broadcast_to(lim_ref[0, :, cr], (SUBLANES, width))

        def local_step(i, st, cr=cr, ci=ci, lr=lr, li=li):
            sr, si = st
            r0 = pl.multiple_of(i * SUBLANES, SUBLANES)
            nr = lr * sr - li * si + xs_scr[pl.ds(r0, SUBLANES), cr]
            ni = lr * si + li * sr + xs_scr[pl.ds(r0, SUBLANES), ci]
            xs_scr[pl.ds(r0, SUBLANES), cr] = nr
            xs_scr[pl.ds(r0, SUBLANES), ci] = ni
            return nr, ni

        zero = jnp.zeros((SUBLANES, width), F32)
        er, ei = lax.fori_loop(0, seg, local_step, (zero, zero), unroll=4)

        ltr = ltre_ref[0, :, cr]
        lti = ltim_ref[0, :, cr]
        c_r = carry_scr[:, cr]
        c_i = carry_scr[:, ci]
        for s in range(SUBLANES):
            cin_scr[s:s + 1, cr] = c_r
            cin_scr[s:s + 1, ci] = c_i
            e_r = er[s:s + 1, :]
            e_i = ei[s:s + 1, :]
            c_r, c_i = ltr * c_r - lti * c_i + e_r, ltr * c_i + lti * c_r + e_i
        carry_scr[:, cr] = c_r
        carry_scr[:, ci] = c_i

        def carry_step(i, st, cr=cr, ci=ci, lr=lr, li=li):
            pr, pi_ = st
            r0 = pl.multiple_of(i * SUBLANES, SUBLANES)
            nr = lr * pr - li * pi_
            ni = lr * pi_ + li * pr
            xs_scr[pl.ds(r0, SUBLANES), cr] += nr
            xs_scr[pl.ds(r0, SUBLANES), ci] += ni
            return nr, ni

        lax.fori_loop(0, seg, carry_step, (cin_scr[:, cr], cin_scr[:, ci]), unroll=4)

    for rs in halves:
        y = jnp.dot(xs_scr[rs, :].astype(BF16), cb_ref[0], preferred_element_type=F32)
        g = jax.nn.gelu(y + d_ref[0] * u[rs])
        for t in range(len(u_refs)):
            up_scr[t, rs, :] = g[:, t * LANES:(t + 1) * LANES]
    for t in range(len(u_refs)):
        for s in range(SUBLANES):
            o_ref[0, s * seg:(s + 1) * seg, t * LANES:(t + 1) * LANES] = (
                up_scr[t, pl.ds(s, seg, stride=SUBLANES), :].astype(o_ref.dtype))


def _cmul(ar, ai, br, bi):
    return ar * br - ai * bi, ar * bi + ai * br


def _s5_discretize(a_re, a_im, log_dt, b_re, b_im, c_re, c_im, seg):
    g, n = a_re.shape
    p = b_re.shape[-1]
    pg = S5_PACK_GROUPS
    packs = g // pg
    a_re, a_im = a_re.astype(F32), a_im.astype(F32)
    dt = jnp.exp(log_dt.astype(F32))[:, None]
    mag = jnp.exp(a_re * dt)
    lb_re, lb_im = mag * jnp.cos(a_im * dt), mag * jnp.sin(a_im * dt)
    den = a_re * a_re + a_im * a_im
    nr, ni = lb_re - 1.0, lb_im
    f_re = (nr * a_re + ni * a_im) / den
    f_im = (ni * a_re - nr * a_im) / den
    bb_re, bb_im = _cmul(f_re[..., None], f_im[..., None], b_re.astype(F32), b_im.astype(F32))
    lt_re, lt_im = lb_re, lb_im
    for _ in range(int(math.log2(seg))):
        lt_re, lt_im = _cmul(lt_re, lt_im, lt_re, lt_im)
    on_diag = (jnp.arange(pg * p)[:, None] // p) == (jnp.arange(pg * n)[None, :] // n)

    def blk_diag(m):
        tiled = jnp.tile(m.astype(BF16).reshape(packs, pg * p, n), (1, 1, pg))
        return jnp.where(on_diag[None], tiled, 0)

    b_blk = jnp.concatenate([blk_diag(jnp.swapaxes(bb_re, 1, 2)), blk_diag(jnp.swapaxes(bb_im, 1, 2))], axis=-1)
    c_blk = jnp.swapaxes(jnp.concatenate([blk_diag(c_re), blk_diag(-c_im)], axis=-1), 1, 2)

    def vec(m):
        return m.reshape(packs, 1, pg * n)

    return b_blk, c_blk, vec(lb_re), vec(lb_im), vec(lt_re), vec(lt_im)


def _s5_scan(u, prep, d_skip, seg):
    bsz, seq, d = u.shape
    b_blk, c_blk, lre, lim, ltre, ltim = prep
    packs, pw, ns2 = b_blk.shape
    ns = ns2 // 2
    rows = SUBLANES * seg
    width = _pick(ns, (1024, 512, 256, 128))
    lane_tiles = pw // LANES
    kern = functools.partial(_s5_kernel, seg=seg, width=width, lane_tiles=lane_tiles)
    vspec = pl.BlockSpec((1, 1, ns), lambda b, k, m: (k, 0, 0))
    u_specs = [pl.BlockSpec((1, rows, LANES), lambda b, k, m, t=t: (b, m, k * lane_tiles + t))
               for t in range(lane_tiles)]
    return pl.pallas_call(
        kern,
        out_shape=jax.ShapeDtypeStruct((bsz, seq, d), BF16),
        grid=(bsz, packs, seq // rows),
        in_specs=u_specs + [
            pl.BlockSpec((1, pw, ns2), lambda b, k, m: (k, 0, 0)),
            pl.BlockSpec((1, ns2, pw), lambda b, k, m: (k, 0, 0)),
            vspec, vspec, vspec, vspec,
            pl.BlockSpec((1, 1, pw), lambda b, k, m: (k, 0, 0)),
        ],
        out_specs=pl.BlockSpec((1, rows, pw), lambda b, k, m: (b, m, k)),
        scratch_shapes=[
            pltpu.VMEM((rows, ns2), F32),
            pltpu.VMEM((1, ns2), F32),
            pltpu.VMEM((SUBLANES, ns2), F32),
            pltpu.VMEM((lane_tiles, rows, LANES), F32),
        ],
        compiler_params=_params(("parallel", "parallel", "arbitrary")),
        name="s5_scan",
    )(*([u] * lane_tiles), b_blk, c_blk, lre, lim, ltre, ltim, d_skip.reshape(packs, 1, pw).astype(F32))


def _layer_norm_rows(tiles, lng_ref, lnb_ref, o_ref, d):
    tn = tiles[0].shape[1]
    tot = tiles[0].sum(axis=1, keepdims=True)
    for t in tiles[1:]:
        tot = tot + t.sum(axis=1, keepdims=True)
    mu = tot * (1.0 / d)
    sq = jnp.square(tiles[0] - mu).sum(axis=1, keepdims=True)
    for t in tiles[1:]:
        sq = sq + jnp.square(t - mu).sum(axis=1, keepdims=True)
    inv = lax.rsqrt(sq * (1.0 / d) + LN_EPS)
    for k, t in enumerate(tiles):
        cs = slice(k * tn, (k + 1) * tn)
        o_ref[0, :, cs] = ((t - mu) * inv * lng_ref[:, cs] + lnb_ref[:, cs]).astype(o_ref.dtype)


def _mmln_kernel(*refs, glu, nj, alpha, d):
    if glu:
        a_ref, w1_ref, w2_ref, b1_ref, b2_ref, x_ref, g_ref, lng_ref, lnb_ref, o_ref, r_scr = refs
    else:
        a_ref, w1_ref, x_ref, g_ref, lng_ref, lnb_ref, o_ref, r_scr = refs
    j = pl.program_id(2)
    for rs in _row_halves(a_ref.shape[1]):
        a = a_ref[0, rs, :]
        y = jnp.dot(a, w1_ref[...], preferred_element_type=F32)
        if glu:
            y = y + b1_ref[...]
            gate = jnp.dot(a, w2_ref[...], preferred_element_type=F32) + b2_ref[...]
            y = y * jax.nn.sigmoid(gate)
        r_scr[j, rs, :] = alpha * x_ref[0, rs, :] + (1.0 + g_ref[0]) * y

    @pl.when(j == nj - 1)
    def _():
        _layer_norm_rows([r_scr[t] for t in range(nj)], lng_ref, lnb_ref, o_ref, d)


def _matmul_res_ln(a, w, bias, xres, gate, ln_g, ln_b, alpha, glu):
    bsz, seq, k = a.shape
    d = xres.shape[-1]
    if glu:
        tm = _pick(seq, (1024, 512, 256, 128))
        tn = _pick(d, (512, 256, 128))
    else:
        tm = _pick(seq, (512, 256, 128))
        tn = d if 2 * k * d * w.dtype.itemsize <= RESIDENT_WEIGHT_BYTES else _pick(d, (512, 256, 128))
    nj = d // tn
    a_spec = pl.BlockSpec((1, tm, k), lambda b, i, j: (b, i, 0))
    w1_spec = pl.BlockSpec((k, tn), lambda b, i, j: (0, j))
    tail_specs = [
        pl.BlockSpec((1, tm, tn), lambda b, i, j: (b, i, j)),
        pl.BlockSpec((1, 1, tn), lambda b, i, j: (b, 0, j)),
        pl.BlockSpec((1, d), lambda b, i, j: (0, 0)),
        pl.BlockSpec((1, d), lambda b, i, j: (0, 0)),
    ]
    tail = (xres, gate, ln_g.reshape(1, d).astype(F32), ln_b.reshape(1, d).astype(F32))
    if glu:
        in_specs = [a_spec, w1_spec,
                    pl.BlockSpec((k, tn), lambda b, i, j: (0, j + nj)),
                    pl.BlockSpec((1, tn), lambda b, i, j: (0, j)),
                    pl.BlockSpec((1, tn), lambda b, i, j: (0, j + nj))] + tail_specs
        b2d = bias.reshape(1, 2 * d).astype(F32)
        args = (a, w, w, b2d, b2d) + tail
    else:
        in_specs = [a_spec, w1_spec] + tail_specs
        args = (a, w) + tail
    kern = functools.partial(_mmln_kernel, glu=glu, nj=nj, alpha=alpha, d=d)
    return pl.pallas_call(
        kern,
        out_shape=jax.ShapeDtypeStruct((bsz, seq, d), F32),
        grid=(bsz, seq // tm, nj),
        in_specs=in_specs,
        out_specs=pl.BlockSpec((1, tm, d), lambda b, i, j: (b, i, 0)),
        scratch_shapes=[pltpu.VMEM((nj, tm, tn), F32)],
        compiler_params=_params(("parallel", "parallel", "arbitrary")),
        name="matmul_res_ln",
    )(*args)


def _ffn_kernel(x_ref, sc_ref, sh_ref, g_ref, wg_ref, wu_ref, wo_ref, lng_ref, lnb_ref, o_ref,
                h_scr, acc_scr, *, nf, alpha, d, tn):
    f = pl.program_id(2)

    @pl.when(f == 0)
    def _():
        h_scr[...] = _modulate(x_ref, sc_ref, sh_ref)
        acc_scr[...] = jnp.zeros_like(acc_scr)

    h = h_scr[...]
    a_g = jnp.dot(h, wg_ref[0], preferred_element_type=F32)
    a_u = jnp.dot(h, wu_ref[0], preferred_element_type=F32)
    act = (jax.nn.silu(a_g) * a_u).astype(BF16)
    acc_scr[...] += jnp.dot(act, wo_ref[0], preferred_element_type=F32)

    @pl.when(f == nf - 1)
    def _():
        tiles = []
        for k in range(d // tn):
            cs = slice(k * tn, (k + 1) * tn)
            tiles.append(alpha * x_ref[0, :, cs] + (1.0 + g_ref[0, :, cs]) * acc_scr[:, cs])
        _layer_norm_rows(tiles, lng_ref, lnb_ref, o_ref, d)


def _ffn_res_ln(x, sc, sh, gate, w_in, w_out, layer, ln_g, ln_b, alpha):
    bsz, seq, d = x.shape
    dff = w_out.shape[1]
    tm = _pick(seq, (512, 256, 128))
    tf = _pick(dff, (512, 256, 128))
    nf = dff // tf
    tn = _pick(d, (512, 256, 128))
    vec = pl.BlockSpec((1, 1, d), lambda b, i, f: (b, 0, 0))
    kern = functools.partial(_ffn_kernel, nf=nf, alpha=alpha, d=d, tn=tn)
    return pl.pallas_call(
        kern,
        out_shape=jax.ShapeDtypeStruct((bsz, seq, d), F32),
        grid=(bsz, seq // tm, nf),
        in_specs=[
            pl.BlockSpec((1, tm, d), lambda b, i, f: (b, i, 0)),
            vec, vec, vec,
            pl.BlockSpec((1, d, tf), lambda b, i, f: (layer, 0, f)),
            pl.BlockSpec((1, d, tf), lambda b, i, f: (layer, 0, f + nf)),
            pl.BlockSpec((1, tf, d), lambda b, i, f: (layer, f, 0)),
            pl.BlockSpec((1, d), lambda b, i, f: (0, 0)),
            pl.BlockSpec((1, d), lambda b, i, f: (0, 0)),
        ],
        out_specs=pl.BlockSpec((1, tm, d), lambda b, i, f: (b, i, 0)),
        scratch_shapes=[pltpu.VMEM((tm, d), BF16), pltpu.VMEM((tm, d), F32)],
        compiler_params=_params(("parallel", "parallel", "arbitrary")),
        name="ffn_res_ln",
    )(x, sc, sh, gate, w_in, w_in, w_out, ln_g.reshape(1, d).astype(F32), ln_b.reshape(1, d).astype(F32))


def _qkv_kernel(x_ref, sc_ref, sh_ref, w_ref, cos_ref, sin_ref, o_ref, h_scr, *, tiles_per_tensor, q_scale):
    j = pl.program_id(2)

    @pl.when(j == 0)
    def _():
        h_scr[...] = _modulate(x_ref, sc_ref, sh_ref)

    tensor = j // tiles_per_tensor
    scale = jnp.where(tensor == 0, q_scale, 1.0)
    is_v = tensor == 2
    chunk = min(MXU_WIDTH, w_ref.shape[1])
    for rs in _row_halves(h_scr.shape[0]):
        a = jnp.where(is_v, 1.0, cos_ref[0, rs, :] * scale)
        b = jnp.where(is_v, 0.0, sin_ref[0, rs, :] * scale)
        for c in range(w_ref.shape[1] // chunk):
            acc = jnp.dot(h_scr[rs, :], w_ref[:, c * chunk:(c + 1) * chunk], preferred_element_type=F32)
            for hh in range(chunk // HEAD_DIM):
                xc = acc[:, hh * HEAD_DIM:(hh + 1) * HEAD_DIM]
                xc = xc * a + pltpu.roll(xc, HEAD_DIM // 2, 1) * b
                o_ref[0, c * (chunk // HEAD_DIM) + hh, rs, :] = xc.astype(o_ref.dtype)


def _qkv_proj(x, sc, sh, w_qkv, cos_t, sin_t):
    bsz, seq, d = x.shape
    tm = _pick(seq, (1024, 512, 256, 128))
    tn = _pick(d, (2048, 1024, 512, 256, 128))
    hpt = tn // HEAD_DIM
    kern = functools.partial(_qkv_kernel, tiles_per_tensor=d // tn, q_scale=HEAD_DIM ** -0.5 * math.log2(math.e))
    return pl.pallas_call(
        kern,
        out_shape=jax.ShapeDtypeStruct((bsz, 3 * d // HEAD_DIM, seq, HEAD_DIM), BF16),
        grid=(bsz, seq // tm, 3 * d // tn),
        in_specs=[
            pl.BlockSpec((1, tm, d), lambda b, i, j: (b, i, 0)),
            pl.BlockSpec((1, 1, d), lambda b, i, j: (b, 0, 0)),
            pl.BlockSpec((1, 1, d), lambda b, i, j: (b, 0, 0)),
            pl.BlockSpec((d, tn), lambda b, i, j: (0, j)),
            pl.BlockSpec((1, tm, HEAD_DIM), lambda b, i, j: (b, i, 0)),
            pl.BlockSpec((1, tm, HEAD_DIM), lambda b, i, j: (b, i, 0)),
        ],
        out_specs=pl.BlockSpec((1, hpt, tm, HEAD_DIM), lambda b, i, j: (b, j, i, 0)),
        scratch_shapes=[pltpu.VMEM((tm, d), BF16)],
        compiler_params=_params(("parallel", "parallel", "arbitrary")),
        name="dsa_qkv_proj",
    )(x, sc, sh, w_qkv, cos_t, sin_t)


def _idx_kernel(x_ref, sc_ref, sh_ref, w_ref, cos_ref, sin_ref, qi_ref, kia_ref, kib_ref, wi_ref,
                *, n_pair, idx_heads, w_scale):
    h = _modulate(x_ref, sc_ref, sh_ref)
    halves = _row_halves(h.shape[0])
    accs = [jnp.dot(h[rs], w_ref[...], preferred_element_type=F32) for rs in halves]
    lane = lax.broadcasted_iota(I32, (halves[0].stop, LANES), 1)
    first_half = (lane % IDX_DIM) < IDX_DIM // 2

    for rs, acc in zip(halves, accs):
        cos = cos_ref[0, rs, :]
        sin = sin_ref[0, rs, :]

        def rope(xc, cos=cos, sin=sin):
            partner = jnp.where(first_half, pltpu.roll(xc, LANES - IDX_DIM // 2, 1),
                                pltpu.roll(xc, IDX_DIM // 2, 1))
            return xc * cos + partner * sin

        for p in range(n_pair):
            cs = slice(p * LANES, (p + 1) * LANES)
            qi_ref[0, rs, cs] = rope(acc[:, cs]).astype(qi_ref.dtype)
        last = acc[:, n_pair * LANES:]
        ka = jnp.where(lane < IDX_DIM, rope(last), 0.0)
        kia_ref[0, rs, :] = ka.astype(kia_ref.dtype)
        kib_ref[0, rs, :] = pltpu.roll(ka, IDX_DIM, 1).astype(kib_ref.dtype)
        wi_ref[0, rs, :] = jnp.where(lane < idx_heads, pltpu.roll(last, LANES - IDX_DIM, 1), 0.0) * w_scale


def _idx_proj(x, sc, sh, w_idx, cos_t, sin_t, idx_heads):
    bsz, seq, d = x.shape
    n = w_idx.shape[1]
    n_pair = idx_heads // 2
    tm = _pick(seq, (1024, 512, 256, 128))
    kern = functools.partial(_idx_kernel, n_pair=n_pair, idx_heads=idx_heads,
                             w_scale=(idx_heads ** -0.5) * (IDX_DIM ** -0.5))
    row = lambda b, i: (b, i, 0)
    return pl.pallas_call(
        kern,
        out_shape=(
            jax.ShapeDtypeStruct((bsz, seq, n_pair * LANES), BF16),
            jax.ShapeDtypeStruct((bsz, seq, LANES), BF16),
            jax.ShapeDtypeStruct((bsz, seq, LANES), BF16),
            jax.ShapeDtypeStruct((bsz, seq, LANES), F32),
        ),
        grid=(bsz, seq // tm),
        in_specs=[
            pl.BlockSpec((1, tm, d), row),
            pl.BlockSpec((1, 1, d), lambda b, i: (b, 0, 0)),
            pl.BlockSpec((1, 1, d), lambda b, i: (b, 0, 0)),
            pl.BlockSpec((d, n), lambda b, i: (0, 0)),
            pl.BlockSpec((1, tm, LANES), row),
            pl.BlockSpec((1, tm, LANES), row),
        ],
        out_specs=(
            pl.BlockSpec((1, tm, n_pair * LANES), row),
            pl.BlockSpec((1, tm, LANES), row),
            pl.BlockSpec((1, tm, LANES), row),
            pl.BlockSpec((1, tm, LANES), row),
        ),
        compiler_params=_params(("parallel", "parallel")),
        name="dsa_idx_proj",
    )(x, sc, sh, w_idx, cos_t, sin_t)


def _bit_transpose32(words):
    a = list(words)
    j = WORD_BITS // 2
    mask = 0x0000FFFF
    while j:
        k = 0
        while k < WORD_BITS:
            t = (a[k] ^ lax.shift_right_logical(a[k + j], jnp.int32(j))) & jnp.int32(mask)
            a[k] = a[k] ^ t
            a[k + j] = a[k + j] ^ (t << j)
            k = (k + j + 1) & ~j
        j >>= 1
        if j:
            mask = (mask ^ (mask << j)) & 0xFFFFFFFF
    return a


def _dsa_kernel(qb_tab, kb_tab, q_ref, k_ref, v_ref, qi_ref, kia_ref, kib_ref, wi_ref, o_ref,
                keys_scr, planes_scr, thr_scr, tie_scr, ngt_scr, nge_scr,
                m_scr, l_scr, acc_scr, bias_scr, s0_scr, s1_scr, r0_scr, r1_scr,
                *, n_pair, k_top, idx_bits):
    heads, qb_rows, _ = acc_scr.shape
    kb_rows = keys_scr.shape[2]
    qb = qb_tab[pl.program_id(1)]
    kb = kb_tab[pl.program_id(1)]
    q0 = qb * qb_rows
    last_kb = (q0 + qb_rows - 1) // kb_rows
    n_chunk = last_kb + 1
    row = lax.broadcasted_iota(I32, (qb_rows, kb_rows), 0) + q0
    col = lax.broadcasted_iota(I32, (qb_rows, kb_rows), 1)
    nt = (((1,), (1,)), ((), ()))

    @pl.when(kb == 0)
    def _select():
        wi = wi_ref[0]

        def score_chunk(c, carry):
            k0 = pl.multiple_of(c * kb_rows, kb_rows)
            ka = kia_ref[0, pl.ds(k0, kb_rows), :]
            kbm = kib_ref[0, pl.ds(k0, kb_rows), :]
            sc = jnp.zeros((qb_rows, kb_rows), F32)
            for p in range(n_pair):
                qp = qi_ref[0, :, p * LANES:(p + 1) * LANES]
                sa = lax.dot_general(qp, ka, nt, preferred_element_type=F32)
                sb = lax.dot_general(qp, kbm, nt, preferred_element_type=F32)
                sc = sc + wi[:, 2 * p:2 * p + 1] * jnp.maximum(sa, 0.0)
                sc = sc + wi[:, 2 * p + 1:2 * p + 2] * jnp.maximum(sb, 0.0)
            bits = lax.bitcast_convert_type(sc, I32)
            key = bits ^ ((bits >> 31) & INT_MAX)
            keys_scr[c] = jnp.where(col + k0 <= row, key, INT_MIN)
            return carry

        lax.fori_loop(0, n_chunk, score_chunk, 0)

        slices_per_chunk = kb_rows // LANES
        chunks_per_set = WORD_BITS // slices_per_chunk
        n_slices = keys_scr.shape[0] * slices_per_chunk
        n_sets = planes_scr.shape[2] // LANES
        live_sets = (n_chunk + chunks_per_set - 1) // chunks_per_set

        def fill_chunk(c, carry):
            keys_scr[c] = jnp.full((qb_rows, kb_rows), INT_MIN, I32)
            return carry

        lax.fori_loop(n_chunk, jnp.minimum(live_sets * chunks_per_set, keys_scr.shape[0]), fill_chunk, 0)

        def pack_rows(g, carry, st):
            r0 = pl.multiple_of(g * SUBLANES, SUBLANES)
            words = []
            for s in range(WORD_BITS):
                sl = st * WORD_BITS + s
                if sl < n_slices:
                    c, off = divmod(sl, slices_per_chunk)
                    words.append(keys_scr[c, pl.ds(r0, SUBLANES), off * LANES:(off + 1) * LANES])
                else:
                    words.append(jnp.full((SUBLANES, LANES), INT_MIN, I32))
            words = _bit_transpose32(words)
            words[0] = ~words[0]
            for i in range(WORD_BITS):
                planes_scr[i, pl.ds(r0, SUBLANES), st * LANES:(st + 1) * LANES] = words[i]
            return carry

        for st in range(n_sets):
            @pl.when(st < live_sets)
            def _(st=st):
                lax.fori_loop(0, qb_rows // SUBLANES, functools.partial(pack_rows, st=st), 0)

        lane_ones = jnp.ones((LANES, LANES), BF16)

        def radix_select(live):
            width = live * LANES

            def row_total(pc):
                tot = pc[:, :LANES]
                for st in range(1, live):
                    tot = tot + pc[:, st * LANES:(st + 1) * LANES]
                return jnp.dot(tot.astype(F32).astype(BF16), lane_ones, preferred_element_type=F32)

            def all_sets(mask):
                return jnp.concatenate([mask] * live, axis=1)

            def select_two_bits(i, carry):
                cand, above, prefix = carry
                hi = planes_scr[2 * i, :, :width]
                lo = planes_scr[2 * i + 1, :, :width]
                c1 = cand & hi
                c0 = cand ^ c1
                c11 = c1 & lo
                c10 = c1 ^ c11
                c01 = c0 & lo
                c00 = c0 ^ c01
                r11 = above + row_total(lax.population_count(c11))
                r10 = r11 + row_total(lax.population_count(c10))
                r01 = r10 + row_total(lax.population_count(c01))
                t11 = r11 >= k_top
                t10 = r10 >= k_top
                t01 = r01 >= k_top
                cand = jnp.where(all_sets(t11), c11,
                                 jnp.where(all_sets(t10), c10, jnp.where(all_sets(t01), c01, c00)))
                above = jnp.where(t11, above, jnp.where(t10, r11, jnp.where(t01, r10, r01)))
                bits = jnp.where(t11, 3, jnp.where(t10, 2, jnp.where(t01, 1, 0)))
                prefix = prefix | (bits << (WORD_BITS - 2 - 2 * i))
                return cand, above, prefix

            cand, above, prefix = lax.fori_loop(
                0, WORD_BITS // 2, select_two_bits,
                (jnp.full((qb_rows, width), -1, I32),
                 jnp.zeros((qb_rows, LANES), F32), jnp.zeros((qb_rows, LANES), I32)))
            equal = row_total(lax.population_count(cand))
            thr_scr[...] = (prefix ^ INT_MIN)[:, :1]
            tie_scr[...] = jnp.full((qb_rows, 1), INT_MAX, I32)
            ngt_scr[...] = above[:, :1]
            nge_scr[...] = (above + equal)[:, :1]

        for live in range(1, n_sets + 1):
            pl.when(live_sets == live)(functools.partial(radix_select, live))

        rg = min(qb_rows, SEARCH_ROWS)
        lane_col = lax.broadcasted_iota(I32, (rg, LANES), 1)

        for r in range(qb_rows // rg):
            rs = slice(r * rg, (r + 1) * rg)

            def count(pred, rs=rs):
                def body(c, acc):
                    for t in range(kb_rows // LANES):
                        kk = keys_scr[c, rs, t * LANES:(t + 1) * LANES]
                        idx = lane_col + (c * kb_rows + t * LANES)
                        acc = acc + jnp.where(pred(kk, idx), 1.0, 0.0)
                    return acc
                acc = lax.fori_loop(0, n_chunk, body, jnp.zeros((rg, LANES), F32))
                return jnp.sum(acc, axis=1, keepdims=True)

            def wide(v):
                return jnp.broadcast_to(v, (rg, LANES))

            @pl.when(jnp.max(nge_scr[rs, :]) > k_top)
            def _ties(count=count, rs=rs):
                thr_w = wide(thr_scr[rs, :])
                need = k_top - ngt_scr[rs, :]

                def tie_bit(i, jt):
                    cand = jt + (jnp.int32(1) << (idx_bits - 1 - i))
                    cand_w = wide(cand)
                    cnt = count(lambda kk, idx: (kk == thr_w) & (idx < cand_w))
                    return jnp.where(cnt < need, cand, jt)

                tie_scr[rs, :] = lax.fori_loop(0, idx_bits, tie_bit, jnp.zeros((rg, 1), I32))

        m_scr[...] = jnp.full(m_scr.shape, M_INIT, F32)
        l_scr[...] = jnp.zeros_like(l_scr)
        acc_scr[...] = jnp.zeros_like(acc_scr)

    def _attend():
        ones = jnp.ones((kb_rows, HEAD_DIM), BF16)

        kk = keys_scr[kb]
        thr = thr_scr[...]
        lim = jnp.minimum(tie_scr[...], row[:, :1])
        tie_bias = jnp.where(col + kb * kb_rows <= lim, 0.0, MASK_VALUE)
        bias_scr[...] = jnp.where(kk > thr, 0.0, jnp.where(kk == thr, tie_bias, MASK_VALUE))

        def logits(h, s_ref, r_ref):
            s = lax.dot_general(q_ref[0, h], k_ref[0, h], nt, preferred_element_type=F32) + bias_scr[...]
            s_ref[...] = s
            r_ref[...] = jnp.broadcast_to(jnp.max(s, axis=1, keepdims=True), r_ref.shape)

        def accumulate(h, s_ref, r_ref):
            m_old = m_scr[h]
            m_new = jnp.maximum(m_old, r_ref[...])
            alpha = jnp.exp2(m_old - m_new)
            p = jnp.concatenate(
                [jnp.exp2(s_ref[:, t * LANES:(t + 1) * LANES] - m_new).astype(BF16)
                 for t in range(kb_rows // LANES)], axis=1)
            v_ext = jnp.concatenate([v_ref[0, h], ones], axis=1)
            pv = jnp.dot(p, v_ext, preferred_element_type=F32)
            acc_scr[h] = alpha * acc_scr[h] + pv[:, :HEAD_DIM]
            l_scr[h] = alpha * l_scr[h] + pv[:, HEAD_DIM:]
            m_scr[h] = m_new

        bufs = ((s0_scr, r0_scr), (s1_scr, r1_scr))
        logits(0, *bufs[0])
        for h in range(heads):
            if h + 1 < heads:
                logits(h + 1, *bufs[(h + 1) % 2])
            accumulate(h, *bufs[h % 2])

    _attend()

    @pl.when(kb == last_kb)
    def _finish():
        for h in range(heads):
            o_ref[0, :, h * HEAD_DIM:(h + 1) * HEAD_DIM] = (acc_scr[h] / l_scr[h]).astype(o_ref.dtype)


def _dsa_attention(qkv, qi, kia, kib, wi, k_top):
    bsz, heads3, seq, _ = qkv.shape
    heads = heads3 // 3
    d = heads * HEAD_DIM
    n_pair = qi.shape[-1] // LANES
    qb_rows = _pick(seq, (256, 128))
    kb_rows = _pick(seq, (512, 256, 128))
    n_kb = seq // kb_rows
    n_sets = -(-seq // (WORD_BITS * LANES))

    pairs = [(i, j) for i in range(seq // qb_rows) for j in range((i * qb_rows + qb_rows - 1) // kb_rows + 1)]
    qb_tab = jnp.asarray([p[0] for p in pairs], I32)
    kb_tab = jnp.asarray([p[1] for p in pairs], I32)

    kern = functools.partial(_dsa_kernel, n_pair=n_pair, k_top=k_top, idx_bits=seq.bit_length())
    grid_spec = pltpu.PrefetchScalarGridSpec(
        num_scalar_prefetch=2,
        grid=(bsz, len(pairs)),
        in_specs=[
            pl.BlockSpec((1, heads, qb_rows, HEAD_DIM), lambda b, s, qt, kt: (b, 0, qt[s], 0)),
            pl.BlockSpec((1, heads, kb_rows, HEAD_DIM), lambda b, s, qt, kt: (b, 1, kt[s], 0)),
            pl.BlockSpec((1, heads, kb_rows, HEAD_DIM), lambda b, s, qt, kt: (b, 2, kt[s], 0)),
            pl.BlockSpec((1, qb_rows, n_pair * LANES), lambda b, s, qt, kt: (b, qt[s], 0)),
            pl.BlockSpec((1, seq, LANES), lambda b, s, qt, kt: (b, 0, 0)),
            pl.BlockSpec((1, seq, LANES), lambda b, s, qt, kt: (b, 0, 0)),
            pl.BlockSpec((1, qb_rows, LANES), lambda b, s, qt, kt: (b, qt[s], 0)),
        ],
        out_specs=pl.BlockSpec((1, qb_rows, d), lambda b, s, qt, kt: (b, qt[s], 0)),
        scratch_shapes=[
            pltpu.VMEM((n_kb, qb_rows, kb_rows), I32),
            pltpu.VMEM((WORD_BITS, qb_rows, n_sets * LANES), I32),
            pltpu.VMEM((qb_rows, 1), I32),
            pltpu.VMEM((qb_rows, 1), I32),
            pltpu.VMEM((qb_rows, 1), F32),
            pltpu.VMEM((qb_rows, 1), F32),
            pltpu.VMEM((heads, qb_rows, LANES), F32),
            pltpu.VMEM((heads, qb_rows, LANES), F32),
            pltpu.VMEM((heads, qb_rows, HEAD_DIM), F32),
            pltpu.VMEM((qb_rows, kb_rows), F32),
            pltpu.VMEM((qb_rows, kb_rows), F32),
            pltpu.VMEM((qb_rows, kb_rows), F32),
            pltpu.VMEM((qb_rows, LANES), F32),
            pltpu.VMEM((qb_rows, LANES), F32),
        ],
    )
    return pl.pallas_call(
        kern,
        out_shape=jax.ShapeDtypeStruct((bsz, seq, d), BF16),
        grid_spec=grid_spec,
        compiler_params=_params(("parallel", "arbitrary")),
        name="dsa_select_attend",
    )(qb_tab, kb_tab, qkv, qkv, qkv, qi, kia, kib, wi)


def _rope_tables(positions, dim):
    inv = 1.0 / (ROPE_THETA ** (jnp.arange(0, dim, 2, dtype=F32) / dim))
    half = dim // 2
    lane = jnp.arange(LANES)
    inv_t = inv[lane % half]
    sign = jnp.where((lane % dim) < half, -1.0, 1.0).astype(F32)
    ang = positions.astype(F32)[..., None] * inv_t
    return jnp.cos(ang), jnp.sin(ang) * sign


def kernel(x, c, positions, ada_w, ada_b, ln_g, ln_b, s5_in_w, s5_a_re, s5_a_im, s5_log_dt, s5_b_re, s5_b_im, s5_c_re, s5_c_im, s5_d, s5_glu_w, s5_glu_b, dsa_in_w, dsa_out_w, ffn_w_in, ffn_w_out):
    bsz, seq, d = x.shape
    depth = ada_w.shape[0]
    alpha = (2.0 * depth) ** 0.25
    idx_heads = (dsa_in_w.shape[-1] - 3 * d - IDX_DIM) // (IDX_DIM + 1)
    k_top = min(TOPK_MAX, seq // 4)
    seg = min(S5_SEG, seq // SUBLANES)

    cos_h, sin_h = _rope_tables(positions, HEAD_DIM)
    cos_i, sin_i = _rope_tables(positions, IDX_DIM)
    mod = _ada_mod(c, ada_w, ada_b)
    ffn_in = ffn_w_in.astype(BF16)
    ffn_out = ffn_w_out.astype(BF16)

    for i in range(depth):
        sh1, sc1, g1, sh2, sc2, g2 = [m[:, None, :] for m in jnp.split(mod[i], 6, axis=-1)]
        j = i // 2
        if i % 2 == 0:
            prep = _s5_discretize(s5_a_re[j], s5_a_im[j], s5_log_dt[j], s5_b_re[j], s5_b_im[j],
                                  s5_c_re[j], s5_c_im[j], seg)
            u = _mod_matmul(x, sc1, sh1, s5_in_w[j].astype(BF16), F32)
            gl = _s5_scan(u, prep, s5_d[j], seg)
            x1 = _matmul_res_ln(gl, s5_glu_w[j].astype(BF16), s5_glu_b[j], x, g1,
                                ln_g[i, 0], ln_b[i, 0], alpha, glu=True)
        else:
            w = dsa_in_w[j].astype(BF16)
            n_qi = idx_heads * IDX_DIM
            pad = jnp.zeros((d, LANES - IDX_DIM - idx_heads), w.dtype)
            w_idx = jnp.concatenate(
                [w[:, 3 * d:3 * d + n_qi], w[:, 3 * d + n_qi + idx_heads:], w[:, 3 * d + n_qi:3 * d + n_qi + idx_heads], pad],
                axis=1)
            qkv = _qkv_proj(x, sc1, sh1, w, cos_h, sin_h)
            qi, kia, kib, wi = _idx_proj(x, sc1, sh1, w_idx, cos_i, sin_i, idx_heads)
            att = _dsa_attention(qkv, qi, kia, kib, wi, k_top)
            x1 = _matmul_res_ln(att, dsa_out_w[j].astype(BF16), None, x, g1,
                                ln_g[i, 0], ln_b[i, 0], alpha, glu=False)
        x = _ffn_res_ln(x1, sc2, sh2, g2, ffn_in, ffn_out, i, ln_g[i, 1], ln_b[i, 1], alpha)
    return x
```

```python
import functools
import math

import jax
import jax.numpy as jnp
from jax import lax
from jax.experimental import pallas as pl
from jax.experimental.pallas import tpu as pltpu

F32 = jnp.float32
BF16 = jnp.bfloat16
I32 = jnp.int32

HEAD_DIM = 128
IDX_DIM = 64
TOPK_MAX = 256
ROPE_THETA = 10000.0
LN_EPS = 1e-5

LANES = 128
SUBLANES = 8
MXU_WIDTH = 256
VMEM_LIMIT_BYTES = 56 * 1024 * 1024
RESIDENT_WEIGHT_BYTES = 16 * 1024 * 1024

S5_PACK_GROUPS = 16
S5_SEG = 256

INT_MIN = -(2 ** 31)
INT_MAX = 2 ** 31 - 1
SEARCH_ROWS = 128
WORD_BITS = 32

MASK_VALUE = -2e30
M_INIT = -1e30


def _pick(n, cands):
    for c in cands:
        if n % c == 0:
            return c
    return n


def _row_halves(rows):
    half = rows // 2
    return (slice(0, half), slice(half, rows))


def _params(sem):
    return pltpu.CompilerParams(dimension_semantics=sem, vmem_limit_bytes=VMEM_LIMIT_BYTES)


def _ada_kernel(c_ref, w_ref, b_ref, o_ref):
    ca = jax.nn.silu(c_ref[...]).astype(BF16)
    o_ref[0] = jnp.dot(ca, w_ref[0].astype(BF16), preferred_element_type=F32) + b_ref[0]


def _ada_mod(c, ada_w, ada_b):
    bsz, d = c.shape
    depth, _, n = ada_w.shape
    rows = SUBLANES * ((bsz + SUBLANES - 1) // SUBLANES)
    cp = jnp.zeros((rows, d), F32).at[:bsz].set(c)
    tn = _pick(n, (2048, 1024, 512, 256, 128))
    out = pl.pallas_call(
        _ada_kernel,
        out_shape=jax.ShapeDtypeStruct((depth, rows, n), F32),
        grid=(depth, n // tn),
        in_specs=[
            pl.BlockSpec((rows, d), lambda l, j: (0, 0)),
            pl.BlockSpec((1, d, tn), lambda l, j: (l, 0, j)),
            pl.BlockSpec((1, 1, tn), lambda l, j: (l, 0, j)),
        ],
        out_specs=pl.BlockSpec((1, rows, tn), lambda l, j: (l, 0, j)),
        compiler_params=_params(("arbitrary", "arbitrary")),
        name="ada_mod",
    )(cp, ada_w, ada_b.reshape(depth, 1, n))
    return out[:, :bsz]


def _modulate(x_ref, sc_ref, sh_ref):
    return (x_ref[0] * (1.0 + sc_ref[0]) + sh_ref[0]).astype(BF16)


def _modmm_kernel(x_ref, sc_ref, sh_ref, w_ref, o_ref, h_scr):
    @pl.when(pl.program_id(2) == 0)
    def _():
        h_scr[...] = _modulate(x_ref, sc_ref, sh_ref)

    for rs in _row_halves(h_scr.shape[0]):
        o_ref[0, rs, :] = jnp.dot(h_scr[rs, :], w_ref[...], preferred_element_type=F32).astype(o_ref.dtype)


def _mod_matmul(x, sc, sh, w, out_dtype):
    bsz, seq, d = x.shape
    n = w.shape[1]
    tm = _pick(seq, (512, 256, 128))
    tn = n if 2 * d * n * w.dtype.itemsize <= RESIDENT_WEIGHT_BYTES else _pick(n, (1024, 512, 256, 128))
    return pl.pallas_call(
        _modmm_kernel,
        out_shape=jax.ShapeDtypeStruct((bsz, seq, n), out_dtype),
        grid=(bsz, seq // tm, n // tn),
        in_specs=[
            pl.BlockSpec((1, tm, d), lambda b, i, j: (b, i, 0)),
            pl.BlockSpec((1, 1, d), lambda b, i, j: (b, 0, 0)),
            pl.BlockSpec((1, 1, d), lambda b, i, j: (b, 0, 0)),
            pl.BlockSpec((d, tn), lambda b, i, j: (0, j)),
        ],
        out_specs=pl.BlockSpec((1, tm, tn), lambda b, i, j: (b, i, j)),
        scratch_shapes=[pltpu.VMEM((tm, d), BF16)],
        compiler_params=_params(("parallel", "parallel", "arbitrary")),
        name="mod_matmul",
    )(x, sc, sh, w)


def _s5_kernel(*refs, seg, width, lane_tiles):
    u_refs = refs[:lane_tiles]
    (bb_ref, cb_ref, lre_ref, lim_ref, ltre_ref, ltim_ref, d_ref, o_ref,
     xs_scr, carry_scr, cin_scr, up_scr) = refs[lane_tiles:]
    ns = xs_scr.shape[1] // 2

    @pl.when(pl.program_id(2) == 0)
    def _():
        carry_scr[...] = jnp.zeros_like(carry_scr)

    for t, u_ref in enumerate(u_refs):
        for i in range(seg):
            up_scr[t, i * SUBLANES:(i + 1) * SUBLANES, :] = u_ref[0, pl.ds(i, SUBLANES, stride=seg), :]
    u = jnp.concatenate([up_scr[t] for t in range(len(u_refs))], axis=1)
    halves = _row_halves(u.shape[0])
    for rs in halves:
        xs_scr[rs, :] = jnp.dot(u[rs].astype(BF16), bb_ref[0], preferred_element_type=F32)

    for part in range(ns // width):
        cr = slice(part * width, (part + 1) * width)
        ci = slice(ns + part * width, ns + (part + 1) * width)
        lr = jnp.broadcast_to(lre_ref[0, :, cr], (SUBLANES, width))
        li = jnp.broadcast_to(lim_ref[0, :, cr], (SUBLANES, width))

        def local_step(i, st, cr=cr, ci=ci, lr=lr, li=li):
            sr, si = st
            r0 = pl.multiple_of(i * SUBLANES, SUBLANES)
            nr = lr * sr - li * si + xs_scr[pl.ds(r0, SUBLANES), cr]
            ni = lr * si + li * sr + xs_scr[pl.ds(r0, SUBLANES), ci]
            xs_scr[pl.ds(r0, SUBLANES), cr] = nr
            xs_scr[pl.ds(r0, SUBLANES), ci] = ni
            return nr, ni

        zero = jnp.zeros((SUBLANES, width), F32)
        er, ei = lax.fori_loop(0, seg, local_step, (zero, zero), unroll=4)

        ltr = ltre_ref[0, :, cr]
        lti = ltim_ref[0, :, cr]
        c_r = carry_scr[:, cr]
        c_i = carry_scr[:, ci]
        for s in range(SUBLANES):
            cin_scr[s:s + 1, cr] = c_r
            cin_scr[s:s + 1, ci] = c_i
            e_r = er[s:s + 1, :]
            e_i = ei[s:s + 1, :]
            c_r, c_i = ltr * c_r - lti * c_i + e_r, ltr * c_i + lti * c_r + e_i
        carry_scr[:, cr] = c_r
        carry_scr[:, ci] = c_i

        def carry_step(i, st, cr=cr, ci=ci, lr=lr, li=li):
            pr, pi_ = st
            r0 = pl.multiple_of(i * SUBLANES, SUBLANES)
            nr = lr * pr - li * pi_
            ni = lr * pi_ + li * pr
            xs_scr[pl.ds(r0, SUBLANES), cr] += nr
            xs_scr[pl.ds(r0, SUBLANES), ci] += ni
            return nr, ni

        lax.fori_loop(0, seg, carry_step, (cin_scr[:, cr], cin_scr[:, ci]), unroll=4)

    for rs in halves:
        y = jnp.dot(xs_scr[rs, :].astype(BF16), cb_ref[0], preferred_element_type=F32)
        g = jax.nn.gelu(y + d_ref[0] * u[rs])
        for t in range(len(u_refs)):
            up_scr[t, rs, :] = g[:, t * LANES:(t + 1) * LANES]
    for t in range(len(u_refs)):
        for s in range(SUBLANES):
            o_ref[0, s * seg:(s + 1) * seg, t * LANES:(t + 1) * LANES] = (
                up_scr[t, pl.ds(s, seg, stride=SUBLANES), :].astype(o_ref.dtype))


def _cmul(ar, ai, br, bi):
    return ar * br - ai * bi, ar * bi + ai * br


def _s5_discretize(a_re, a_im, log_dt, b_re, b_im, c_re, c_im, seg):
    g, n = a_re.shape
    p = b_re.shape[-1]
    pg = S5_PACK_GROUPS
    packs = g // pg
    a_re, a_im = a_re.astype(F32), a_im.astype(F32)
    dt = jnp.exp(log_dt.astype(F32))[:, None]
    mag = jnp.exp(a_re * dt)
    lb_re, lb_im = mag * jnp.cos(a_im * dt), mag * jnp.sin(a_im * dt)
    den = a_re * a_re + a_im * a_im
    nr, ni = lb_re - 1.0, lb_im
    f_re = (nr * a_re + ni * a_im) / den
    f_im = (ni * a_re - nr * a_im) / den
    bb_re, bb_im = _cmul(f_re[..., None], f_im[..., None], b_re.astype(F32), b_im.astype(F32))
    lt_re, lt_im = lb_re, lb_im
    for _ in range(int(math.log2(seg))):
        lt_re, lt_im = _cmul(lt_re, lt_im, lt_re, lt_im)
    on_diag = (jnp.arange(pg * p)[:, None] // p) == (jnp.arange(pg * n)[None, :] // n)

    def blk_diag(m):
        tiled = jnp.tile(m.astype(BF16).reshape(packs, pg * p, n), (1, 1, pg))
        return jnp.where(on_diag[None], tiled, 0)

    b_blk = jnp.concatenate([blk_diag(jnp.swapaxes(bb_re, 1, 2)), blk_diag(jnp.swapaxes(bb_im, 1, 2))], axis=-1)
    c_blk = jnp.swapaxes(jnp.concatenate([blk_diag(c_re), blk_diag(-c_im)], axis=-1), 1, 2)

    def vec(m):
        return m.reshape(packs, 1, pg * n)

    return b_blk, c_blk, vec(lb_re), vec(lb_im), vec(lt_re), vec(lt_im)


def _s5_scan(u, prep, d_skip, seg):
    bsz, seq, d = u.shape
    b_blk, c_blk, lre, lim, ltre, ltim = prep
    packs, pw, ns2 = b_blk.shape
    ns = ns2 // 2
    rows = SUBLANES * seg
    width = _pick(ns, (1024, 512, 256, 128))
    lane_tiles = pw // LANES
    kern = functools.partial(_s5_kernel, seg=seg, width=width, lane_tiles=lane_tiles)
    vspec = pl.BlockSpec((1, 1, ns), lambda b, k, m: (k, 0, 0))
    u_specs = [pl.BlockSpec((1, rows, LANES), lambda b, k, m, t=t: (b, m, k * lane_tiles + t))
               for t in range(lane_tiles)]
    return pl.pallas_call(
        kern,
        out_shape=jax.ShapeDtypeStruct((bsz, seq, d), BF16),
        grid=(bsz, packs, seq // rows),
        in_specs=u_specs + [
            pl.BlockSpec((1, pw, ns2), lambda b, k, m: (k, 0, 0)),
            pl.BlockSpec((1, ns2, pw), lambda b, k, m: (k, 0, 0)),
            vspec, vspec, vspec, vspec,
            pl.BlockSpec((1, 1, pw), lambda b, k, m: (k, 0, 0)),
        ],
        out_specs=pl.BlockSpec((1, rows, pw), lambda b, k, m: (b, m, k)),
        scratch_shapes=[
            pltpu.VMEM((rows, ns2), F32),
            pltpu.VMEM((1, ns2), F32),
            pltpu.VMEM((SUBLANES, ns2), F32),
            pltpu.VMEM((lane_tiles, rows, LANES), F32),
        ],
        compiler_params=_params(("parallel", "parallel", "arbitrary")),
        name="s5_scan",
    )(*([u] * lane_tiles), b_blk, c_blk, lre, lim, ltre, ltim, d_skip.reshape(packs, 1, pw).astype(F32))


def _layer_norm_rows(tiles, lng_ref, lnb_ref, o_ref, d):
    tn = tiles[0].shape[1]
    tot = tiles[0].sum(axis=1, keepdims=True)
    for t in tiles[1:]:
        tot = tot + t.sum(axis=1, keepdims=True)
    mu = tot * (1.0 / d)
    sq = jnp.square(tiles[0] - mu).sum(axis=1, keepdims=True)
    for t in tiles[1:]:
        sq = sq + jnp.square(t - mu).sum(axis=1, keepdims=True)
    inv = lax.rsqrt(sq * (1.0 / d) + LN_EPS)
    for k, t in enumerate(tiles):
        cs = slice(k * tn, (k + 1) * tn)
        o_ref[0, :, cs] = ((t - mu) * inv * lng_ref[:, cs] + lnb_ref[:, cs]).astype(o_ref.dtype)


def _mmln_kernel(*refs, glu, nj, alpha, d):
    if glu:
        a_ref, w1_ref, w2_ref, b1_ref, b2_ref, x_ref, g_ref, lng_ref, lnb_ref, o_ref, r_scr = refs
    else:
        a_ref, w1_ref, x_ref, g_ref, lng_ref, lnb_ref, o_ref, r_scr = refs
    j = pl.program_id(2)
    for rs in _row_halves(a_ref.shape[1]):
        a = a_ref[0, rs, :]
        y = jnp.dot(a, w1_ref[...], preferred_element_type=F32)
        if glu:
            y = y + b1_ref[...]
            gate = jnp.dot(a, w2_ref[...], preferred_element_type=F32) + b2_ref[...]
            y = y * jax.nn.sigmoid(gate)
        r_scr[j, rs, :] = alpha * x_ref[0, rs, :] + (1.0 + g_ref[0]) * y

    @pl.when(j == nj - 1)
    def _():
        _layer_norm_rows([r_scr[t] for t in range(nj)], lng_ref, lnb_ref, o_ref, d)


def _matmul_res_ln(a, w, bias, xres, gate, ln_g, ln_b, alpha, glu):
    bsz, seq, k = a.shape
    d = xres.shape[-1]
    if glu:
        tm = _pick(seq, (1024, 512, 256, 128))
        tn = _pick(d, (512, 256, 128))
    else:
        tm = _pick(seq, (512, 256, 128))
        tn = d if 2 * k * d * w.dtype.itemsize <= RESIDENT_WEIGHT_BYTES else _pick(d, (512, 256, 128))
    nj = d // tn
    a_spec = pl.BlockSpec((1, tm, k), lambda b, i, j: (b, i, 0))
    w1_spec = pl.BlockSpec((k, tn), lambda b, i, j: (0, j))
    tail_specs = [
        pl.BlockSpec((1, tm, tn), lambda b, i, j: (b, i, j)),
        pl.BlockSpec((1, 1, tn), lambda b, i, j: (b, 0, j)),
        pl.BlockSpec((1, d), lambda b, i, j: (0, 0)),
        pl.BlockSpec((1, d), lambda b, i, j: (0, 0)),
    ]
    tail = (xres, gate, ln_g.reshape(1, d).astype(F32), ln_b.reshape(1, d).astype(F32))
    if glu:
        in_specs = [a_spec, w1_spec,
                    pl.BlockSpec((k, tn), lambda b, i, j: (0, j + nj)),
                    pl.BlockSpec((1, tn), lambda b, i, j: (0, j)),
                    pl.BlockSpec((1, tn), lambda b, i, j: (0, j + nj))] + tail_specs
        b2d = bias.reshape(1, 2 * d).astype(F32)
        args = (a, w, w, b2d, b2d) + tail
    else:
        in_specs = [a_spec, w1_spec] + tail_specs
        args = (a, w) + tail
    kern = functools.partial(_mmln_kernel, glu=glu, nj=nj, alpha=alpha, d=d)
    return pl.pallas_call(
        kern,
        out_shape=jax.ShapeDtypeStruct((bsz, seq, d), F32),
        grid=(bsz, seq // tm, nj),
        in_specs=in_specs,
        out_specs=pl.BlockSpec((1, tm, d), lambda b, i, j: (b, i, 0)),
        scratch_shapes=[pltpu.VMEM((nj, tm, tn), F32)],
        compiler_params=_params(("parallel", "parallel", "arbitrary")),
        name="matmul_res_ln",
    )(*args)


def _ffn_kernel(x_ref, sc_ref, sh_ref, g_ref, wg_ref, wu_ref, wo_ref, lng_ref, lnb_ref, o_ref,
                h_scr, acc_scr, *, nf, alpha, d, tn):
    f = pl.program_id(2)

    @pl.when(f == 0)
    def _():
        h_scr[...] = _modulate(x_ref, sc_ref, sh_ref)
        acc_scr[...] = jnp.zeros_like(acc_scr)

    h = h_scr[...]
    a_g = jnp.dot(h, wg_ref[0], preferred_element_type=F32)
    a_u = jnp.dot(h, wu_ref[0], preferred_element_type=F32)
    act = (jax.nn.silu(a_g) * a_u).astype(BF16)
    acc_scr[...] += jnp.dot(act, wo_ref[0], preferred_element_type=F32)

    @pl.when(f == nf - 1)
    def _():
        tiles = []
        for k in range(d // tn):
            cs = slice(k * tn, (k + 1) * tn)
            tiles.append(alpha * x_ref[0, :, cs] + (1.0 + g_ref[0, :, cs]) * acc_scr[:, cs])
        _layer_norm_rows(tiles, lng_ref, lnb_ref, o_ref, d)


def _ffn_res_ln(x, sc, sh, gate, w_in, w_out, layer, ln_g, ln_b, alpha):
    bsz, seq, d = x.shape
    dff = w_out.shape[1]
    tm = _pick(seq, (512, 256, 128))
    tf = _pick(dff, (512, 256, 128))
    nf = dff // tf
    tn = _pick(d, (512, 256, 128))
    vec = pl.BlockSpec((1, 1, d), lambda b, i, f: (b, 0, 0))
    kern = functools.partial(_ffn_kernel, nf=nf, alpha=alpha, d=d, tn=tn)
    return pl.pallas_call(
        kern,
        out_shape=jax.ShapeDtypeStruct((bsz, seq, d), F32),
        grid=(bsz, seq // tm, nf),
        in_specs=[
            pl.BlockSpec((1, tm, d), lambda b, i, f: (b, i, 0)),
            vec, vec, vec,
            pl.BlockSpec((1, d, tf), lambda b, i, f: (layer, 0, f)),
            pl.BlockSpec((1, d, tf), lambda b, i, f: (layer, 0, f + nf)),
            pl.BlockSpec((1, tf, d), lambda b, i, f: (layer, f, 0)),
            pl.BlockSpec((1, d), lambda b, i, f: (0, 0)),
            pl.BlockSpec((1, d), lambda b, i, f: (0, 0)),
        ],
        out_specs=pl.BlockSpec((1, tm, d), lambda b, i, f: (b, i, 0)),
        scratch_shapes=[pltpu.VMEM((tm, d), BF16), pltpu.VMEM((tm, d), F32)],
        compiler_params=_params(("parallel", "parallel", "arbitrary")),
        name="ffn_res_ln",
    )(x, sc, sh, gate, w_in, w_in, w_out, ln_g.reshape(1, d).astype(F32), ln_b.reshape(1, d).astype(F32))


def _qkv_kernel(x_ref, sc_ref, sh_ref, w_ref, cos_ref, sin_ref, o_ref, h_scr, *, tiles_per_tensor, q_scale):
    j = pl.program_id(2)

    @pl.when(j == 0)
    def _():
        h_scr[...] = _modulate(x_ref, sc_ref, sh_ref)

    tensor = j // tiles_per_tensor
    scale = jnp.where(tensor == 0, q_scale, 1.0)
    is_v = tensor == 2
    chunk = min(MXU_WIDTH, w_ref.shape[1])
    for rs in _row_halves(h_scr.shape[0]):
        a = jnp.where(is_v, 1.0, cos_ref[0, rs, :] * scale)
        b = jnp.where(is_v, 0.0, sin_ref[0, rs, :] * scale)
        for c in range(w_ref.shape[1] // chunk):
            acc = jnp.dot(h_scr[rs, :], w_ref[:, c * chunk:(c + 1) * chunk], preferred_element_type=F32)
            for hh in range(chunk // HEAD_DIM):
                xc = acc[:, hh * HEAD_DIM:(hh + 1) * HEAD_DIM]
                xc = xc * a + pltpu.roll(xc, HEAD_DIM // 2, 1) * b
                o_ref[0, c * (chunk // HEAD_DIM) + hh, rs, :] = xc.astype(o_ref.dtype)


def _qkv_proj(x, sc, sh, w_qkv, cos_t, sin_t):
    bsz, seq, d = x.shape
    tm = _pick(seq, (1024, 512, 256, 128))
    tn = _pick(d, (2048, 1024, 512, 256, 128))
    hpt = tn // HEAD_DIM
    kern = functools.partial(_qkv_kernel, tiles_per_tensor=d // tn, q_scale=HEAD_DIM ** -0.5 * math.log2(math.e))
    return pl.pallas_call(
        kern,
        out_shape=jax.ShapeDtypeStruct((bsz, 3 * d // HEAD_DIM, seq, HEAD_DIM), BF16),
        grid=(bsz, seq // tm, 3 * d // tn),
        in_specs=[
            pl.BlockSpec((1, tm, d), lambda b, i, j: (b, i, 0)),
            pl.BlockSpec((1, 1, d), lambda b, i, j: (b, 0, 0)),
            pl.BlockSpec((1, 1, d), lambda b, i, j: (b, 0, 0)),
            pl.BlockSpec((d, tn), lambda b, i, j: (0, j)),
            pl.BlockSpec((1, tm, HEAD_DIM), lambda b, i, j: (b, i, 0)),
            pl.BlockSpec((1, tm, HEAD_DIM), lambda b, i, j: (b, i, 0)),
        ],
        out_specs=pl.BlockSpec((1, hpt, tm, HEAD_DIM), lambda b, i, j: (b, j, i, 0)),
        scratch_shapes=[pltpu.VMEM((tm, d), BF16)],
        compiler_params=_params(("parallel", "parallel", "arbitrary")),
        name="dsa_qkv_proj",
    )(x, sc, sh, w_qkv, cos_t, sin_t)


def _idx_kernel(x_ref, sc_ref, sh_ref, w_ref, cos_ref, sin_ref, qi_ref, kia_ref, kib_ref, wi_ref,
                *, n_pair, idx_heads, w_scale):
    h = _modulate(x_ref, sc_ref, sh_ref)
    halves = _row_halves(h.shape[0])
    accs = [jnp.dot(h[rs], w_ref[...], preferred_element_type=F32) for rs in halves]
    lane = lax.broadcasted_iota(I32, (halves[0].stop, LANES), 1)
    first_half = (lane % IDX_DIM) < IDX_DIM // 2

    for rs, acc in zip(halves, accs):
        cos = cos_ref[0, rs, :]
        sin = sin_ref[0, rs, :]

        def rope(xc, cos=cos, sin=sin):
            partner = jnp.where(first_half, pltpu.roll(xc, LANES - IDX_DIM // 2, 1),
                                pltpu.roll(xc, IDX_DIM // 2, 1))
            return xc * cos + partner * sin

        for p in range(n_pair):
            cs = slice(p * LANES, (p + 1) * LANES)
            qi_ref[0, rs, cs] = rope(acc[:, cs]).astype(qi_ref.dtype)
        last = acc[:, n_pair * LANES:]
        ka = jnp.where(lane < IDX_DIM, rope(last), 0.0)
        kia_ref[0, rs, :] = ka.astype(kia_ref.dtype)
        kib_ref[0, rs, :] = pltpu.roll(ka, IDX_DIM, 1).astype(kib_ref.dtype)
        wi_ref[0, rs, :] = jnp.where(lane < idx_heads, pltpu.roll(last, LANES - IDX_DIM, 1), 0.0) * w_scale


def _idx_proj(x, sc, sh, w_idx, cos_t, sin_t, idx_heads):
    bsz, seq, d = x.shape
    n = w_idx.shape[1]
    n_pair = idx_heads // 2
    tm = _pick(seq, (1024, 512, 256, 128))
    kern = functools.partial(_idx_kernel, n_pair=n_pair, idx_heads=idx_heads,
                             w_scale=(idx_heads ** -0.5) * (IDX_DIM ** -0.5))
    row = lambda b, i: (b, i, 0)
    return pl.pallas_call(
        kern,
        out_shape=(
            jax.ShapeDtypeStruct((bsz, seq, n_pair * LANES), BF16),
            jax.ShapeDtypeStruct((bsz, seq, LANES), BF16),
            jax.ShapeDtypeStruct((bsz, seq, LANES), BF16),
            jax.ShapeDtypeStruct((bsz, seq, LANES), F32),
        ),
        grid=(bsz, seq // tm),
        in_specs=[
            pl.BlockSpec((1, tm, d), row),
            pl.BlockSpec((1, 1, d), lambda b, i: (b, 0, 0)),
            pl.BlockSpec((1, 1, d), lambda b, i: (b, 0, 0)),
            pl.BlockSpec((d, n), lambda b, i: (0, 0)),
            pl.BlockSpec((1, tm, LANES), row),
            pl.BlockSpec((1, tm, LANES), row),
        ],
        out_specs=(
            pl.BlockSpec((1, tm, n_pair * LANES), row),
            pl.BlockSpec((1, tm, LANES), row),
            pl.BlockSpec((1, tm, LANES), row),
            pl.BlockSpec((1, tm, LANES), row),
        ),
        compiler_params=_params(("parallel", "parallel")),
        name="dsa_idx_proj",
    )(x, sc, sh, w_idx, cos_t, sin_t)


def _bit_transpose32(words):
    a = list(words)
    j = WORD_BITS // 2
    mask = 0x0000FFFF
    while j:
        k = 0
        while k < WORD_BITS:
            t = (a[k] ^ lax.shift_right_logical(a[k + j], jnp.int32(j))) & jnp.int32(mask)
            a[k] = a[k] ^ t
            a[k + j] = a[k + j] ^ (t << j)
            k = (k + j + 1) & ~j
        j >>= 1
        if j:
            mask = (mask ^ (mask << j)) & 0xFFFFFFFF
    return a


def _dsa_kernel(qb_tab, kb_tab, q_ref, k_ref, v_ref, qi_ref, kia_ref, kib_ref, wi_ref, o_ref,
                keys_scr, planes_scr, thr_scr, tie_scr, ngt_scr, nge_scr,
                m_scr, l_scr, acc_scr, bias_scr, s0_scr, s1_scr, r0_scr, r1_scr,
                *, n_pair, k_top, idx_bits):
    heads, qb_rows, _ = acc_scr.shape
    kb_rows = keys_scr.shape[2]
    qb = qb_tab[pl.program_id(1)]
    kb = kb_tab[pl.program_id(1)]
    q0 = qb * qb_rows
    last_kb = (q0 + qb_rows - 1) // kb_rows
    n_chunk = last_kb + 1
    row = lax.broadcasted_iota(I32, (qb_rows, kb_rows), 0) + q0
    col = lax.broadcasted_iota(I32, (qb_rows, kb_rows), 1)
    nt = (((1,), (1,)), ((), ()))

    @pl.when(kb == 0)
    def _select():
        wi = wi_ref[0]

        def score_chunk(c, carry):
            k0 = pl.multiple_of(c * kb_rows, kb_rows)
            ka = kia_ref[0, pl.ds(k0, kb_rows), :]
            kbm = kib_ref[0, pl.ds(k0, kb_rows), :]
            sc = jnp.zeros((qb_rows, kb_rows), F32)
            for p in range(n_pair):
                qp = qi_ref[0, :, p * LANES:(p + 1) * LANES]
                sa = lax.dot_general(qp, ka, nt, preferred_element_type=F32)
                sb = lax.dot_general(qp, kbm, nt, preferred_element_type=F32)
                sc = sc + wi[:, 2 * p:2 * p + 1] * jnp.maximum(sa, 0.0)
                sc = sc + wi[:, 2 * p + 1:2 * p + 2] * jnp.maximum(sb, 0.0)
            bits = lax.bitcast_convert_type(sc, I32)
            key = bits ^ ((bits >> 31) & INT_MAX)
            keys_scr[c] = jnp.where(col + k0 <= row, key, INT_MIN)
            return carry

        lax.fori_loop(0, n_chunk, score_chunk, 0)

        slices_per_chunk = kb_rows // LANES
        chunks_per_set = WORD_BITS // slices_per_chunk
        n_slices = keys_scr.shape[0] * slices_per_chunk
        n_sets = planes_scr.shape[2] // LANES
        live_sets = (n_chunk + chunks_per_set - 1) // chunks_per_set

        def fill_chunk(c, carry):
            keys_scr[c] = jnp.full((qb_rows, kb_rows), INT_MIN, I32)
            return carry

        lax.fori_loop(n_chunk, jnp.minimum(live_sets * chunks_per_set, keys_scr.shape[0]), fill_chunk, 0)

        def pack_rows(g, carry, st):
            r0 = pl.multiple_of(g * SUBLANES, SUBLANES)
            words = []
            for s in range(WORD_BITS):
                sl = st * WORD_BITS + s
                if sl < n_slices:
                    c, off = divmod(sl, slices_per_chunk)
                    words.append(keys_scr[c, pl.ds(r0, SUBLANES), off * LANES:(off + 1) * LANES])
                else:
                    words.append(jnp.full((SUBLANES, LANES), INT_MIN, I32))
            words = _bit_transpose32(words)
            words[0] = ~words[0]
            for i in range(WORD_BITS):
                planes_scr[i, pl.ds(r0, SUBLANES), st * LANES:(st + 1) * LANES] = words[i]
            return carry

        for st in range(n_sets):
            @pl.when(st < live_sets)
            def _(st=st):
                lax.fori_loop(0, qb_rows // SUBLANES, functools.partial(pack_rows, st=st), 0)

        lane_ones = jnp.ones((LANES, LANES), BF16)

        def radix_select(live):
            width = live * LANES

            def row_total(pc):
                tot = pc[:, :LANES]
                for st in range(1, live):
                    tot = tot + pc[:, st * LANES:(st + 1) * LANES]
                return jnp.dot(tot.astype(F32).astype(BF16), lane_ones, preferred_element_type=F32)

            def all_sets(mask):
                return jnp.concatenate([mask] * live, axis=1)

            def select_two_bits(i, carry):
                cand, above, prefix = carry
                hi = planes_scr[2 * i, :, :width]
                lo = planes_scr[2 * i + 1, :, :width]
                c1 = cand & hi
                c0 = cand ^ c1
                c11 = c1 & lo
                c10 = c1 ^ c11
                c01 = c0 & lo
                c00 = c0 ^ c01
                r11 = above + row_total(lax.population_count(c11))
                r10 = r11 + row_total(lax.population_count(c10))
                r01 = r10 + row_total(lax.population_count(c01))
                t11 = r11 >= k_top
                t10 = r10 >= k_top
                t01 = r01 >= k_top
                cand = jnp.where(all_sets(t11), c11,
                                 jnp.where(all_sets(t10), c10, jnp.where(all_sets(t01), c01, c00)))
                above = jnp.where(t11, above, jnp.where(t10, r11, jnp.where(t01, r10, r01)))
                bits = jnp.where(t11, 3, jnp.where(t10, 2, jnp.where(t01, 1, 0)))
                prefix = prefix | (bits << (WORD_BITS - 2 - 2 * i))
                return cand, above, prefix

            cand, above, prefix = lax.fori_loop(
                0, WORD_BITS // 2, select_two_bits,
                (jnp.full((qb_rows, width), -1, I32),
                 jnp.zeros((qb_rows, LANES), F32), jnp.zeros((qb_rows, LANES), I32)))
            equal = row_total(lax.population_count(cand))
            thr_scr[...] = (prefix ^ INT_MIN)[:, :1]
            tie_scr[...] = jnp.full((qb_rows, 1), INT_MAX, I32)
            ngt_scr[...] = above[:, :1]
            nge_scr[...] = (above + equal)[:, :1]

        for live in range(1, n_sets + 1):
            pl.when(live_sets == live)(functools.partial(radix_select, live))

        rg = min(qb_rows, SEARCH_ROWS)
        lane_col = lax.broadcasted_iota(I32, (rg, LANES), 1)

        for r in range(qb_rows // rg):
            rs = slice(r * rg, (r + 1) * rg)

            def count(pred, rs=rs):
                def body(c, acc):
                    for t in range(kb_rows // LANES):
                        kk = keys_scr[c, rs, t * LANES:(t + 1) * LANES]
                        idx = lane_col + (c * kb_rows + t * LANES)
                        acc = acc + jnp.where(pred(kk, idx), 1.0, 0.0)
                    return acc
                acc = lax.fori_loop(0, n_chunk, body, jnp.zeros((rg, LANES), F32))
                return jnp.sum(acc, axis=1, keepdims=True)

            def wide(v):
                return jnp.broadcast_to(v, (rg, LANES))

            @pl.when(jnp.max(nge_scr[rs, :]) > k_top)
            def _ties(count=count, rs=rs):
                thr_w = wide(thr_scr[rs, :])
                need = k_top - ngt_scr[rs, :]

                def tie_bit(i, jt):
                    cand = jt + (jnp.int32(1) << (idx_bits - 1 - i))
                    cand_w = wide(cand)
                    cnt = count(lambda kk, idx: (kk == thr_w) & (idx < cand_w))
                    return jnp.where(cnt < need, cand, jt)

                tie_scr[rs, :] = lax.fori_loop(0, idx_bits, tie_bit, jnp.zeros((rg, 1), I32))

        m_scr[...] = jnp.full(m_scr.shape, M_INIT, F32)
        l_scr[...] = jnp.zeros_like(l_scr)
        acc_scr[...] = jnp.zeros_like(acc_scr)

    def _attend():
        ones = jnp.ones((kb_rows, HEAD_DIM), BF16)

        kk = keys_scr[kb]
        thr = thr_scr[...]
        lim = jnp.minimum(tie_scr[...], row[:, :1])
        tie_bias = jnp.where(col + kb * kb_rows <= lim, 0.0, MASK_VALUE)
        bias_scr[...] = jnp.where(kk > thr, 0.0, jnp.where(kk == thr, tie_bias, MASK_VALUE))

        def logits(h, s_ref, r_ref):
            s = lax.dot_general(q_ref[0, h], k_ref[0, h], nt, preferred_element_type=F32) + bias_scr[...]
            s_ref[...] = s
            r_ref[...] = jnp.broadcast_to(jnp.max(s, axis=1, keepdims=True), r_ref.shape)

        def accumulate(h, s_ref, r_ref):
            m_old = m_scr[h]
            m_new = jnp.maximum(m_old, r_ref[...])
            alpha = jnp.exp2(m_old - m_new)
            p = jnp.concatenate(
                [jnp.exp2(s_ref[:, t * LANES:(t + 1) * LANES] - m_new).astype(BF16)
                 for t in range(kb_rows // LANES)], axis=1)
            v_ext = jnp.concatenate([v_ref[0, h], ones], axis=1)
            pv = jnp.dot(p, v_ext, preferred_element_type=F32)
            acc_scr[h] = alpha * acc_scr[h] + pv[:, :HEAD_DIM]
            l_scr[h] = alpha * l_scr[h] + pv[:, HEAD_DIM:]
            m_scr[h] = m_new

        bufs = ((s0_scr, r0_scr), (s1_scr, r1_scr))
        logits(0, *bufs[0])
        for h in range(heads):
            if h + 1 < heads:
                logits(h + 1, *bufs[(h + 1) % 2])
            accumulate(h, *bufs[h % 2])

    _attend()

    @pl.when(kb == last_kb)
    def _finish():
        for h in range(heads):
            o_ref[0, :, h * HEAD_DIM:(h + 1) * HEAD_DIM] = (acc_scr[h] / l_scr[h]).astype(o_ref.dtype)


def _dsa_attention(qkv, qi, kia, kib, wi, k_top):
    bsz, heads3, seq, _ = qkv.shape
    heads = heads3 // 3
    d = heads * HEAD_DIM
    n_pair = qi.shape[-1] // LANES
    qb_rows = _pick(seq, (256, 128))
    kb_rows = _pick(seq, (512, 256, 128))
    n_kb = seq // kb_rows
    n_sets = -(-seq // (WORD_BITS * LANES))

    pairs = [(i, j) for i in range(seq // qb_rows) for j in range((i * qb_rows + qb_rows - 1) // kb_rows + 1)]
    qb_tab = jnp.asarray([p[0] for p in pairs], I32)
    kb_tab = jnp.asarray([p[1] for p in pairs], I32)

    kern = functools.partial(_dsa_kernel, n_pair=n_pair, k_top=k_top, idx_bits=seq.bit_length())
    grid_spec = pltpu.PrefetchScalarGridSpec(
        num_scalar_prefetch=2,
        grid=(bsz, len(pairs)),
        in_specs=[
            pl.BlockSpec((1, heads, qb_rows, HEAD_DIM), lambda b, s, qt, kt: (b, 0, qt[s], 0)),
            pl.BlockSpec((1, heads, kb_rows, HEAD_DIM), lambda b, s, qt, kt: (b, 1, kt[s], 0)),
            pl.BlockSpec((1, heads, kb_rows, HEAD_DIM), lambda b, s, qt, kt: (b, 2, kt[s], 0)),
            pl.BlockSpec((1, qb_rows, n_pair * LANES), lambda b, s, qt, kt: (b, qt[s], 0)),
            pl.BlockSpec((1, seq, LANES), lambda b, s, qt, kt: (b, 0, 0)),
            pl.BlockSpec((1, seq, LANES), lambda b, s, qt, kt: (b, 0, 0)),
            pl.BlockSpec((1, qb_rows, LANES), lambda b, s, qt, kt: (b, qt[s], 0)),
        ],
        out_specs=pl.BlockSpec((1, qb_rows, d), lambda b, s, qt, kt: (b, qt[s], 0)),
        scratch_shapes=[
            pltpu.VMEM((n_kb, qb_rows, kb_rows), I32),
            pltpu.VMEM((WORD_BITS, qb_rows, n_sets * LANES), I32),
            pltpu.VMEM((qb_rows, 1), I32),
            pltpu.VMEM((qb_rows, 1), I32),
            pltpu.VMEM((qb_rows, 1), F32),
            pltpu.VMEM((qb_rows, 1), F32),
            pltpu.VMEM((heads, qb_rows, LANES), F32),
            pltpu.VMEM((heads, qb_rows, LANES), F32),
            pltpu.VMEM((heads, qb_rows, HEAD_DIM), F32),
            pltpu.VMEM((qb_rows, kb_rows), F32),
            pltpu.VMEM((qb_rows, kb_rows), F32),
            pltpu.VMEM((qb_rows, kb_rows), F32),
            pltpu.VMEM((qb_rows, LANES), F32),
            pltpu.VMEM((qb_rows, LANES), F32),
        ],
    )
    return pl.pallas_call(
        kern,
        out_shape=jax.ShapeDtypeStruct((bsz, seq, d), BF16),
        grid_spec=grid_spec,
        compiler_params=_params(("parallel", "arbitrary")),
        name="dsa_select_attend",
    )(qb_tab, kb_tab, qkv, qkv, qkv, qi, kia, kib, wi)


def _rope_tables(positions, dim):
    inv = 1.0 / (ROPE_THETA ** (jnp.arange(0, dim, 2, dtype=F32) / dim))
    half = dim // 2
    lane = jnp.arange(LANES)
    inv_t = inv[lane % half]
    sign = jnp.where((lane % dim) < half, -1.0, 1.0).astype(F32)
    ang = positions.astype(F32)[..., None] * inv_t
    return jnp.cos(ang), jnp.sin(ang) * sign


def kernel(x, c, positions, ada_w, ada_b, ln_g, ln_b, s5_in_w, s5_a_re, s5_a_im, s5_log_dt, s5_b_re, s5_b_im, s5_c_re, s5_c_im, s5_d, s5_glu_w, s5_glu_b, dsa_in_w, dsa_out_w, ffn_w_in, ffn_w_out):
    bsz, seq, d = x.shape
    depth = ada_w.shape[0]
    alpha = (2.0 * depth) ** 0.25
    idx_heads = (dsa_in_w.shape[-1] - 3 * d - IDX_DIM) // (IDX_DIM + 1)
    k_top = min(TOPK_MAX, seq // 4)
    seg = min(S5_SEG, seq // SUBLANES)

    cos_h, sin_h = _rope_tables(positions, HEAD_DIM)
    cos_i, sin_i = _rope_tables(positions, IDX_DIM)
    mod = _ada_mod(c, ada_w, ada_b)
    ffn_in = ffn_w_in.astype(BF16)
    ffn_out = ffn_w_out.astype(BF16)

    for i in range(depth):
        sh1, sc1, g1, sh2, sc2, g2 = [m[:, None, :] for m in jnp.split(mod[i], 6, axis=-1)]
        j = i // 2
        if i % 2 == 0:
            prep = _s5_discretize(s5_a_re[j], s5_a_im[j], s5_log_dt[j], s5_b_re[j], s5_b_im[j],
                                  s5_c_re[j], s5_c_im[j], seg)
            u = _mod_matmul(x, sc1, sh1, s5_in_w[j].astype(BF16), F32)
            gl = _s5_scan(u, prep, s5_d[j], seg)
            x1 = _matmul_res_ln(gl, s5_glu_w[j].astype(BF16), s5_glu_b[j], x, g1,
                                ln_g[i, 0], ln_b[i, 0], alpha, glu=True)
        else:
            w = dsa_in_w[j].astype(BF16)
            n_qi = idx_heads * IDX_DIM
            pad = jnp.zeros((d, LANES - IDX_DIM - idx_heads), w.dtype)
            w_idx = jnp.concatenate(
                [w[:, 3 * d:3 * d + n_qi], w[:, 3 * d + n_qi + idx_heads:], w[:, 3 * d + n_qi:3 * d + n_qi + idx_heads], pad],
                axis=1)
            qkv = _qkv_proj(x, sc1, sh1, w, cos_h, sin_h)
            qi, kia, kib, wi = _idx_proj(x, sc1, sh1, w_idx, cos_i, sin_i, idx_heads)
            att = _dsa_attention(qkv, qi, kia, kib, wi, k_top)
            x1 = _matmul_res_ln(att, dsa_out_w[j].astype(BF16), None, x, g1,
                                ln_g[i, 0], ln_b[i, 0], alpha, glu=False)
        x = _ffn_res_ln(x1, sc2, sh2, g2, ffn_in, ffn_out, i, ln_g[i, 1], ln_b[i, 1], alpha)
    return x
```

```python
import functools
import math

import jax
import jax.numpy as jnp
from jax import lax
from jax.experimental import pallas as pl
from jax.experimental.pallas import tpu as pltpu

F32 = jnp.float32
BF16 = jnp.bfloat16
I32 = jnp.int32

S5_GROUP = 16
S5_STATE = 64
HEAD_DIM = 128
IDX_DIM = 64
TOPK_MAX = 256
ROPE_THETA = 10000.0
LN_EPS = 1e-5

LANES = 128
SUBLANES = 8
MXU_WIDTH = 256
VMEM_LIMIT_BYTES = 56 * 1024 * 1024
RESIDENT_WEIGHT_BYTES = 16 * 1024 * 1024

S5_PACK_GROUPS = 16
S5_SEG = 256

INT_MIN = -(2 ** 31)
INT_MAX = 2 ** 31 - 1
SEARCH_ROWS = 128
WORD_BITS = 32

MASK_VALUE = -2e30
M_INIT = -1e30


def _pick(n, cands):
    for c in cands:
        if n % c == 0:
            return c
    return n


def _row_halves(rows):
    half = rows // 2
    return (slice(0, half), slice(half, rows))


def _params(sem):
    return pltpu.CompilerParams(dimension_semantics=sem, vmem_limit_bytes=VMEM_LIMIT_BYTES)


def _ada_kernel(c_ref, w_ref, b_ref, o_ref):
    ca = jax.nn.silu(c_ref[...]).astype(BF16)
    o_ref[0] = jnp.dot(ca, w_ref[0].astype(BF16), preferred_element_type=F32) + b_ref[0]


def _ada_mod(c, ada_w, ada_b):
    bsz, d = c.shape
    depth, _, n = ada_w.shape
    rows = SUBLANES * ((bsz + SUBLANES - 1) // SUBLANES)
    cp = jnp.zeros((rows, d), F32).at[:bsz].set(c)
    tn = _pick(n, (1024, 512, 256, 128))
    out = pl.pallas_call(
        _ada_kernel,
        out_shape=jax.ShapeDtypeStruct((depth, rows, n), F32),
        grid=(depth, n // tn),
        in_specs=[
            pl.BlockSpec((rows, d), lambda l, j: (0, 0)),
            pl.BlockSpec((1, d, tn), lambda l, j: (l, 0, j)),
            pl.BlockSpec((1, 1, tn), lambda l, j: (l, 0, j)),
        ],
        out_specs=pl.BlockSpec((1, rows, tn), lambda l, j: (l, 0, j)),
        compiler_params=_params(("arbitrary", "arbitrary")),
        name="ada_mod",
    )(cp, ada_w, ada_b.reshape(depth, 1, n))
    return out[:, :bsz]


def _modulate(x_ref, sc_ref, sh_ref):
    return (x_ref[0] * (1.0 + sc_ref[0]) + sh_ref[0]).astype(BF16)


def _modmm_kernel(x_ref, sc_ref, sh_ref, w_ref, o_ref, h_scr):
    @pl.when(pl.program_id(2) == 0)
    def _():
        h_scr[...] = _modulate(x_ref, sc_ref, sh_ref)

    for rs in _row_halves(h_scr.shape[0]):
        o_ref[0, rs, :] = jnp.dot(h_scr[rs, :], w_ref[...], preferred_element_type=F32).astype(o_ref.dtype)


def _mod_matmul(x, sc, sh, w, out_dtype):
    bsz, seq, d = x.shape
    n = w.shape[1]
    tm = _pick(seq, (512, 256, 128))
    tn = n if 2 * d * n * w.dtype.itemsize <= RESIDENT_WEIGHT_BYTES else _pick(n, (1024, 512, 256, 128))
    return pl.pallas_call(
        _modmm_kernel,
        out_shape=jax.ShapeDtypeStruct((bsz, seq, n), out_dtype),
        grid=(bsz, seq // tm, n // tn),
        in_specs=[
            pl.BlockSpec((1, tm, d), lambda b, i, j: (b, i, 0)),
            pl.BlockSpec((1, 1, d), lambda b, i, j: (b, 0, 0)),
            pl.BlockSpec((1, 1, d), lambda b, i, j: (b, 0, 0)),
            pl.BlockSpec((d, tn), lambda b, i, j: (0, j)),
        ],
        out_specs=pl.BlockSpec((1, tm, tn), lambda b, i, j: (b, i, j)),
        scratch_shapes=[pltpu.VMEM((tm, d), BF16)],
        compiler_params=_params(("parallel", "parallel", "arbitrary")),
        name="mod_matmul",
    )(x, sc, sh, w)


def _s5_kernel(*refs, seg, width, lane_tiles):
    u_refs = refs[:lane_tiles]
    (bb_ref, cb_ref, lre_ref, lim_ref, ltre_ref, ltim_ref, d_ref, o_ref,
     xs_scr, carry_scr, cin_scr, up_scr) = refs[lane_tiles:]
    ns = xs_scr.shape[1] // 2

    @pl.when(pl.program_id(2) == 0)
    def _():
        carry_scr[...] = jnp.zeros_like(carry_scr)

    for t, u_ref in enumerate(u_refs):
        for i in range(seg):
            up_scr[t, i * SUBLANES:(i + 1) * SUBLANES, :] = u_ref[0, pl.ds(i, SUBLANES, stride=seg), :]
    u = jnp.concatenate([up_scr[t] for t in range(len(u_refs))], axis=1)
    halves = _row_halves(u.shape[0])
    for rs in halves:
        xs_scr[rs, :] = jnp.dot(u[rs].astype(BF16), bb_ref[0], preferred_element_type=F32)

    for part in range(ns // width):
        cr = slice(part * width, (part + 1) * width)
        ci = slice(ns + part * width, ns + (part + 1) * width)
        lr = jnp.broadcast_to(lre_ref[0, :, cr], (SUBLANES, width))
        li = jnp.broadcast_to(lim_ref[0, :, cr], (SUBLANES, width))

        def local_step(i, st, cr=cr, ci=ci, lr=lr, li=li):
            sr, si = st
            r0 = pl.multiple_of(i * SUBLANES, SUBLANES)
            nr = lr * sr - li * si + xs_scr[pl.ds(r0, SUBLANES), cr]
            ni = lr * si + li * sr + xs_scr[pl.ds(r0, SUBLANES), ci]
            xs_scr[pl.ds(r0, SUBLANES), cr] = nr
            xs_scr[pl.ds(r0, SUBLANES), ci] = ni
            return nr, ni

        zero = jnp.zeros((SUBLANES, width), F32)
        er, ei = lax.fori_loop(0, seg, local_step, (zero, zero), unroll=4)

        ltr = ltre_ref[0, :, cr]
        lti = ltim_ref[0, :, cr]
        c_r = carry_scr[:, cr]
        c_i = carry_scr[:, ci]
        for s in range(SUBLANES):
            cin_scr[s:s + 1, cr] = c_r
            cin_scr[s:s + 1, ci] = c_i
            e_r = er[s:s + 1, :]
            e_i = ei[s:s + 1, :]
            c_r, c_i = ltr * c_r - lti * c_i + e_r, ltr * c_i + lti * c_r + e_i
        carry_scr[:, cr] = c_r
        carry_scr[:, ci] = c_i

        def carry_step(i, st, cr=cr, ci=ci, lr=lr, li=li):
            pr, pi_ = st
            r0 = pl.multiple_of(i * SUBLANES, SUBLANES)
            nr = lr * pr - li * pi_
            ni = lr * pi_ + li * pr
            xs_scr[pl.ds(r0, SUBLANES), cr] += nr
            xs_scr[pl.ds(r0, SUBLANES), ci] += ni
            return nr, ni

        lax.fori_loop(0, seg, carry_step, (cin_scr[:, cr], cin_scr[:, ci]), unroll=4)

    for rs in halves:
        y = jnp.dot(xs_scr[rs, :].astype(BF16), cb_ref[0], preferred_element_type=F32)
        g = jax.nn.gelu(y + d_ref[0] * u[rs])
        for t in range(len(u_refs)):
            up_scr[t, rs, :] = g[:, t * LANES:(t + 1) * LANES]
    for t in range(len(u_refs)):
        for s in range(SUBLANES):
            o_ref[0, s * seg:(s + 1) * seg, t * LANES:(t + 1) * LANES] = (
                up_scr[t, pl.ds(s, seg, stride=SUBLANES), :].astype(o_ref.dtype))


def _cmul(ar, ai, br, bi):
    return ar * br - ai * bi, ar * bi + ai * br


def _s5_discretize(a_re, a_im, log_dt, b_re, b_im, c_re, c_im, seg):
    g, n = a_re.shape
    p = b_re.shape[-1]
    pg = S5_PACK_GROUPS
    packs = g // pg
    a_re, a_im = a_re.astype(F32), a_im.astype(F32)
    dt = jnp.exp(log_dt.astype(F32))[:, None]
    mag = jnp.exp(a_re * dt)
    lb_re, lb_im = mag * jnp.cos(a_im * dt), mag * jnp.sin(a_im * dt)
    den = a_re * a_re + a_im * a_im
    nr, ni = lb_re - 1.0, lb_im
    f_re = (nr * a_re + ni * a_im) / den
    f_im = (ni * a_re - nr * a_im) / den
    bb_re, bb_im = _cmul(f_re[..., None], f_im[..., None], b_re.astype(F32), b_im.astype(F32))
    lt_re, lt_im = lb_re, lb_im
    for _ in range(int(math.log2(seg))):
        lt_re, lt_im = _cmul(lt_re, lt_im, lt_re, lt_im)
    on_diag = (jnp.arange(pg * p)[:, None] // p) == (jnp.arange(pg * n)[None, :] // n)

    def blk_diag(m):
        tiled = jnp.tile(m.astype(BF16).reshape(packs, pg * p, n), (1, 1, pg))
        return jnp.where(on_diag[None], tiled, 0)

    b_blk = jnp.concatenate([blk_diag(jnp.swapaxes(bb_re, 1, 2)), blk_diag(jnp.swapaxes(bb_im, 1, 2))], axis=-1)
    c_blk = jnp.swapaxes(jnp.concatenate([blk_diag(c_re), blk_diag(-c_im)], axis=-1), 1, 2)

    def vec(m):
        return m.reshape(packs, 1, pg * n)

    return b_blk, c_blk, vec(lb_re), vec(lb_im), vec(lt_re), vec(lt_im)


def _s5_scan(u, prep, d_skip, seg):
    bsz, seq, d = u.shape
    b_blk, c_blk, lre, lim, ltre, ltim = prep
    packs, pw, ns2 = b_blk.shape
    ns = ns2 // 2
    rows = SUBLANES * seg
    width = _pick(ns, (1024, 512, 256, 128))
    lane_tiles = pw // LANES
    kern = functools.partial(_s5_kernel, seg=seg, width=width, lane_tiles=lane_tiles)
    vspec = pl.BlockSpec((1, 1, ns), lambda b, k, m: (k, 0, 0))
    u_specs = [pl.BlockSpec((1, rows, LANES), lambda b, k, m, t=t: (b, m, k * lane_tiles + t))
               for t in range(lane_tiles)]
    return pl.pallas_call(
        kern,
        out_shape=jax.ShapeDtypeStruct((bsz, seq, d), BF16),
        grid=(bsz, packs, seq // rows),
        in_specs=u_specs + [
            pl.BlockSpec((1, pw, ns2), lambda b, k, m: (k, 0, 0)),
            pl.BlockSpec((1, ns2, pw), lambda b, k, m: (k, 0, 0)),
            vspec, vspec, vspec, vspec,
            pl.BlockSpec((1, 1, pw), lambda b, k, m: (k, 0, 0)),
        ],
        out_specs=pl.BlockSpec((1, rows, pw), lambda b, k, m: (b, m, k)),
        scratch_shapes=[
            pltpu.VMEM((rows, ns2), F32),
            pltpu.VMEM((1, ns2), F32),
            pltpu.VMEM((SUBLANES, ns2), F32),
            pltpu.VMEM((lane_tiles, rows, LANES), F32),
        ],
        compiler_params=_params(("parallel", "parallel", "arbitrary")),
        name="s5_scan",
    )(*([u] * lane_tiles), b_blk, c_blk, lre, lim, ltre, ltim, d_skip.reshape(packs, 1, pw).astype(F32))


def _layer_norm_rows(tiles, lng_ref, lnb_ref, o_ref, d):
    tn = tiles[0].shape[1]
    tot = tiles[0].sum(axis=1, keepdims=True)
    for t in tiles[1:]:
        tot = tot + t.sum(axis=1, keepdims=True)
    mu = tot * (1.0 / d)
    sq = jnp.square(tiles[0] - mu).sum(axis=1, keepdims=True)
    for t in tiles[1:]:
        sq = sq + jnp.square(t - mu).sum(axis=1, keepdims=True)
    inv = lax.rsqrt(sq * (1.0 / d) + LN_EPS)
    for k, t in enumerate(tiles):
        cs = slice(k * tn, (k + 1) * tn)
        o_ref[0, :, cs] = ((t - mu) * inv * lng_ref[:, cs] + lnb_ref[:, cs]).astype(o_ref.dtype)


def _mmln_kernel(*refs, glu, nj, alpha, d):
    if glu:
        a_ref, w1_ref, w2_ref, b1_ref, b2_ref, x_ref, g_ref, lng_ref, lnb_ref, o_ref, r_scr = refs
    else:
        a_ref, w1_ref, x_ref, g_ref, lng_ref, lnb_ref, o_ref, r_scr = refs
    j = pl.program_id(2)
    for rs in _row_halves(a_ref.shape[1]):
        a = a_ref[0, rs, :]
        y = jnp.dot(a, w1_ref[...], preferred_element_type=F32)
        if glu:
            y = y + b1_ref[...]
            gate = jnp.dot(a, w2_ref[...], preferred_element_type=F32) + b2_ref[...]
            y = y * jax.nn.sigmoid(gate)
        r_scr[j, rs, :] = alpha * x_ref[0, rs, :] + (1.0 + g_ref[0]) * y

    @pl.when(j == nj - 1)
    def _():
        _layer_norm_rows([r_scr[t] for t in range(nj)], lng_ref, lnb_ref, o_ref, d)


def _matmul_res_ln(a, w, bias, xres, gate, ln_g, ln_b, alpha, glu):
    bsz, seq, k = a.shape
    d = xres.shape[-1]
    if glu:
        tm = _pick(seq, (1024, 512, 256, 128))
        tn = _pick(d, (512, 256, 128))
    else:
        tm = _pick(seq, (512, 256, 128))
        tn = d if 2 * k * d * w.dtype.itemsize <= RESIDENT_WEIGHT_BYTES else _pick(d, (512, 256, 128))
    nj = d // tn
    a_spec = pl.BlockSpec((1, tm, k), lambda b, i, j: (b, i, 0))
    w1_spec = pl.BlockSpec((k, tn), lambda b, i, j: (0, j))
    tail_specs = [
        pl.BlockSpec((1, tm, tn), lambda b, i, j: (b, i, j)),
        pl.BlockSpec((1, 1, tn), lambda b, i, j: (b, 0, j)),
        pl.BlockSpec((1, d), lambda b, i, j: (0, 0)),
        pl.BlockSpec((1, d), lambda b, i, j: (0, 0)),
    ]
    tail = (xres, gate, ln_g.reshape(1, d).astype(F32), ln_b.reshape(1, d).astype(F32))
    if glu:
        in_specs = [a_spec, w1_spec,
                    pl.BlockSpec((k, tn), lambda b, i, j: (0, j + nj)),
                    pl.BlockSpec((1, tn), lambda b, i, j: (0, j)),
                    pl.BlockSpec((1, tn), lambda b, i, j: (0, j + nj))] + tail_specs
        b2d = bias.reshape(1, 2 * d).astype(F32)
        args = (a, w, w, b2d, b2d) + tail
    else:
        in_specs = [a_spec, w1_spec] + tail_specs
        args = (a, w) + tail
    kern = functools.partial(_mmln_kernel, glu=glu, nj=nj, alpha=alpha, d=d)
    return pl.pallas_call(
        kern,
        out_shape=jax.ShapeDtypeStruct((bsz, seq, d), F32),
        grid=(bsz, seq // tm, nj),
        in_specs=in_specs,
        out_specs=pl.BlockSpec((1, tm, d), lambda b, i, j: (b, i, 0)),
        scratch_shapes=[pltpu.VMEM((nj, tm, tn), F32)],
        compiler_params=_params(("parallel", "parallel", "arbitrary")),
        name="matmul_res_ln",
    )(*args)


def _ffn_kernel(x_ref, sc_ref, sh_ref, g_ref, wg_ref, wu_ref, wo_ref, lng_ref, lnb_ref, o_ref,
                h_scr, acc_scr, *, nf, alpha, d, tn):
    f = pl.program_id(2)

    @pl.when(f == 0)
    def _():
        h_scr[...] = _modulate(x_ref, sc_ref, sh_ref)
        acc_scr[...] = jnp.zeros_like(acc_scr)

    h = h_scr[...]
    a_g = jnp.dot(h, wg_ref[0], preferred_element_type=F32)
    a_u = jnp.dot(h, wu_ref[0], preferred_element_type=F32)
    act = (jax.nn.silu(a_g) * a_u).astype(BF16)
    acc_scr[...] += jnp.dot(act, wo_ref[0], preferred_element_type=F32)

    @pl.when(f == nf - 1)
    def _():
        tiles = []
        for k in range(d // tn):
            cs = slice(k * tn, (k + 1) * tn)
            tiles.append(alpha * x_ref[0, :, cs] + (1.0 + g_ref[0, :, cs]) * acc_scr[:, cs])
        _layer_norm_rows(tiles, lng_ref, lnb_ref, o_ref, d)


def _ffn_res_ln(x, sc, sh, gate, w_in, w_out, layer, ln_g, ln_b, alpha):
    bsz, seq, d = x.shape
    dff = w_out.shape[1]
    tm = _pick(seq, (512, 256, 128))
    tf = _pick(dff, (512, 256, 128))
    nf = dff // tf
    tn = _pick(d, (512, 256, 128))
    vec = pl.BlockSpec((1, 1, d), lambda b, i, f: (b, 0, 0))
    kern = functools.partial(_ffn_kernel, nf=nf, alpha=alpha, d=d, tn=tn)
    return pl.pallas_call(
        kern,
        out_shape=jax.ShapeDtypeStruct((bsz, seq, d), F32),
        grid=(bsz, seq // tm, nf),
        in_specs=[
            pl.BlockSpec((1, tm, d), lambda b, i, f: (b, i, 0)),
            vec, vec, vec,
            pl.BlockSpec((1, d, tf), lambda b, i, f: (layer, 0, f)),
            pl.BlockSpec((1, d, tf), lambda b, i, f: (layer, 0, f + nf)),
            pl.BlockSpec((1, tf, d), lambda b, i, f: (layer, f, 0)),
            pl.BlockSpec((1, d), lambda b, i, f: (0, 0)),
            pl.BlockSpec((1, d), lambda b, i, f: (0, 0)),
        ],
        out_specs=pl.BlockSpec((1, tm, d), lambda b, i, f: (b, i, 0)),
        scratch_shapes=[pltpu.VMEM((tm, d), BF16), pltpu.VMEM((tm, d), F32)],
        compiler_params=_params(("parallel", "parallel", "arbitrary")),
        name="ffn_res_ln",
    )(x, sc, sh, gate, w_in, w_in, w_out, ln_g.reshape(1, d).astype(F32), ln_b.reshape(1, d).astype(F32))


def _qkv_kernel(x_ref, sc_ref, sh_ref, w_ref, cos_ref, sin_ref, o_ref, h_scr, *, tiles_per_tensor, q_scale):
    j = pl.program_id(2)

    @pl.when(j == 0)
    def _():
        h_scr[...] = _modulate(x_ref, sc_ref, sh_ref)

    tensor = j // tiles_per_tensor
    scale = jnp.where(tensor == 0, q_scale, 1.0)
    is_v = tensor == 2
    chunk = min(MXU_WIDTH, w_ref.shape[1])
    for rs in _row_halves(h_scr.shape[0]):
        a = jnp.where(is_v, 1.0, cos_ref[0, rs, :] * scale)
        b = jnp.where(is_v, 0.0, sin_ref[0, rs, :] * scale)
        for c in range(w_ref.shape[1] // chunk):
            acc = jnp.dot(h_scr[rs, :], w_ref[:, c * chunk:(c + 1) * chunk], preferred_element_type=F32)
            for hh in range(chunk // HEAD_DIM):
                xc = acc[:, hh * HEAD_DIM:(hh + 1) * HEAD_DIM]
                xc = xc * a + pltpu.roll(xc, HEAD_DIM // 2, 1) * b
                o_ref[0, c * (chunk // HEAD_DIM) + hh, rs, :] = xc.astype(o_ref.dtype)


def _qkv_proj(x, sc, sh, w_qkv, cos_t, sin_t):
    bsz, seq, d = x.shape
    tm = _pick(seq, (1024, 512, 256, 128))
    tn = _pick(d, (2048, 1024, 512, 256, 128))
    hpt = tn // HEAD_DIM
    kern = functools.partial(_qkv_kernel, tiles_per_tensor=d // tn, q_scale=HEAD_DIM ** -0.5 * math.log2(math.e))
    return pl.pallas_call(
        kern,
        out_shape=jax.ShapeDtypeStruct((bsz, 3 * d // HEAD_DIM, seq, HEAD_DIM), BF16),
        grid=(bsz, seq // tm, 3 * d // tn),
        in_specs=[
            pl.BlockSpec((1, tm, d), lambda b, i, j: (b, i, 0)),
            pl.BlockSpec((1, 1, d), lambda b, i, j: (b, 0, 0)),
            pl.BlockSpec((1, 1, d), lambda b, i, j: (b, 0, 0)),
            pl.BlockSpec((d, tn), lambda b, i, j: (0, j)),
            pl.BlockSpec((1, tm, HEAD_DIM), lambda b, i, j: (b, i, 0)),
            pl.BlockSpec((1, tm, HEAD_DIM), lambda b, i, j: (b, i, 0)),
        ],
        out_specs=pl.BlockSpec((1, hpt, tm, HEAD_DIM), lambda b, i, j: (b, j, i, 0)),
        scratch_shapes=[pltpu.VMEM((tm, d), BF16)],
        compiler_params=_params(("parallel", "parallel", "arbitrary")),
        name="dsa_qkv_proj",
    )(x, sc, sh, w_qkv, cos_t, sin_t)


def _idx_kernel(x_ref, sc_ref, sh_ref, w_ref, cos_ref, sin_ref, qi_ref, kia_ref, kib_ref, wi_ref,
                *, n_pair, idx_heads, w_scale):
    h = _modulate(x_ref, sc_ref, sh_ref)
    halves = _row_halves(h.shape[0])
    accs = [jnp.dot(h[rs], w_ref[...], preferred_element_type=F32) for rs in halves]
    lane = lax.broadcasted_iota(I32, (halves[0].stop, LANES), 1)
    first_half = (lane % IDX_DIM) < IDX_DIM // 2

    for rs, acc in zip(halves, accs):
        cos = cos_ref[0, rs, :]
        sin = sin_ref[0, rs, :]

        def rope(xc, cos=cos, sin=sin):
            partner = jnp.where(first_half, pltpu.roll(xc, LANES - IDX_DIM // 2, 1),
                                pltpu.roll(xc, IDX_DIM // 2, 1))
            return xc * cos + partner * sin

        for p in range(n_pair):
            cs = slice(p * LANES, (p + 1) * LANES)
            qi_ref[0, rs, cs] = rope(acc[:, cs]).astype(qi_ref.dtype)
        last = acc[:, n_pair * LANES:]
        ka = jnp.where(lane < IDX_DIM, rope(last), 0.0)
        kia_ref[0, rs, :] = ka.astype(kia_ref.dtype)
        kib_ref[0, rs, :] = pltpu.roll(ka, IDX_DIM, 1).astype(kib_ref.dtype)
        wi_ref[0, rs, :] = jnp.where(lane < idx_heads, pltpu.roll(last, LANES - IDX_DIM, 1), 0.0) * w_scale


def _idx_proj(x, sc, sh, w_idx, cos_t, sin_t, idx_heads):
    bsz, seq, d = x.shape
    n = w_idx.shape[1]
    n_pair = idx_heads // 2
    tm = _pick(seq, (1024, 512, 256, 128))
    kern = functools.partial(_idx_kernel, n_pair=n_pair, idx_heads=idx_heads,
                             w_scale=(idx_heads ** -0.5) * (IDX_DIM ** -0.5))
    row = lambda b, i: (b, i, 0)
    return pl.pallas_call(
        kern,
        out_shape=(
            jax.ShapeDtypeStruct((bsz, seq, n_pair * LANES), BF16),
            jax.ShapeDtypeStruct((bsz, seq, LANES), BF16),
            jax.ShapeDtypeStruct((bsz, seq, LANES), BF16),
            jax.ShapeDtypeStruct((bsz, seq, LANES), F32),
        ),
        grid=(bsz, seq // tm),
        in_specs=[
            pl.BlockSpec((1, tm, d), row),
            pl.BlockSpec((1, 1, d), lambda b, i: (b, 0, 0)),
            pl.BlockSpec((1, 1, d), lambda b, i: (b, 0, 0)),
            pl.BlockSpec((d, n), lambda b, i: (0, 0)),
            pl.BlockSpec((1, tm, LANES), row),
            pl.BlockSpec((1, tm, LANES), row),
        ],
        out_specs=(
            pl.BlockSpec((1, tm, n_pair * LANES), row),
            pl.BlockSpec((1, tm, LANES), row),
            pl.BlockSpec((1, tm, LANES), row),
            pl.BlockSpec((1, tm, LANES), row),
        ),
        compiler_params=_params(("parallel", "parallel")),
        name="dsa_idx_proj",
    )(x, sc, sh, w_idx, cos_t, sin_t)


def _bit_transpose32(words):
    a = list(words)
    j = WORD_BITS // 2
    mask = 0x0000FFFF
    while j:
        k = 0
        while k < WORD_BITS:
            t = (a[k] ^ lax.shift_right_logical(a[k + j], jnp.int32(j))) & jnp.int32(mask)
            a[k] = a[k] ^ t
            a[k + j] = a[k + j] ^ (t << j)
            k = (k + j + 1) & ~j
        j >>= 1
        if j:
            mask = (mask ^ (mask << j)) & 0xFFFFFFFF
    return a


def _dsa_kernel(qb_tab, kb_tab, q_ref, k_ref, v_ref, qi_ref, kia_ref, kib_ref, wi_ref, o_ref,
                keys_scr, planes_scr, thr_scr, tie_scr, ngt_scr, nge_scr,
                m_scr, l_scr, acc_scr, bias_scr, s0_scr, s1_scr, s2_scr, r0_scr, r1_scr, r2_scr,
                *, n_pair, k_top, idx_bits):
    heads, qb_rows, _ = acc_scr.shape
    kb_rows = keys_scr.shape[2]
    qb = qb_tab[pl.program_id(1)]
    kb = kb_tab[pl.program_id(1)]
    q0 = qb * qb_rows
    last_kb = (q0 + qb_rows - 1) // kb_rows
    n_chunk = last_kb + 1
    row = lax.broadcasted_iota(I32, (qb_rows, kb_rows), 0) + q0
    col = lax.broadcasted_iota(I32, (qb_rows, kb_rows), 1)
    nt = (((1,), (1,)), ((), ()))

    @pl.when(kb == 0)
    def _select():
        wi = wi_ref[0]

        def score_chunk(c, carry):
            k0 = pl.multiple_of(c * kb_rows, kb_rows)
            ka = kia_ref[0, pl.ds(k0, kb_rows), :]
            kbm = kib_ref[0, pl.ds(k0, kb_rows), :]
            sc = jnp.zeros((qb_rows, kb_rows), F32)
            for p in range(n_pair):
                qp = qi_ref[0, :, p * LANES:(p + 1) * LANES]
                sa = lax.dot_general(qp, ka, nt, preferred_element_type=F32)
                sb = lax.dot_general(qp, kbm, nt, preferred_element_type=F32)
                sc = sc + wi[:, 2 * p:2 * p + 1] * jnp.maximum(sa, 0.0)
                sc = sc + wi[:, 2 * p + 1:2 * p + 2] * jnp.maximum(sb, 0.0)
            bits = lax.bitcast_convert_type(sc, I32)
            key = bits ^ ((bits >> 31) & INT_MAX)
            keys_scr[c] = jnp.where(col + k0 <= row, key, INT_MIN)
            return carry

        lax.fori_loop(0, n_chunk, score_chunk, 0)

        slices_per_chunk = kb_rows // LANES
        chunks_per_set = WORD_BITS // slices_per_chunk
        n_slices = keys_scr.shape[0] * slices_per_chunk
        n_sets = planes_scr.shape[2] // LANES
        live_sets = (n_chunk + chunks_per_set - 1) // chunks_per_set

        def fill_chunk(c, carry):
            keys_scr[c] = jnp.full((qb_rows, kb_rows), INT_MIN, I32)
            return carry

        lax.fori_loop(n_chunk, jnp.minimum(live_sets * chunks_per_set, keys_scr.shape[0]), fill_chunk, 0)

        def pack_rows(g, carry, st):
            r0 = pl.multiple_of(g * SUBLANES, SUBLANES)
            words = []
            for s in range(WORD_BITS):
                sl = st * WORD_BITS + s
                if sl < n_slices:
                    c, off = divmod(sl, slices_per_chunk)
                    words.append(keys_scr[c, pl.ds(r0, SUBLANES), off * LANES:(off + 1) * LANES])
                else:
                    words.append(jnp.full((SUBLANES, LANES), INT_MIN, I32))
            words = _bit_transpose32(words)
            words[0] = ~words[0]
            for i in range(WORD_BITS):
                planes_scr[i, pl.ds(r0, SUBLANES), st * LANES:(st + 1) * LANES] = words[i]
            return carry

        for st in range(n_sets):
            @pl.when(st < live_sets)
            def _(st=st):
                lax.fori_loop(0, qb_rows // SUBLANES, functools.partial(pack_rows, st=st), 0)

        lane_ones = jnp.ones((LANES, LANES), BF16)

        def radix_select(live):
            width = live * LANES

            def row_total(pc):
                tot = pc[:, :LANES]
                for st in range(1, live):
                    tot = tot + pc[:, st * LANES:(st + 1) * LANES]
                return jnp.dot(tot.astype(F32).astype(BF16), lane_ones, preferred_element_type=F32)

            def all_sets(mask):
                return jnp.concatenate([mask] * live, axis=1)

            def select_two_bits(i, carry):
                cand, above, prefix = carry
                hi = planes_scr[2 * i, :, :width]
                lo = planes_scr[2 * i + 1, :, :width]
                c1 = cand & hi
                c0 = cand ^ c1
                c11 = c1 & lo
                c10 = c1 ^ c11
                c01 = c0 & lo
                c00 = c0 ^ c01
                r11 = above + row_total(lax.population_count(c11))
                r10 = r11 + row_total(lax.population_count(c10))
                r01 = r10 + row_total(lax.population_count(c01))
                t11 = r11 >= k_top
                t10 = r10 >= k_top
                t01 = r01 >= k_top
                cand = jnp.where(all_sets(t11), c11,
                                 jnp.where(all_sets(t10), c10, jnp.where(all_sets(t01), c01, c00)))
                above = jnp.where(t11, above, jnp.where(t10, r11, jnp.where(t01, r10, r01)))
                bits = jnp.where(t11, 3, jnp.where(t10, 2, jnp.where(t01, 1, 0)))
                prefix = prefix | (bits << (WORD_BITS - 2 - 2 * i))
                return cand, above, prefix

            cand, above, prefix = lax.fori_loop(
                0, WORD_BITS // 2, select_two_bits,
                (jnp.full((qb_rows, width), -1, I32),
                 jnp.zeros((qb_rows, LANES), F32), jnp.zeros((qb_rows, LANES), I32)))
            equal = row_total(lax.population_count(cand))
            thr_scr[...] = (prefix ^ INT_MIN)[:, :1]
            tie_scr[...] = jnp.full((qb_rows, 1), INT_MAX, I32)
            ngt_scr[...] = above[:, :1]
            nge_scr[...] = (above + equal)[:, :1]

        for live in range(1, n_sets + 1):
            pl.when(live_sets == live)(functools.partial(radix_select, live))

        rg = min(qb_rows, SEARCH_ROWS)
        lane_col = lax.broadcasted_iota(I32, (rg, LANES), 1)

        for r in range(qb_rows // rg):
            rs = slice(r * rg, (r + 1) * rg)

            def count(pred, rs=rs):
                def body(c, acc):
                    for t in range(kb_rows // LANES):
                        kk = keys_scr[c, rs, t * LANES:(t + 1) * LANES]
                        idx = lane_col + (c * kb_rows + t * LANES)
                        acc = acc + jnp.where(pred(kk, idx), 1.0, 0.0)
                    return acc
                acc = lax.fori_loop(0, n_chunk, body, jnp.zeros((rg, LANES), F32))
                return jnp.sum(acc, axis=1, keepdims=True)

            def wide(v):
                return jnp.broadcast_to(v, (rg, LANES))

            @pl.when(jnp.max(nge_scr[rs, :]) > k_top)
            def _ties(count=count, rs=rs):
                thr_w = wide(thr_scr[rs, :])
                need = k_top - ngt_scr[rs, :]

                def tie_bit(i, jt):
                    cand = jt + (jnp.int32(1) << (idx_bits - 1 - i))
                    cand_w = wide(cand)
                    cnt = count(lambda kk, idx: (kk == thr_w) & (idx < cand_w))
                    return jnp.where(cnt < need, cand, jt)

                tie_scr[rs, :] = lax.fori_loop(0, idx_bits, tie_bit, jnp.zeros((rg, 1), I32))

        m_scr[...] = jnp.full(m_scr.shape, M_INIT, F32)
        l_scr[...] = jnp.zeros_like(l_scr)
        acc_scr[...] = jnp.zeros_like(acc_scr)

    def _attend():
        ones = jnp.ones((kb_rows, HEAD_DIM), BF16)

        kk = keys_scr[kb]
        thr = thr_scr[...]
        lim = jnp.minimum(tie_scr[...], row[:, :1])
        tie_bias = jnp.where(col + kb * kb_rows <= lim, 0.0, MASK_VALUE)
        bias_scr[...] = jnp.where(kk > thr, 0.0, jnp.where(kk == thr, tie_bias, MASK_VALUE))

        def logits(h, s_ref, r_ref):
            s = lax.dot_general(q_ref[0, h], k_ref[0, h], nt, preferred_element_type=F32) + bias_scr[...]
            s_ref[...] = s
            r_ref[...] = jnp.broadcast_to(jnp.max(s, axis=1, keepdims=True), r_ref.shape)

        def accumulate(h, s_ref, r_ref):
            m_old = m_scr[h]
            m_new = jnp.maximum(m_old, r_ref[...])
            alpha = jnp.exp2(m_old - m_new)
            p = jnp.concatenate(
                [jnp.exp2(s_ref[:, t * LANES:(t + 1) * LANES] - m_new).astype(BF16)
                 for t in range(kb_rows // LANES)], axis=1)
            v_ext = jnp.concatenate([v_ref[0, h], ones], axis=1)
            pv = jnp.dot(p, v_ext, preferred_element_type=F32)
            acc_scr[h] = alpha * acc_scr[h] + pv[:, :HEAD_DIM]
            l_scr[h] = alpha * l_scr[h] + pv[:, HEAD_DIM:]
            m_scr[h] = m_new

        bufs = ((s0_scr, r0_scr), (s1_scr, r1_scr), (s2_scr, r2_scr))
        depth = len(bufs) - 1
        for h in range(min(depth, heads)):
            logits(h, *bufs[h % len(bufs)])
        for h in range(heads):
            if h + depth < heads:
                logits(h + depth, *bufs[(h + depth) % len(bufs)])
            accumulate(h, *bufs[h % len(bufs)])

    _attend()

    @pl.when(kb == last_kb)
    def _finish():
        for h in range(heads):
            o_ref[0, :, h * HEAD_DIM:(h + 1) * HEAD_DIM] = (acc_scr[h] / l_scr[h]).astype(o_ref.dtype)


def _dsa_attention(qkv, qi, kia, kib, wi, k_top):
    bsz, heads3, seq, _ = qkv.shape
    heads = heads3 // 3
    d = heads * HEAD_DIM
    n_pair = qi.shape[-1] // LANES
    qb_rows = _pick(seq, (256, 128))
    kb_rows = _pick(seq, (512, 256, 128))
    n_kb = seq // kb_rows
    n_sets = -(-seq // (WORD_BITS * LANES))

    pairs = [(i, j) for i in range(seq // qb_rows) for j in range((i * qb_rows + qb_rows - 1) // kb_rows + 1)]
    qb_tab = jnp.asarray([p[0] for p in pairs], I32)
    kb_tab = jnp.asarray([p[1] for p in pairs], I32)

    kern = functools.partial(_dsa_kernel, n_pair=n_pair, k_top=k_top, idx_bits=seq.bit_length())
    grid_spec = pltpu.PrefetchScalarGridSpec(
        num_scalar_prefetch=2,
        grid=(bsz, len(pairs)),
        in_specs=[
            pl.BlockSpec((1, heads, qb_rows, HEAD_DIM), lambda b, s, qt, kt: (b, 0, qt[s], 0)),
            pl.BlockSpec((1, heads, kb_rows, HEAD_DIM), lambda b, s, qt, kt: (b, 1, kt[s], 0)),
            pl.BlockSpec((1, heads, kb_rows, HEAD_DIM), lambda b, s, qt, kt: (b, 2, kt[s], 0)),
            pl.BlockSpec((1, qb_rows, n_pair * LANES), lambda b, s, qt, kt: (b, qt[s], 0)),
            pl.BlockSpec((1, seq, LANES), lambda b, s, qt, kt: (b, 0, 0)),
            pl.BlockSpec((1, seq, LANES), lambda b, s, qt, kt: (b, 0, 0)),
            pl.BlockSpec((1, qb_rows, LANES), lambda b, s, qt, kt: (b, qt[s], 0)),
        ],
        out_specs=pl.BlockSpec((1, qb_rows, d), lambda b, s, qt, kt: (b, qt[s], 0)),
        scratch_shapes=[
            pltpu.VMEM((n_kb, qb_rows, kb_rows), I32),
            pltpu.VMEM((WORD_BITS, qb_rows, n_sets * LANES), I32),
            pltpu.VMEM((qb_rows, 1), I32),
            pltpu.VMEM((qb_rows, 1), I32),
            pltpu.VMEM((qb_rows, 1), F32),
            pltpu.VMEM((qb_rows, 1), F32),
            pltpu.VMEM((heads, qb_rows, LANES), F32),
            pltpu.VMEM((heads, qb_rows, LANES), F32),
            pltpu.VMEM((heads, qb_rows, HEAD_DIM), F32),
            pltpu.VMEM((qb_rows, kb_rows), F32),
            pltpu.VMEM((qb_rows, kb_rows), F32),
            pltpu.VMEM((qb_rows, kb_rows), F32),
            pltpu.VMEM((qb_rows, kb_rows), F32),
            pltpu.VMEM((qb_rows, LANES), F32),
            pltpu.VMEM((qb_rows, LANES), F32),
            pltpu.VMEM((qb_rows, LANES), F32),
        ],
    )
    return pl.pallas_call(
        kern,
        out_shape=jax.ShapeDtypeStruct((bsz, seq, d), BF16),
        grid_spec=grid_spec,
        compiler_params=_params(("parallel", "arbitrary")),
        name="dsa_select_attend",
    )(qb_tab, kb_tab, qkv, qkv, qkv, qi, kia, kib, wi)


def _rope_tables(positions, dim):
    inv = 1.0 / (ROPE_THETA ** (jnp.arange(0, dim, 2, dtype=F32) / dim))
    half = dim // 2
    lane = jnp.arange(LANES)
    inv_t = inv[lane % half]
    sign = jnp.where((lane % dim) < half, -1.0, 1.0).astype(F32)
    ang = positions.astype(F32)[..., None] * inv_t
    return jnp.cos(ang), jnp.sin(ang) * sign


def kernel(x, c, positions, ada_w, ada_b, ln_g, ln_b, s5_in_w, s5_a_re, s5_a_im, s5_log_dt, s5_b_re, s5_b_im, s5_c_re, s5_c_im, s5_d, s5_glu_w, s5_glu_b, dsa_in_w, dsa_out_w, ffn_w_in, ffn_w_out):
    bsz, seq, d = x.shape
    depth = ada_w.shape[0]
    alpha = (2.0 * depth) ** 0.25
    idx_heads = (dsa_in_w.shape[-1] - 3 * d - IDX_DIM) // (IDX_DIM + 1)
    k_top = min(TOPK_MAX, seq // 4)
    seg = min(S5_SEG, seq // SUBLANES)

    cos_h, sin_h = _rope_tables(positions, HEAD_DIM)
    cos_i, sin_i = _rope_tables(positions, IDX_DIM)
    mod = _ada_mod(c, ada_w, ada_b)
    ffn_in = ffn_w_in.astype(BF16)
    ffn_out = ffn_w_out.astype(BF16)

    for i in range(depth):
        sh1, sc1, g1, sh2, sc2, g2 = [m[:, None, :] for m in jnp.split(mod[i], 6, axis=-1)]
        j = i // 2
        if i % 2 == 0:
            prep = _s5_discretize(s5_a_re[j], s5_a_im[j], s5_log_dt[j], s5_b_re[j], s5_b_im[j],
                                  s5_c_re[j], s5_c_im[j], seg)
            u = _mod_matmul(x, sc1, sh1, s5_in_w[j].astype(BF16), F32)
            gl = _s5_scan(u, prep, s5_d[j], seg)
            x1 = _matmul_res_ln(gl, s5_glu_w[j].astype(BF16), s5_glu_b[j], x, g1,
                                ln_g[i, 0], ln_b[i, 0], alpha, glu=True)
        else:
            w = dsa_in_w[j].astype(BF16)
            n_qi = idx_heads * IDX_DIM
            pad = jnp.zeros((d, LANES - IDX_DIM - idx_heads), w.dtype)
            w_idx = jnp.concatenate(
                [w[:, 3 * d:3 * d + n_qi], w[:, 3 * d + n_qi + idx_heads:], w[:, 3 * d + n_qi:3 * d + n_qi + idx_heads], pad],
                axis=1)
            qkv = _qkv_proj(x, sc1, sh1, w, cos_h, sin_h)
            qi, kia, kib, wi = _idx_proj(x, sc1, sh1, w_idx, cos_i, sin_i, idx_heads)
            att = _dsa_attention(qkv, qi, kia, kib, wi, k_top)
            x1 = _matmul_res_ln(att, dsa_out_w[j].astype(BF16), None, x, g1,
                                ln_g[i, 0], ln_b[i, 0], alpha, glu=False)
        x = _ffn_res_ln(x1, sc2, sh2, g2, ffn_in, ffn_out, i, ln_g[i, 1], ln_b[i, 1], alpha)
    return x
```

```python
import functools
import math

import jax
import jax.numpy as jnp
from jax import lax
from jax.experimental import pallas as pl
from jax.experimental.pallas import tpu as pltpu

F32 = jnp.float32
BF16 = jnp.bfloat16
I32 = jnp.int32

S5_GROUP = 16
S5_STATE = 64
HEAD_DIM = 128
IDX_DIM = 64
TOPK_MAX = 256
ROPE_THETA = 10000.0
LN_EPS = 1e-5

LANES = 128
SUBLANES = 8
MXU_WIDTH = 256
VMEM_LIMIT_BYTES = 56 * 1024 * 1024
RESIDENT_WEIGHT_BYTES = 16 * 1024 * 1024

S5_PACK_GROUPS = 16
S5_SEG = 256

INT_MIN = -(2 ** 31)
INT_MAX = 2 ** 31 - 1
SEARCH_ROWS = 128
WORD_BITS = 32

MASK_VALUE = -2e30
M_INIT = -1e30


def _pick(n, cands):
    for c in cands:
        if n % c == 0:
            return c
    return n


def _row_halves(rows):
    half = rows // 2
    return (slice(0, half), slice(half, rows))


def _params(sem):
    return pltpu.CompilerParams(dimension_semantics=sem, vmem_limit_bytes=VMEM_LIMIT_BYTES)


def _ada_kernel(c_ref, w_ref, b_ref, o_ref):
    ca = jax.nn.silu(c_ref[...]).astype(BF16)
    o_ref[0] = jnp.dot(ca, w_ref[0].astype(BF16), preferred_element_type=F32) + b_ref[0]


def _ada_mod(c, ada_w, ada_b):
    bsz, d = c.shape
    depth, _, n = ada_w.shape
    rows = SUBLANES * ((bsz + SUBLANES - 1) // SUBLANES)
    cp = jnp.zeros((rows, d), F32).at[:bsz].set(c)
    tn = _pick(n, (1024, 512, 256, 128))
    out = pl.pallas_call(
        _ada_kernel,
        out_shape=jax.ShapeDtypeStruct((depth, rows, n), F32),
        grid=(depth, n // tn),
        in_specs=[
            pl.BlockSpec((rows, d), lambda l, j: (0, 0)),
            pl.BlockSpec((1, d, tn), lambda l, j: (l, 0, j)),
            pl.BlockSpec((1, 1, tn), lambda l, j: (l, 0, j)),
        ],
        out_specs=pl.BlockSpec((1, rows, tn), lambda l, j: (l, 0, j)),
        compiler_params=_params(("arbitrary", "arbitrary")),
        name="ada_mod",
    )(cp, ada_w, ada_b.reshape(depth, 1, n))
    return out[:, :bsz]


def _modulate(x_ref, sc_ref, sh_ref):
    return (x_ref[0] * (1.0 + sc_ref[0]) + sh_ref[0]).astype(BF16)


def _modmm_kernel(x_ref, sc_ref, sh_ref, w_ref, o_ref, h_scr):
    @pl.when(pl.program_id(2) == 0)
    def _():
        h_scr[...] = _modulate(x_ref, sc_ref, sh_ref)

    for rs in _row_halves(h_scr.shape[0]):
        o_ref[0, rs, :] = jnp.dot(h_scr[rs, :], w_ref[...], preferred_element_type=F32).astype(o_ref.dtype)


def _mod_matmul(x, sc, sh, w, out_dtype):
    bsz, seq, d = x.shape
    n = w.shape[1]
    tm = _pick(seq, (512, 256, 128))
    tn = n if 2 * d * n * w.dtype.itemsize <= RESIDENT_WEIGHT_BYTES else _pick(n, (1024, 512, 256, 128))
    return pl.pallas_call(
        _modmm_kernel,
        out_shape=jax.ShapeDtypeStruct((bsz, seq, n), out_dtype),
        grid=(bsz, seq // tm, n // tn),
        in_specs=[
            pl.BlockSpec((1, tm, d), lambda b, i, j: (b, i, 0)),
            pl.BlockSpec((1, 1, d), lambda b, i, j: (b, 0, 0)),
            pl.BlockSpec((1, 1, d), lambda b, i, j: (b, 0, 0)),
            pl.BlockSpec((d, tn), lambda b, i, j: (0, j)),
        ],
        out_specs=pl.BlockSpec((1, tm, tn), lambda b, i, j: (b, i, j)),
        scratch_shapes=[pltpu.VMEM((tm, d), BF16)],
        compiler_params=_params(("parallel", "parallel", "arbitrary")),
        name="mod_matmul",
    )(x, sc, sh, w)


def _s5_kernel(*refs, seg, width, lane_tiles):
    u_refs = refs[:lane_tiles]
    (bb_ref, cb_ref, lre_ref, lim_ref, ltre_ref, ltim_ref, d_ref, o_ref,
     xs_scr, carry_scr, cin_scr, up_scr) = refs[lane_tiles:]
    ns = xs_scr.shape[1] // 2

    @pl.when(pl.program_id(2) == 0)
    def _():
        carry_scr[...] = jnp.zeros_like(carry_scr)

    for t, u_ref in enumerate(u_refs):
        for i in range(seg):
            up_scr[t, i * SUBLANES:(i + 1) * SUBLANES, :] = u_ref[0, pl.ds(i, SUBLANES, stride=seg), :]
    u = jnp.concatenate([up_scr[t] for t in range(len(u_refs))], axis=1)
    halves = _row_halves(u.shape[0])
    for rs in halves:
        xs_scr[rs, :] = jnp.dot(u[rs].astype(BF16), bb_ref[0], preferred_element_type=F32)

    for part in range(ns // width):
        cr = slice(part * width, (part + 1) * width)
        ci = slice(ns + part * width, ns + (part + 1) * width)
        lr = jnp.broadcast_to(lre_ref[0, :, cr], (SUBLANES, width))
        li = jnp.broadcast_to(lim_ref[0, :, cr], (SUBLANES, width))

        def local_step(i, st, cr=cr, ci=ci, lr=lr, li=li):
            sr, si = st
            r0 = pl.multiple_of(i * SUBLANES, SUBLANES)
            nr = lr * sr - li * si + xs_scr[pl.ds(r0, SUBLANES), cr]
            ni = lr * si + li * sr + xs_scr[pl.ds(r0, SUBLANES), ci]
            xs_scr[pl.ds(r0, SUBLANES), cr] = nr
            xs_scr[pl.ds(r0, SUBLANES), ci] = ni
            return nr, ni

        zero = jnp.zeros((SUBLANES, width), F32)
        er, ei = lax.fori_loop(0, seg, local_step, (zero, zero), unroll=4)

        ltr = ltre_ref[0, :, cr]
        lti = ltim_ref[0, :, cr]
        c_r = carry_scr[:, cr]
        c_i = carry_scr[:, ci]
        for s in range(SUBLANES):
            cin_scr[s:s + 1, cr] = c_r
            cin_scr[s:s + 1, ci] = c_i
            e_r = er[s:s + 1, :]
            e_i = ei[s:s + 1, :]
            c_r, c_i = ltr * c_r - lti * c_i + e_r, ltr * c_i + lti * c_r + e_i
        carry_scr[:, cr] = c_r
        carry_scr[:, ci] = c_i

        def carry_step(i, st, cr=cr, ci=ci, lr=lr, li=li):
            pr, pi_ = st
            r0 = pl.multiple_of(i * SUBLANES, SUBLANES)
            nr = lr * pr - li * pi_
            ni = lr * pi_ + li * pr
            xs_scr[pl.ds(r0, SUBLANES), cr] += nr
            xs_scr[pl.ds(r0, SUBLANES), ci] += ni
            return nr, ni

        lax.fori_loop(0, seg, carry_step, (cin_scr[:, cr], cin_scr[:, ci]), unroll=4)

    for rs in halves:
        y = jnp.dot(xs_scr[rs, :].astype(BF16), cb_ref[0], preferred_element_type=F32)
        g = jax.nn.gelu(y + d_ref[0] * u[rs])
        for t in range(len(u_refs)):
            up_scr[t, rs, :] = g[:, t * LANES:(t + 1) * LANES]
    for t in range(len(u_refs)):
        for s in range(SUBLANES):
            o_ref[0, s * seg:(s + 1) * seg, t * LANES:(t + 1) * LANES] = (
                up_scr[t, pl.ds(s, seg, stride=SUBLANES), :].astype(o_ref.dtype))


def _cmul(ar, ai, br, bi):
    return ar * br - ai * bi, ar * bi + ai * br


def _s5_discretize(a_re, a_im, log_dt, b_re, b_im, c_re, c_im, seg):
    g, n = a_re.shape
    p = b_re.shape[-1]
    pg = S5_PACK_GROUPS
    packs = g // pg
    a_re, a_im = a_re.astype(F32), a_im.astype(F32)
    dt = jnp.exp(log_dt.astype(F32))[:, None]
    mag = jnp.exp(a_re * dt)
    lb_re, lb_im = mag * jnp.cos(a_im * dt), mag * jnp.sin(a_im * dt)
    den = a_re * a_re + a_im * a_im
    nr, ni = lb_re - 1.0, lb_im
    f_re = (nr * a_re + ni * a_im) / den
    f_im = (ni * a_re - nr * a_im) / den
    bb_re, bb_im = _cmul(f_re[..., None], f_im[..., None], b_re.astype(F32), b_im.astype(F32))
    lt_re, lt_im = lb_re, lb_im
    for _ in range(int(math.log2(seg))):
        lt_re, lt_im = _cmul(lt_re, lt_im, lt_re, lt_im)
    on_diag = (jnp.arange(pg * p)[:, None] // p) == (jnp.arange(pg * n)[None, :] // n)

    def blk_diag(m):
        tiled = jnp.tile(m.astype(BF16).reshape(packs, pg * p, n), (1, 1, pg))
        return jnp.where(on_diag[None], tiled, 0)

    b_blk = jnp.concatenate([blk_diag(jnp.swapaxes(bb_re, 1, 2)), blk_diag(jnp.swapaxes(bb_im, 1, 2))], axis=-1)
    c_blk = jnp.swapaxes(jnp.concatenate([blk_diag(c_re), blk_diag(-c_im)], axis=-1), 1, 2)

    def vec(m):
        return m.reshape(packs, 1, pg * n)

    return b_blk, c_blk, vec(lb_re), vec(lb_im), vec(lt_re), vec(lt_im)


def _s5_scan(u, prep, d_skip, seg):
    bsz, seq, d = u.shape
    b_blk, c_blk, lre, lim, ltre, ltim = prep
    packs, pw, ns2 = b_blk.shape
    ns = ns2 // 2
    rows = SUBLANES * seg
    width = _pick(ns, (1024, 512, 256, 128))
    lane_tiles = pw // LANES
    kern = functools.partial(_s5_kernel, seg=seg, width=width, lane_tiles=lane_tiles)
    vspec = pl.BlockSpec((1, 1, ns), lambda b, k, m: (k, 0, 0))
    u_specs = [pl.BlockSpec((1, rows, LANES), lambda b, k, m, t=t: (b, m, k * lane_tiles + t))
               for t in range(lane_tiles)]
    return pl.pallas_call(
        kern,
        out_shape=jax.ShapeDtypeStruct((bsz, seq, d), BF16),
        grid=(bsz, packs, seq // rows),
        in_specs=u_specs + [
            pl.BlockSpec((1, pw, ns2), lambda b, k, m: (k, 0, 0)),
            pl.BlockSpec((1, ns2, pw), lambda b, k, m: (k, 0, 0)),
            vspec, vspec, vspec, vspec,
            pl.BlockSpec((1, 1, pw), lambda b, k, m: (k, 0, 0)),
        ],
        out_specs=pl.BlockSpec((1, rows, pw), lambda b, k, m: (b, m, k)),
        scratch_shapes=[
            pltpu.VMEM((rows, ns2), F32),
            pltpu.VMEM((1, ns2), F32),
            pltpu.VMEM((SUBLANES, ns2), F32),
            pltpu.VMEM((lane_tiles, rows, LANES), F32),
        ],
        compiler_params=_params(("parallel", "parallel", "arbitrary")),
        name="s5_scan",
    )(*([u] * lane_tiles), b_blk, c_blk, lre, lim, ltre, ltim, d_skip.reshape(packs, 1, pw).astype(F32))


def _layer_norm_rows(tiles, lng_ref, lnb_ref, o_ref, d):
    tn = tiles[0].shape[1]
    tot = tiles[0].sum(axis=1, keepdims=True)
    for t in tiles[1:]:
        tot = tot + t.sum(axis=1, keepdims=True)
    mu = tot * (1.0 / d)
    sq = jnp.square(tiles[0] - mu).sum(axis=1, keepdims=True)
    for t in tiles[1:]:
        sq = sq + jnp.square(t - mu).sum(axis=1, keepdims=True)
    inv = lax.rsqrt(sq * (1.0 / d) + LN_EPS)
    for k, t in enumerate(tiles):
        cs = slice(k * tn, (k + 1) * tn)
        o_ref[0, :, cs] = ((t - mu) * inv * lng_ref[:, cs] + lnb_ref[:, cs]).astype(o_ref.dtype)


def _mmln_kernel(*refs, glu, nj, alpha, d):
    if glu:
        a_ref, w1_ref, w2_ref, b1_ref, b2_ref, x_ref, g_ref, lng_ref, lnb_ref, o_ref, r_scr = refs
    else:
        a_ref, w1_ref, x_ref, g_ref, lng_ref, lnb_ref, o_ref, r_scr = refs
    j = pl.program_id(2)
    for rs in _row_halves(a_ref.shape[1]):
        a = a_ref[0, rs, :]
        y = jnp.dot(a, w1_ref[...], preferred_element_type=F32)
        if glu:
            y = y + b1_ref[...]
            gate = jnp.dot(a, w2_ref[...], preferred_element_type=F32) + b2_ref[...]
            y = y * jax.nn.sigmoid(gate)
        r_scr[j, rs, :] = alpha * x_ref[0, rs, :] + (1.0 + g_ref[0]) * y

    @pl.when(j == nj - 1)
    def _():
        _layer_norm_rows([r_scr[t] for t in range(nj)], lng_ref, lnb_ref, o_ref, d)


def _matmul_res_ln(a, w, bias, xres, gate, ln_g, ln_b, alpha, glu):
    bsz, seq, k = a.shape
    d = xres.shape[-1]
    resident = w.size * w.dtype.itemsize <= RESIDENT_WEIGHT_BYTES
    tm = _pick(seq, (512, 256, 128))
    tn = d if resident else _pick(d, (512, 256, 128))
    nj = d // tn
    w_mode = dict(pipeline_mode=pl.Buffered(1)) if resident else {}
    a_spec = pl.BlockSpec((1, tm, k), lambda b, i, j: (b, i, 0))
    w1_spec = pl.BlockSpec((k, tn), lambda b, i, j: (0, j), **w_mode)
    tail_specs = [
        pl.BlockSpec((1, tm, tn), lambda b, i, j: (b, i, j)),
        pl.BlockSpec((1, 1, tn), lambda b, i, j: (b, 0, j)),
        pl.BlockSpec((1, d), lambda b, i, j: (0, 0)),
        pl.BlockSpec((1, d), lambda b, i, j: (0, 0)),
    ]
    tail = (xres, gate, ln_g.reshape(1, d).astype(F32), ln_b.reshape(1, d).astype(F32))
    if glu:
        in_specs = [a_spec, w1_spec,
                    pl.BlockSpec((k, tn), lambda b, i, j: (0, j + nj), **w_mode),
                    pl.BlockSpec((1, tn), lambda b, i, j: (0, j)),
                    pl.BlockSpec((1, tn), lambda b, i, j: (0, j + nj))] + tail_specs
        b2d = bias.reshape(1, 2 * d).astype(F32)
        args = (a, w, w, b2d, b2d) + tail
    else:
        in_specs = [a_spec, w1_spec] + tail_specs
        args = (a, w) + tail
    kern = functools.partial(_mmln_kernel, glu=glu, nj=nj, alpha=alpha, d=d)
    return pl.pallas_call(
        kern,
        out_shape=jax.ShapeDtypeStruct((bsz, seq, d), F32),
        grid=(bsz, seq // tm, nj),
        in_specs=in_specs,
        out_specs=pl.BlockSpec((1, tm, d), lambda b, i, j: (b, i, 0)),
        scratch_shapes=[pltpu.VMEM((nj, tm, tn), F32)],
        compiler_params=_params(("parallel", "parallel", "arbitrary")),
        name="matmul_res_ln",
    )(*args)


def _ffn_kernel(x_ref, sc_ref, sh_ref, g_ref, wg_ref, wu_ref, wo_ref, lng_ref, lnb_ref, o_ref,
                h_scr, acc_scr, *, nf, alpha, d, tn):
    f = pl.program_id(2)

    @pl.when(f == 0)
    def _():
        h_scr[...] = _modulate(x_ref, sc_ref, sh_ref)
        acc_scr[...] = jnp.zeros_like(acc_scr)

    h = h_scr[...]
    a_g = jnp.dot(h, wg_ref[0], preferred_element_type=F32)
    a_u = jnp.dot(h, wu_ref[0], preferred_element_type=F32)
    act = (jax.nn.silu(a_g) * a_u).astype(BF16)
    acc_scr[...] += jnp.dot(act, wo_ref[0], preferred_element_type=F32)

    @pl.when(f == nf - 1)
    def _():
        tiles = []
        for k in range(d // tn):
            cs = slice(k * tn, (k + 1) * tn)
            tiles.append(alpha * x_ref[0, :, cs] + (1.0 + g_ref[0, :, cs]) * acc_scr[:, cs])
        _layer_norm_rows(tiles, lng_ref, lnb_ref, o_ref, d)


def _ffn_res_ln(x, sc, sh, gate, w_in, w_out, layer, ln_g, ln_b, alpha):
    bsz, seq, d = x.shape
    dff = w_out.shape[1]
    tm = _pick(seq, (512, 256, 128))
    tf = _pick(dff, (512, 256, 128))
    nf = dff // tf
    tn = _pick(d, (512, 256, 128))
    vec = pl.BlockSpec((1, 1, d), lambda b, i, f: (b, 0, 0))
    kern = functools.partial(_ffn_kernel, nf=nf, alpha=alpha, d=d, tn=tn)
    return pl.pallas_call(
        kern,
        out_shape=jax.ShapeDtypeStruct((bsz, seq, d), F32),
        grid=(bsz, seq // tm, nf),
        in_specs=[
            pl.BlockSpec((1, tm, d), lambda b, i, f: (b, i, 0)),
            vec, vec, vec,
            pl.BlockSpec((1, d, tf), lambda b, i, f: (layer, 0, f)),
            pl.BlockSpec((1, d, tf), lambda b, i, f: (layer, 0, f + nf)),
            pl.BlockSpec((1, tf, d), lambda b, i, f: (layer, f, 0)),
            pl.BlockSpec((1, d), lambda b, i, f: (0, 0)),
            pl.BlockSpec((1, d), lambda b, i, f: (0, 0)),
        ],
        out_specs=pl.BlockSpec((1, tm, d), lambda b, i, f: (b, i, 0)),
        scratch_shapes=[pltpu.VMEM((tm, d), BF16), pltpu.VMEM((tm, d), F32)],
        compiler_params=_params(("parallel", "parallel", "arbitrary")),
        name="ffn_res_ln",
    )(x, sc, sh, gate, w_in, w_in, w_out, ln_g.reshape(1, d).astype(F32), ln_b.reshape(1, d).astype(F32))


def _qkv_kernel(x_ref, sc_ref, sh_ref, w_ref, cos_ref, sin_ref, o_ref, h_scr, *, tiles_per_tensor, q_scale):
    j = pl.program_id(2)

    @pl.when(j == 0)
    def _():
        h_scr[...] = _modulate(x_ref, sc_ref, sh_ref)

    tensor = j // tiles_per_tensor
    scale = jnp.where(tensor == 0, q_scale, 1.0)
    is_v = tensor == 2
    chunk = min(MXU_WIDTH, w_ref.shape[1])
    for rs in _row_halves(h_scr.shape[0]):
        a = jnp.where(is_v, 1.0, cos_ref[0, rs, :] * scale)
        b = jnp.where(is_v, 0.0, sin_ref[0, rs, :] * scale)
        for c in range(w_ref.shape[1] // chunk):
            acc = jnp.dot(h_scr[rs, :], w_ref[:, c * chunk:(c + 1) * chunk], preferred_element_type=F32)
            for hh in range(chunk // HEAD_DIM):
                xc = acc[:, hh * HEAD_DIM:(hh + 1) * HEAD_DIM]
                xc = xc * a + pltpu.roll(xc, HEAD_DIM // 2, 1) * b
                o_ref[0, c * (chunk // HEAD_DIM) + hh, rs, :] = xc.astype(o_ref.dtype)


def _qkv_proj(x, sc, sh, w_qkv, cos_t, sin_t):
    bsz, seq, d = x.shape
    tm = _pick(seq, (1024, 512, 256, 128))
    tn = _pick(d, (2048, 1024, 512, 256, 128))
    hpt = tn // HEAD_DIM
    kern = functools.partial(_qkv_kernel, tiles_per_tensor=d // tn, q_scale=HEAD_DIM ** -0.5 * math.log2(math.e))
    return pl.pallas_call(
        kern,
        out_shape=jax.ShapeDtypeStruct((bsz, 3 * d // HEAD_DIM, seq, HEAD_DIM), BF16),
        grid=(bsz, seq // tm, 3 * d // tn),
        in_specs=[
            pl.BlockSpec((1, tm, d), lambda b, i, j: (b, i, 0)),
            pl.BlockSpec((1, 1, d), lambda b, i, j: (b, 0, 0)),
            pl.BlockSpec((1, 1, d), lambda b, i, j: (b, 0, 0)),
            pl.BlockSpec((d, tn), lambda b, i, j: (0, j)),
            pl.BlockSpec((1, tm, HEAD_DIM), lambda b, i, j: (b, i, 0)),
            pl.BlockSpec((1, tm, HEAD_DIM), lambda b, i, j: (b, i, 0)),
        ],
        out_specs=pl.BlockSpec((1, hpt, tm, HEAD_DIM), lambda b, i, j: (b, j, i, 0)),
        scratch_shapes=[pltpu.VMEM((tm, d), BF16)],
        compiler_params=_params(("parallel", "parallel", "arbitrary")),
        name="dsa_qkv_proj",
    )(x, sc, sh, w_qkv, cos_t, sin_t)


def _idx_kernel(x_ref, sc_ref, sh_ref, w_ref, cos_ref, sin_ref, qi_ref, kia_ref, kib_ref, wi_ref,
                *, n_pair, idx_heads, w_scale):
    h = _modulate(x_ref, sc_ref, sh_ref)
    halves = _row_halves(h.shape[0])
    accs = [jnp.dot(h[rs], w_ref[...], preferred_element_type=F32) for rs in halves]
    lane = lax.broadcasted_iota(I32, (halves[0].stop, LANES), 1)
    first_half = (lane % IDX_DIM) < IDX_DIM // 2

    for rs, acc in zip(halves, accs):
        cos = cos_ref[0, rs, :]
        sin = sin_ref[0, rs, :]

        def rope(xc, cos=cos, sin=sin):
            partner = jnp.where(first_half, pltpu.roll(xc, LANES - IDX_DIM // 2, 1),
                                pltpu.roll(xc, IDX_DIM // 2, 1))
            return xc * cos + partner * sin

        for p in range(n_pair):
            cs = slice(p * LANES, (p + 1) * LANES)
            qi_ref[0, rs, cs] = rope(acc[:, cs]).astype(qi_ref.dtype)
        last = acc[:, n_pair * LANES:]
        ka = jnp.where(lane < IDX_DIM, rope(last), 0.0)
        kia_ref[0, rs, :] = ka.astype(kia_ref.dtype)
        kib_ref[0, rs, :] = pltpu.roll(ka, IDX_DIM, 1).astype(kib_ref.dtype)
        wi_ref[0, rs, :] = jnp.where(lane < idx_heads, pltpu.roll(last, LANES - IDX_DIM, 1), 0.0) * w_scale


def _idx_proj(x, sc, sh, w_idx, cos_t, sin_t, idx_heads):
    bsz, seq, d = x.shape
    n = w_idx.shape[1]
    n_pair = idx_heads // 2
    tm = _pick(seq, (1024, 512, 256, 128))
    kern = functools.partial(_idx_kernel, n_pair=n_pair, idx_heads=idx_heads,
                             w_scale=(idx_heads ** -0.5) * (IDX_DIM ** -0.5))
    row = lambda b, i: (b, i, 0)
    return pl.pallas_call(
        kern,
        out_shape=(
            jax.ShapeDtypeStruct((bsz, seq, n_pair * LANES), BF16),
            jax.ShapeDtypeStruct((bsz, seq, LANES), BF16),
            jax.ShapeDtypeStruct((bsz, seq, LANES), BF16),
            jax.ShapeDtypeStruct((bsz, seq, LANES), F32),
        ),
        grid=(bsz, seq // tm),
        in_specs=[
            pl.BlockSpec((1, tm, d), row),
            pl.BlockSpec((1, 1, d), lambda b, i: (b, 0, 0)),
            pl.BlockSpec((1, 1, d), lambda b, i: (b, 0, 0)),
            pl.BlockSpec((d, n), lambda b, i: (0, 0)),
            pl.BlockSpec((1, tm, LANES), row),
            pl.BlockSpec((1, tm, LANES), row),
        ],
        out_specs=(
            pl.BlockSpec((1, tm, n_pair * LANES), row),
            pl.BlockSpec((1, tm, LANES), row),
            pl.BlockSpec((1, tm, LANES), row),
            pl.BlockSpec((1, tm, LANES), row),
        ),
        compiler_params=_params(("parallel", "parallel")),
        name="dsa_idx_proj",
    )(x, sc, sh, w_idx, cos_t, sin_t)


def _bit_transpose32(words):
    a = list(words)
    j = WORD_BITS // 2
    mask = 0x0000FFFF
    while j:
        k = 0
        while k < WORD_BITS:
            t = (a[k] ^ lax.shift_right_logical(a[k + j], jnp.int32(j))) & jnp.int32(mask)
            a[k] = a[k] ^ t
            a[k + j] = a[k + j] ^ (t << j)
            k = (k + j + 1) & ~j
        j >>= 1
        if j:
            mask = (mask ^ (mask << j)) & 0xFFFFFFFF
    return a


def _dsa_kernel(qb_tab, kb_tab, q_ref, k_ref, v_ref, qi_ref, kia_ref, kib_ref, wi_ref, o_ref,
                keys_scr, planes_scr, thr_scr, tie_scr, ngt_scr, nge_scr,
                m_scr, l_scr, acc_scr, bias_scr, s0_scr, s1_scr, s2_scr, r0_scr, r1_scr, r2_scr,
                *, n_pair, k_top, idx_bits):
    heads, qb_rows, _ = acc_scr.shape
    kb_rows = keys_scr.shape[2]
    qb = qb_tab[pl.program_id(1)]
    kb = kb_tab[pl.program_id(1)]
    q0 = qb * qb_rows
    last_kb = (q0 + qb_rows - 1) // kb_rows
    n_chunk = last_kb + 1
    row = lax.broadcasted_iota(I32, (qb_rows, kb_rows), 0) + q0
    col = lax.broadcasted_iota(I32, (qb_rows, kb_rows), 1)
    nt = (((1,), (1,)), ((), ()))

    @pl.when(kb == 0)
    def _select():
        wi = wi_ref[0]

        def score_chunk(c, carry):
            k0 = pl.multiple_of(c * kb_rows, kb_rows)
            ka = kia_ref[0, pl.ds(k0, kb_rows), :]
            kbm = kib_ref[0, pl.ds(k0, kb_rows), :]
            sc = jnp.zeros((qb_rows, kb_rows), F32)
            for p in range(n_pair):
                qp = qi_ref[0, :, p * LANES:(p + 1) * LANES]
                sa = lax.dot_general(qp, ka, nt, preferred_element_type=F32)
                sb = lax.dot_general(qp, kbm, nt, preferred_element_type=F32)
                sc = sc + wi[:, 2 * p:2 * p + 1] * jnp.maximum(sa, 0.0)
                sc = sc + wi[:, 2 * p + 1:2 * p + 2] * jnp.maximum(sb, 0.0)
            bits = lax.bitcast_convert_type(sc, I32)
            key = bits ^ ((bits >> 31) & INT_MAX)
            keys_scr[c] = jnp.where(col + k0 <= row, key, INT_MIN)
            return carry

        lax.fori_loop(0, n_chunk, score_chunk, 0)

        slices_per_chunk = kb_rows // LANES
        chunks_per_set = WORD_BITS // slices_per_chunk
        n_slices = keys_scr.shape[0] * slices_per_chunk
        n_sets = planes_scr.shape[2] // LANES
        live_sets = (n_chunk + chunks_per_set - 1) // chunks_per_set

        def fill_chunk(c, carry):
            keys_scr[c] = jnp.full((qb_rows, kb_rows), INT_MIN, I32)
            return carry

        lax.fori_loop(n_chunk, jnp.minimum(live_sets * chunks_per_set, keys_scr.shape[0]), fill_chunk, 0)

        def pack_rows(g, carry, st):
            r0 = pl.multiple_of(g * SUBLANES, SUBLANES)
            words = []
            for s in range(WORD_BITS):
                sl = st * WORD_BITS + s
                if sl < n_slices:
                    c, off = divmod(sl, slices_per_chunk)
                    words.append(keys_scr[c, pl.ds(r0, SUBLANES), off * LANES:(off + 1) * LANES])
                else:
                    words.append(jnp.full((SUBLANES, LANES), INT_MIN, I32))
            words = _bit_transpose32(words)
            words[0] = ~words[0]
            for i in range(WORD_BITS):
                planes_scr[i, pl.ds(r0, SUBLANES), st * LANES:(st + 1) * LANES] = words[i]
            return carry

        for st in range(n_sets):
            @pl.when(st < live_sets)
            def _(st=st):
                lax.fori_loop(0, qb_rows // SUBLANES, functools.partial(pack_rows, st=st), 0)

        lane_ones = jnp.ones((LANES, LANES), BF16)

        def radix_select(live):
            width = live * LANES

            def row_total(pc):
                tot = pc[:, :LANES]
                for st in range(1, live):
                    tot = tot + pc[:, st * LANES:(st + 1) * LANES]
                return jnp.dot(tot.astype(F32).astype(BF16), lane_ones, preferred_element_type=F32)

            def all_sets(mask):
                return jnp.concatenate([mask] * live, axis=1)

            def select_two_bits(i, carry):
                cand, above, prefix = carry
                hi = planes_scr[2 * i, :, :width]
                lo = planes_scr[2 * i + 1, :, :width]
                c1 = cand & hi
                c0 = cand ^ c1
                c11 = c1 & lo
                c10 = c1 ^ c11
                c01 = c0 & lo
                c00 = c0 ^ c01
                r11 = above + row_total(lax.population_count(c11))
                r10 = r11 + row_total(lax.population_count(c10))
                r01 = r10 + row_total(lax.population_count(c01))
                t11 = r11 >= k_top
                t10 = r10 >= k_top
                t01 = r01 >= k_top
                cand = jnp.where(all_sets(t11), c11,
                                 jnp.where(all_sets(t10), c10, jnp.where(all_sets(t01), c01, c00)))
                above = jnp.where(t11, above, jnp.where(t10, r11, jnp.where(t01, r10, r01)))
                bits = jnp.where(t11, 3, jnp.where(t10, 2, jnp.where(t01, 1, 0)))
                prefix = prefix | (bits << (WORD_BITS - 2 - 2 * i))
                return cand, above, prefix

            cand, above, prefix = lax.fori_loop(
                0, WORD_BITS // 2, select_two_bits,
                (jnp.full((qb_rows, width), -1, I32),
                 jnp.zeros((qb_rows, LANES), F32), jnp.zeros((qb_rows, LANES), I32)))
            equal = row_total(lax.population_count(cand))
            thr_scr[...] = (prefix ^ INT_MIN)[:, :1]
            tie_scr[...] = jnp.full((qb_rows, 1), INT_MAX, I32)
            ngt_scr[...] = above[:, :1]
            nge_scr[...] = (above + equal)[:, :1]

        for live in range(1, n_sets + 1):
            pl.when(live_sets == live)(functools.partial(radix_select, live))

        rg = min(qb_rows, SEARCH_ROWS)
        lane_col = lax.broadcasted_iota(I32, (rg, LANES), 1)

        for r in range(qb_rows // rg):
            rs = slice(r * rg, (r + 1) * rg)

            def count(pred, rs=rs):
                def body(c, acc):
                    for t in range(kb_rows // LANES):
                        kk = keys_scr[c, rs, t * LANES:(t + 1) * LANES]
                        idx = lane_col + (c * kb_rows + t * LANES)
                        acc = acc + jnp.where(pred(kk, idx), 1.0, 0.0)
                    return acc
                acc = lax.fori_loop(0, n_chunk, body, jnp.zeros((rg, LANES), F32))
                return jnp.sum(acc, axis=1, keepdims=True)

            def wide(v):
                return jnp.broadcast_to(v, (rg, LANES))

            @pl.when(jnp.max(nge_scr[rs, :]) > k_top)
            def _ties(count=count, rs=rs):
                thr_w = wide(thr_scr[rs, :])
                need = k_top - ngt_scr[rs, :]

                def tie_bit(i, jt):
                    cand = jt + (jnp.int32(1) << (idx_bits - 1 - i))
                    cand_w = wide(cand)
                    cnt = count(lambda kk, idx: (kk == thr_w) & (idx < cand_w))
                    return jnp.where(cnt < need, cand, jt)

                tie_scr[rs, :] = lax.fori_loop(0, idx_bits, tie_bit, jnp.zeros((rg, 1), I32))

        m_scr[...] = jnp.full(m_scr.shape, M_INIT, F32)
        l_scr[...] = jnp.zeros_like(l_scr)
        acc_scr[...] = jnp.zeros_like(acc_scr)

    def _attend():
        ones = jnp.ones((kb_rows, HEAD_DIM), BF16)

        kk = keys_scr[kb]
        thr = thr_scr[...]
        lim = jnp.minimum(tie_scr[...], row[:, :1])
        tie_bias = jnp.where(col + kb * kb_rows <= lim, 0.0, MASK_VALUE)
        bias_scr[...] = jnp.where(kk > thr, 0.0, jnp.where(kk == thr, tie_bias, MASK_VALUE))

        def logits(h, s_ref, r_ref):
            s = lax.dot_general(q_ref[0, h], k_ref[0, h], nt, preferred_element_type=F32) + bias_scr[...]
            s_ref[...] = s
            r_ref[...] = jnp.broadcast_to(jnp.max(s, axis=1, keepdims=True), r_ref.shape)

        def accumulate(h, s_ref, r_ref):
            m_old = m_scr[h]
            m_new = jnp.maximum(m_old, r_ref[...])
            alpha = jnp.exp2(m_old - m_new)
            p = jnp.concatenate(
                [jnp.exp2(s_ref[:, t * LANES:(t + 1) * LANES] - m_new).astype(BF16)
                 for t in range(kb_rows // LANES)], axis=1)
            v_ext = jnp.concatenate([v_ref[0, h], ones], axis=1)
            pv = jnp.dot(p, v_ext, preferred_element_type=F32)
            acc_scr[h] = alpha * acc_scr[h] + pv[:, :HEAD_DIM]
            l_scr[h] = alpha * l_scr[h] + pv[:, HEAD_DIM:]
            m_scr[h] = m_new

        bufs = ((s0_scr, r0_scr), (s1_scr, r1_scr), (s2_scr, r2_scr))
        depth = len(bufs) - 1
        for h in range(min(depth, heads)):
            logits(h, *bufs[h % len(bufs)])
        for h in range(heads):
            if h + depth < heads:
                logits(h + depth, *bufs[(h + depth) % len(bufs)])
            accumulate(h, *bufs[h % len(bufs)])

    _attend()

    @pl.when(kb == last_kb)
    def _finish():
        for h in range(heads):
            o_ref[0, :, h * HEAD_DIM:(h + 1) * HEAD_DIM] = (acc_scr[h] / l_scr[h]).astype(o_ref.dtype)


def _dsa_attention(qkv, qi, kia, kib, wi, k_top):
    bsz, heads3, seq, _ = qkv.shape
    heads = heads3 // 3
    d = heads * HEAD_DIM
    n_pair = qi.shape[-1] // LANES
    qb_rows = _pick(seq, (256, 128))
    kb_rows = _pick(seq, (512, 256, 128))
    n_kb = seq // kb_rows
    n_sets = -(-seq // (WORD_BITS * LANES))

    pairs = [(i, j) for i in range(seq // qb_rows) for j in range((i * qb_rows + qb_rows - 1) // kb_rows + 1)]
    qb_tab = jnp.asarray([p[0] for p in pairs], I32)
    kb_tab = jnp.asarray([p[1] for p in pairs], I32)

    kern = functools.partial(_dsa_kernel, n_pair=n_pair, k_top=k_top, idx_bits=seq.bit_length())
    grid_spec = pltpu.PrefetchScalarGridSpec(
        num_scalar_prefetch=2,
        grid=(bsz, len(pairs)),
        in_specs=[
            pl.BlockSpec((1, heads, qb_rows, HEAD_DIM), lambda b, s, qt, kt: (b, 0, qt[s], 0)),
            pl.BlockSpec((1, heads, kb_rows, HEAD_DIM), lambda b, s, qt, kt: (b, 1, kt[s], 0)),
            pl.BlockSpec((1, heads, kb_rows, HEAD_DIM), lambda b, s, qt, kt: (b, 2, kt[s], 0)),
            pl.BlockSpec((1, qb_rows, n_pair * LANES), lambda b, s, qt, kt: (b, qt[s], 0)),
            pl.BlockSpec((1, seq, LANES), lambda b, s, qt, kt: (b, 0, 0)),
            pl.BlockSpec((1, seq, LANES), lambda b, s, qt, kt: (b, 0, 0)),
            pl.BlockSpec((1, qb_rows, LANES), lambda b, s, qt, kt: (b, qt[s], 0)),
        ],
        out_specs=pl.BlockSpec((1, qb_rows, d), lambda b, s, qt, kt: (b, qt[s], 0)),
        scratch_shapes=[
            pltpu.VMEM((n_kb, qb_rows, kb_rows), I32),
            pltpu.VMEM((WORD_BITS, qb_rows, n_sets * LANES), I32),
            pltpu.VMEM((qb_rows, 1), I32),
            pltpu.VMEM((qb_rows, 1), I32),
            pltpu.VMEM((qb_rows, 1), F32),
            pltpu.VMEM((qb_rows, 1), F32),
            pltpu.VMEM((heads, qb_rows, LANES), F32),
            pltpu.VMEM((heads, qb_rows, LANES), F32),
            pltpu.VMEM((heads, qb_rows, HEAD_DIM), F32),
            pltpu.VMEM((qb_rows, kb_rows), F32),
            pltpu.VMEM((qb_rows, kb_rows), F32),
            pltpu.VMEM((qb_rows, kb_rows), F32),
            pltpu.VMEM((qb_rows, kb_rows), F32),
            pltpu.VMEM((qb_rows, LANES), F32),
            pltpu.VMEM((qb_rows, LANES), F32),
            pltpu.VMEM((qb_rows, LANES), F32),
        ],
    )
    return pl.pallas_call(
        kern,
        out_shape=jax.ShapeDtypeStruct((bsz, seq, d), BF16),
        grid_spec=grid_spec,
        compiler_params=_params(("parallel", "arbitrary")),
        name="dsa_select_attend",
    )(qb_tab, kb_tab, qkv, qkv, qkv, qi, kia, kib, wi)


def _rope_tables(positions, dim):
    inv = 1.0 / (ROPE_THETA ** (jnp.arange(0, dim, 2, dtype=F32) / dim))
    half = dim // 2
    lane = jnp.arange(LANES)
    inv_t = inv[lane % half]
    sign = jnp.where((lane % dim) < half, -1.0, 1.0).astype(F32)
    ang = positions.astype(F32)[..., None] * inv_t
    return jnp.cos(ang), jnp.sin(ang) * sign


def kernel(x, c, positions, ada_w, ada_b, ln_g, ln_b, s5_in_w, s5_a_re, s5_a_im, s5_log_dt, s5_b_re, s5_b_im, s5_c_re, s5_c_im, s5_d, s5_glu_w, s5_glu_b, dsa_in_w, dsa_out_w, ffn_w_in, ffn_w_out):
    bsz, seq, d = x.shape
    depth = ada_w.shape[0]
    alpha = (2.0 * depth) ** 0.25
    idx_heads = (dsa_in_w.shape[-1] - 3 * d - IDX_DIM) // (IDX_DIM + 1)
    k_top = min(TOPK_MAX, seq // 4)
    seg = min(S5_SEG, seq // SUBLANES)

    cos_h, sin_h = _rope_tables(positions, HEAD_DIM)
    cos_i, sin_i = _rope_tables(positions, IDX_DIM)
    mod = _ada_mod(c, ada_w, ada_b)
    ffn_in = ffn_w_in.astype(BF16)
    ffn_out = ffn_w_out.astype(BF16)

    for i in range(depth):
        sh1, sc1, g1, sh2, sc2, g2 = [m[:, None, :] for m in jnp.split(mod[i], 6, axis=-1)]
        j = i // 2
        if i % 2 == 0:
            prep = _s5_discretize(s5_a_re[j], s5_a_im[j], s5_log_dt[j], s5_b_re[j], s5_b_im[j],
                                  s5_c_re[j], s5_c_im[j], seg)
            u = _mod_matmul(x, sc1, sh1, s5_in_w[j].astype(BF16), F32)
            gl = _s5_scan(u, prep, s5_d[j], seg)
            x1 = _matmul_res_ln(gl, s5_glu_w[j].astype(BF16), s5_glu_b[j], x, g1,
                                ln_g[i, 0], ln_b[i, 0], alpha, glu=True)
        else:
            w = dsa_in_w[j].astype(BF16)
            n_qi = idx_heads * IDX_DIM
            pad = jnp.zeros((d, LANES - IDX_DIM - idx_heads), w.dtype)
            w_idx = jnp.concatenate(
                [w[:, 3 * d:3 * d + n_qi], w[:, 3 * d + n_qi + idx_heads:], w[:, 3 * d + n_qi:3 * d + n_qi + idx_heads], pad],
                axis=1)
            qkv = _qkv_proj(x, sc1, sh1, w, cos_h, sin_h)
            qi, kia, kib, wi = _idx_proj(x, sc1, sh1, w_idx, cos_i, sin_i, idx_heads)
            att = _dsa_attention(qkv, qi, kia, kib, wi, k_top)
            x1 = _matmul_res_ln(att, dsa_out_w[j].astype(BF16), None, x, g1,
                                ln_g[i, 0], ln_b[i, 0], alpha, glu=False)
        x = _ffn_res_ln(x1, sc2, sh2, g2, ffn_in, ffn_out, i, ln_g[i, 1], ln_b[i, 1], alpha)
    return x
```

```python
import functools
import math

import jax
import jax.numpy as jnp
from jax import lax
from jax.experimental import pallas as pl
from jax.experimental.pallas import tpu as pltpu

F32 = jnp.float32
BF16 = jnp.bfloat16
I32 = jnp.int32

S5_GROUP = 16
S5_STATE = 64
HEAD_DIM = 128
IDX_DIM = 64
TOPK_MAX = 256
ROPE_THETA = 10000.0
LN_EPS = 1e-5

LANES = 128
SUBLANES = 8
MXU_WIDTH = 256
VMEM_LIMIT_BYTES = 56 * 1024 * 1024
RESIDENT_WEIGHT_BYTES = 16 * 1024 * 1024

S5_PACK_GROUPS = 16
S5_SEG = 256

INT_MIN = -(2 ** 31)
INT_MAX = 2 ** 31 - 1
SEARCH_ROWS = 128
WORD_BITS = 32

MASK_VALUE = -2e30
M_INIT = -1e30


def _pick(n, cands):
    for c in cands:
        if n % c == 0:
            return c
    return n


def _row_halves(rows):
    half = rows // 2
    return (slice(0, half), slice(half, rows))


def _params(sem):
    return pltpu.CompilerParams(dimension_semantics=sem, vmem_limit_bytes=VMEM_LIMIT_BYTES)


def _ada_kernel(c_ref, w_ref, b_ref, o_ref):
    ca = jax.nn.silu(c_ref[...]).astype(BF16)
    o_ref[0] = jnp.dot(ca, w_ref[0].astype(BF16), preferred_element_type=F32) + b_ref[0]


def _ada_mod(c, ada_w, ada_b):
    bsz, d = c.shape
    depth, _, n = ada_w.shape
    rows = SUBLANES * ((bsz + SUBLANES - 1) // SUBLANES)
    cp = jnp.zeros((rows, d), F32).at[:bsz].set(c)
    tn = _pick(n, (1024, 512, 256, 128))
    out = pl.pallas_call(
        _ada_kernel,
        out_shape=jax.ShapeDtypeStruct((depth, rows, n), F32),
        grid=(depth, n // tn),
        in_specs=[
            pl.BlockSpec((rows, d), lambda l, j: (0, 0)),
            pl.BlockSpec((1, d, tn), lambda l, j: (l, 0, j)),
            pl.BlockSpec((1, 1, tn), lambda l, j: (l, 0, j)),
        ],
        out_specs=pl.BlockSpec((1, rows, tn), lambda l, j: (l, 0, j)),
        compiler_params=_params(("arbitrary", "arbitrary")),
        name="ada_mod",
    )(cp, ada_w, ada_b.reshape(depth, 1, n))
    return out[:, :bsz]


def _modulate(x_ref, sc_ref, sh_ref):
    return (x_ref[0] * (1.0 + sc_ref[0]) + sh_ref[0]).astype(BF16)


def _modmm_kernel(x_ref, sc_ref, sh_ref, w_ref, o_ref, h_scr):
    @pl.when(pl.program_id(2) == 0)
    def _():
        h_scr[...] = _modulate(x_ref, sc_ref, sh_ref)

    for rs in _row_halves(h_scr.shape[0]):
        o_ref[0, rs, :] = jnp.dot(h_scr[rs, :], w_ref[...], preferred_element_type=F32).astype(o_ref.dtype)


def _mod_matmul(x, sc, sh, w, out_dtype):
    bsz, seq, d = x.shape
    n = w.shape[1]
    resident = w.size * w.dtype.itemsize <= RESIDENT_WEIGHT_BYTES
    tm = _pick(seq, (1024, 512, 256, 128)) if resident else _pick(seq, (512, 256, 128))
    tn = n if resident else _pick(n, (1024, 512, 256, 128))
    w_mode = dict(pipeline_mode=pl.Buffered(1)) if resident else {}
    return pl.pallas_call(
        _modmm_kernel,
        out_shape=jax.ShapeDtypeStruct((bsz, seq, n), out_dtype),
        grid=(bsz, seq // tm, n // tn),
        in_specs=[
            pl.BlockSpec((1, tm, d), lambda b, i, j: (b, i, 0)),
            pl.BlockSpec((1, 1, d), lambda b, i, j: (b, 0, 0)),
            pl.BlockSpec((1, 1, d), lambda b, i, j: (b, 0, 0)),
            pl.BlockSpec((d, tn), lambda b, i, j: (0, j), **w_mode),
        ],
        out_specs=pl.BlockSpec((1, tm, tn), lambda b, i, j: (b, i, j)),
        scratch_shapes=[pltpu.VMEM((tm, d), BF16)],
        compiler_params=_params(("parallel", "parallel", "arbitrary")),
        name="mod_matmul",
    )(x, sc, sh, w)


def _s5_kernel(*refs, seg, width, lane_tiles):
    u_refs = refs[:lane_tiles]
    (bb_ref, cb_ref, lre_ref, lim_ref, ltre_ref, ltim_ref, d_ref, o_ref,
     xs_scr, carry_scr, cin_scr, up_scr) = refs[lane_tiles:]
    ns = xs_scr.shape[1] // 2

    @pl.when(pl.program_id(2) == 0)
    def _():
        carry_scr[...] = jnp.zeros_like(carry_scr)

    for t, u_ref in enumerate(u_refs):
        for i in range(seg):
            up_scr[t, i * SUBLANES:(i + 1) * SUBLANES, :] = u_ref[0, pl.ds(i, SUBLANES, stride=seg), :]
    u = jnp.concatenate([up_scr[t] for t in range(len(u_refs))], axis=1)
    halves = _row_halves(u.shape[0])
    for rs in halves:
        xs_scr[rs, :] = jnp.dot(u[rs].astype(BF16), bb_ref[0], preferred_element_type=F32)

    for part in range(ns // width):
        cr = slice(part * width, (part + 1) * width)
        ci = slice(ns + part * width, ns + (part + 1) * width)
        lr = jnp.broadcast_to(lre_ref[0, :, cr], (SUBLANES, width))
        li = jnp.broadcast_to(lim_ref[0, :, cr], (SUBLANES, width))

        def local_step(i, st, cr=cr, ci=ci, lr=lr, li=li):
            sr, si = st
            r0 = pl.multiple_of(i * SUBLANES, SUBLANES)
            nr = lr * sr - li * si + xs_scr[pl.ds(r0, SUBLANES), cr]
            ni = lr * si + li * sr + xs_scr[pl.ds(r0, SUBLANES), ci]
            xs_scr[pl.ds(r0, SUBLANES), cr] = nr
            xs_scr[pl.ds(r0, SUBLANES), ci] = ni
            return nr, ni

        zero = jnp.zeros((SUBLANES, width), F32)
        er, ei = lax.fori_loop(0, seg, local_step, (zero, zero), unroll=4)

        ltr = ltre_ref[0, :, cr]
        lti = ltim_ref[0, :, cr]
        c_r = carry_scr[:, cr]
        c_i = carry_scr[:, ci]
        for s in range(SUBLANES):
            cin_scr[s:s + 1, cr] = c_r
            cin_scr[s:s + 1, ci] = c_i
            e_r = er[s:s + 1, :]
            e_i = ei[s:s + 1, :]
            c_r, c_i = ltr * c_r - lti * c_i + e_r, ltr * c_i + lti * c_r + e_i
        carry_scr[:, cr] = c_r
        carry_scr[:, ci] = c_i

        def carry_step(i, st, cr=cr, ci=ci, lr=lr, li=li):
            pr, pi_ = st
            r0 = pl.multiple_of(i * SUBLANES, SUBLANES)
            nr = lr * pr - li * pi_
            ni = lr * pi_ + li * pr
            xs_scr[pl.ds(r0, SUBLANES), cr] += nr
            xs_scr[pl.ds(r0, SUBLANES), ci] += ni
            return nr, ni

        lax.fori_loop(0, seg, carry_step, (cin_scr[:, cr], cin_scr[:, ci]), unroll=4)

    for rs in halves:
        y = jnp.dot(xs_scr[rs, :].astype(BF16), cb_ref[0], preferred_element_type=F32)
        g = jax.nn.gelu(y + d_ref[0] * u[rs])
        for t in range(len(u_refs)):
            up_scr[t, rs, :] = g[:, t * LANES:(t + 1) * LANES]
    for t in range(len(u_refs)):
        for s in range(SUBLANES):
            o_ref[0, s * seg:(s + 1) * seg, t * LANES:(t + 1) * LANES] = (
                up_scr[t, pl.ds(s, seg, stride=SUBLANES), :].astype(o_ref.dtype))


def _cmul(ar, ai, br, bi):
    return ar * br - ai * bi, ar * bi + ai * br


def _s5_discretize(a_re, a_im, log_dt, b_re, b_im, c_re, c_im, seg):
    g, n = a_re.shape
    p = b_re.shape[-1]
    pg = S5_PACK_GROUPS
    packs = g // pg
    a_re, a_im = a_re.astype(F32), a_im.astype(F32)
    dt = jnp.exp(log_dt.astype(F32))[:, None]
    mag = jnp.exp(a_re * dt)
    lb_re, lb_im = mag * jnp.cos(a_im * dt), mag * jnp.sin(a_im * dt)
    den = a_re * a_re + a_im * a_im
    nr, ni = lb_re - 1.0, lb_im
    f_re = (nr * a_re + ni * a_im) / den
    f_im = (ni * a_re - nr * a_im) / den
    bb_re, bb_im = _cmul(f_re[..., None], f_im[..., None], b_re.astype(F32), b_im.astype(F32))
    lt_re, lt_im = lb_re, lb_im
    for _ in range(int(math.log2(seg))):
        lt_re, lt_im = _cmul(lt_re, lt_im, lt_re, lt_im)
    on_diag = (jnp.arange(pg * p)[:, None] // p) == (jnp.arange(pg * n)[None, :] // n)

    def blk_diag(m):
        tiled = jnp.tile(m.astype(BF16).reshape(packs, pg * p, n), (1, 1, pg))
        return jnp.where(on_diag[None], tiled, 0)

    b_blk = jnp.concatenate([blk_diag(jnp.swapaxes(bb_re, 1, 2)), blk_diag(jnp.swapaxes(bb_im, 1, 2))], axis=-1)
    c_blk = jnp.swapaxes(jnp.concatenate([blk_diag(c_re), blk_diag(-c_im)], axis=-1), 1, 2)

    def vec(m):
        return m.reshape(packs, 1, pg * n)

    return b_blk, c_blk, vec(lb_re), vec(lb_im), vec(lt_re), vec(lt_im)


def _s5_scan(u, prep, d_skip, seg):
    bsz, seq, d = u.shape
    b_blk, c_blk, lre, lim, ltre, ltim = prep
    packs, pw, ns2 = b_blk.shape
    ns = ns2 // 2
    rows = SUBLANES * seg
    width = _pick(ns, (1024, 512, 256, 128))
    lane_tiles = pw // LANES
    kern = functools.partial(_s5_kernel, seg=seg, width=width, lane_tiles=lane_tiles)
    vspec = pl.BlockSpec((1, 1, ns), lambda b, k, m: (k, 0, 0))
    u_specs = [pl.BlockSpec((1, rows, LANES), lambda b, k, m, t=t: (b, m, k * lane_tiles + t))
               for t in range(lane_tiles)]
    return pl.pallas_call(
        kern,
        out_shape=jax.ShapeDtypeStruct((bsz, seq, d), BF16),
        grid=(bsz, packs, seq // rows),
        in_specs=u_specs + [
            pl.BlockSpec((1, pw, ns2), lambda b, k, m: (k, 0, 0)),
            pl.BlockSpec((1, ns2, pw), lambda b, k, m: (k, 0, 0)),
            vspec, vspec, vspec, vspec,
            pl.BlockSpec((1, 1, pw), lambda b, k, m: (k, 0, 0)),
        ],
        out_specs=pl.BlockSpec((1, rows, pw), lambda b, k, m: (b, m, k)),
        scratch_shapes=[
            pltpu.VMEM((rows, ns2), F32),
            pltpu.VMEM((1, ns2), F32),
            pltpu.VMEM((SUBLANES, ns2), F32),
            pltpu.VMEM((lane_tiles, rows, LANES), F32),
        ],
        compiler_params=_params(("parallel", "parallel", "arbitrary")),
        name="s5_scan",
    )(*([u] * lane_tiles), b_blk, c_blk, lre, lim, ltre, ltim, d_skip.reshape(packs, 1, pw).astype(F32))


def _layer_norm_rows(tiles, lng_ref, lnb_ref, o_ref, d):
    tn = tiles[0].shape[1]
    tot = tiles[0].sum(axis=1, keepdims=True)
    for t in tiles[1:]:
        tot = tot + t.sum(axis=1, keepdims=True)
    mu = tot * (1.0 / d)
    sq = jnp.square(tiles[0] - mu).sum(axis=1, keepdims=True)
    for t in tiles[1:]:
        sq = sq + jnp.square(t - mu).sum(axis=1, keepdims=True)
    inv = lax.rsqrt(sq * (1.0 / d) + LN_EPS)
    for k, t in enumerate(tiles):
        cs = slice(k * tn, (k + 1) * tn)
        o_ref[0, :, cs] = ((t - mu) * inv * lng_ref[:, cs] + lnb_ref[:, cs]).astype(o_ref.dtype)


def _mmln_kernel(*refs, glu, nj, alpha, d):
    if glu:
        a_ref, w1_ref, w2_ref, b1_ref, b2_ref, x_ref, g_ref, lng_ref, lnb_ref, o_ref, r_scr = refs
    else:
        a_ref, w1_ref, x_ref, g_ref, lng_ref, lnb_ref, o_ref, r_scr = refs
    j = pl.program_id(2)
    for rs in _row_halves(a_ref.shape[1]):
        a = a_ref[0, rs, :]
        y = jnp.dot(a, w1_ref[...], preferred_element_type=F32)
        if glu:
            y = y + b1_ref[...]
            gate = jnp.dot(a, w2_ref[...], preferred_element_type=F32) + b2_ref[...]
            y = y * jax.nn.sigmoid(gate)
        r_scr[j, rs, :] = alpha * x_ref[0, rs, :] + (1.0 + g_ref[0]) * y

    @pl.when(j == nj - 1)
    def _():
        _layer_norm_rows([r_scr[t] for t in range(nj)], lng_ref, lnb_ref, o_ref, d)


def _matmul_res_ln(a, w, bias, xres, gate, ln_g, ln_b, alpha, glu):
    bsz, seq, k = a.shape
    d = xres.shape[-1]
    resident = w.size * w.dtype.itemsize <= RESIDENT_WEIGHT_BYTES
    tm = _pick(seq, (512, 256, 128))
    tn = d if resident else _pick(d, (512, 256, 128))
    nj = d // tn
    w_mode = dict(pipeline_mode=pl.Buffered(1)) if resident else {}
    a_spec = pl.BlockSpec((1, tm, k), lambda b, i, j: (b, i, 0))
    w1_spec = pl.BlockSpec((k, tn), lambda b, i, j: (0, j), **w_mode)
    tail_specs = [
        pl.BlockSpec((1, tm, tn), lambda b, i, j: (b, i, j)),
        pl.BlockSpec((1, 1, tn), lambda b, i, j: (b, 0, j)),
        pl.BlockSpec((1, d), lambda b, i, j: (0, 0)),
        pl.BlockSpec((1, d), lambda b, i, j: (0, 0)),
    ]
    tail = (xres, gate, ln_g.reshape(1, d).astype(F32), ln_b.reshape(1, d).astype(F32))
    if glu:
        in_specs = [a_spec, w1_spec,
                    pl.BlockSpec((k, tn), lambda b, i, j: (0, j + nj), **w_mode),
                    pl.BlockSpec((1, tn), lambda b, i, j: (0, j)),
                    pl.BlockSpec((1, tn), lambda b, i, j: (0, j + nj))] + tail_specs
        b2d = bias.reshape(1, 2 * d).astype(F32)
        args = (a, w, w, b2d, b2d) + tail
    else:
        in_specs = [a_spec, w1_spec] + tail_specs
        args = (a, w) + tail
    kern = functools.partial(_mmln_kernel, glu=glu, nj=nj, alpha=alpha, d=d)
    return pl.pallas_call(
        kern,
        out_shape=jax.ShapeDtypeStruct((bsz, seq, d), F32),
        grid=(bsz, seq // tm, nj),
        in_specs=in_specs,
        out_specs=pl.BlockSpec((1, tm, d), lambda b, i, j: (b, i, 0)),
        scratch_shapes=[pltpu.VMEM((nj, tm, tn), F32)],
        compiler_params=_params(("parallel", "parallel", "arbitrary")),
        name="matmul_res_ln",
    )(*args)


def _ffn_kernel(x_ref, sc_ref, sh_ref, g_ref, wg_ref, wu_ref, wo_ref, lng_ref, lnb_ref, o_ref,
                h_scr, acc_scr, *, nf, alpha, d, tn):
    f = pl.program_id(2)

    @pl.when(f == 0)
    def _():
        h_scr[...] = _modulate(x_ref, sc_ref, sh_ref)
        acc_scr[...] = jnp.zeros_like(acc_scr)

    h = h_scr[...]
    a_g = jnp.dot(h, wg_ref[0], preferred_element_type=F32)
    a_u = jnp.dot(h, wu_ref[0], preferred_element_type=F32)
    act = (jax.nn.silu(a_g) * a_u).astype(BF16)
    acc_scr[...] += jnp.dot(act, wo_ref[0], preferred_element_type=F32)

    @pl.when(f == nf - 1)
    def _():
        tiles = []
        for k in range(d // tn):
            cs = slice(k * tn, (k + 1) * tn)
            tiles.append(alpha * x_ref[0, :, cs] + (1.0 + g_ref[0, :, cs]) * acc_scr[:, cs])
        _layer_norm_rows(tiles, lng_ref, lnb_ref, o_ref, d)


def _ffn_res_ln(x, sc, sh, gate, w_in, w_out, layer, ln_g, ln_b, alpha):
    bsz, seq, d = x.shape
    dff = w_out.shape[1]
    tm = _pick(seq, (512, 256, 128))
    tf = _pick(dff, (512, 256, 128))
    nf = dff // tf
    tn = _pick(d, (512, 256, 128))
    vec = pl.BlockSpec((1, 1, d), lambda b, i, f: (b, 0, 0))
    kern = functools.partial(_ffn_kernel, nf=nf, alpha=alpha, d=d, tn=tn)
    return pl.pallas_call(
        kern,
        out_shape=jax.ShapeDtypeStruct((bsz, seq, d), F32),
        grid=(bsz, seq // tm, nf),
        in_specs=[
            pl.BlockSpec((1, tm, d), lambda b, i, f: (b, i, 0)),
            vec, vec, vec,
            pl.BlockSpec((1, d, tf), lambda b, i, f: (layer, 0, f)),
            pl.BlockSpec((1, d, tf), lambda b, i, f: (layer, 0, f + nf)),
            pl.BlockSpec((1, tf, d), lambda b, i, f: (layer, f, 0)),
            pl.BlockSpec((1, d), lambda b, i, f: (0, 0)),
            pl.BlockSpec((1, d), lambda b, i, f: (0, 0)),
        ],
        out_specs=pl.BlockSpec((1, tm, d), lambda b, i, f: (b, i, 0)),
        scratch_shapes=[pltpu.VMEM((tm, d), BF16), pltpu.VMEM((tm, d), F32)],
        compiler_params=_params(("parallel", "parallel", "arbitrary")),
        name="ffn_res_ln",
    )(x, sc, sh, gate, w_in, w_in, w_out, ln_g.reshape(1, d).astype(F32), ln_b.reshape(1, d).astype(F32))


def _qkv_kernel(x_ref, sc_ref, sh_ref, w_ref, cos_ref, sin_ref, o_ref, h_scr, *, tiles_per_tensor, q_scale):
    j = pl.program_id(2)

    @pl.when(j == 0)
    def _():
        h_scr[...] = _modulate(x_ref, sc_ref, sh_ref)

    tensor = j // tiles_per_tensor
    scale = jnp.where(tensor == 0, q_scale, 1.0)
    is_v = tensor == 2
    chunk = min(MXU_WIDTH, w_ref.shape[1])
    for rs in _row_halves(h_scr.shape[0]):
        a = jnp.where(is_v, 1.0, cos_ref[0, rs, :] * scale)
        b = jnp.where(is_v, 0.0, sin_ref[0, rs, :] * scale)
        for c in range(w_ref.shape[1] // chunk):
            acc = jnp.dot(h_scr[rs, :], w_ref[:, c * chunk:(c + 1) * chunk], preferred_element_type=F32)
            for hh in range(chunk // HEAD_DIM):
                xc = acc[:, hh * HEAD_DIM:(hh + 1) * HEAD_DIM]
                xc = xc * a + pltpu.roll(xc, HEAD_DIM // 2, 1) * b
                o_ref[0, c * (chunk // HEAD_DIM) + hh, rs, :] = xc.astype(o_ref.dtype)


def _qkv_proj(x, sc, sh, w_qkv, cos_t, sin_t):
    bsz, seq, d = x.shape
    tm = _pick(seq, (1024, 512, 256, 128))
    tn = _pick(d, (2048, 1024, 512, 256, 128))
    hpt = tn // HEAD_DIM
    kern = functools.partial(_qkv_kernel, tiles_per_tensor=d // tn, q_scale=HEAD_DIM ** -0.5 * math.log2(math.e))
    return pl.pallas_call(
        kern,
        out_shape=jax.ShapeDtypeStruct((bsz, 3 * d // HEAD_DIM, seq, HEAD_DIM), BF16),
        grid=(bsz, seq // tm, 3 * d // tn),
        in_specs=[
            pl.BlockSpec((1, tm, d), lambda b, i, j: (b, i, 0)),
            pl.BlockSpec((1, 1, d), lambda b, i, j: (b, 0, 0)),
            pl.BlockSpec((1, 1, d), lambda b, i, j: (b, 0, 0)),
            pl.BlockSpec((d, tn), lambda b, i, j: (0, j)),
            pl.BlockSpec((1, tm, HEAD_DIM), lambda b, i, j: (b, i, 0)),
            pl.BlockSpec((1, tm, HEAD_DIM), lambda b, i, j: (b, i, 0)),
        ],
        out_specs=pl.BlockSpec((1, hpt, tm, HEAD_DIM), lambda b, i, j: (b, j, i, 0)),
        scratch_shapes=[pltpu.VMEM((tm, d), BF16)],
        compiler_params=_params(("parallel", "parallel", "arbitrary")),
        name="dsa_qkv_proj",
    )(x, sc, sh, w_qkv, cos_t, sin_t)


def _idx_kernel(x_ref, sc_ref, sh_ref, w_ref, cos_ref, sin_ref, qi_ref, kia_ref, kib_ref, wi_ref,
                *, n_pair, idx_heads, w_scale):
    h = _modulate(x_ref, sc_ref, sh_ref)
    halves = _row_halves(h.shape[0])
    accs = [jnp.dot(h[rs], w_ref[...], preferred_element_type=F32) for rs in halves]
    lane = lax.broadcasted_iota(I32, (halves[0].stop, LANES), 1)
    first_half = (lane % IDX_DIM) < IDX_DIM // 2

    for rs, acc in zip(halves, accs):
        cos = cos_ref[0, rs, :]
        sin = sin_ref[0, rs, :]

        def rope(xc, cos=cos, sin=sin):
            partner = jnp.where(first_half, pltpu.roll(xc, LANES - IDX_DIM // 2, 1),
                                pltpu.roll(xc, IDX_DIM // 2, 1))
            return xc * cos + partner * sin

        for p in range(n_pair):
            cs = slice(p * LANES, (p + 1) * LANES)
            qi_ref[0, rs, cs] = rope(acc[:, cs]).astype(qi_ref.dtype)
        last = acc[:, n_pair * LANES:]
        ka = jnp.where(lane < IDX_DIM, rope(last), 0.0)
        kia_ref[0, rs, :] = ka.astype(kia_ref.dtype)
        kib_ref[0, rs, :] = pltpu.roll(ka, IDX_DIM, 1).astype(kib_ref.dtype)
        wi_ref[0, rs, :] = jnp.where(lane < idx_heads, pltpu.roll(last, LANES - IDX_DIM, 1), 0.0) * w_scale


def _idx_proj(x, sc, sh, w_idx, cos_t, sin_t, idx_heads):
    bsz, seq, d = x.shape
    n = w_idx.shape[1]
    n_pair = idx_heads // 2
    tm = _pick(seq, (1024, 512, 256, 128))
    kern = functools.partial(_idx_kernel, n_pair=n_pair, idx_heads=idx_heads,
                             w_scale=(idx_heads ** -0.5) * (IDX_DIM ** -0.5))
    row = lambda b, i: (b, i, 0)
    return pl.pallas_call(
        kern,
        out_shape=(
            jax.ShapeDtypeStruct((bsz, seq, n_pair * LANES), BF16),
            jax.ShapeDtypeStruct((bsz, seq, LANES), BF16),
            jax.ShapeDtypeStruct((bsz, seq, LANES), BF16),
            jax.ShapeDtypeStruct((bsz, seq, LANES), F32),
        ),
        grid=(bsz, seq // tm),
        in_specs=[
            pl.BlockSpec((1, tm, d), row),
            pl.BlockSpec((1, 1, d), lambda b, i: (b, 0, 0)),
            pl.BlockSpec((1, 1, d), lambda b, i: (b, 0, 0)),
            pl.BlockSpec((d, n), lambda b, i: (0, 0)),
            pl.BlockSpec((1, tm, LANES), row),
            pl.BlockSpec((1, tm, LANES), row),
        ],
        out_specs=(
            pl.BlockSpec((1, tm, n_pair * LANES), row),
            pl.BlockSpec((1, tm, LANES), row),
            pl.BlockSpec((1, tm, LANES), row),
            pl.BlockSpec((1, tm, LANES), row),
        ),
        compiler_params=_params(("parallel", "parallel")),
        name="dsa_idx_proj",
    )(x, sc, sh, w_idx, cos_t, sin_t)


def _bit_transpose32(words):
    a = list(words)
    j = WORD_BITS // 2
    mask = 0x0000FFFF
    while j:
        k = 0
        while k < WORD_BITS:
            t = (a[k] ^ lax.shift_right_logical(a[k + j], jnp.int32(j))) & jnp.int32(mask)
            a[k] = a[k] ^ t
            a[k + j] = a[k + j] ^ (t << j)
            k = (k + j + 1) & ~j
        j >>= 1
        if j:
            mask = (mask ^ (mask << j)) & 0xFFFFFFFF
    return a


def _dsa_kernel(qb_tab, kb_tab, q_ref, k_ref, v_ref, qi_ref, kia_ref, kib_ref, wi_ref, o_ref,
                keys_scr, planes_scr, thr_scr, tie_scr, ngt_scr, nge_scr,
                m_scr, l_scr, acc_scr, bias_scr, s0_scr, s1_scr, s2_scr, r0_scr, r1_scr, r2_scr,
                *, n_pair, k_top, idx_bits):
    heads, qb_rows, _ = acc_scr.shape
    kb_rows = keys_scr.shape[2]
    qb = qb_tab[pl.program_id(1)]
    kb = kb_tab[pl.program_id(1)]
    q0 = qb * qb_rows
    last_kb = (q0 + qb_rows - 1) // kb_rows
    n_chunk = last_kb + 1
    row = lax.broadcasted_iota(I32, (qb_rows, kb_rows), 0) + q0
    col = lax.broadcasted_iota(I32, (qb_rows, kb_rows), 1)
    nt = (((1,), (1,)), ((), ()))

    @pl.when(kb == 0)
    def _select():
        wi = wi_ref[0]

        def score_chunk(c, carry):
            k0 = pl.multiple_of(c * kb_rows, kb_rows)
            ka = kia_ref[0, pl.ds(k0, kb_rows), :]
            kbm = kib_ref[0, pl.ds(k0, kb_rows), :]
            sc = jnp.zeros((qb_rows, kb_rows), F32)
            for p in range(n_pair):
                qp = qi_ref[0, :, p * LANES:(p + 1) * LANES]
                sa = lax.dot_general(qp, ka, nt, preferred_element_type=F32)
                sb = lax.dot_general(qp, kbm, nt, preferred_element_type=F32)
                sc = sc + wi[:, 2 * p:2 * p + 1] * jnp.maximum(sa, 0.0)
                sc = sc + wi[:, 2 * p + 1:2 * p + 2] * jnp.maximum(sb, 0.0)
            bits = lax.bitcast_convert_type(sc, I32)
            key = bits ^ ((bits >> 31) & INT_MAX)
            keys_scr[c] = jnp.where(col + k0 <= row, key, INT_MIN)
            return carry

        lax.fori_loop(0, n_chunk, score_chunk, 0)

        slices_per_chunk = kb_rows // LANES
        chunks_per_set = WORD_BITS // slices_per_chunk
        n_slices = keys_scr.shape[0] * slices_per_chunk
        n_sets = planes_scr.shape[2] // LANES
        live_sets = (n_chunk + chunks_per_set - 1) // chunks_per_set

        def fill_chunk(c, carry):
            keys_scr[c] = jnp.full((qb_rows, kb_rows), INT_MIN, I32)
            return carry

        lax.fori_loop(n_chunk, jnp.minimum(live_sets * chunks_per_set, keys_scr.shape[0]), fill_chunk, 0)

        def pack_rows(g, carry, st):
            r0 = pl.multiple_of(g * SUBLANES, SUBLANES)
            words = []
            for s in range(WORD_BITS):
                sl = st * WORD_BITS + s
                if sl < n_slices:
                    c, off = divmod(sl, slices_per_chunk)
                    words.append(keys_scr[c, pl.ds(r0, SUBLANES), off * LANES:(off + 1) * LANES])
                else:
                    words.append(jnp.full((SUBLANES, LANES), INT_MIN, I32))
            words = _bit_transpose32(words)
            words[0] = ~words[0]
            for i in range(WORD_BITS):
                planes_scr[i, pl.ds(r0, SUBLANES), st * LANES:(st + 1) * LANES] = words[i]
            return carry

        for st in range(n_sets):
            @pl.when(st < live_sets)
            def _(st=st):
                lax.fori_loop(0, qb_rows // SUBLANES, functools.partial(pack_rows, st=st), 0)

        lane_ones = jnp.ones((LANES, LANES), BF16)

        def radix_select(live):
            width = live * LANES

            def row_total(pc):
                tot = pc[:, :LANES]
                for st in range(1, live):
                    tot = tot + pc[:, st * LANES:(st + 1) * LANES]
                return jnp.dot(tot.astype(F32).astype(BF16), lane_ones, preferred_element_type=F32)

            def all_sets(mask):
                return jnp.concatenate([mask] * live, axis=1)

            def select_two_bits(i, carry):
                cand, above, prefix = carry
                hi = planes_scr[2 * i, :, :width]
                lo = planes_scr[2 * i + 1, :, :width]
                c1 = cand & hi
                c0 = cand ^ c1
                c11 = c1 & lo
                c10 = c1 ^ c11
                c01 = c0 & lo
                c00 = c0 ^ c01
                r11 = above + row_total(lax.population_count(c11))
                r10 = r11 + row_total(lax.population_count(c10))
                r01 = r10 + row_total(lax.population_count(c01))
                t11 = r11 >= k_top
                t10 = r10 >= k_top
                t01 = r01 >= k_top
                cand = jnp.where(all_sets(t11), c11,
                                 jnp.where(all_sets(t10), c10, jnp.where(all_sets(t01), c01, c00)))
                above = jnp.where(t11, above, jnp.where(t10, r11, jnp.where(t01, r10, r01)))
                bits = jnp.where(t11, 3, jnp.where(t10, 2, jnp.where(t01, 1, 0)))
                prefix = prefix | (bits << (WORD_BITS - 2 - 2 * i))
                return cand, above, prefix

            cand, above, prefix = lax.fori_loop(
                0, WORD_BITS // 2, select_two_bits,
                (jnp.full((qb_rows, width), -1, I32),
                 jnp.zeros((qb_rows, LANES), F32), jnp.zeros((qb_rows, LANES), I32)))
            equal = row_total(lax.population_count(cand))
            thr_scr[...] = (prefix ^ INT_MIN)[:, :1]
            tie_scr[...] = jnp.full((qb_rows, 1), INT_MAX, I32)
            ngt_scr[...] = above[:, :1]
            nge_scr[...] = (above + equal)[:, :1]

        for live in range(1, n_sets + 1):
            pl.when(live_sets == live)(functools.partial(radix_select, live))

        rg = min(qb_rows, SEARCH_ROWS)
        lane_col = lax.broadcasted_iota(I32, (rg, LANES), 1)

        for r in range(qb_rows // rg):
            rs = slice(r * rg, (r + 1) * rg)

            def count(pred, rs=rs):
                def body(c, acc):
                    for t in range(kb_rows // LANES):
                        kk = keys_scr[c, rs, t * LANES:(t + 1) * LANES]
                        idx = lane_col + (c * kb_rows + t * LANES)
                        acc = acc + jnp.where(pred(kk, idx), 1.0, 0.0)
                    return acc
                acc = lax.fori_loop(0, n_chunk, body, jnp.zeros((rg, LANES), F32))
                return jnp.sum(acc, axis=1, keepdims=True)

            def wide(v):
                return jnp.broadcast_to(v, (rg, LANES))

            @pl.when(jnp.max(nge_scr[rs, :]) > k_top)
            def _ties(count=count, rs=rs):
                thr_w = wide(thr_scr[rs, :])
                need = k_top - ngt_scr[rs, :]

                def tie_bit(i, jt):
                    cand = jt + (jnp.int32(1) << (idx_bits - 1 - i))
                    cand_w = wide(cand)
                    cnt = count(lambda kk, idx: (kk == thr_w) & (idx < cand_w))
                    return jnp.where(cnt < need, cand, jt)

                tie_scr[rs, :] = lax.fori_loop(0, idx_bits, tie_bit, jnp.zeros((rg, 1), I32))

        m_scr[...] = jnp.full(m_scr.shape, M_INIT, F32)
        l_scr[...] = jnp.zeros_like(l_scr)
        acc_scr[...] = jnp.zeros_like(acc_scr)

    def _attend():
        ones = jnp.ones((kb_rows, HEAD_DIM), BF16)

        kk = keys_scr[kb]
        thr = thr_scr[...]
        lim = jnp.minimum(tie_scr[...], row[:, :1])
        tie_bias = jnp.where(col + kb * kb_rows <= lim, 0.0, MASK_VALUE)
        bias_scr[...] = jnp.where(kk > thr, 0.0, jnp.where(kk == thr, tie_bias, MASK_VALUE))

        def logits(h, s_ref, r_ref):
            s = lax.dot_general(q_ref[0, h], k_ref[0, h], nt, preferred_element_type=F32) + bias_scr[...]
            s_ref[...] = s
            r_ref[...] = jnp.broadcast_to(jnp.max(s, axis=1, keepdims=True), r_ref.shape)

        def accumulate(h, s_ref, r_ref):
            m_old = m_scr[h]
            m_new = jnp.maximum(m_old, r_ref[...])
            alpha = jnp.exp2(m_old - m_new)
            p = jnp.concatenate(
                [jnp.exp2(s_ref[:, t * LANES:(t + 1) * LANES] - m_new).astype(BF16)
                 for t in range(kb_rows // LANES)], axis=1)
            v_ext = jnp.concatenate([v_ref[0, h], ones], axis=1)
            pv = jnp.dot(p, v_ext, preferred_element_type=F32)
            acc_scr[h] = alpha * acc_scr[h] + pv[:, :HEAD_DIM]
            l_scr[h] = alpha * l_scr[h] + pv[:, HEAD_DIM:]
            m_scr[h] = m_new

        bufs = ((s0_scr, r0_scr), (s1_scr, r1_scr), (s2_scr, r2_scr))
        depth = len(bufs) - 1
        for h in range(min(depth, heads)):
            logits(h, *bufs[h % len(bufs)])
        for h in range(heads):
            if h + depth < heads:
                logits(h + depth, *bufs[(h + depth) % len(bufs)])
            accumulate(h, *bufs[h % len(bufs)])

    _attend()

    @pl.when(kb == last_kb)
    def _finish():
        for h in range(heads):
            o_ref[0, :, h * HEAD_DIM:(h + 1) * HEAD_DIM] = (acc_scr[h] / l_scr[h]).astype(o_ref.dtype)


def _dsa_attention(qkv, qi, kia, kib, wi, k_top):
    bsz, heads3, seq, _ = qkv.shape
    heads = heads3 // 3
    d = heads * HEAD_DIM
    n_pair = qi.shape[-1] // LANES
    qb_rows = _pick(seq, (256, 128))
    kb_rows = _pick(seq, (512, 256, 128))
    n_kb = seq // kb_rows
    n_sets = -(-seq // (WORD_BITS * LANES))

    pairs = [(i, j) for i in range(seq // qb_rows) for j in range((i * qb_rows + qb_rows - 1) // kb_rows + 1)]
    qb_tab = jnp.asarray([p[0] for p in pairs], I32)
    kb_tab = jnp.asarray([p[1] for p in pairs], I32)

    kern = functools.partial(_dsa_kernel, n_pair=n_pair, k_top=k_top, idx_bits=seq.bit_length())
    grid_spec = pltpu.PrefetchScalarGridSpec(
        num_scalar_prefetch=2,
        grid=(bsz, len(pairs)),
        in_specs=[
            pl.BlockSpec((1, heads, qb_rows, HEAD_DIM), lambda b, s, qt, kt: (b, 0, qt[s], 0)),
            pl.BlockSpec((1, heads, kb_rows, HEAD_DIM), lambda b, s, qt, kt: (b, 1, kt[s], 0)),
            pl.BlockSpec((1, heads, kb_rows, HEAD_DIM), lambda b, s, qt, kt: (b, 2, kt[s], 0)),
            pl.BlockSpec((1, qb_rows, n_pair * LANES), lambda b, s, qt, kt: (b, qt[s], 0)),
            pl.BlockSpec((1, seq, LANES), lambda b, s, qt, kt: (b, 0, 0)),
            pl.BlockSpec((1, seq, LANES), lambda b, s, qt, kt: (b, 0, 0)),
            pl.BlockSpec((1, qb_rows, LANES), lambda b, s, qt, kt: (b, qt[s], 0)),
        ],
        out_specs=pl.BlockSpec((1, qb_rows, d), lambda b, s, qt, kt: (b, qt[s], 0)),
        scratch_shapes=[
            pltpu.VMEM((n_kb, qb_rows, kb_rows), I32),
            pltpu.VMEM((WORD_BITS, qb_rows, n_sets * LANES), I32),
            pltpu.VMEM((qb_rows, 1), I32),
            pltpu.VMEM((qb_rows, 1), I32),
            pltpu.VMEM((qb_rows, 1), F32),
            pltpu.VMEM((qb_rows, 1), F32),
            pltpu.VMEM((heads, qb_rows, LANES), F32),
            pltpu.VMEM((heads, qb_rows, LANES), F32),
            pltpu.VMEM((heads, qb_rows, HEAD_DIM), F32),
            pltpu.VMEM((qb_rows, kb_rows), F32),
            pltpu.VMEM((qb_rows, kb_rows), F32),
            pltpu.VMEM((qb_rows, kb_rows), F32),
            pltpu.VMEM((qb_rows, kb_rows), F32),
            pltpu.VMEM((qb_rows, LANES), F32),
            pltpu.VMEM((qb_rows, LANES), F32),
            pltpu.VMEM((qb_rows, LANES), F32),
        ],
    )
    return pl.pallas_call(
        kern,
        out_shape=jax.ShapeDtypeStruct((bsz, seq, d), BF16),
        grid_spec=grid_spec,
        compiler_params=_params(("parallel", "arbitrary")),
        name="dsa_select_attend",
    )(qb_tab, kb_tab, qkv, qkv, qkv, qi, kia, kib, wi)


def _rope_tables(positions, dim):
    inv = 1.0 / (ROPE_THETA ** (jnp.arange(0, dim, 2, dtype=F32) / dim))
    half = dim // 2
    lane = jnp.arange(LANES)
    inv_t = inv[lane % half]
    sign = jnp.where((lane % dim) < half, -1.0, 1.0).astype(F32)
    ang = positions.astype(F32)[..., None] * inv_t
    return jnp.cos(ang), jnp.sin(ang) * sign


def kernel(x, c, positions, ada_w, ada_b, ln_g, ln_b, s5_in_w, s5_a_re, s5_a_im, s5_log_dt, s5_b_re, s5_b_im, s5_c_re, s5_c_im, s5_d, s5_glu_w, s5_glu_b, dsa_in_w, dsa_out_w, ffn_w_in, ffn_w_out):
    bsz, seq, d = x.shape
    depth = ada_w.shape[0]
    alpha = (2.0 * depth) ** 0.25
    idx_heads = (dsa_in_w.shape[-1] - 3 * d - IDX_DIM) // (IDX_DIM + 1)
    k_top = min(TOPK_MAX, seq // 4)
    seg = min(S5_SEG, seq // SUBLANES)

    cos_h, sin_h = _rope_tables(positions, HEAD_DIM)
    cos_i, sin_i = _rope_tables(positions, IDX_DIM)
    mod = _ada_mod(c, ada_w, ada_b)
    ffn_in = ffn_w_in.astype(BF16)
    ffn_out = ffn_w_out.astype(BF16)

    for i in range(depth):
        sh1, sc1, g1, sh2, sc2, g2 = [m[:, None, :] for m in jnp.split(mod[i], 6, axis=-1)]
        j = i // 2
        if i % 2 == 0:
            prep = _s5_discretize(s5_a_re[j], s5_a_im[j], s5_log_dt[j], s5_b_re[j], s5_b_im[j],
                                  s5_c_re[j], s5_c_im[j], seg)
            u = _mod_matmul(x, sc1, sh1, s5_in_w[j].astype(BF16), F32)
            gl = _s5_scan(u, prep, s5_d[j], seg)
            x1 = _matmul_res_ln(gl, s5_glu_w[j].astype(BF16), s5_glu_b[j], x, g1,
                                ln_g[i, 0], ln_b[i, 0], alpha, glu=True)
        else:
            w = dsa_in_w[j].astype(BF16)
            n_qi = idx_heads * IDX_DIM
            pad = jnp.zeros((d, LANES - IDX_DIM - idx_heads), w.dtype)
            w_idx = jnp.concatenate(
                [w[:, 3 * d:3 * d + n_qi], w[:, 3 * d + n_qi + idx_heads:], w[:, 3 * d + n_qi:3 * d + n_qi + idx_heads], pad],
                axis=1)
            qkv = _qkv_proj(x, sc1, sh1, w, cos_h, sin_h)
            qi, kia, kib, wi = _idx_proj(x, sc1, sh1, w_idx, cos_i, sin_i, idx_heads)
            att = _dsa_attention(qkv, qi, kia, kib, wi, k_top)
            x1 = _matmul_res_ln(att, dsa_out_w[j].astype(BF16), None, x, g1,
                                ln_g[i, 0], ln_b[i, 0], alpha, glu=False)
        x = _ffn_res_ln(x1, sc2, sh2, g2, ffn_in, ffn_out, i, ln_g[i, 1], ln_b[i, 1], alpha)
    return x
```
